```python
import math
import jax
import jax.numpy as jnp
from jax import lax
import numpy as np

D_MODEL = 1024
BATCH = 2
SEQ = 16384
DEPTH = 2

GRID_W = 64
CTX_LEN = 256
N_EVEN = (DEPTH + 1) // 2
N_ODD = DEPTH // 2

CONV_WIDTH = D_MODEL // 2
CONV_WIDTH_TAPS = 31
DIFF_HEAD_DIM = 64
DIFF_V_DIM = 2 * DIFF_HEAD_DIM
DIFF_HEADS = (D_MODEL // 2) // DIFF_V_DIM
QK_W = DIFF_HEADS * 2 * DIFF_HEAD_DIM
V_W = DIFF_HEADS * DIFF_V_DIM
EV_A0 = 2 * QK_W + V_W
EVEN_IN = EV_A0 + 2 * CONV_WIDTH
EVEN_OUT = CONV_WIDTH + V_W
ROPE_BASE = 10000.0
Q_BLOCK = 128

CHUNK = 128
GMLP_GROUPS = 4
GMLP_W = D_MODEL // 2
GMLP_GROUP_DIM = GMLP_W // GMLP_GROUPS
FOURIER_GROUPS = 4
FOURIER_W = D_MODEL // 2
FOURIER_GROUP_DIM = FOURIER_W // FOURIER_GROUPS
ODD_IN = 2 * GMLP_W + FOURIER_W
ODD_OUT = GMLP_W + FOURIER_W

N_EXPERTS = 32
TOP_K = 4
D_EXPERT = D_MODEL
SWIGLU_LIMIT = 7.0
SWIGLU_ALPHA = 1.702
EXPERT_BLOCK = 128

LN_EPS = 1e-5
DEEPNORM_ALPHA = (2 * DEPTH) ** 0.25
DEEPNORM_BETA = (8 * DEPTH) ** -0.25

kernel_name = 'hybrid_conformer_diffattn_gmlp_fnet_moe_dit'


def layer_norm(x, g, b):
    xf = x.astype(jnp.float32)
    mu = jnp.mean(xf, -1, keepdims=True)
    var = jnp.mean(jnp.square(xf - mu), -1, keepdims=True)
    return ((xf - mu) * lax.rsqrt(var + LN_EPS) * g.astype(jnp.float32) + b.astype(jnp.float32)).astype(x.dtype)


def rms_norm(x, g):
    xf = x.astype(jnp.float32)
    return (xf * lax.rsqrt(jnp.mean(xf * xf, -1, keepdims=True) + LN_EPS) * g.astype(jnp.float32)).astype(x.dtype)


def modulate(x, shift, scale):
    return x * (1.0 + scale) + shift


def axial_rope_tables(length):
    rows = length // GRID_W
    row = jnp.repeat(jnp.arange(rows, dtype=jnp.float32), GRID_W)
    col = jnp.tile(jnp.arange(GRID_W, dtype=jnp.float32), rows)
    n_freq = DIFF_HEAD_DIM // 4
    inv_freq = ROPE_BASE ** (-jnp.arange(n_freq, dtype=jnp.float32) / n_freq)
    ang = jnp.stack([row[:, None] * inv_freq, col[:, None] * inv_freq], axis=1)
    return jnp.cos(ang), jnp.sin(ang)


def apply_axial_rope(x, cos, sin):
    xs = x.astype(jnp.float32).reshape(x.shape[:-1] + (2, 2, DIFF_HEAD_DIM // 4))
    x1, x2 = xs[..., 0, :], xs[..., 1, :]
    c, s = cos[:, None, None], sin[:, None, None]
    out = jnp.stack([x1 * c - x2 * s, x1 * s + x2 * c], axis=-2)
    return out.reshape(x.shape).astype(x.dtype)


def heads_qk(z):
    return z.reshape(z.shape[:2] + (DIFF_HEADS, 2, DIFF_HEAD_DIM))


def heads_v(z):
    return z.reshape(z.shape[:2] + (DIFF_HEADS, DIFF_V_DIM))


def diff_lambda(lq1, lk1, lq2, lk2, lam_init):
    f32 = jnp.float32
    return (jnp.exp(jnp.sum(lq1.astype(f32) * lk1.astype(f32)))
            - jnp.exp(jnp.sum(lq2.astype(f32) * lk2.astype(f32))) + lam_init)


def diff_attend(q, k, v, lam):
    s = jnp.einsum('bqhcd,bkhcd->bhcqk', q, k, preferred_element_type=jnp.float32) * (DIFF_HEAD_DIM ** -0.5)
    p = jax.nn.softmax(s, axis=-1)
    w = p[:, :, 0] - lam * p[:, :, 1]
    return jnp.einsum('bhqk,bkhv->bqhv', w.astype(v.dtype), v)


def diff_attend_blocks(q, k, v, lam):
    b, length = q.shape[:2]
    n_blk = length // Q_BLOCK
    qb = jnp.moveaxis(q.reshape((b, n_blk, Q_BLOCK) + q.shape[2:]), 1, 0)
    out = lax.map(lambda qi: diff_attend(qi, k, v, lam), qb)
    return jnp.moveaxis(out, 0, 1).reshape((b, length) + out.shape[3:])


def diff_heads_out(o, g, lam_init):
    b, length = o.shape[:2]
    return (rms_norm(o, g) * (1.0 - lam_init)).reshape(b, length, V_W)


def conformer_conv(z, conv_w, conv_b, ln_g, ln_b):
    a = z[..., EV_A0:EV_A0 + CONV_WIDTH] * jax.nn.sigmoid(z[..., EV_A0 + CONV_WIDTH:])
    y = lax.conv_general_dilated(a, conv_w[:, None, :], window_strides=(1,),
                                 padding=[(CONV_WIDTH_TAPS // 2, CONV_WIDTH_TAPS // 2)],
                                 dimension_numbers=('NWC', 'WIO', 'NWC'),
                                 feature_group_count=CONV_WIDTH)
    return jax.nn.silu(layer_norm(y + conv_b, ln_g, ln_b))


def chunk_gmlp(u, v, ln_g, ln_b, ws, bs):
    b, length, _ = u.shape
    u = jax.nn.gelu(u)
    v = layer_norm(jax.nn.gelu(v), ln_g, ln_b)
    vc = v.reshape(b, length // CHUNK, CHUNK, GMLP_GROUPS, GMLP_GROUP_DIM)
    sv = jnp.einsum('gpq,bnqgc->bnpgc', ws, vc) + bs.T[:, :, None]
    return u * sv.reshape(b, length, GMLP_W)


def fourier_mix(f, ln_g, ln_b):
    b, length, _ = f.shape
    fg = f.reshape(b, length, FOURIER_GROUPS, FOURIER_GROUP_DIM)
    fg = layer_norm(fg, ln_g.reshape(FOURIER_GROUPS, FOURIER_GROUP_DIM), ln_b.reshape(FOURIER_GROUPS, FOURIER_GROUP_DIM))
    spec = jnp.fft.fft2(fg.astype(jnp.float32), axes=(1, 3), norm='ortho')
    return jnp.real(spec).astype(f.dtype).reshape(b, length, FOURIER_W)


def odd_mixer(z, gln_g, gln_b, ws, bs, fln_g, fln_b):
    spatial = chunk_gmlp(z[..., :GMLP_W], z[..., GMLP_W:2 * GMLP_W], gln_g, gln_b, ws, bs)
    spectral = fourier_mix(z[..., 2 * GMLP_W:], fln_g, fln_b)
    return jnp.concatenate([spatial, spectral], -1)


def expert_ffn(xb, wgu, bgu, wd, bd):
    gu = xb @ wgu + bgu
    x_glu, x_lin = jnp.split(gu, 2, axis=-1)
    x_glu = jnp.minimum(x_glu, SWIGLU_LIMIT)
    x_lin = jnp.clip(x_lin, -SWIGLU_LIMIT, SWIGLU_LIMIT)
    act = x_glu * jax.nn.sigmoid(SWIGLU_ALPHA * x_glu) * (x_lin + 1.0)
    return act @ wd + bd


def moe_channel_mixer(h, router_w, router_b, wgu, bgu, wd, bd):
    n_tok, d = h.shape
    logits = (h @ router_w + router_b).astype(jnp.float32)
    top_logit, top_e = lax.top_k(logits, TOP_K)
    gate = jax.nn.softmax(top_logit, axis=-1)
    n_asg = n_tok * TOP_K
    flat_e = top_e.reshape(-1)
    order = jnp.argsort(flat_e)
    sorted_e = flat_e[order]
    counts = jnp.bincount(flat_e, length=N_EXPERTS)
    padded = (counts + EXPERT_BLOCK - 1) // EXPERT_BLOCK * EXPERT_BLOCK
    pad_end = jnp.cumsum(padded)
    start = jnp.cumsum(counts) - counts
    dest = (pad_end - padded)[sorted_e] + jnp.arange(n_asg) - start[sorted_e]
    n_blk = -(-n_asg // EXPERT_BLOCK) + N_EXPERTS
    n_rows = n_blk * EXPERT_BLOCK
    row_tok = jnp.full((n_rows,), n_tok, jnp.int32).at[dest].set((order // TOP_K).astype(jnp.int32))
    row_gate = jnp.zeros((n_rows,), h.dtype).at[dest].set(gate.reshape(-1)[order].astype(h.dtype))
    blk_e = jnp.minimum(jnp.searchsorted(pad_end, jnp.arange(n_blk) * EXPERT_BLOCK, side='right'), N_EXPERTS - 1)
    h_pad = jnp.concatenate([h, jnp.zeros((1, d), h.dtype)], 0)
    xb = h_pad[row_tok].reshape(n_blk, EXPERT_BLOCK, d)
    yb = lax.map(lambda a: expert_ffn(a[0], wgu[a[1]], bgu[a[1]], wd[a[1]], bd[a[1]]), (xb, blk_e))
    y = yb.reshape(n_rows, d) * row_gate[:, None]
    return jnp.zeros((n_tok + 1, d), h.dtype).at[row_tok].add(y)[:n_tok]


def setup_inputs(seed: int = 0) -> dict:
    key = jax.random.key(seed)
    ks = iter(jax.random.split(key, 64))

    def nrm(shape, scale):
        return jax.random.normal(next(ks), shape, jnp.float32) * scale

    def gain(shape):
        return 1.0 + nrm(shape, 0.02)

    d, ne, no = D_MODEL, N_EVEN, N_ODD
    return {
        'x': nrm((BATCH, SEQ, d), 1.0),
        'c': nrm((BATCH, d), 1.0),
        'ctx': nrm((BATCH, CTX_LEN, d), 1.0),
        'c_ctx': nrm((d,), 1.0),
        'w_mod': nrm((DEPTH, d, 6 * d), 0.5 * d ** -0.5),
        'b_mod': nrm((DEPTH, 6 * d), 0.02),
        'ln1_g': gain((DEPTH, d)),
        'ln1_b': nrm((DEPTH, d), 0.02),
        'ln2_g': gain((DEPTH, d)),
        'ln2_b': nrm((DEPTH, d), 0.02),
        'ev_w_in': nrm((ne, d, EVEN_IN), d ** -0.5),
        'ev_w_out': nrm((ne, EVEN_OUT, d), DEEPNORM_BETA * EVEN_OUT ** -0.5),
        'conv_w': nrm((ne, CONV_WIDTH_TAPS, CONV_WIDTH), CONV_WIDTH_TAPS ** -0.5),
        'conv_b': nrm((ne, CONV_WIDTH), 0.02),
        'conv_ln_g': gain((ne, CONV_WIDTH)),
        'conv_ln_b': nrm((ne, CONV_WIDTH), 0.02),
        'lam_q1': nrm((ne, DIFF_HEAD_DIM), 0.1),
        'lam_k1': nrm((ne, DIFF_HEAD_DIM), 0.1),
        'lam_q2': nrm((ne, DIFF_HEAD_DIM), 0.1),
        'lam_k2': nrm((ne, DIFF_HEAD_DIM), 0.1),
        'diff_norm_g': gain((ne, DIFF_V_DIM)),
        'od_w_in': nrm((no, d, ODD_IN), d ** -0.5),
        'od_w_out': nrm((no, ODD_OUT, d), DEEPNORM_BETA * ODD_OUT ** -0.5),
        'gmlp_ln_g': gain((no, GMLP_W)),
        'gmlp_ln_b': nrm((no, GMLP_W), 0.02),
        'gmlp_ws': nrm((no, GMLP_GROUPS, CHUNK, CHUNK), CHUNK ** -0.5),
        'gmlp_bs': 1.0 + nrm((no, GMLP_GROUPS, CHUNK), 0.1),
        'four_ln_g': gain((no, FOURIER_W)),
        'four_ln_b': nrm((no, FOURIER_W), 0.02),
        'router_w': nrm((DEPTH, d, N_EXPERTS), d ** -0.5),
        'router_b': nrm((DEPTH, N_EXPERTS), 0.01),
        'w_gate_up': nrm((DEPTH, N_EXPERTS, d, 2 * D_EXPERT), d ** -0.5),
        'b_gate_up': nrm((DEPTH, N_EXPERTS, 2 * D_EXPERT), 0.02),
        'w_down': nrm((DEPTH, N_EXPERTS, D_EXPERT, d), DEEPNORM_BETA * D_EXPERT ** -0.5),
        'b_down': nrm((DEPTH, N_EXPERTS, d), 0.02),
    }


def reference(x, c, ctx, c_ctx, w_mod, b_mod, ln1_g, ln1_b, ln2_g, ln2_b,
              ev_w_in, ev_w_out, conv_w, conv_b, conv_ln_g, conv_ln_b,
              lam_q1, lam_k1, lam_q2, lam_k2, diff_norm_g,
              od_w_in, od_w_out, gmlp_ln_g, gmlp_ln_b, gmlp_ws, gmlp_bs,
              four_ln_g, four_ln_b,
              router_w, router_b, w_gate_up, b_gate_up, w_down, b_down):
    b, length, d = x.shape
    n_lat = b * length
    cos, sin = axial_rope_tables(length)
    h, hc = x, ctx
    for layer in range(DEPTH):
        even = layer % 2 == 0
        j = layer // 2
        ctx_out = any(l2 % 2 == 0 for l2 in range(layer + 1, DEPTH))
        mod = jax.nn.silu(c) @ w_mod[layer] + b_mod[layer]
        sh1, sc1, g1, sh2, sc2, g2 = jnp.split(mod[:, None, :], 6, axis=-1)
        if even or ctx_out:
            csh1, csc1, cg1, csh2, csc2, cg2 = jnp.split(jax.nn.silu(c_ctx) @ w_mod[layer] + b_mod[layer], 6)
            uc = modulate(hc, csh1, csc1)
        u = modulate(h, sh1, sc1)
        if even:
            w_in = ev_w_in[j]
            lam_init = 0.8 - 0.6 * math.exp(-0.3 * layer)
            lam = diff_lambda(lam_q1[j], lam_k1[j], lam_q2[j], lam_k2[j], lam_init)
            z = u @ w_in
            zc = uc @ (w_in if ctx_out else w_in[:, QK_W:EV_A0])
            zc_kv = zc[..., QK_W:EV_A0] if ctx_out else zc
            k_c = heads_qk(zc_kv[..., :QK_W])
            v_c = heads_v(zc_kv[..., QK_W:])
            q = apply_axial_rope(heads_qk(z[..., :QK_W]), cos, sin)
            k = apply_axial_rope(heads_qk(z[..., QK_W:2 * QK_W]), cos, sin)
            v = heads_v(z[..., 2 * QK_W:EV_A0])
            att = diff_attend_blocks(q, jnp.concatenate([k, k_c], 1), jnp.concatenate([v, v_c], 1), lam)
            conv = conformer_conv(z, conv_w[j], conv_b[j], conv_ln_g[j], conv_ln_b[j])
            y = jnp.concatenate([conv, diff_heads_out(att, diff_norm_g[j], lam_init)], -1) @ ev_w_out[j]
            if ctx_out:
                att_c = diff_attend(heads_qk(zc[..., :QK_W]), k_c, v_c, lam)
                conv_c = conformer_conv(zc, conv_w[j], conv_b[j], conv_ln_g[j], conv_ln_b[j])
                yc = jnp.concatenate([conv_c, diff_heads_out(att_c, diff_norm_g[j], lam_init)], -1) @ ev_w_out[j]
        else:
            y = odd_mixer(u @ od_w_in[j], gmlp_ln_g[j], gmlp_ln_b[j], gmlp_ws[j], gmlp_bs[j],
                          four_ln_g[j], four_ln_b[j]) @ od_w_out[j]
            if ctx_out:
                yc = odd_mixer(uc @ od_w_in[j], gmlp_ln_g[j], gmlp_ln_b[j], gmlp_ws[j], gmlp_bs[j],
                               four_ln_g[j], four_ln_b[j]) @ od_w_out[j]
        h = layer_norm(DEEPNORM_ALPHA * h + g1 * y, ln1_g[layer], ln1_b[layer])
        u2 = modulate(h, sh2, sc2).reshape(n_lat, d)
        if ctx_out:
            hc = layer_norm(DEEPNORM_ALPHA * hc + cg1 * yc, ln1_g[layer], ln1_b[layer])
            u2 = jnp.concatenate([u2, modulate(hc, csh2, csc2).reshape(-1, d)], 0)
        y2 = moe_channel_mixer(u2, router_w[layer], router_b[layer], w_gate_up[layer], b_gate_up[layer],
                               w_down[layer], b_down[layer])
        h = layer_norm(DEEPNORM_ALPHA * h + g2 * y2[:n_lat].reshape(b, length, d), ln2_g[layer], ln2_b[layer])
        if ctx_out:
            hc = layer_norm(DEEPNORM_ALPHA * hc + cg2 * y2[n_lat:].reshape(hc.shape), ln2_g[layer], ln2_b[layer])
    return h
```

```python
import functools
import math

import jax
import jax.numpy as jnp
from jax import lax
from jax.experimental import pallas as pl
from jax.experimental.pallas import tpu as pltpu

F32 = jnp.float32
BF16 = jnp.bfloat16
I32 = jnp.int32
HIGHEST = lax.Precision.HIGHEST

LN_EPS = 1e-5
GRID_W = 64
HEAD_DIM = 64
HEAD_V = 128
N_HEADS = 4
QK_W = N_HEADS * 2 * HEAD_DIM
V_W = N_HEADS * HEAD_V
CONV_W = 512
CONV_TAPS = 31
CONV_HALO = 16
ROPE_BASE = 10000.0
CHUNK = 128
N_GROUPS = 4
GROUP_W = 128
N_EXPERTS = 32
TOP_K = 4
SWIGLU_LIMIT = 7.0
SWIGLU_ALPHA = 1.702
LANES = 128
SUBLANES = 8
EXPERT_ROWS = 256
NEG_BIG = -1e30
VMEM_LIMIT = 56 * 1024 * 1024


def _params(n_axes):
    return pltpu.CompilerParams(dimension_semantics=("arbitrary",) * n_axes,
                                vmem_limit_bytes=VMEM_LIMIT)


def _pick(n, candidates):
    for c in candidates:
        if n % c == 0:
            return c
    return n


def _layer_norm(r, g, b):
    mu = jnp.mean(r, axis=-1, keepdims=True)
    d = r - mu
    var = jnp.mean(d * d, axis=-1, keepdims=True)
    return d * lax.rsqrt(var + LN_EPS) * g + b


def _sigmoid(x):
    return 1.0 / (1.0 + jnp.exp(-x))


def _mod_kernel(cs_ref, w_ref, b_ref, o_ref):
    cs = cs_ref[...]
    a = cs * _sigmoid(cs)
    o_ref[0] = jnp.dot(a, w_ref[0], precision=HIGHEST, preferred_element_type=F32) + b_ref[0]


def _modulation(c, c_ctx, w_mod, b_mod):
    depth, d, n = w_mod.shape
    b = c.shape[0]
    cs = jnp.concatenate([c, c_ctx[None], jnp.zeros((SUBLANES - b - 1, d), F32)], 0)
    tn = _pick(n, (1536, 1024, 512))
    return pl.pallas_call(
        _mod_kernel,
        grid=(depth, n // tn),
        in_specs=[pl.BlockSpec((SUBLANES, d), lambda l, j: (0, 0)),
                  pl.BlockSpec((1, d, tn), lambda l, j: (l, 0, j)),
                  pl.BlockSpec((1, 1, tn), lambda l, j: (l, 0, j))],
        out_specs=pl.BlockSpec((1, SUBLANES, tn), lambda l, j: (l, 0, j)),
        out_shape=jax.ShapeDtypeStruct((depth, SUBLANES, n), F32),
        compiler_params=_params(2),
        name="modulation",
    )(cs, w_mod, b_mod.reshape(depth, 1, n))


def _even_in_kernel(x_ref, sc_ref, sh_ref, w_ref, cos_ref, sin_ref, q_ref, k_ref, v_ref, a_ref):
    u = (x_ref[0] * (1.0 + sc_ref[0]) + sh_ref[0]).astype(BF16)
    z = jnp.dot(u, w_ref[...], preferred_element_type=F32)
    cosv = cos_ref[...]
    sinv = sin_ref[...]
    lane = lax.broadcasted_iota(I32, cosv.shape, 1)
    first = (lane % 32) < 16

    def rope(t):
        partner = jnp.where(first, pltpu.roll(t, LANES - 16, 1), pltpu.roll(t, 16, 1))
        return t * cosv + partner * sinv

    for j in range(QK_W // LANES):
        sl = slice(j * LANES, (j + 1) * LANES)
        q_ref[0, :, sl] = (rope(z[:, sl]) * (HEAD_DIM ** -0.5)).astype(BF16)
        k_ref[0, :, sl] = rope(z[:, QK_W + j * LANES:QK_W + (j + 1) * LANES]).astype(BF16)
    v_ref[0] = z[:, 2 * QK_W:2 * QK_W + V_W].astype(BF16)
    a0 = 2 * QK_W + V_W
    a_ref[0] = z[:, a0:a0 + CONV_W] * _sigmoid(z[:, a0 + CONV_W:a0 + 2 * CONV_W])


def _rope_tables(length):
    rows = length // GRID_W
    row = jnp.repeat(jnp.arange(rows, dtype=F32), GRID_W)
    col = jnp.tile(jnp.arange(GRID_W, dtype=F32), rows)
    n_freq = HEAD_DIM // 4
    inv_freq = ROPE_BASE ** (-jnp.arange(n_freq, dtype=F32) / n_freq)
    ar = row[:, None] * inv_freq
    ac = col[:, None] * inv_freq
    cos64 = jnp.concatenate([jnp.cos(ar), jnp.cos(ar), jnp.cos(ac), jnp.cos(ac)], 1)
    sin64 = jnp.concatenate([-jnp.sin(ar), jnp.sin(ar), -jnp.sin(ac), jnp.sin(ac)], 1)
    return jnp.tile(cos64, (1, LANES // HEAD_DIM)), jnp.tile(sin64, (1, LANES // HEAD_DIM))


def _even_in_proj(x, sc, sh, w_bf, cos_t, sin_t):
    b, length, d = x.shape
    n = w_bf.shape[1]
    tm = _pick(length, (512, 256, 128))
    row = lambda bi, i: (bi, i, 0)
    vec = lambda bi, i: (bi, 0, 0)
    return pl.pallas_call(
        _even_in_kernel,
        grid=(b, length // tm),
        in_specs=[pl.BlockSpec((1, tm, d), row),
                  pl.BlockSpec((1, 1, d), vec),
                  pl.BlockSpec((1, 1, d), vec),
                  pl.BlockSpec((d, n), lambda bi, i: (0, 0)),
                  pl.BlockSpec((tm, LANES), lambda bi, i: (i, 0)),
                  pl.BlockSpec((tm, LANES), lambda bi, i: (i, 0))],
        out_specs=[pl.BlockSpec((1, tm, QK_W), row), pl.BlockSpec((1, tm, QK_W), row),
                   pl.BlockSpec((1, tm, V_W), row), pl.BlockSpec((1, tm, CONV_W), row)],
        out_shape=[jax.ShapeDtypeStruct((b, length, QK_W), BF16), jax.ShapeDtypeStruct((b, length, QK_W), BF16),
                   jax.ShapeDtypeStruct((b, length, V_W), BF16), jax.ShapeDtypeStruct((b, length, CONV_W), F32)],
        compiler_params=_params(2),
        name="even_in_proj",
    )(x, sc, sh, w_bf, cos_t, sin_t)


def _modmm_kernel(x_ref, sc_ref, sh_ref, w_ref, o_ref):
    u = (x_ref[0] * (1.0 + sc_ref[0]) + sh_ref[0]).astype(BF16)
    o_ref[0] = jnp.dot(u, w_ref[...], preferred_element_type=F32).astype(o_ref.dtype)


def _mod_matmul(x, sc, sh, w_bf, out_dtype):
    b, rows, d = x.shape
    n = w_bf.shape[1]
    tm = _pick(rows, (512, 256, 128))
    return pl.pallas_call(
        _modmm_kernel,
        grid=(b, rows // tm),
        in_specs=[pl.BlockSpec((1, tm, d), lambda bi, i: (bi, i, 0)),
                  pl.BlockSpec((1, 1, d), lambda bi, i: (bi, 0, 0)),
                  pl.BlockSpec((1, 1, d), lambda bi, i: (bi, 0, 0)),
                  pl.BlockSpec((d, n), lambda bi, i: (0, 0))],
        out_specs=pl.BlockSpec((1, tm, n), lambda bi, i: (bi, i, 0)),
        out_shape=jax.ShapeDtypeStruct((b, rows, n), out_dtype),
        compiler_params=_params(2),
        name="context_kv_proj",
    )(x, sc, sh, w_bf)


def _attn_kernel(lq1_ref, lk1_ref, lq2_ref, lk2_ref, g_ref, q_ref, k_ref, v_ref, o_ref,
                 m_sc, l_sc, acc_sc, *, lam_init):
    kk = pl.program_id(3)

    @pl.when(kk == 0)
    def _():
        m_sc[...] = jnp.full(m_sc.shape, NEG_BIG, F32)
        l_sc[...] = jnp.zeros(l_sc.shape, F32)
        acc_sc[...] = jnp.zeros(acc_sc.shape, F32)

    q = q_ref[0]
    k = k_ref[0]
    v = v_ref[0]
    comp = lax.broadcasted_iota(I32, q.shape, 1) // HEAD_DIM
    for c in range(2):
        qc = jnp.where(comp == c, q, jnp.zeros_like(q))
        s = lax.dot_general(qc, k, (((1,), (1,)), ((), ())), preferred_element_type=F32)
        m_prev = m_sc[c]
        m_new = jnp.maximum(m_prev, jnp.max(s, axis=1, keepdims=True))
        alpha = jnp.exp(m_prev - m_new)
        p = jnp.exp(s - m_new)
        l_sc[c] = alpha * l_sc[c] + jnp.sum(p, axis=1, keepdims=True)
        acc_sc[c] = alpha * acc_sc[c] + jnp.dot(p.astype(BF16), v, preferred_element_type=F32)
        m_sc[c] = m_new

    @pl.when(kk == pl.num_programs(3) - 1)
    def _():
        lam = (jnp.exp(jnp.sum(lq1_ref[...] * lk1_ref[...], keepdims=True))
               - jnp.exp(jnp.sum(lq2_ref[...] * lk2_ref[...], keepdims=True)) + lam_init)
        o = acc_sc[0] / l_sc[0] - lam * (acc_sc[1] / l_sc[1])
        ms = jnp.mean(o * o, axis=1, keepdims=True)
        o_ref[0] = (o * lax.rsqrt(ms + LN_EPS) * g_ref[...] * (1.0 - lam_init)).astype(o_ref.dtype)


def _diff_attention(q, k_all, v_all, lq1, lk1, lq2, lk2, norm_g, lam_init):
    b, length, _ = q.shape
    n_keys = k_all.shape[1]
    mq = _pick(length, (512, 256, 128))
    kt = _pick(n_keys, (1280, 640, 256, 128))
    small = lambda bi, h, i, kk: (0, 0)
    return pl.pallas_call(
        functools.partial(_attn_kernel, lam_init=lam_init),
        grid=(b, N_HEADS, length // mq, n_keys // kt),
        in_specs=[pl.BlockSpec((1, HEAD_DIM), small), pl.BlockSpec((1, HEAD_DIM), small),
                  pl.BlockSpec((1, HEAD_DIM), small), pl.BlockSpec((1, HEAD_DIM), small),
                  pl.BlockSpec((1, HEAD_V), small),
                  pl.BlockSpec((1, mq, HEAD_V), lambda bi, h, i, kk: (bi, i, h)),
                  pl.BlockSpec((1, kt, HEAD_V), lambda bi, h, i, kk: (bi, kk, h)),
                  pl.BlockSpec((1, kt, HEAD_V), lambda bi, h, i, kk: (bi, kk, h))],
        out_specs=pl.BlockSpec((1, mq, HEAD_V), lambda bi, h, i, kk: (bi, i, h)),
        out_shape=jax.ShapeDtypeStruct((b, length, V_W), BF16),
        scratch_shapes=[pltpu.VMEM((2, mq, 1), F32), pltpu.VMEM((2, mq, 1), F32),
                        pltpu.VMEM((2, mq, HEAD_V), F32)],
        compiler_params=_params(4),
        name="diff_attention",
    )(lq1[None], lk1[None], lq2[None], lk2[None], norm_g[None], q, k_all, v_all)


def _conv_kernel(prev_ref, cur_ref, next_ref, w_ref, b_ref, g_ref, beta_ref, o_ref, win, *, row_chunk):
    i = pl.program_id(1)
    tc = cur_ref.shape[1]
    win[CONV_HALO:CONV_HALO + tc, :] = cur_ref[0]
    win[0:CONV_HALO, :] = jnp.where(i > 0, prev_ref[0], 0.0)
    win[CONV_HALO + tc:, :] = jnp.where(i < pl.num_programs(1) - 1, next_ref[0], 0.0)
    first_tap = CONV_HALO - CONV_TAPS // 2
    for c0 in range(0, tc, row_chunk):
        acc = jnp.zeros((row_chunk, CONV_W), F32)
        for t in range(CONV_TAPS):
            acc = acc + w_ref[t:t + 1, :] * win[c0 + first_tap + t:c0 + first_tap + t + row_chunk, :]
        y = _layer_norm(acc + b_ref[...], g_ref[...], beta_ref[...])
        o_ref[0, c0:c0 + row_chunk, :] = (y * _sigmoid(y)).astype(o_ref.dtype)


def _conformer_conv(a, conv_w, conv_b, ln_g, ln_b):
    b, length, w = a.shape
    tc = _pick(length, (512, 256, 128))
    hb = tc // CONV_HALO
    n_halo = length // CONV_HALO
    w_pad = jnp.concatenate([conv_w, jnp.zeros((1, w), F32)], 0)
    vec = lambda bi, i: (0, 0)
    return pl.pallas_call(
        functools.partial(_conv_kernel, row_chunk=64),
        grid=(b, length // tc),
        in_specs=[pl.BlockSpec((1, CONV_HALO, w), lambda bi, i: (bi, jnp.maximum(i * hb - 1, 0), 0)),
                  pl.BlockSpec((1, tc, w), lambda bi, i: (bi, i, 0)),
                  pl.BlockSpec((1, CONV_HALO, w), lambda bi, i: (bi, jnp.minimum((i + 1) * hb, n_halo - 1), 0)),
                  pl.BlockSpec((CONV_TAPS + 1, w), vec),
                  pl.BlockSpec((1, w), vec), pl.BlockSpec((1, w), vec), pl.BlockSpec((1, w), vec)],
        out_specs=pl.BlockSpec((1, tc, w), lambda bi, i: (bi, i, 0)),
        out_shape=jax.ShapeDtypeStruct((b, length, w), BF16),
        scratch_shapes=[pltpu.VMEM((tc + 2 * CONV_HALO, w), F32)],
        compiler_params=_params(2),
        name="conformer_conv",
    )(a, a, a, w_pad, conv_b[None], ln_g[None], ln_b[None])


def _post_mix_kernel(ya_ref, yb_ref, wa_ref, wb_ref, x_ref, g1_ref, lng_ref, lnb_ref, sc_ref, sh_ref,
                     rw_ref, rb_ref, h_ref, e_ref, gate_ref, pos_ref, cnt_ref, carry, *, alpha):
    first_step = jnp.logical_and(pl.program_id(0) == 0, pl.program_id(1) == 0)

    @pl.when(first_step)
    def _():
        carry[...] = jnp.zeros(carry.shape, F32)

    y = (jnp.dot(ya_ref[0].astype(BF16), wa_ref[...], preferred_element_type=F32)
         + jnp.dot(yb_ref[0].astype(BF16), wb_ref[...], preferred_element_type=F32))
    h = _layer_norm(alpha * x_ref[0] + g1_ref[0] * y, lng_ref[...], lnb_ref[...])
    h_ref[0] = h
    u2 = h * (1.0 + sc_ref[0]) + sh_ref[0]
    logits = jnp.dot(u2, rw_ref[...], precision=HIGHEST, preferred_element_type=F32) + rb_ref[...]

    tm = logits.shape[0]
    lane = lax.broadcasted_iota(I32, logits.shape, 1).astype(F32)
    vals, idxs = [], []
    lg = logits
    for _ in range(TOP_K):
        mx = jnp.max(lg, axis=1, keepdims=True)
        idx = jnp.min(jnp.where(lg == mx, lane, float(LANES)), axis=1, keepdims=True)
        vals.append(mx)
        idxs.append(idx)
        lg = jnp.where(lane == idx, -3e38, lg)
    exps = [jnp.exp(v - vals[0]) for v in vals]
    den = exps[0] + exps[1] + exps[2] + exps[3]

    hot = jnp.zeros(logits.shape, F32)
    for idx in idxs:
        hot = hot + (lane == idx).astype(F32)
    r_i = lax.broadcasted_iota(I32, (tm, tm), 0)
    c_i = lax.broadcasted_iota(I32, (tm, tm), 1)
    tri = (c_i < r_i).astype(BF16)
    before = jnp.dot(tri, hot.astype(BF16), preferred_element_type=F32) + carry[...]
    e_out = jnp.zeros(logits.shape, F32)
    g_out = jnp.zeros(logits.shape, F32)
    p_out = jnp.zeros(logits.shape, F32)
    for j in range(TOP_K):
        pos_j = jnp.sum(jnp.where(lane == idxs[j], before, 0.0), axis=1, keepdims=True)
        sel = lane == float(j)
        e_out = jnp.where(sel, idxs[j], e_out)
        g_out = jnp.where(sel, exps[j] / den, g_out)
        p_out = jnp.where(sel, pos_j, p_out)
    e_ref[...] = e_out.astype(I32)
    gate_ref[...] = g_out
    pos_ref[...] = p_out.astype(I32)
    new_carry = carry[...] + jnp.sum(hot, axis=0, keepdims=True)
    carry[...] = new_carry
    cnt_ref[...] = jnp.broadcast_to(new_carry, cnt_ref.shape).astype(I32)


def _post_mix(ya, yb, w_out_bf, x, g1, ln_g, ln_b, sc2, sh2, router_w, router_b, alpha):
    b, length, d = x.shape
    wa = ya.shape[2]
    tm = _pick(length, (512, 256, 128))
    n_tok = b * length
    rw = jnp.concatenate([router_w, jnp.zeros((d, LANES - N_EXPERTS), F32)], 1)
    rb = jnp.concatenate([router_b, jnp.full((LANES - N_EXPERTS,), NEG_BIG, F32)])[None]
    row = lambda bi, i: (bi, i, 0)
    vec = lambda bi, i: (bi, 0, 0)
    const = lambda bi, i: (0, 0)
    nt = length // tm
    tok = lambda bi, i: (bi * nt + i, 0)
    return pl.pallas_call(
        functools.partial(_post_mix_kernel, alpha=alpha),
        grid=(b, nt),
        in_specs=[pl.BlockSpec((1, tm, wa), row), pl.BlockSpec((1, tm, d - wa), row),
                  pl.BlockSpec((wa, d), const), pl.BlockSpec((d - wa, d), const),
                  pl.BlockSpec((1, tm, d), row), pl.BlockSpec((1, 1, d), vec),
                  pl.BlockSpec((1, d), const), pl.BlockSpec((1, d), const),
                  pl.BlockSpec((1, 1, d), vec), pl.BlockSpec((1, 1, d), vec),
                  pl.BlockSpec((d, LANES), const), pl.BlockSpec((1, LANES), const)],
        out_specs=[pl.BlockSpec((1, tm, d), row), pl.BlockSpec((tm, LANES), tok),
                   pl.BlockSpec((tm, LANES), tok), pl.BlockSpec((tm, LANES), tok),
                   pl.BlockSpec((SUBLANES, LANES), const)],
        out_shape=[jax.ShapeDtypeStruct((b, length, d), F32), jax.ShapeDtypeStruct((n_tok, LANES), I32),
                   jax.ShapeDtypeStruct((n_tok, LANES), F32), jax.ShapeDtypeStruct((n_tok, LANES), I32),
                   jax.ShapeDtypeStruct((SUBLANES, LANES), I32)],
        scratch_shapes=[pltpu.VMEM((1, LANES), F32)],
        compiler_params=_params(2),
        name="out_proj_norm_router",
    )(ya, yb, w_out_bf[:wa], w_out_bf[wa:], x, g1, ln_g[None], ln_b[None], sc2, sh2, rw, rb)


def _dispatch_kernel(zstart_ref, zvalid_ref, dest_ref, h_ref, sc_ref, sh_ref, xs_ref, u_sc, zero_sc, sem, zsem):
    tm = h_ref.shape[0]

    @pl.when(pl.program_id(0) == 0)
    def _():
        zero_sc[...] = jnp.zeros(zero_sc.shape, F32)

        def fill(e, c):
            @pl.when(zvalid_ref[e] > 0)
            def _():
                start = pl.multiple_of(zstart_ref[e], SUBLANES)
                pltpu.make_async_copy(zero_sc, xs_ref.at[pl.ds(start, EXPERT_ROWS)], zsem).start()
            return c

        lax.fori_loop(0, N_EXPERTS, fill, 0)

        def drain(e, c):
            @pl.when(zvalid_ref[e] > 0)
            def _():
                pltpu.make_async_copy(zero_sc, xs_ref.at[pl.ds(0, EXPERT_ROWS)], zsem).wait()
            return c

        lax.fori_loop(0, N_EXPERTS, drain, 0)

    u_sc[...] = h_ref[...] * (1.0 + sc_ref[0]) + sh_ref[0]

    def issue(r, c):
        for j in range(TOP_K):
            d = dest_ref[0, 0, r * TOP_K + j]
            pltpu.make_async_copy(u_sc.at[pl.ds(r, 1)], xs_ref.at[pl.ds(d, 1)], sem).start()
        return c

    lax.fori_loop(0, tm, issue, 0)

    def drain_rows(r, c):
        pltpu.make_async_copy(u_sc.at[pl.ds(0, 1)], xs_ref.at[pl.ds(0, 1)], sem).wait()
        return c

    lax.fori_loop(0, tm * TOP_K, drain_rows, 0)


def _dispatch(h_flat, sc2, sh2, dest, zstart, zvalid, n_rows, tokens_per_batch):
    n_tok, d = h_flat.shape
    tm = _pick(tokens_per_batch, (512, 256, 128))
    per_b = tokens_per_batch // tm
    return pl.pallas_call(
        _dispatch_kernel,
        grid_spec=pltpu.PrefetchScalarGridSpec(
            num_scalar_prefetch=2,
            grid=(n_tok // tm,),
            in_specs=[pl.BlockSpec((1, 1, tm * TOP_K), lambda i, zs, zv: (i, 0, 0), memory_space=pltpu.SMEM),
                      pl.BlockSpec((tm, d), lambda i, zs, zv: (i, 0)),
                      pl.BlockSpec((1, 1, d), lambda i, zs, zv: (i // per_b, 0, 0)),
                      pl.BlockSpec((1, 1, d), lambda i, zs, zv: (i // per_b, 0, 0))],
            out_specs=pl.BlockSpec(memory_space=pl.ANY),
            scratch_shapes=[pltpu.VMEM((tm, d), F32), pltpu.VMEM((EXPERT_ROWS, d), F32),
                            pltpu.SemaphoreType.DMA(()), pltpu.SemaphoreType.DMA(())]),
        out_shape=jax.ShapeDtypeStruct((n_rows, d), F32),
        compiler_params=_params(1),
        name="moe_dispatch",
    )(zstart, zvalid, dest.reshape(n_tok // tm, 1, tm * TOP_K), h_flat, sc2, sh2)


def _expert_kernel(blk_e_ref, n_used_ref, x_ref, wgu_ref, bgu_ref, wd_ref, bd_ref, o_ref):
    @pl.when(pl.program_id(0) < n_used_ref[0])
    def _():
        d_e = wd_ref.shape[1]
        gu = jnp.dot(x_ref[...].astype(BF16), wgu_ref[0], preferred_element_type=F32) + bgu_ref[0]
        x_glu = jnp.minimum(gu[:, :d_e], SWIGLU_LIMIT)
        x_lin = jnp.clip(gu[:, d_e:], -SWIGLU_LIMIT, SWIGLU_LIMIT)
        act = x_glu * _sigmoid(SWIGLU_ALPHA * x_glu) * (x_lin + 1.0)
        o_ref[...] = jnp.dot(act.astype(BF16), wd_ref[0], preferred_element_type=F32) + bd_ref[0]


def _experts(xs, blk_e, n_used, wgu_bf, bgu, wd_bf, bd):
    n_rows, d = xs.shape
    n_e, _, n_gu = wgu_bf.shape
    n_blk = n_rows // EXPERT_ROWS
    rows = lambda i, be, nu: (jnp.minimum(i, nu[0] - 1), 0)
    by_e = lambda i, be, nu: (be[i], 0, 0)
    return pl.pallas_call(
        _expert_kernel,
        grid_spec=pltpu.PrefetchScalarGridSpec(
            num_scalar_prefetch=2,
            grid=(n_blk,),
            in_specs=[pl.BlockSpec((EXPERT_ROWS, d), rows),
                      pl.BlockSpec((1, d, n_gu), by_e), pl.BlockSpec((1, 1, n_gu), by_e),
                      pl.BlockSpec((1, n_gu // 2, d), by_e), pl.BlockSpec((1, 1, d), by_e)],
            out_specs=pl.BlockSpec((EXPERT_ROWS, d), rows)),
        out_shape=jax.ShapeDtypeStruct((n_rows, d), F32),
        compiler_params=_params(1),
        name="moe_experts",
    )(blk_e, n_used, xs, wgu_bf, bgu.reshape(n_e, 1, n_gu), wd_bf, bd.reshape(n_e, 1, d))


def _combine_kernel(dest_ref, y_ref, h_ref, gate_ref, g2_ref, lng_ref, lnb_ref, o_ref, buf, sem, *, alpha):
    tm = h_ref.shape[0]

    def issue(r, c):
        for j in range(TOP_K):
            d = dest_ref[0, 0, r * TOP_K + j]
            pltpu.make_async_copy(y_ref.at[pl.ds(d, 1)], buf.at[j, pl.ds(r, 1)], sem).start()
        return c

    lax.fori_loop(0, tm, issue, 0)

    def drain(r, c):
        pltpu.make_async_copy(y_ref.at[pl.ds(0, 1)], buf.at[0, pl.ds(0, 1)], sem).wait()
        return c

    lax.fori_loop(0, tm * TOP_K, drain, 0)

    gate = gate_ref[...]
    y2 = gate[:, 0:1] * buf[0]
    for j in range(1, TOP_K):
        y2 = y2 + gate[:, j:j + 1] * buf[j]
    o_ref[...] = _layer_norm(alpha * h_ref[...] + g2_ref[0] * y2, lng_ref[...], lnb_ref[...])


def _combine(y_rows, dest, h_flat, gate, g2, ln_g, ln_b, alpha, tokens_per_batch):
    n_tok, d = h_flat.shape
    tm = _pick(tokens_per_batch, (256, 128))
    per_b = tokens_per_batch // tm
    return pl.pallas_call(
        functools.partial(_combine_kernel, alpha=alpha),
        grid=(n_tok // tm,),
        in_specs=[pl.BlockSpec((1, 1, tm * TOP_K), lambda i: (i, 0, 0), memory_space=pltpu.SMEM),
                  pl.BlockSpec(memory_space=pl.ANY),
                  pl.BlockSpec((tm, d), lambda i: (i, 0)),
                  pl.BlockSpec((tm, LANES), lambda i: (i, 0)),
                  pl.BlockSpec((1, 1, d), lambda i: (i // per_b, 0, 0)),
                  pl.BlockSpec((1, d), lambda i: (0, 0)), pl.BlockSpec((1, d), lambda i: (0, 0))],
        out_specs=pl.BlockSpec((tm, d), lambda i: (i, 0)),
        out_shape=jax.ShapeDtypeStruct((n_tok, d), F32),
        scratch_shapes=[pltpu.VMEM((TOP_K, tm, d), F32), pltpu.SemaphoreType.DMA(())],
        compiler_params=_params(1),
        name="moe_combine_norm",
    )(dest.reshape(n_tok // tm, 1, tm * TOP_K), y_rows, h_flat, gate, g2, ln_g[None], ln_b[None])


def _moe_and_norm(h1, top_e, gate, pos, counts, sc2, sh2, g2, wgu_bf, bgu, wd_bf, bd, ln_g, ln_b, alpha):
    b, length, d = h1.shape
    n_tok = b * length
    n_blk = n_tok * TOP_K // EXPERT_ROWS + N_EXPERTS
    padded = (counts + EXPERT_ROWS - 1) // EXPERT_ROWS * EXPERT_ROWS
    pad_end = jnp.cumsum(padded)
    offset = pad_end - padded
    n_used = (pad_end[-1:] // EXPERT_ROWS).astype(I32)
    blk_e = jnp.minimum(jnp.searchsorted(pad_end, jnp.arange(n_blk, dtype=I32) * EXPERT_ROWS, side='right'),
                        N_EXPERTS - 1).astype(I32)
    dest = (offset[top_e] + pos).astype(I32)
    zstart = jnp.maximum(pad_end - EXPERT_ROWS, 0).astype(I32)
    zvalid = (counts > 0).astype(I32)

    h_flat = h1.reshape(n_tok, d)
    xs = _dispatch(h_flat, sc2, sh2, dest, zstart, zvalid, n_blk * EXPERT_ROWS, length)
    y_rows = _experts(xs, blk_e, n_used, wgu_bf, bgu, wd_bf, bd)
    out = _combine(y_rows, dest, h_flat, gate, g2, ln_g, ln_b, alpha, length)
    return out.reshape(b, length, d)


def _odd_in_kernel(x_ref, sc_ref, sh_ref, w_ref, glg_ref, glb_ref, ws_ref, bs_ref, flg_ref, flb_ref, dft_ref,
                   sp_ref, zr_ref, zi_ref):
    u = (x_ref[0] * (1.0 + sc_ref[0]) + sh_ref[0]).astype(BF16)
    z = jnp.dot(u, w_ref[...], preferred_element_type=F32)
    tm = z.shape[0]
    gw = N_GROUPS * GROUP_W
    ug = jax.nn.gelu(z[:, :gw])
    vn = _layer_norm(jax.nn.gelu(z[:, gw:2 * gw]), glg_ref[...], glb_ref[...]).astype(BF16)
    f = z[:, 2 * gw:]
    for g in range(N_GROUPS):
        cols = slice(g * GROUP_W, (g + 1) * GROUP_W)
        for c0 in range(0, tm, CHUNK):
            rows = slice(c0, c0 + CHUNK)
            sv = jnp.dot(ws_ref[g], vn[rows, cols], preferred_element_type=F32) + bs_ref[:, cols]
            sp_ref[0, rows, cols] = (ug[rows, cols] * sv).astype(sp_ref.dtype)
        fn = _layer_norm(f[:, cols], flg_ref[:, cols], flb_ref[:, cols]).astype(BF16)
        zz = jnp.dot(fn, dft_ref[...], preferred_element_type=F32)
        zr_ref[0, :, cols] = zz[:, :GROUP_W]
        zi_ref[0, :, cols] = zz[:, GROUP_W:]


def _odd_in_proj(x, sc, sh, w_bf, gln_g, gln_b, ws, bs, fln_g, fln_b):
    b, length, d = x.shape
    n = w_bf.shape[1]
    gw = N_GROUPS * GROUP_W
    tm = _pick(length, (512, 256, 128))
    kk = jnp.arange(GROUP_W, dtype=I32)
    ang = (2.0 * math.pi / GROUP_W) * ((kk[:, None] * kk[None, :]) % GROUP_W).astype(F32)
    dft = jnp.concatenate([jnp.cos(ang), -jnp.sin(ang)], 1).astype(BF16)
    bs_exp = jnp.repeat(bs.T, GROUP_W, axis=1)
    row = lambda bi, i: (bi, i, 0)
    vec = lambda bi, i: (bi, 0, 0)
    const2 = lambda bi, i: (0, 0)
    return pl.pallas_call(
        _odd_in_kernel,
        grid=(b, length // tm),
        in_specs=[pl.BlockSpec((1, tm, d), row), pl.BlockSpec((1, 1, d), vec), pl.BlockSpec((1, 1, d), vec),
                  pl.BlockSpec((d, n), const2),
                  pl.BlockSpec((1, gw), const2), pl.BlockSpec((1, gw), const2),
                  pl.BlockSpec((N_GROUPS, CHUNK, CHUNK), lambda bi, i: (0, 0, 0)),
                  pl.BlockSpec((CHUNK, gw), const2),
                  pl.BlockSpec((1, gw), const2), pl.BlockSpec((1, gw), const2),
                  pl.BlockSpec((GROUP_W, 2 * GROUP_W), const2)],
        out_specs=[pl.BlockSpec((1, tm, gw), row), pl.BlockSpec((1, tm, gw), row), pl.BlockSpec((1, tm, gw), row)],
        out_shape=[jax.ShapeDtypeStruct((b, length, gw), BF16), jax.ShapeDtypeStruct((b, length, gw), F32),
                   jax.ShapeDtypeStruct((b, length, gw), F32)],
        compiler_params=_params(2),
        name="odd_in_proj",
    )(x, sc, sh, w_bf, gln_g[None], gln_b[None], ws.astype(BF16), bs_exp, fln_g[None], fln_b[None], dft)


def _fft_a_kernel(zr_ref, zi_ref, m_ref, ar_ref, ai_ref):
    l1, nb, w = zr_ref.shape[1:]
    zr = zr_ref[0].reshape(l1 * nb, w).astype(BF16)
    zi = zi_ref[0].reshape(l1 * nb, w).astype(BF16)
    a = jnp.dot(m_ref[...], jnp.concatenate([zr, zi], 0), preferred_element_type=F32)
    ar_ref[0] = a[:l1 * nb].reshape(l1, nb, w)
    ai_ref[0] = a[l1 * nb:].reshape(l1, nb, w)


def _fft_b_kernel(ar_ref, ai_ref, m_ref, o_ref, *, norm):
    nb, l2, w = ar_ref.shape[1:]
    ar = ar_ref[0].reshape(nb * l2, w).astype(BF16)
    ai = ai_ref[0].reshape(nb * l2, w).astype(BF16)
    y = jnp.dot(m_ref[0], jnp.concatenate([ar, ai], 0), preferred_element_type=F32)
    o_ref[0] = (y * norm).reshape(l2, nb, w)


def _length_dft_real(zr, zi):
    b, length, w = zr.shape
    l2 = CHUNK
    l1 = length // l2
    nb = SUBLANES
    eye = jnp.eye(nb, dtype=F32)
    k1 = jnp.arange(l1, dtype=I32)
    ang_a = (2.0 * math.pi / l1) * ((k1[:, None] * k1[None, :]) % l1).astype(F32)
    ca = jnp.kron(jnp.cos(ang_a), eye)
    sa = jnp.kron(jnp.sin(ang_a), eye)
    m_a = jnp.concatenate([jnp.concatenate([ca, sa], 1), jnp.concatenate([-sa, ca], 1)], 0).astype(BF16)
    k2 = jnp.arange(l2, dtype=I32)
    phase = (k2[None, None, :] * (k1[None, :, None] + l1 * k2[:, None, None])) % length
    ang_b = (2.0 * math.pi / length) * phase.astype(F32)

    def expand(t):
        t = t.reshape(l2, l1 // nb, nb, l2).transpose(1, 0, 2, 3)
        return (t[:, :, :, None, :] * eye[None, None, :, :, None]).reshape(l1 // nb, l2 * nb, nb * l2)

    m_b = jnp.concatenate([expand(jnp.cos(ang_b)), expand(jnp.sin(ang_b))], 2).astype(BF16)

    zr4 = zr.reshape(b, l1, l2, w)
    zi4 = zi.reshape(b, l1, l2, w)
    blk_a = pl.BlockSpec((1, l1, nb, w), lambda bi, i: (bi, 0, i, 0))
    ar, ai = pl.pallas_call(
        _fft_a_kernel,
        grid=(b, l2 // nb),
        in_specs=[blk_a, blk_a, pl.BlockSpec((2 * l1 * nb, 2 * l1 * nb), lambda bi, i: (0, 0))],
        out_specs=[blk_a, blk_a],
        out_shape=[jax.ShapeDtypeStruct((b, l1, l2, w), F32)] * 2,
        compiler_params=_params(2),
        name="fourier_stage_a",
    )(zr4, zi4, m_a)
    blk_b = pl.BlockSpec((1, nb, l2, w), lambda bi, i: (bi, i, 0, 0))
    out = pl.pallas_call(
        functools.partial(_fft_b_kernel, norm=float((length * GROUP_W) ** -0.5)),
        grid=(b, l1 // nb),
        in_specs=[blk_b, blk_b, pl.BlockSpec((1, l2 * nb, 2 * nb * l2), lambda bi, i: (i, 0, 0))],
        out_specs=pl.BlockSpec((1, l2, nb, w), lambda bi, i: (bi, 0, i, 0)),
        out_shape=jax.ShapeDtypeStruct((b, l2, l1, w), F32),
        compiler_params=_params(2),
        name="fourier_stage_b",
    )(ar, ai, m_b)
    return out.reshape(b, length, w)


def kernel(x, c, ctx, c_ctx, w_mod, b_mod, ln1_g, ln1_b, ln2_g, ln2_b, ev_w_in, ev_w_out, conv_w, conv_b, conv_ln_g, conv_ln_b, lam_q1, lam_k1, lam_q2, lam_k2, diff_norm_g, od_w_in, od_w_out, gmlp_ln_g, gmlp_ln_b, gmlp_ws, gmlp_bs, four_ln_g, four_ln_b, router_w, router_b, w_gate_up, b_gate_up, w_down, b_down):
    b, length, d = x.shape
    depth = w_mod.shape[0]
    alpha = float((2 * depth) ** 0.25)
    assert b + 1 <= SUBLANES and length % (CHUNK * SUBLANES) == 0 and length % GRID_W == 0

    mod = _modulation(c, c_ctx, w_mod, b_mod)
    cos_t, sin_t = _rope_tables(length)
    h = x
    for layer in range(depth):
        j = layer // 2
        m = mod[layer]
        sh1, sc1, g1, sh2, sc2, g2 = [m[:b, i * d:(i + 1) * d][:, None, :] for i in range(6)]
        if layer % 2 == 0:
            lam_init = 0.8 - 0.6 * math.exp(-0.3 * layer)
            w_in = ev_w_in[j].astype(BF16)
            q, k, v, a = _even_in_proj(h, sc1, sh1, w_in, cos_t, sin_t)
            csh1 = jnp.broadcast_to(m[b:b + 1, 0:d][:, None, :], (b, 1, d))
            csc1 = jnp.broadcast_to(m[b:b + 1, d:2 * d][:, None, :], (b, 1, d))
            kv_c = _mod_matmul(ctx, csc1, csh1, w_in[:, QK_W:2 * QK_W + V_W], BF16)
            k_all = jnp.concatenate([k, kv_c[..., :QK_W]], 1)
            v_all = jnp.concatenate([v, kv_c[..., QK_W:]], 1)
            att = _diff_attention(q, k_all, v_all, lam_q1[j], lam_k1[j], lam_q2[j], lam_k2[j],
                                  diff_norm_g[j], lam_init)
            conv = _conformer_conv(a, conv_w[j], conv_b[j], conv_ln_g[j], conv_ln_b[j])
            ya, yb, w_out = conv, att, ev_w_out[j]
        else:
            spatial, zr, zi = _odd_in_proj(h, sc1, sh1, od_w_in[j].astype(BF16), gmlp_ln_g[j], gmlp_ln_b[j],
                                           gmlp_ws[j], gmlp_bs[j], four_ln_g[j], four_ln_b[j])
            ya, yb, w_out = spatial, _length_dft_real(zr, zi), od_w_out[j]
        h1, top_e, gate, pos, counts = _post_mix(ya, yb, w_out.astype(BF16), h, g1, ln1_g[layer], ln1_b[layer],
                                                 sc2, sh2, router_w[layer], router_b[layer], alpha)
        h = _moe_and_norm(h1, top_e[:, :TOP_K], gate, pos[:, :TOP_K], counts[0, :N_EXPERTS], sc2, sh2, g2,
                          w_gate_up[layer].astype(BF16), b_gate_up[layer], w_down[layer].astype(BF16),
                          b_down[layer], ln2_g[layer], ln2_b[layer], alpha)
    return h
```

```python
import functools
import math

import jax
import jax.numpy as jnp
from jax import lax
from jax.experimental import pallas as pl
from jax.experimental.pallas import tpu as pltpu

F32 = jnp.float32
BF16 = jnp.bfloat16
I32 = jnp.int32
HIGHEST = lax.Precision.HIGHEST

LN_EPS = 1e-5
GRID_W = 64
HEAD_DIM = 64
HEAD_V = 128
N_HEADS = 4
QK_W = N_HEADS * 2 * HEAD_DIM
V_W = N_HEADS * HEAD_V
CONV_W = 512
CONV_TAPS = 31
CONV_HALO = 16
ROPE_BASE = 10000.0
CHUNK = 128
N_GROUPS = 4
GROUP_W = 128
N_EXPERTS = 32
TOP_K = 4
SWIGLU_LIMIT = 7.0
SWIGLU_ALPHA = 1.702
LANES = 128
SUBLANES = 8
EXPERT_ROWS = 256
NEG_BIG = -1e30
VMEM_LIMIT = 56 * 1024 * 1024


def _params(n_axes):
    return pltpu.CompilerParams(dimension_semantics=("arbitrary",) * n_axes,
                                vmem_limit_bytes=VMEM_LIMIT)


def _pick(n, candidates):
    for c in candidates:
        if n % c == 0:
            return c
    return n


def _layer_norm(r, g, b):
    mu = jnp.mean(r, axis=-1, keepdims=True)
    d = r - mu
    var = jnp.mean(d * d, axis=-1, keepdims=True)
    return d * lax.rsqrt(var + LN_EPS) * g + b


def _sigmoid(x):
    return 1.0 / (1.0 + jnp.exp(-x))


def _mod_kernel(cs_ref, w_ref, b_ref, o_ref):
    cs = cs_ref[...]
    a = cs * _sigmoid(cs)
    o_ref[0] = jnp.dot(a, w_ref[0], precision=HIGHEST, preferred_element_type=F32) + b_ref[0]


def _modulation(c, c_ctx, w_mod, b_mod):
    depth, d, n = w_mod.shape
    b = c.shape[0]
    cs = jnp.concatenate([c, c_ctx[None], jnp.zeros((SUBLANES - b - 1, d), F32)], 0)
    tn = _pick(n, (1536, 1024, 512))
    return pl.pallas_call(
        _mod_kernel,
        grid=(depth, n // tn),
        in_specs=[pl.BlockSpec((SUBLANES, d), lambda l, j: (0, 0)),
                  pl.BlockSpec((1, d, tn), lambda l, j: (l, 0, j)),
                  pl.BlockSpec((1, 1, tn), lambda l, j: (l, 0, j))],
        out_specs=pl.BlockSpec((1, SUBLANES, tn), lambda l, j: (l, 0, j)),
        out_shape=jax.ShapeDtypeStruct((depth, SUBLANES, n), F32),
        compiler_params=_params(2),
        name="modulation",
    )(cs, w_mod, b_mod.reshape(depth, 1, n))


def _even_in_kernel(x_ref, sc_ref, sh_ref, w_ref, cos_ref, sin_ref, q_ref, k_ref, v_ref, a_ref):
    u = (x_ref[0] * (1.0 + sc_ref[0]) + sh_ref[0]).astype(BF16)
    z = jnp.dot(u, w_ref[...], preferred_element_type=F32)
    cosv = cos_ref[...]
    sinv = sin_ref[...]
    lane = lax.broadcasted_iota(I32, cosv.shape, 1)
    first = (lane % 32) < 16

    def rope(t):
        partner = jnp.where(first, pltpu.roll(t, LANES - 16, 1), pltpu.roll(t, 16, 1))
        return t * cosv + partner * sinv

    for j in range(QK_W // LANES):
        sl = slice(j * LANES, (j + 1) * LANES)
        rq = rope(z[:, sl]) * (HEAD_DIM ** -0.5 * math.log2(math.e))
        for c in range(2):
            q_ref[c, 0, :, sl] = jnp.where(lane // HEAD_DIM == c, rq, 0.0).astype(BF16)
        k_ref[0, :, sl] = rope(z[:, QK_W + j * LANES:QK_W + (j + 1) * LANES]).astype(BF16)
    v_ref[0] = z[:, 2 * QK_W:2 * QK_W + V_W].astype(BF16)
    a0 = 2 * QK_W + V_W
    a_ref[0] = z[:, a0:a0 + CONV_W] * _sigmoid(z[:, a0 + CONV_W:a0 + 2 * CONV_W])


def _rope_tables(length):
    rows = length // GRID_W
    row = jnp.repeat(jnp.arange(rows, dtype=F32), GRID_W)
    col = jnp.tile(jnp.arange(GRID_W, dtype=F32), rows)
    n_freq = HEAD_DIM // 4
    inv_freq = ROPE_BASE ** (-jnp.arange(n_freq, dtype=F32) / n_freq)
    ar = row[:, None] * inv_freq
    ac = col[:, None] * inv_freq
    cos64 = jnp.concatenate([jnp.cos(ar), jnp.cos(ar), jnp.cos(ac), jnp.cos(ac)], 1)
    sin64 = jnp.concatenate([-jnp.sin(ar), jnp.sin(ar), -jnp.sin(ac), jnp.sin(ac)], 1)
    return jnp.tile(cos64, (1, LANES // HEAD_DIM)), jnp.tile(sin64, (1, LANES // HEAD_DIM))


def _even_in_proj(x, sc, sh, w_bf, cos_t, sin_t):
    b, length, d = x.shape
    n = w_bf.shape[1]
    tm = _pick(length, (512, 256, 128))
    row = lambda bi, i: (bi, i, 0)
    vec = lambda bi, i: (bi, 0, 0)
    return pl.pallas_call(
        _even_in_kernel,
        grid=(b, length // tm),
        in_specs=[pl.BlockSpec((1, tm, d), row),
                  pl.BlockSpec((1, 1, d), vec),
                  pl.BlockSpec((1, 1, d), vec),
                  pl.BlockSpec((d, n), lambda bi, i: (0, 0)),
                  pl.BlockSpec((tm, LANES), lambda bi, i: (i, 0)),
                  pl.BlockSpec((tm, LANES), lambda bi, i: (i, 0))],
        out_specs=[pl.BlockSpec((2, 1, tm, QK_W), lambda bi, i: (0, bi, i, 0)), pl.BlockSpec((1, tm, QK_W), row),
                   pl.BlockSpec((1, tm, V_W), row), pl.BlockSpec((1, tm, CONV_W), row)],
        out_shape=[jax.ShapeDtypeStruct((2, b, length, QK_W), BF16), jax.ShapeDtypeStruct((b, length, QK_W), BF16),
                   jax.ShapeDtypeStruct((b, length, V_W), BF16), jax.ShapeDtypeStruct((b, length, CONV_W), F32)],
        compiler_params=_params(2),
        name="even_in_proj",
    )(x, sc, sh, w_bf, cos_t, sin_t)


def _modmm_kernel(x_ref, sc_ref, sh_ref, w_ref, o_ref):
    u = (x_ref[0] * (1.0 + sc_ref[0]) + sh_ref[0]).astype(BF16)
    o_ref[0] = jnp.dot(u, w_ref[...], preferred_element_type=F32).astype(o_ref.dtype)


def _mod_matmul(x, sc, sh, w_bf, out_dtype):
    b, rows, d = x.shape
    n = w_bf.shape[1]
    tm = _pick(rows, (512, 256, 128))
    return pl.pallas_call(
        _modmm_kernel,
        grid=(b, rows // tm),
        in_specs=[pl.BlockSpec((1, tm, d), lambda bi, i: (bi, i, 0)),
                  pl.BlockSpec((1, 1, d), lambda bi, i: (bi, 0, 0)),
                  pl.BlockSpec((1, 1, d), lambda bi, i: (bi, 0, 0)),
                  pl.BlockSpec((d, n), lambda bi, i: (0, 0))],
        out_specs=pl.BlockSpec((1, tm, n), lambda bi, i: (bi, i, 0)),
        out_shape=jax.ShapeDtypeStruct((b, rows, n), out_dtype),
        compiler_params=_params(2),
        name="context_kv_proj",
    )(x, sc, sh, w_bf)


VT_ROWS = HEAD_V + 16


def _attn_kernel(lq1_ref, lk1_ref, lq2_ref, lk2_ref, g_ref, q_ref, k_ref, vt_ref, o_ref,
                 s0_sc, s1_sc, p0_sc, p1_sc, acc_sc, *, lam_init, kc):
    mq = q_ref.shape[2]
    n = k_ref.shape[1] // kc
    s_sc = (s0_sc, s1_sc)
    p_sc = (p0_sc, p1_sc)

    def scores(j, slot):
        kj = k_ref[0, pl.ds(pl.multiple_of(j * kc, kc), kc), :]
        for c in range(2):
            s_sc[slot][c] = lax.dot_general(kj, q_ref[c, 0], (((1,), (1,)), ((), ())),
                                            preferred_element_type=F32)

    def softmax(slot, m):
        m_out, alpha = [], []
        for c in range(2):
            s = s_sc[slot][c]
            m_new = jnp.maximum(m[c], jnp.max(s, axis=0, keepdims=True))
            p_sc[slot][c] = jnp.exp2((s - m_new).astype(BF16))
            m_out.append(m_new)
            alpha.append(jnp.exp2(m[c] - m_new))
        return tuple(m_out), tuple(alpha)

    def accumulate(j, slot, alpha):
        vt = vt_ref[0, 0, j]
        for c in range(2):
            acc_sc[c] = alpha[c] * acc_sc[c] + jnp.dot(vt, p_sc[slot][c], preferred_element_type=F32)

    def step(j, slot, m, alpha):
        scores(j + 2, slot)
        m, alpha_next = softmax(1 - slot, m)
        accumulate(j, slot, alpha)
        return m, alpha_next

    acc_sc[...] = jnp.zeros(acc_sc.shape, F32)
    m = (jnp.full((1, mq), NEG_BIG, F32),) * 2
    scores(0, 0)
    m, alpha = softmax(0, m)
    scores(1, 1)

    def pair(t, carry):
        m, alpha = step(2 * t, 0, *carry)
        return step(2 * t + 1, 1, m, alpha)

    n_steps = n - 2
    m, alpha = lax.fori_loop(0, n_steps // 2, pair, (m, alpha))
    if n_steps % 2:
        m, alpha = step(n_steps - 1, 0, m, alpha)
    m, alpha_last = softmax((n - 1) % 2, m)
    accumulate(n - 2, (n - 2) % 2, alpha)
    accumulate(n - 1, (n - 1) % 2, alpha_last)
    acc0 = acc_sc[0]
    acc1 = acc_sc[1]
    lam = (jnp.exp(jnp.sum(lq1_ref[...] * lk1_ref[...], keepdims=True))
           - jnp.exp(jnp.sum(lq2_ref[...] * lk2_ref[...], keepdims=True)) + lam_init)
    o = (acc0[:HEAD_V] / acc0[HEAD_V:HEAD_V + 1]
         - lam * (acc1[:HEAD_V] / acc1[HEAD_V:HEAD_V + 1]))
    ms = jnp.mean(o * o, axis=0, keepdims=True)
    o = o * lax.rsqrt(ms + LN_EPS) * g_ref[...] * (1.0 - lam_init)
    o_ref[0] = o.T.astype(o_ref.dtype)


def _diff_attention(q, k_all, v_all, lq1, lk1, lq2, lk2, norm_g, lam_init):
    _, b, length, _ = q.shape
    n_keys = k_all.shape[1]
    mq = _pick(length, (512, 256, 128))
    kc = _pick(n_keys, (256, 128))
    nc = n_keys // kc
    assert nc >= 3
    vt = v_all.reshape(b, nc, kc, N_HEADS, HEAD_V).transpose(0, 3, 1, 4, 2)
    extra = jnp.zeros((b, N_HEADS, nc, VT_ROWS - HEAD_V, kc), BF16).at[:, :, :, 0, :].set(1.0)
    vt = jnp.concatenate([vt, extra], 3)
    small = lambda bi, h, i: (0, 0)
    return pl.pallas_call(
        functools.partial(_attn_kernel, lam_init=lam_init, kc=kc),
        grid=(b, N_HEADS, length // mq),
        in_specs=[pl.BlockSpec((1, HEAD_DIM), small), pl.BlockSpec((1, HEAD_DIM), small),
                  pl.BlockSpec((1, HEAD_DIM), small), pl.BlockSpec((1, HEAD_DIM), small),
                  pl.BlockSpec((HEAD_V, 1), small),
                  pl.BlockSpec((2, 1, mq, HEAD_V), lambda bi, h, i: (0, bi, i, h)),
                  pl.BlockSpec((1, n_keys, HEAD_V), lambda bi, h, i: (bi, 0, h)),
                  pl.BlockSpec((1, 1, nc, VT_ROWS, kc), lambda bi, h, i: (bi, h, 0, 0, 0))],
        out_specs=pl.BlockSpec((1, mq, HEAD_V), lambda bi, h, i: (bi, i, h)),
        out_shape=jax.ShapeDtypeStruct((b, length, V_W), BF16),
        scratch_shapes=[pltpu.VMEM((2, kc, mq), F32), pltpu.VMEM((2, kc, mq), F32),
                        pltpu.VMEM((2, kc, mq), BF16), pltpu.VMEM((2, kc, mq), BF16),
                        pltpu.VMEM((2, VT_ROWS, mq), F32)],
        compiler_params=_params(3),
        name="diff_attention",
    )(lq1[None], lk1[None], lq2[None], lk2[None], norm_g[:, None], q, k_all, vt)


def _conv_kernel(prev_ref, cur_ref, next_ref, w_ref, b_ref, g_ref, beta_ref, o_ref, win, *, row_chunk):
    i = pl.program_id(1)
    tc = cur_ref.shape[1]
    win[CONV_HALO:CONV_HALO + tc, :] = cur_ref[0]
    win[0:CONV_HALO, :] = jnp.where(i > 0, prev_ref[0], 0.0)
    win[CONV_HALO + tc:, :] = jnp.where(i < pl.num_programs(1) - 1, next_ref[0], 0.0)
    first_tap = CONV_HALO - CONV_TAPS // 2
    for c0 in range(0, tc, row_chunk):
        acc = jnp.zeros((row_chunk, CONV_W), F32)
        for t in range(CONV_TAPS):
            acc = acc + w_ref[t:t + 1, :] * win[c0 + first_tap + t:c0 + first_tap + t + row_chunk, :]
        y = _layer_norm(acc + b_ref[...], g_ref[...], beta_ref[...])
        o_ref[0, c0:c0 + row_chunk, :] = (y * _sigmoid(y)).astype(o_ref.dtype)


def _conformer_conv(a, conv_w, conv_b, ln_g, ln_b):
    b, length, w = a.shape
    tc = _pick(length, (512, 256, 128))
    hb = tc // CONV_HALO
    n_halo = length // CONV_HALO
    w_pad = jnp.concatenate([conv_w, jnp.zeros((1, w), F32)], 0)
    vec = lambda bi, i: (0, 0)
    return pl.pallas_call(
        functools.partial(_conv_kernel, row_chunk=64),
        grid=(b, length // tc),
        in_specs=[pl.BlockSpec((1, CONV_HALO, w), lambda bi, i: (bi, jnp.maximum(i * hb - 1, 0), 0)),
                  pl.BlockSpec((1, tc, w), lambda bi, i: (bi, i, 0)),
                  pl.BlockSpec((1, CONV_HALO, w), lambda bi, i: (bi, jnp.minimum((i + 1) * hb, n_halo - 1), 0)),
                  pl.BlockSpec((CONV_TAPS + 1, w), vec),
                  pl.BlockSpec((1, w), vec), pl.BlockSpec((1, w), vec), pl.BlockSpec((1, w), vec)],
        out_specs=pl.BlockSpec((1, tc, w), lambda bi, i: (bi, i, 0)),
        out_shape=jax.ShapeDtypeStruct((b, length, w), BF16),
        scratch_shapes=[pltpu.VMEM((tc + 2 * CONV_HALO, w), F32)],
        compiler_params=_params(2),
        name="conformer_conv",
    )(a, a, a, w_pad, conv_b[None], ln_g[None], ln_b[None])


def _post_mix_kernel(ya_ref, yb_ref, wa_ref, wb_ref, x_ref, g1_ref, lng_ref, lnb_ref, sc_ref, sh_ref,
                     rw_ref, rb_ref, h_ref, e_ref, gate_ref, pos_ref, cnt_ref, carry, *, alpha):
    first_step = jnp.logical_and(pl.program_id(0) == 0, pl.program_id(1) == 0)

    @pl.when(first_step)
    def _():
        carry[...] = jnp.zeros(carry.shape, F32)

    y = (jnp.dot(ya_ref[0].astype(BF16), wa_ref[...], preferred_element_type=F32)
         + jnp.dot(yb_ref[0].astype(BF16), wb_ref[...], preferred_element_type=F32))
    h = _layer_norm(alpha * x_ref[0] + g1_ref[0] * y, lng_ref[...], lnb_ref[...])
    h_ref[0] = h
    u2 = h * (1.0 + sc_ref[0]) + sh_ref[0]
    logits = jnp.dot(u2, rw_ref[...], precision=HIGHEST, preferred_element_type=F32) + rb_ref[...]

    tm = logits.shape[0]
    lane = lax.broadcasted_iota(I32, logits.shape, 1).astype(F32)
    vals, idxs = [], []
    lg = logits
    for _ in range(TOP_K):
        mx = jnp.max(lg, axis=1, keepdims=True)
        idx = jnp.min(jnp.where(lg == mx, lane, float(LANES)), axis=1, keepdims=True)
        vals.append(mx)
        idxs.append(idx)
        lg = jnp.where(lane == idx, -3e38, lg)
    exps = [jnp.exp(v - vals[0]) for v in vals]
    den = exps[0] + exps[1] + exps[2] + exps[3]

    hot = jnp.zeros(logits.shape, F32)
    for idx in idxs:
        hot = hot + (lane == idx).astype(F32)
    r_i = lax.broadcasted_iota(I32, (tm, tm), 0)
    c_i = lax.broadcasted_iota(I32, (tm, tm), 1)
    tri = (c_i < r_i).astype(BF16)
    before = jnp.dot(tri, hot.astype(BF16), preferred_element_type=F32) + carry[...]
    e_out = jnp.zeros(logits.shape, F32)
    g_out = jnp.zeros(logits.shape, F32)
    p_out = jnp.zeros(logits.shape, F32)
    for j in range(TOP_K):
        pos_j = jnp.sum(jnp.where(lane == idxs[j], before, 0.0), axis=1, keepdims=True)
        sel = lane == float(j)
        e_out = jnp.where(sel, idxs[j], e_out)
        g_out = jnp.where(sel, exps[j] / den, g_out)
        p_out = jnp.where(sel, pos_j, p_out)
    e_ref[...] = e_out.astype(I32)
    gate_ref[...] = g_out
    pos_ref[...] = p_out.astype(I32)
    new_carry = carry[...] + jnp.sum(hot, axis=0, keepdims=True)
    carry[...] = new_carry
    cnt_ref[...] = jnp.broadcast_to(new_carry, cnt_ref.shape).astype(I32)


def _post_mix(ya, yb, w_out_bf, x, g1, ln_g, ln_b, sc2, sh2, router_w, router_b, alpha):
    b, length, d = x.shape
    wa = ya.shape[2]
    tm = _pick(length, (512, 256, 128))
    n_tok = b * length
    rw = jnp.concatenate([router_w, jnp.zeros((d, LANES - N_EXPERTS), F32)], 1)
    rb = jnp.concatenate([router_b, jnp.full((LANES - N_EXPERTS,), NEG_BIG, F32)])[None]
    row = lambda bi, i: (bi, i, 0)
    vec = lambda bi, i: (bi, 0, 0)
    const = lambda bi, i: (0, 0)
    nt = length // tm
    tok = lambda bi, i: (bi * nt + i, 0)
    return pl.pallas_call(
        functools.partial(_post_mix_kernel, alpha=alpha),
        grid=(b, nt),
        in_specs=[pl.BlockSpec((1, tm, wa), row), pl.BlockSpec((1, tm, d - wa), row),
                  pl.BlockSpec((wa, d), const), pl.BlockSpec((d - wa, d), const),
                  pl.BlockSpec((1, tm, d), row), pl.BlockSpec((1, 1, d), vec),
                  pl.BlockSpec((1, d), const), pl.BlockSpec((1, d), const),
                  pl.BlockSpec((1, 1, d), vec), pl.BlockSpec((1, 1, d), vec),
                  pl.BlockSpec((d, LANES), const), pl.BlockSpec((1, LANES), const)],
        out_specs=[pl.BlockSpec((1, tm, d), row), pl.BlockSpec((tm, LANES), tok),
                   pl.BlockSpec((tm, LANES), tok), pl.BlockSpec((tm, LANES), tok),
                   pl.BlockSpec((SUBLANES, LANES), const)],
        out_shape=[jax.ShapeDtypeStruct((b, length, d), F32), jax.ShapeDtypeStruct((n_tok, LANES), I32),
                   jax.ShapeDtypeStruct((n_tok, LANES), F32), jax.ShapeDtypeStruct((n_tok, LANES), I32),
                   jax.ShapeDtypeStruct((SUBLANES, LANES), I32)],
        scratch_shapes=[pltpu.VMEM((1, LANES), F32)],
        compiler_params=_params(2),
        name="out_proj_norm_router",
    )(ya, yb, w_out_bf[:wa], w_out_bf[wa:], x, g1, ln_g[None], ln_b[None], sc2, sh2, rw, rb)


def _dispatch_kernel(zstart_ref, zvalid_ref, dest_ref, h_ref, sc_ref, sh_ref, xs_ref, u_sc, zero_sc, sem, zsem):
    tm = h_ref.shape[0]

    @pl.when(pl.program_id(0) == 0)
    def _():
        zero_sc[...] = jnp.zeros(zero_sc.shape, F32)

        def fill(e, c):
            @pl.when(zvalid_ref[e] > 0)
            def _():
                start = pl.multiple_of(zstart_ref[e], SUBLANES)
                pltpu.make_async_copy(zero_sc, xs_ref.at[pl.ds(start, EXPERT_ROWS)], zsem).start()
            return c

        lax.fori_loop(0, N_EXPERTS, fill, 0)

        def drain(e, c):
            @pl.when(zvalid_ref[e] > 0)
            def _():
                pltpu.make_async_copy(zero_sc, xs_ref.at[pl.ds(0, EXPERT_ROWS)], zsem).wait()
            return c

        lax.fori_loop(0, N_EXPERTS, drain, 0)

    u_sc[...] = h_ref[...] * (1.0 + sc_ref[0]) + sh_ref[0]

    def issue(r, c):
        for j in range(TOP_K):
            d = dest_ref[0, 0, r * TOP_K + j]
            pltpu.make_async_copy(u_sc.at[pl.ds(r, 1)], xs_ref.at[pl.ds(d, 1)], sem).start()
        return c

    lax.fori_loop(0, tm, issue, 0)

    def drain_rows(r, c):
        pltpu.make_async_copy(u_sc.at[pl.ds(0, 1)], xs_ref.at[pl.ds(0, 1)], sem).wait()
        return c

    lax.fori_loop(0, tm * TOP_K, drain_rows, 0)


def _dispatch(h_flat, sc2, sh2, dest, zstart, zvalid, n_rows, tokens_per_batch):
    n_tok, d = h_flat.shape
    tm = _pick(tokens_per_batch, (512, 256, 128))
    per_b = tokens_per_batch // tm
    return pl.pallas_call(
        _dispatch_kernel,
        grid_spec=pltpu.PrefetchScalarGridSpec(
            num_scalar_prefetch=2,
            grid=(n_tok // tm,),
            in_specs=[pl.BlockSpec((1, 1, tm * TOP_K), lambda i, zs, zv: (i, 0, 0), memory_space=pltpu.SMEM),
                      pl.BlockSpec((tm, d), lambda i, zs, zv: (i, 0)),
                      pl.BlockSpec((1, 1, d), lambda i, zs, zv: (i // per_b, 0, 0)),
                      pl.BlockSpec((1, 1, d), lambda i, zs, zv: (i // per_b, 0, 0))],
            out_specs=pl.BlockSpec(memory_space=pl.ANY),
            scratch_shapes=[pltpu.VMEM((tm, d), F32), pltpu.VMEM((EXPERT_ROWS, d), F32),
                            pltpu.SemaphoreType.DMA(()), pltpu.SemaphoreType.DMA(())]),
        out_shape=jax.ShapeDtypeStruct((n_rows, d), F32),
        compiler_params=_params(1),
        name="moe_dispatch",
    )(zstart, zvalid, dest.reshape(n_tok // tm, 1, tm * TOP_K), h_flat, sc2, sh2)


def _expert_kernel(blk_e_ref, n_used_ref, x_ref, wgu_ref, bgu_ref, wd_ref, bd_ref, o_ref):
    @pl.when(pl.program_id(0) < n_used_ref[0])
    def _():
        d_e = wd_ref.shape[1]
        gu = jnp.dot(x_ref[...].astype(BF16), wgu_ref[0], preferred_element_type=F32) + bgu_ref[0]
        x_glu = jnp.minimum(gu[:, :d_e], SWIGLU_LIMIT)
        x_lin = jnp.clip(gu[:, d_e:], -SWIGLU_LIMIT, SWIGLU_LIMIT)
        act = x_glu * _sigmoid(SWIGLU_ALPHA * x_glu) * (x_lin + 1.0)
        o_ref[...] = jnp.dot(act.astype(BF16), wd_ref[0], preferred_element_type=F32) + bd_ref[0]


def _experts(xs, blk_e, n_used, wgu_bf, bgu, wd_bf, bd):
    n_rows, d = xs.shape
    n_e, _, n_gu = wgu_bf.shape
    n_blk = n_rows // EXPERT_ROWS
    rows = lambda i, be, nu: (jnp.minimum(i, nu[0] - 1), 0)
    by_e = lambda i, be, nu: (be[i], 0, 0)
    return pl.pallas_call(
        _expert_kernel,
        grid_spec=pltpu.PrefetchScalarGridSpec(
            num_scalar_prefetch=2,
            grid=(n_blk,),
            in_specs=[pl.BlockSpec((EXPERT_ROWS, d), rows),
                      pl.BlockSpec((1, d, n_gu), by_e), pl.BlockSpec((1, 1, n_gu), by_e),
                      pl.BlockSpec((1, n_gu // 2, d), by_e), pl.BlockSpec((1, 1, d), by_e)],
            out_specs=pl.BlockSpec((EXPERT_ROWS, d), rows)),
        out_shape=jax.ShapeDtypeStruct((n_rows, d), F32),
        compiler_params=_params(1),
        name="moe_experts",
    )(blk_e, n_used, xs, wgu_bf, bgu.reshape(n_e, 1, n_gu), wd_bf, bd.reshape(n_e, 1, d))


def _combine_kernel(dest_ref, y_ref, h_ref, gate_ref, g2_ref, lng_ref, lnb_ref, o_ref, buf, sem, *, alpha):
    tm = h_ref.shape[0]

    def issue(r, c):
        for j in range(TOP_K):
            d = dest_ref[0, 0, r * TOP_K + j]
            pltpu.make_async_copy(y_ref.at[pl.ds(d, 1)], buf.at[j, pl.ds(r, 1)], sem).start()
        return c

    lax.fori_loop(0, tm, issue, 0)

    def drain(r, c):
        pltpu.make_async_copy(y_ref.at[pl.ds(0, 1)], buf.at[0, pl.ds(0, 1)], sem).wait()
        return c

    lax.fori_loop(0, tm * TOP_K, drain, 0)

    gate = gate_ref[...]
    y2 = gate[:, 0:1] * buf[0]
    for j in range(1, TOP_K):
        y2 = y2 + gate[:, j:j + 1] * buf[j]
    o_ref[...] = _layer_norm(alpha * h_ref[...] + g2_ref[0] * y2, lng_ref[...], lnb_ref[...])


def _combine(y_rows, dest, h_flat, gate, g2, ln_g, ln_b, alpha, tokens_per_batch):
    n_tok, d = h_flat.shape
    tm = _pick(tokens_per_batch, (256, 128))
    per_b = tokens_per_batch // tm
    return pl.pallas_call(
        functools.partial(_combine_kernel, alpha=alpha),
        grid=(n_tok // tm,),
        in_specs=[pl.BlockSpec((1, 1, tm * TOP_K), lambda i: (i, 0, 0), memory_space=pltpu.SMEM),
                  pl.BlockSpec(memory_space=pl.ANY),
                  pl.BlockSpec((tm, d), lambda i: (i, 0)),
                  pl.BlockSpec((tm, LANES), lambda i: (i, 0)),
                  pl.BlockSpec((1, 1, d), lambda i: (i // per_b, 0, 0)),
                  pl.BlockSpec((1, d), lambda i: (0, 0)), pl.BlockSpec((1, d), lambda i: (0, 0))],
        out_specs=pl.BlockSpec((tm, d), lambda i: (i, 0)),
        out_shape=jax.ShapeDtypeStruct((n_tok, d), F32),
        scratch_shapes=[pltpu.VMEM((TOP_K, tm, d), F32), pltpu.SemaphoreType.DMA(())],
        compiler_params=_params(1),
        name="moe_combine_norm",
    )(dest.reshape(n_tok // tm, 1, tm * TOP_K), y_rows, h_flat, gate, g2, ln_g[None], ln_b[None])


def _moe_and_norm(h1, top_e, gate, pos, counts, sc2, sh2, g2, wgu_bf, bgu, wd_bf, bd, ln_g, ln_b, alpha):
    b, length, d = h1.shape
    n_tok = b * length
    n_blk = n_tok * TOP_K // EXPERT_ROWS + N_EXPERTS
    padded = (counts + EXPERT_ROWS - 1) // EXPERT_ROWS * EXPERT_ROWS
    pad_end = jnp.cumsum(padded)
    offset = pad_end - padded
    n_used = (pad_end[-1:] // EXPERT_ROWS).astype(I32)
    blk_start = jnp.arange(n_blk, dtype=I32) * EXPERT_ROWS
    blk_e = jnp.minimum(jnp.sum(pad_end[None, :] <= blk_start[:, None], axis=1), N_EXPERTS - 1).astype(I32)
    dest = (offset[top_e] + pos).astype(I32)
    zstart = jnp.maximum(pad_end - EXPERT_ROWS, 0).astype(I32)
    zvalid = (counts > 0).astype(I32)

    h_flat = h1.reshape(n_tok, d)
    xs = _dispatch(h_flat, sc2, sh2, dest, zstart, zvalid, n_blk * EXPERT_ROWS, length)
    y_rows = _experts(xs, blk_e, n_used, wgu_bf, bgu, wd_bf, bd)
    out = _combine(y_rows, dest, h_flat, gate, g2, ln_g, ln_b, alpha, length)
    return out.reshape(b, length, d)


def _odd_in_kernel(x_ref, sc_ref, sh_ref, w_ref, glg_ref, glb_ref, ws_ref, bs_ref, flg_ref, flb_ref, dft_ref,
                   sp_ref, zr_ref, zi_ref):
    u = (x_ref[0] * (1.0 + sc_ref[0]) + sh_ref[0]).astype(BF16)
    z = jnp.dot(u, w_ref[...], preferred_element_type=F32)
    tm = z.shape[0]
    gw = N_GROUPS * GROUP_W
    ug = jax.nn.gelu(z[:, :gw])
    vn = _layer_norm(jax.nn.gelu(z[:, gw:2 * gw]), glg_ref[...], glb_ref[...]).astype(BF16)
    f = z[:, 2 * gw:]
    for g in range(N_GROUPS):
        cols = slice(g * GROUP_W, (g + 1) * GROUP_W)
        for c0 in range(0, tm, CHUNK):
            rows = slice(c0, c0 + CHUNK)
            sv = jnp.dot(ws_ref[g], vn[rows, cols], preferred_element_type=F32) + bs_ref[:, cols]
            sp_ref[0, rows, cols] = (ug[rows, cols] * sv).astype(sp_ref.dtype)
        fn = _layer_norm(f[:, cols], flg_ref[:, cols], flb_ref[:, cols]).astype(BF16)
        zz = jnp.dot(fn, dft_ref[...], preferred_element_type=F32)
        zr_ref[0, :, cols] = zz[:, :GROUP_W]
        zi_ref[0, :, cols] = zz[:, GROUP_W:]


def _odd_in_proj(x, sc, sh, w_bf, gln_g, gln_b, ws, bs, fln_g, fln_b):
    b, length, d = x.shape
    n = w_bf.shape[1]
    gw = N_GROUPS * GROUP_W
    tm = _pick(length, (512, 256, 128))
    kk = jnp.arange(GROUP_W, dtype=I32)
    ang = (2.0 * math.pi / GROUP_W) * ((kk[:, None] * kk[None, :]) % GROUP_W).astype(F32)
    dft = jnp.concatenate([jnp.cos(ang), -jnp.sin(ang)], 1).astype(BF16)
    bs_exp = jnp.repeat(bs.T, GROUP_W, axis=1)
    row = lambda bi, i: (bi, i, 0)
    vec = lambda bi, i: (bi, 0, 0)
    const2 = lambda bi, i: (0, 0)
    return pl.pallas_call(
        _odd_in_kernel,
        grid=(b, length // tm),
        in_specs=[pl.BlockSpec((1, tm, d), row), pl.BlockSpec((1, 1, d), vec), pl.BlockSpec((1, 1, d), vec),
                  pl.BlockSpec((d, n), const2),
                  pl.BlockSpec((1, gw), const2), pl.BlockSpec((1, gw), const2),
                  pl.BlockSpec((N_GROUPS, CHUNK, CHUNK), lambda bi, i: (0, 0, 0)),
                  pl.BlockSpec((CHUNK, gw), const2),
                  pl.BlockSpec((1, gw), const2), pl.BlockSpec((1, gw), const2),
                  pl.BlockSpec((GROUP_W, 2 * GROUP_W), const2)],
        out_specs=[pl.BlockSpec((1, tm, gw), row), pl.BlockSpec((1, tm, gw), row), pl.BlockSpec((1, tm, gw), row)],
        out_shape=[jax.ShapeDtypeStruct((b, length, gw), BF16), jax.ShapeDtypeStruct((b, length, gw), F32),
                   jax.ShapeDtypeStruct((b, length, gw), F32)],
        compiler_params=_params(2),
        name="odd_in_proj",
    )(x, sc, sh, w_bf, gln_g[None], gln_b[None], ws.astype(BF16), bs_exp, fln_g[None], fln_b[None], dft)


def _fft_a_kernel(zr_ref, zi_ref, m_ref, ar_ref, ai_ref):
    l1, nb, w = zr_ref.shape[1:]
    zr = zr_ref[0].reshape(l1 * nb, w).astype(BF16)
    zi = zi_ref[0].reshape(l1 * nb, w).astype(BF16)
    a = jnp.dot(m_ref[...], jnp.concatenate([zr, zi], 0), preferred_element_type=F32)
    ar_ref[0] = a[:l1 * nb].reshape(l1, nb, w)
    ai_ref[0] = a[l1 * nb:].reshape(l1, nb, w)


def _fft_b_kernel(ar_ref, ai_ref, m_ref, o_ref, *, norm):
    nb, l2, w = ar_ref.shape[1:]
    ar = ar_ref[0].reshape(nb * l2, w).astype(BF16)
    ai = ai_ref[0].reshape(nb * l2, w).astype(BF16)
    y = jnp.dot(m_ref[0], jnp.concatenate([ar, ai], 0), preferred_element_type=F32)
    o_ref[0] = (y * norm).reshape(l2, nb, w)


def _length_dft_real(zr, zi):
    b, length, w = zr.shape
    l2 = CHUNK
    l1 = length // l2
    nb = SUBLANES
    eye = jnp.eye(nb, dtype=F32)
    k1 = jnp.arange(l1, dtype=I32)
    ang_a = (2.0 * math.pi / l1) * ((k1[:, None] * k1[None, :]) % l1).astype(F32)
    ca = jnp.kron(jnp.cos(ang_a), eye)
    sa = jnp.kron(jnp.sin(ang_a), eye)
    m_a = jnp.concatenate([jnp.concatenate([ca, sa], 1), jnp.concatenate([-sa, ca], 1)], 0).astype(BF16)
    k2 = jnp.arange(l2, dtype=I32)
    phase = (k2[None, None, :] * (k1[None, :, None] + l1 * k2[:, None, None])) % length
    ang_b = (2.0 * math.pi / length) * phase.astype(F32)

    def expand(t):
        t = t.reshape(l2, l1 // nb, nb, l2).transpose(1, 0, 2, 3)
        return (t[:, :, :, None, :] * eye[None, None, :, :, None]).reshape(l1 // nb, l2 * nb, nb * l2)

    m_b = jnp.concatenate([expand(jnp.cos(ang_b)), expand(jnp.sin(ang_b))], 2).astype(BF16)

    zr4 = zr.reshape(b, l1, l2, w)
    zi4 = zi.reshape(b, l1, l2, w)
    blk_a = pl.BlockSpec((1, l1, nb, w), lambda bi, i: (bi, 0, i, 0))
    ar, ai = pl.pallas_call(
        _fft_a_kernel,
        grid=(b, l2 // nb),
        in_specs=[blk_a, blk_a, pl.BlockSpec((2 * l1 * nb, 2 * l1 * nb), lambda bi, i: (0, 0))],
        out_specs=[blk_a, blk_a],
        out_shape=[jax.ShapeDtypeStruct((b, l1, l2, w), F32)] * 2,
        compiler_params=_params(2),
        name="fourier_stage_a",
    )(zr4, zi4, m_a)
    blk_b = pl.BlockSpec((1, nb, l2, w), lambda bi, i: (bi, i, 0, 0))
    out = pl.pallas_call(
        functools.partial(_fft_b_kernel, norm=float((length * GROUP_W) ** -0.5)),
        grid=(b, l1 // nb),
        in_specs=[blk_b, blk_b, pl.BlockSpec((1, l2 * nb, 2 * nb * l2), lambda bi, i: (i, 0, 0))],
        out_specs=pl.BlockSpec((1, l2, nb, w), lambda bi, i: (bi, 0, i, 0)),
        out_shape=jax.ShapeDtypeStruct((b, l2, l1, w), F32),
        compiler_params=_params(2),
        name="fourier_stage_b",
    )(ar, ai, m_b)
    return out.reshape(b, length, w)


def kernel(x, c, ctx, c_ctx, w_mod, b_mod, ln1_g, ln1_b, ln2_g, ln2_b, ev_w_in, ev_w_out, conv_w, conv_b, conv_ln_g, conv_ln_b, lam_q1, lam_k1, lam_q2, lam_k2, diff_norm_g, od_w_in, od_w_out, gmlp_ln_g, gmlp_ln_b, gmlp_ws, gmlp_bs, four_ln_g, four_ln_b, router_w, router_b, w_gate_up, b_gate_up, w_down, b_down):
    b, length, d = x.shape
    depth = w_mod.shape[0]
    alpha = float((2 * depth) ** 0.25)
    assert b + 1 <= SUBLANES and length % (CHUNK * SUBLANES) == 0 and length % GRID_W == 0

    mod = _modulation(c, c_ctx, w_mod, b_mod)
    cos_t, sin_t = _rope_tables(length)
    h = x
    for layer in range(depth):
        j = layer // 2
        m = mod[layer]
        sh1, sc1, g1, sh2, sc2, g2 = [m[:b, i * d:(i + 1) * d][:, None, :] for i in range(6)]
        if layer % 2 == 0:
            lam_init = 0.8 - 0.6 * math.exp(-0.3 * layer)
            w_in = ev_w_in[j].astype(BF16)
            q, k, v, a = _even_in_proj(h, sc1, sh1, w_in, cos_t, sin_t)
            csh1 = jnp.broadcast_to(m[b:b + 1, 0:d][:, None, :], (b, 1, d))
            csc1 = jnp.broadcast_to(m[b:b + 1, d:2 * d][:, None, :], (b, 1, d))
            kv_c = _mod_matmul(ctx, csc1, csh1, w_in[:, QK_W:2 * QK_W + V_W], BF16)
            k_all = jnp.concatenate([k, kv_c[..., :QK_W]], 1)
            v_all = jnp.concatenate([v, kv_c[..., QK_W:]], 1)
            att = _diff_attention(q, k_all, v_all, lam_q1[j], lam_k1[j], lam_q2[j], lam_k2[j],
                                  diff_norm_g[j], lam_init)
            conv = _conformer_conv(a, conv_w[j], conv_b[j], conv_ln_g[j], conv_ln_b[j])
            ya, yb, w_out = conv, att, ev_w_out[j]
        else:
            spatial, zr, zi = _odd_in_proj(h, sc1, sh1, od_w_in[j].astype(BF16), gmlp_ln_g[j], gmlp_ln_b[j],
                                           gmlp_ws[j], gmlp_bs[j], four_ln_g[j], four_ln_b[j])
            ya, yb, w_out = spatial, _length_dft_real(zr, zi), od_w_out[j]
        h1, top_e, gate, pos, counts = _post_mix(ya, yb, w_out.astype(BF16), h, g1, ln1_g[layer], ln1_b[layer],
                                                 sc2, sh2, router_w[layer], router_b[layer], alpha)
        h = _moe_and_norm(h1, top_e[:, :TOP_K], gate, pos[:, :TOP_K], counts[0, :N_EXPERTS], sc2, sh2, g2,
                          w_gate_up[layer].astype(BF16), b_gate_up[layer], w_down[layer].astype(BF16),
                          b_down[layer], ln2_g[layer], ln2_b[layer], alpha)
    return h
```

```python
import functools
import math

import jax
import jax.numpy as jnp
from jax import lax
from jax.experimental import pallas as pl
from jax.experimental.pallas import tpu as pltpu

F32 = jnp.float32
BF16 = jnp.bfloat16
I32 = jnp.int32
HIGHEST = lax.Precision.HIGHEST

LN_EPS = 1e-5
GRID_W = 64
HEAD_DIM = 64
HEAD_V = 128
N_HEADS = 4
QK_W = N_HEADS * 2 * HEAD_DIM
V_W = N_HEADS * HEAD_V
CONV_W = 512
CONV_TAPS = 31
CONV_HALO = 16
ROPE_BASE = 10000.0
CHUNK = 128
N_GROUPS = 4
GROUP_W = 128
N_EXPERTS = 32
TOP_K = 4
SWIGLU_LIMIT = 7.0
SWIGLU_ALPHA = 1.702
LANES = 128
SUBLANES = 8
EXPERT_ROWS = 256
ISSUE_UNROLL = 8
NEG_BIG = -1e30
VMEM_LIMIT = 56 * 1024 * 1024


def _params(n_axes):
    return pltpu.CompilerParams(dimension_semantics=("arbitrary",) * n_axes,
                                vmem_limit_bytes=VMEM_LIMIT)


def _pick(n, candidates):
    for c in candidates:
        if n % c == 0:
            return c
    return n


def _layer_norm(r, g, b):
    mu = jnp.mean(r, axis=-1, keepdims=True)
    d = r - mu
    var = jnp.mean(d * d, axis=-1, keepdims=True)
    return d * lax.rsqrt(var + LN_EPS) * g + b


def _sigmoid(x):
    return 1.0 / (1.0 + jnp.exp(-x))


def _mod_kernel(cs_ref, w_ref, b_ref, o_ref):
    cs = cs_ref[...]
    a = cs * _sigmoid(cs)
    o_ref[0] = jnp.dot(a, w_ref[0], precision=HIGHEST, preferred_element_type=F32) + b_ref[0]


def _modulation(c, c_ctx, w_mod, b_mod):
    depth, d, n = w_mod.shape
    b = c.shape[0]
    cs = jnp.concatenate([c, c_ctx[None], jnp.zeros((SUBLANES - b - 1, d), F32)], 0)
    tn = _pick(n, (1536, 1024, 512))
    return pl.pallas_call(
        _mod_kernel,
        grid=(depth, n // tn),
        in_specs=[pl.BlockSpec((SUBLANES, d), lambda l, j: (0, 0)),
                  pl.BlockSpec((1, d, tn), lambda l, j: (l, 0, j)),
                  pl.BlockSpec((1, 1, tn), lambda l, j: (l, 0, j))],
        out_specs=pl.BlockSpec((1, SUBLANES, tn), lambda l, j: (l, 0, j)),
        out_shape=jax.ShapeDtypeStruct((depth, SUBLANES, n), F32),
        compiler_params=_params(2),
        name="modulation",
    )(cs, w_mod, b_mod.reshape(depth, 1, n))


def _even_in_kernel(x_ref, sc_ref, sh_ref, w_ref, cos_ref, sin_ref, q_ref, k_ref, v_ref, a_ref):
    u = (x_ref[0] * (1.0 + sc_ref[0]) + sh_ref[0]).astype(BF16)
    z = jnp.dot(u, w_ref[...], preferred_element_type=F32)
    cosv = cos_ref[...]
    sinv = sin_ref[...]
    lane = lax.broadcasted_iota(I32, cosv.shape, 1)
    first = (lane % 32) < 16

    def rope(t):
        partner = jnp.where(first, pltpu.roll(t, LANES - 16, 1), pltpu.roll(t, 16, 1))
        return t * cosv + partner * sinv

    for j in range(QK_W // LANES):
        sl = slice(j * LANES, (j + 1) * LANES)
        rq = rope(z[:, sl]) * (HEAD_DIM ** -0.5 * math.log2(math.e))
        for c in range(2):
            q_ref[c, 0, :, sl] = jnp.where(lane // HEAD_DIM == c, rq, 0.0).astype(BF16)
        k_ref[0, :, sl] = rope(z[:, QK_W + j * LANES:QK_W + (j + 1) * LANES]).astype(BF16)
    v_ref[0] = z[:, 2 * QK_W:2 * QK_W + V_W].astype(BF16)
    a0 = 2 * QK_W + V_W
    a_ref[0] = z[:, a0:a0 + CONV_W] * _sigmoid(z[:, a0 + CONV_W:a0 + 2 * CONV_W])


def _rope_tables(length):
    rows = length // GRID_W
    row = jnp.repeat(jnp.arange(rows, dtype=F32), GRID_W)
    col = jnp.tile(jnp.arange(GRID_W, dtype=F32), rows)
    n_freq = HEAD_DIM // 4
    inv_freq = ROPE_BASE ** (-jnp.arange(n_freq, dtype=F32) / n_freq)
    ar = row[:, None] * inv_freq
    ac = col[:, None] * inv_freq
    cos64 = jnp.concatenate([jnp.cos(ar), jnp.cos(ar), jnp.cos(ac), jnp.cos(ac)], 1)
    sin64 = jnp.concatenate([-jnp.sin(ar), jnp.sin(ar), -jnp.sin(ac), jnp.sin(ac)], 1)
    return jnp.tile(cos64, (1, LANES // HEAD_DIM)), jnp.tile(sin64, (1, LANES // HEAD_DIM))


def _even_in_proj(x, sc, sh, w_bf, cos_t, sin_t):
    b, length, d = x.shape
    n = w_bf.shape[1]
    tm = _pick(length, (512, 256, 128))
    row = lambda bi, i: (bi, i, 0)
    vec = lambda bi, i: (bi, 0, 0)
    return pl.pallas_call(
        _even_in_kernel,
        grid=(b, length // tm),
        in_specs=[pl.BlockSpec((1, tm, d), row),
                  pl.BlockSpec((1, 1, d), vec),
                  pl.BlockSpec((1, 1, d), vec),
                  pl.BlockSpec((d, n), lambda bi, i: (0, 0)),
                  pl.BlockSpec((tm, LANES), lambda bi, i: (i, 0)),
                  pl.BlockSpec((tm, LANES), lambda bi, i: (i, 0))],
        out_specs=[pl.BlockSpec((2, 1, tm, QK_W), lambda bi, i: (0, bi, i, 0)), pl.BlockSpec((1, tm, QK_W), row),
                   pl.BlockSpec((1, tm, V_W), row), pl.BlockSpec((1, tm, CONV_W), row)],
        out_shape=[jax.ShapeDtypeStruct((2, b, length, QK_W), BF16), jax.ShapeDtypeStruct((b, length, QK_W), BF16),
                   jax.ShapeDtypeStruct((b, length, V_W), BF16), jax.ShapeDtypeStruct((b, length, CONV_W), F32)],
        compiler_params=_params(2),
        name="even_in_proj",
    )(x, sc, sh, w_bf, cos_t, sin_t)


def _modmm_kernel(x_ref, sc_ref, sh_ref, w_ref, o_ref):
    u = (x_ref[0] * (1.0 + sc_ref[0]) + sh_ref[0]).astype(BF16)
    o_ref[0] = jnp.dot(u, w_ref[...], preferred_element_type=F32).astype(o_ref.dtype)


def _mod_matmul(x, sc, sh, w_bf, out_dtype):
    b, rows, d = x.shape
    n = w_bf.shape[1]
    tm = _pick(rows, (512, 256, 128))
    return pl.pallas_call(
        _modmm_kernel,
        grid=(b, rows // tm),
        in_specs=[pl.BlockSpec((1, tm, d), lambda bi, i: (bi, i, 0)),
                  pl.BlockSpec((1, 1, d), lambda bi, i: (bi, 0, 0)),
                  pl.BlockSpec((1, 1, d), lambda bi, i: (bi, 0, 0)),
                  pl.BlockSpec((d, n), lambda bi, i: (0, 0))],
        out_specs=pl.BlockSpec((1, tm, n), lambda bi, i: (bi, i, 0)),
        out_shape=jax.ShapeDtypeStruct((b, rows, n), out_dtype),
        compiler_params=_params(2),
        name="context_kv_proj",
    )(x, sc, sh, w_bf)


VT_ROWS = HEAD_V + 16


def _attn_kernel(lq1_ref, lk1_ref, lq2_ref, lk2_ref, g_ref, q_ref, k_ref, vt_ref, o_ref,
                 s0_sc, s1_sc, p0_sc, p1_sc, acc_sc, *, lam_init, kc):
    mq = q_ref.shape[2]
    n = k_ref.shape[1] // kc
    s_sc = (s0_sc, s1_sc)
    p_sc = (p0_sc, p1_sc)

    def scores(j, slot):
        kj = k_ref[0, pl.ds(pl.multiple_of(j * kc, kc), kc), :]
        for c in range(2):
            s_sc[slot][c] = lax.dot_general(kj, q_ref[c, 0], (((1,), (1,)), ((), ())),
                                            preferred_element_type=F32)

    def softmax(slot, m):
        m_out, alpha = [], []
        for c in range(2):
            s = s_sc[slot][c]
            m_new = jnp.maximum(m[c], jnp.max(s, axis=0, keepdims=True))
            p_sc[slot][c] = jnp.exp2((s - m_new).astype(BF16))
            m_out.append(m_new)
            alpha.append(jnp.exp2(m[c] - m_new))
        return tuple(m_out), tuple(alpha)

    def accumulate(j, slot, alpha):
        vt = vt_ref[0, 0, j]
        for c in range(2):
            acc_sc[c] = alpha[c] * acc_sc[c] + jnp.dot(vt, p_sc[slot][c], preferred_element_type=F32)

    def step(j, slot, m, alpha):
        scores(j + 2, slot)
        m, alpha_next = softmax(1 - slot, m)
        accumulate(j, slot, alpha)
        return m, alpha_next

    acc_sc[...] = jnp.zeros(acc_sc.shape, F32)
    m = (jnp.full((1, mq), NEG_BIG, F32),) * 2
    scores(0, 0)
    m, alpha = softmax(0, m)
    scores(1, 1)

    def pair(t, carry):
        m, alpha = step(2 * t, 0, *carry)
        return step(2 * t + 1, 1, m, alpha)

    n_steps = n - 2
    m, alpha = lax.fori_loop(0, n_steps // 2, pair, (m, alpha))
    if n_steps % 2:
        m, alpha = step(n_steps - 1, 0, m, alpha)
    m, alpha_last = softmax((n - 1) % 2, m)
    accumulate(n - 2, (n - 2) % 2, alpha)
    accumulate(n - 1, (n - 1) % 2, alpha_last)
    acc0 = acc_sc[0]
    acc1 = acc_sc[1]
    lam = (jnp.exp(jnp.sum(lq1_ref[...] * lk1_ref[...], keepdims=True))
           - jnp.exp(jnp.sum(lq2_ref[...] * lk2_ref[...], keepdims=True)) + lam_init)
    o = (acc0[:HEAD_V] / acc0[HEAD_V:HEAD_V + 1]
         - lam * (acc1[:HEAD_V] / acc1[HEAD_V:HEAD_V + 1]))
    ms = jnp.mean(o * o, axis=0, keepdims=True)
    o = o * lax.rsqrt(ms + LN_EPS) * g_ref[...] * (1.0 - lam_init)
    o_ref[0] = o.T.astype(o_ref.dtype)


def _diff_attention(q, k_all, v_all, lq1, lk1, lq2, lk2, norm_g, lam_init):
    _, b, length, _ = q.shape
    n_keys = k_all.shape[1]
    mq = _pick(length, (512, 256, 128))
    kc = _pick(n_keys, (256, 128))
    nc = n_keys // kc
    assert nc >= 3
    vt = v_all.reshape(b, nc, kc, N_HEADS, HEAD_V).transpose(0, 3, 1, 4, 2)
    extra = jnp.zeros((b, N_HEADS, nc, VT_ROWS - HEAD_V, kc), BF16).at[:, :, :, 0, :].set(1.0)
    vt = jnp.concatenate([vt, extra], 3)
    small = lambda bi, h, i: (0, 0)
    return pl.pallas_call(
        functools.partial(_attn_kernel, lam_init=lam_init, kc=kc),
        grid=(b, N_HEADS, length // mq),
        in_specs=[pl.BlockSpec((1, HEAD_DIM), small), pl.BlockSpec((1, HEAD_DIM), small),
                  pl.BlockSpec((1, HEAD_DIM), small), pl.BlockSpec((1, HEAD_DIM), small),
                  pl.BlockSpec((HEAD_V, 1), small),
                  pl.BlockSpec((2, 1, mq, HEAD_V), lambda bi, h, i: (0, bi, i, h)),
                  pl.BlockSpec((1, n_keys, HEAD_V), lambda bi, h, i: (bi, 0, h)),
                  pl.BlockSpec((1, 1, nc, VT_ROWS, kc), lambda bi, h, i: (bi, h, 0, 0, 0))],
        out_specs=pl.BlockSpec((1, mq, HEAD_V), lambda bi, h, i: (bi, i, h)),
        out_shape=jax.ShapeDtypeStruct((b, length, V_W), BF16),
        scratch_shapes=[pltpu.VMEM((2, kc, mq), F32), pltpu.VMEM((2, kc, mq), F32),
                        pltpu.VMEM((2, kc, mq), BF16), pltpu.VMEM((2, kc, mq), BF16),
                        pltpu.VMEM((2, VT_ROWS, mq), F32)],
        compiler_params=_params(3),
        name="diff_attention",
    )(lq1[None], lk1[None], lq2[None], lk2[None], norm_g[:, None], q, k_all, vt)


def _conv_kernel(prev_ref, cur_ref, next_ref, w_ref, b_ref, g_ref, beta_ref, o_ref, win, *, row_chunk):
    i = pl.program_id(1)
    tc = cur_ref.shape[1]
    win[CONV_HALO:CONV_HALO + tc, :] = cur_ref[0]
    win[0:CONV_HALO, :] = jnp.where(i > 0, prev_ref[0], 0.0)
    win[CONV_HALO + tc:, :] = jnp.where(i < pl.num_programs(1) - 1, next_ref[0], 0.0)
    first_tap = CONV_HALO - CONV_TAPS // 2
    for c0 in range(0, tc, row_chunk):
        acc = jnp.zeros((row_chunk, CONV_W), F32)
        for t in range(CONV_TAPS):
            acc = acc + w_ref[t:t + 1, :] * win[c0 + first_tap + t:c0 + first_tap + t + row_chunk, :]
        y = _layer_norm(acc + b_ref[...], g_ref[...], beta_ref[...])
        o_ref[0, c0:c0 + row_chunk, :] = (y * _sigmoid(y)).astype(o_ref.dtype)


def _conformer_conv(a, conv_w, conv_b, ln_g, ln_b):
    b, length, w = a.shape
    tc = _pick(length, (512, 256, 128))
    hb = tc // CONV_HALO
    n_halo = length // CONV_HALO
    w_pad = jnp.concatenate([conv_w, jnp.zeros((1, w), F32)], 0)
    vec = lambda bi, i: (0, 0)
    return pl.pallas_call(
        functools.partial(_conv_kernel, row_chunk=64),
        grid=(b, length // tc),
        in_specs=[pl.BlockSpec((1, CONV_HALO, w), lambda bi, i: (bi, jnp.maximum(i * hb - 1, 0), 0)),
                  pl.BlockSpec((1, tc, w), lambda bi, i: (bi, i, 0)),
                  pl.BlockSpec((1, CONV_HALO, w), lambda bi, i: (bi, jnp.minimum((i + 1) * hb, n_halo - 1), 0)),
                  pl.BlockSpec((CONV_TAPS + 1, w), vec),
                  pl.BlockSpec((1, w), vec), pl.BlockSpec((1, w), vec), pl.BlockSpec((1, w), vec)],
        out_specs=pl.BlockSpec((1, tc, w), lambda bi, i: (bi, i, 0)),
        out_shape=jax.ShapeDtypeStruct((b, length, w), BF16),
        scratch_shapes=[pltpu.VMEM((tc + 2 * CONV_HALO, w), F32)],
        compiler_params=_params(2),
        name="conformer_conv",
    )(a, a, a, w_pad, conv_b[None], ln_g[None], ln_b[None])


def _post_mix_kernel(ya_ref, yb_ref, wa_ref, wb_ref, x_ref, g1_ref, lng_ref, lnb_ref, sc_ref, sh_ref,
                     rw_ref, rb_ref, h_ref, e_ref, gate_ref, pos_ref, cnt_ref, carry, *, alpha):
    first_step = jnp.logical_and(pl.program_id(0) == 0, pl.program_id(1) == 0)

    @pl.when(first_step)
    def _():
        carry[...] = jnp.zeros(carry.shape, F32)

    y = (jnp.dot(ya_ref[0].astype(BF16), wa_ref[...], preferred_element_type=F32)
         + jnp.dot(yb_ref[0].astype(BF16), wb_ref[...], preferred_element_type=F32))
    h = _layer_norm(alpha * x_ref[0] + g1_ref[0] * y, lng_ref[...], lnb_ref[...])
    h_ref[0] = h
    u2 = h * (1.0 + sc_ref[0]) + sh_ref[0]
    logits = jnp.dot(u2, rw_ref[...], precision=HIGHEST, preferred_element_type=F32) + rb_ref[...]

    tm = logits.shape[0]
    lane = lax.broadcasted_iota(I32, logits.shape, 1).astype(F32)
    vals, idxs = [], []
    lg = logits
    for _ in range(TOP_K):
        mx = jnp.max(lg, axis=1, keepdims=True)
        idx = jnp.min(jnp.where(lg == mx, lane, float(LANES)), axis=1, keepdims=True)
        vals.append(mx)
        idxs.append(idx)
        lg = jnp.where(lane == idx, -3e38, lg)
    exps = [jnp.exp(v - vals[0]) for v in vals]
    den = exps[0] + exps[1] + exps[2] + exps[3]

    hot = jnp.zeros(logits.shape, F32)
    for idx in idxs:
        hot = hot + (lane == idx).astype(F32)
    r_i = lax.broadcasted_iota(I32, (tm, tm), 0)
    c_i = lax.broadcasted_iota(I32, (tm, tm), 1)
    tri = (c_i < r_i).astype(BF16)
    before = jnp.dot(tri, hot.astype(BF16), preferred_element_type=F32) + carry[...]
    e_out = jnp.zeros(logits.shape, F32)
    g_out = jnp.zeros(logits.shape, F32)
    p_out = jnp.zeros(logits.shape, F32)
    for j in range(TOP_K):
        pos_j = jnp.sum(jnp.where(lane == idxs[j], before, 0.0), axis=1, keepdims=True)
        sel = lane == float(j)
        e_out = jnp.where(sel, idxs[j], e_out)
        g_out = jnp.where(sel, exps[j] / den, g_out)
        p_out = jnp.where(sel, pos_j, p_out)
    e_ref[...] = e_out.astype(I32)
    gate_ref[...] = g_out
    pos_ref[...] = p_out.astype(I32)
    new_carry = carry[...] + jnp.sum(hot, axis=0, keepdims=True)
    carry[...] = new_carry
    cnt_ref[...] = jnp.broadcast_to(new_carry, cnt_ref.shape).astype(I32)


def _post_mix(ya, yb, w_out_bf, x, g1, ln_g, ln_b, sc2, sh2, router_w, router_b, alpha):
    b, length, d = x.shape
    wa = ya.shape[2]
    tm = _pick(length, (512, 256, 128))
    n_tok = b * length
    rw = jnp.concatenate([router_w, jnp.zeros((d, LANES - N_EXPERTS), F32)], 1)
    rb = jnp.concatenate([router_b, jnp.full((LANES - N_EXPERTS,), NEG_BIG, F32)])[None]
    row = lambda bi, i: (bi, i, 0)
    vec = lambda bi, i: (bi, 0, 0)
    const = lambda bi, i: (0, 0)
    nt = length // tm
    tok = lambda bi, i: (bi * nt + i, 0)
    return pl.pallas_call(
        functools.partial(_post_mix_kernel, alpha=alpha),
        grid=(b, nt),
        in_specs=[pl.BlockSpec((1, tm, wa), row), pl.BlockSpec((1, tm, d - wa), row),
                  pl.BlockSpec((wa, d), const), pl.BlockSpec((d - wa, d), const),
                  pl.BlockSpec((1, tm, d), row), pl.BlockSpec((1, 1, d), vec),
                  pl.BlockSpec((1, d), const), pl.BlockSpec((1, d), const),
                  pl.BlockSpec((1, 1, d), vec), pl.BlockSpec((1, 1, d), vec),
                  pl.BlockSpec((d, LANES), const), pl.BlockSpec((1, LANES), const)],
        out_specs=[pl.BlockSpec((1, tm, d), row), pl.BlockSpec((tm, LANES), tok),
                   pl.BlockSpec((tm, LANES), tok), pl.BlockSpec((tm, LANES), tok),
                   pl.BlockSpec((SUBLANES, LANES), const)],
        out_shape=[jax.ShapeDtypeStruct((b, length, d), F32), jax.ShapeDtypeStruct((n_tok, LANES), I32),
                   jax.ShapeDtypeStruct((n_tok, LANES), F32), jax.ShapeDtypeStruct((n_tok, LANES), I32),
                   jax.ShapeDtypeStruct((SUBLANES, LANES), I32)],
        scratch_shapes=[pltpu.VMEM((1, LANES), F32)],
        compiler_params=_params(2),
        name="out_proj_norm_router",
    )(ya, yb, w_out_bf[:wa], w_out_bf[wa:], x, g1, ln_g[None], ln_b[None], sc2, sh2, rw, rb)


def _dispatch_kernel(zstart_ref, zvalid_ref, dest_ref, h_ref, sc_ref, sh_ref, xs_ref, u_sc, zero_sc, sem, zsem):
    tm = h_ref.shape[0]

    @pl.when(pl.program_id(0) == 0)
    def _():
        zero_sc[...] = jnp.zeros(zero_sc.shape, F32)

        def fill(e, c):
            @pl.when(zvalid_ref[e] > 0)
            def _():
                start = pl.multiple_of(zstart_ref[e], SUBLANES)
                pltpu.make_async_copy(zero_sc, xs_ref.at[pl.ds(start, EXPERT_ROWS)], zsem).start()
            return c

        lax.fori_loop(0, N_EXPERTS, fill, 0)

        def drain(e, c):
            @pl.when(zvalid_ref[e] > 0)
            def _():
                pltpu.make_async_copy(zero_sc, xs_ref.at[pl.ds(0, EXPERT_ROWS)], zsem).wait()
            return c

        lax.fori_loop(0, N_EXPERTS, drain, 0)

    u_sc[...] = h_ref[...] * (1.0 + sc_ref[0]) + sh_ref[0]

    def issue(g, c):
        for rr in range(ISSUE_UNROLL):
            r = g * ISSUE_UNROLL + rr
            for j in range(TOP_K):
                d = dest_ref[0, 0, r * TOP_K + j]
                pltpu.make_async_copy(u_sc.at[pl.ds(r, 1)], xs_ref.at[pl.ds(d, 1)], sem).start()
        return c

    lax.fori_loop(0, tm // ISSUE_UNROLL, issue, 0)
    rows = xs_ref.at[pl.ds(0, tm * TOP_K)]
    pltpu.make_async_copy(rows, rows, sem).wait()


def _dispatch(h_flat, sc2, sh2, dest, zstart, zvalid, n_rows, tokens_per_batch):
    n_tok, d = h_flat.shape
    tm = _pick(tokens_per_batch, (512, 256, 128))
    per_b = tokens_per_batch // tm
    return pl.pallas_call(
        _dispatch_kernel,
        grid_spec=pltpu.PrefetchScalarGridSpec(
            num_scalar_prefetch=2,
            grid=(n_tok // tm,),
            in_specs=[pl.BlockSpec((1, 1, tm * TOP_K), lambda i, zs, zv: (i, 0, 0), memory_space=pltpu.SMEM),
                      pl.BlockSpec((tm, d), lambda i, zs, zv: (i, 0)),
                      pl.BlockSpec((1, 1, d), lambda i, zs, zv: (i // per_b, 0, 0)),
                      pl.BlockSpec((1, 1, d), lambda i, zs, zv: (i // per_b, 0, 0))],
            out_specs=pl.BlockSpec(memory_space=pl.ANY),
            scratch_shapes=[pltpu.VMEM((tm, d), F32), pltpu.VMEM((EXPERT_ROWS, d), F32),
                            pltpu.SemaphoreType.DMA(()), pltpu.SemaphoreType.DMA(())]),
        out_shape=jax.ShapeDtypeStruct((n_rows, d), F32),
        compiler_params=_params(1),
        name="moe_dispatch",
    )(zstart, zvalid, dest.reshape(n_tok // tm, 1, tm * TOP_K), h_flat, sc2, sh2)


def _expert_kernel(blk_e_ref, n_used_ref, x_ref, wgu_ref, bgu_ref, wd_ref, bd_ref, o_ref, wgu_bf, wd_bf):
    i = pl.program_id(0)
    new_expert = jnp.logical_or(i == 0, blk_e_ref[i] != blk_e_ref[jnp.maximum(i - 1, 0)])

    @pl.when(new_expert)
    def _():
        wgu_bf[...] = wgu_ref[0].astype(BF16)
        wd_bf[...] = wd_ref[0].astype(BF16)

    @pl.when(i < n_used_ref[0])
    def _():
        d_e = wd_ref.shape[1]
        gu = jnp.dot(x_ref[...].astype(BF16), wgu_bf[...], preferred_element_type=F32) + bgu_ref[0]
        x_glu = jnp.minimum(gu[:, :d_e], SWIGLU_LIMIT)
        x_lin = jnp.clip(gu[:, d_e:], -SWIGLU_LIMIT, SWIGLU_LIMIT)
        act = x_glu * _sigmoid(SWIGLU_ALPHA * x_glu) * (x_lin + 1.0)
        o_ref[...] = jnp.dot(act.astype(BF16), wd_bf[...], preferred_element_type=F32) + bd_ref[0]


def _experts(xs, blk_e, n_used, wgu, bgu, wd, bd, layer):
    n_rows, d = xs.shape
    depth, n_e, _, n_gu = wgu.shape
    n_blk = n_rows // EXPERT_ROWS
    rows = lambda i, be, nu: (jnp.minimum(i, nu[0] - 1), 0)
    first = layer * n_e
    by_e = lambda i, be, nu: (first + be[i], 0, 0)
    n_e = depth * n_e
    wgu = wgu.reshape(n_e, d, n_gu)
    wd = wd.reshape(n_e, n_gu // 2, d)
    return pl.pallas_call(
        _expert_kernel,
        grid_spec=pltpu.PrefetchScalarGridSpec(
            num_scalar_prefetch=2,
            grid=(n_blk,),
            in_specs=[pl.BlockSpec((EXPERT_ROWS, d), rows),
                      pl.BlockSpec((1, d, n_gu), by_e), pl.BlockSpec((1, 1, n_gu), by_e),
                      pl.BlockSpec((1, n_gu // 2, d), by_e), pl.BlockSpec((1, 1, d), by_e)],
            out_specs=pl.BlockSpec((EXPERT_ROWS, d), rows),
            scratch_shapes=[pltpu.VMEM((d, n_gu), BF16), pltpu.VMEM((n_gu // 2, d), BF16)]),
        out_shape=jax.ShapeDtypeStruct((n_rows, d), F32),
        compiler_params=_params(1),
        name="moe_experts",
    )(blk_e, n_used, xs, wgu, bgu.reshape(n_e, 1, n_gu), wd, bd.reshape(n_e, 1, d))


def _combine_kernel(dest_ref, dest_next_ref, y_ref, h_ref, gate_ref, g2_ref, lng_ref, lnb_ref, o_ref,
                    buf, sem, *, alpha):
    i = pl.program_id(0)
    tm = h_ref.shape[0]

    def fetch(idx_ref, slot):
        def issue(g, c):
            for rr in range(ISSUE_UNROLL):
                r = g * ISSUE_UNROLL + rr
                for j in range(TOP_K):
                    d = idx_ref[0, 0, r * TOP_K + j]
                    pltpu.make_async_copy(y_ref.at[pl.ds(d, 1)], buf.at[slot, j, pl.ds(r, 1)], sem.at[slot]).start()
            return c

        lax.fori_loop(0, tm // ISSUE_UNROLL, issue, 0)

    slot = i % 2

    @pl.when(i == 0)
    def _():
        fetch(dest_ref, 0)

    @pl.when(i + 1 < pl.num_programs(0))
    def _():
        fetch(dest_next_ref, 1 - slot)

    pltpu.make_async_copy(buf.at[slot], buf.at[slot], sem.at[slot]).wait()
    gate = gate_ref[...]
    y2 = gate[:, 0:1] * buf[slot, 0]
    for j in range(1, TOP_K):
        y2 = y2 + gate[:, j:j + 1] * buf[slot, j]
    o_ref[...] = _layer_norm(alpha * h_ref[...] + g2_ref[0] * y2, lng_ref[...], lnb_ref[...])


def _combine(y_rows, dest, h_flat, gate, g2, ln_g, ln_b, alpha, tokens_per_batch):
    n_tok, d = h_flat.shape
    tm = _pick(tokens_per_batch, (256, 128))
    per_b = tokens_per_batch // tm
    n_tiles = n_tok // tm
    dest3 = dest.reshape(n_tiles, 1, tm * TOP_K)
    return pl.pallas_call(
        functools.partial(_combine_kernel, alpha=alpha),
        grid=(n_tiles,),
        in_specs=[pl.BlockSpec((1, 1, tm * TOP_K), lambda i: (i, 0, 0), memory_space=pltpu.SMEM),
                  pl.BlockSpec((1, 1, tm * TOP_K), lambda i: (jnp.minimum(i + 1, n_tiles - 1), 0, 0),
                               memory_space=pltpu.SMEM),
                  pl.BlockSpec(memory_space=pl.ANY),
                  pl.BlockSpec((tm, d), lambda i: (i, 0)),
                  pl.BlockSpec((tm, LANES), lambda i: (i, 0)),
                  pl.BlockSpec((1, 1, d), lambda i: (i // per_b, 0, 0)),
                  pl.BlockSpec((1, d), lambda i: (0, 0)), pl.BlockSpec((1, d), lambda i: (0, 0))],
        out_specs=pl.BlockSpec((tm, d), lambda i: (i, 0)),
        out_shape=jax.ShapeDtypeStruct((n_tok, d), F32),
        scratch_shapes=[pltpu.VMEM((2, TOP_K, tm, d), F32), pltpu.SemaphoreType.DMA((2,))],
        compiler_params=_params(1),
        name="moe_combine_norm",
    )(dest3, dest3, y_rows, h_flat, gate, g2, ln_g[None], ln_b[None])


def _moe_and_norm(h1, top_e, gate, pos, counts, sc2, sh2, g2, wgu, bgu, wd, bd, layer, ln_g, ln_b, alpha):
    b, length, d = h1.shape
    n_tok = b * length
    n_blk = n_tok * TOP_K // EXPERT_ROWS + N_EXPERTS
    padded = (counts + EXPERT_ROWS - 1) // EXPERT_ROWS * EXPERT_ROWS
    pad_end = jnp.cumsum(padded)
    offset = pad_end - padded
    n_used = (pad_end[-1:] // EXPERT_ROWS).astype(I32)
    blk_start = jnp.arange(n_blk, dtype=I32) * EXPERT_ROWS
    blk_e = jnp.minimum(jnp.sum(pad_end[None, :] <= blk_start[:, None], axis=1), N_EXPERTS - 1).astype(I32)
    dest = (offset[top_e] + pos).astype(I32)
    zstart = jnp.maximum(pad_end - EXPERT_ROWS, 0).astype(I32)
    zvalid = (counts > 0).astype(I32)

    h_flat = h1.reshape(n_tok, d)
    xs = _dispatch(h_flat, sc2, sh2, dest, zstart, zvalid, n_blk * EXPERT_ROWS, length)
    y_rows = _experts(xs, blk_e, n_used, wgu, bgu, wd, bd, layer)
    out = _combine(y_rows, dest, h_flat, gate, g2, ln_g, ln_b, alpha, length)
    return out.reshape(b, length, d)


def _odd_in_kernel(x_ref, sc_ref, sh_ref, w_ref, glg_ref, glb_ref, ws_ref, bs_ref, flg_ref, flb_ref, dft_ref,
                   sp_ref, zr_ref, zi_ref):
    u = (x_ref[0] * (1.0 + sc_ref[0]) + sh_ref[0]).astype(BF16)
    z = jnp.dot(u, w_ref[...], preferred_element_type=F32)
    tm = z.shape[0]
    gw = N_GROUPS * GROUP_W
    ug = jax.nn.gelu(z[:, :gw])
    vn = _layer_norm(jax.nn.gelu(z[:, gw:2 * gw]), glg_ref[...], glb_ref[...]).astype(BF16)
    f = z[:, 2 * gw:]
    for g in range(N_GROUPS):
        cols = slice(g * GROUP_W, (g + 1) * GROUP_W)
        for c0 in range(0, tm, CHUNK):
            rows = slice(c0, c0 + CHUNK)
            sv = jnp.dot(ws_ref[g], vn[rows, cols], preferred_element_type=F32) + bs_ref[:, cols]
            sp_ref[0, rows, cols] = (ug[rows, cols] * sv).astype(sp_ref.dtype)
        fn = _layer_norm(f[:, cols], flg_ref[:, cols], flb_ref[:, cols]).astype(BF16)
        zz = jnp.dot(fn, dft_ref[...], preferred_element_type=F32)
        zr_ref[0, :, cols] = zz[:, :GROUP_W]
        zi_ref[0, :, cols] = zz[:, GROUP_W:]


def _odd_in_proj(x, sc, sh, w_bf, gln_g, gln_b, ws, bs, fln_g, fln_b):
    b, length, d = x.shape
    n = w_bf.shape[1]
    gw = N_GROUPS * GROUP_W
    tm = _pick(length, (512, 256, 128))
    kk = jnp.arange(GROUP_W, dtype=I32)
    ang = (2.0 * math.pi / GROUP_W) * ((kk[:, None] * kk[None, :]) % GROUP_W).astype(F32)
    dft = jnp.concatenate([jnp.cos(ang), -jnp.sin(ang)], 1).astype(BF16)
    bs_exp = jnp.repeat(bs.T, GROUP_W, axis=1)
    row = lambda bi, i: (bi, i, 0)
    vec = lambda bi, i: (bi, 0, 0)
    const2 = lambda bi, i: (0, 0)
    return pl.pallas_call(
        _odd_in_kernel,
        grid=(b, length // tm),
        in_specs=[pl.BlockSpec((1, tm, d), row), pl.BlockSpec((1, 1, d), vec), pl.BlockSpec((1, 1, d), vec),
                  pl.BlockSpec((d, n), const2),
                  pl.BlockSpec((1, gw), const2), pl.BlockSpec((1, gw), const2),
                  pl.BlockSpec((N_GROUPS, CHUNK, CHUNK), lambda bi, i: (0, 0, 0)),
                  pl.BlockSpec((CHUNK, gw), const2),
                  pl.BlockSpec((1, gw), const2), pl.BlockSpec((1, gw), const2),
                  pl.BlockSpec((GROUP_W, 2 * GROUP_W), const2)],
        out_specs=[pl.BlockSpec((1, tm, gw), row), pl.BlockSpec((1, tm, gw), row), pl.BlockSpec((1, tm, gw), row)],
        out_shape=[jax.ShapeDtypeStruct((b, length, gw), BF16), jax.ShapeDtypeStruct((b, length, gw), F32),
                   jax.ShapeDtypeStruct((b, length, gw), F32)],
        compiler_params=_params(2),
        name="odd_in_proj",
    )(x, sc, sh, w_bf, gln_g[None], gln_b[None], ws.astype(BF16), bs_exp, fln_g[None], fln_b[None], dft)


def _fft_a_kernel(zr_ref, zi_ref, m_ref, ar_ref, ai_ref):
    l1, nb, w = zr_ref.shape[1:]
    zr = zr_ref[0].reshape(l1 * nb, w).astype(BF16)
    zi = zi_ref[0].reshape(l1 * nb, w).astype(BF16)
    a = jnp.dot(m_ref[...], jnp.concatenate([zr, zi], 0), preferred_element_type=F32)
    ar_ref[0] = a[:l1 * nb].reshape(l1, nb, w)
    ai_ref[0] = a[l1 * nb:].reshape(l1, nb, w)


def _fft_b_kernel(ar_ref, ai_ref, m_ref, o_ref, *, norm):
    nb, l2, w = ar_ref.shape[1:]
    ar = ar_ref[0].reshape(nb * l2, w).astype(BF16)
    ai = ai_ref[0].reshape(nb * l2, w).astype(BF16)
    y = jnp.dot(m_ref[0], jnp.concatenate([ar, ai], 0), preferred_element_type=F32)
    o_ref[0] = (y * norm).reshape(l2, nb, w)


def _length_dft_real(zr, zi):
    b, length, w = zr.shape
    l2 = CHUNK
    l1 = length // l2
    nb = SUBLANES
    eye = jnp.eye(nb, dtype=F32)
    k1 = jnp.arange(l1, dtype=I32)
    ang_a = (2.0 * math.pi / l1) * ((k1[:, None] * k1[None, :]) % l1).astype(F32)
    ca = jnp.kron(jnp.cos(ang_a), eye)
    sa = jnp.kron(jnp.sin(ang_a), eye)
    m_a = jnp.concatenate([jnp.concatenate([ca, sa], 1), jnp.concatenate([-sa, ca], 1)], 0).astype(BF16)
    k2 = jnp.arange(l2, dtype=I32)
    phase = (k2[None, None, :] * (k1[None, :, None] + l1 * k2[:, None, None])) % length
    ang_b = (2.0 * math.pi / length) * phase.astype(F32)

    def expand(t):
        t = t.reshape(l2, l1 // nb, nb, l2).transpose(1, 0, 2, 3)
        return (t[:, :, :, None, :] * eye[None, None, :, :, None]).reshape(l1 // nb, l2 * nb, nb * l2)

    m_b = jnp.concatenate([expand(jnp.cos(ang_b)), expand(jnp.sin(ang_b))], 2).astype(BF16)

    zr4 = zr.reshape(b, l1, l2, w)
    zi4 = zi.reshape(b, l1, l2, w)
    blk_a = pl.BlockSpec((1, l1, nb, w), lambda bi, i: (bi, 0, i, 0))
    ar, ai = pl.pallas_call(
        _fft_a_kernel,
        grid=(b, l2 // nb),
        in_specs=[blk_a, blk_a, pl.BlockSpec((2 * l1 * nb, 2 * l1 * nb), lambda bi, i: (0, 0))],
        out_specs=[blk_a, blk_a],
        out_shape=[jax.ShapeDtypeStruct((b, l1, l2, w), F32)] * 2,
        compiler_params=_params(2),
        name="fourier_stage_a",
    )(zr4, zi4, m_a)
    blk_b = pl.BlockSpec((1, nb, l2, w), lambda bi, i: (bi, i, 0, 0))
    out = pl.pallas_call(
        functools.partial(_fft_b_kernel, norm=float((length * GROUP_W) ** -0.5)),
        grid=(b, l1 // nb),
        in_specs=[blk_b, blk_b, pl.BlockSpec((1, l2 * nb, 2 * nb * l2), lambda bi, i: (i, 0, 0))],
        out_specs=pl.BlockSpec((1, l2, nb, w), lambda bi, i: (bi, 0, i, 0)),
        out_shape=jax.ShapeDtypeStruct((b, l2, l1, w), F32),
        compiler_params=_params(2),
        name="fourier_stage_b",
    )(ar, ai, m_b)
    return out.reshape(b, length, w)


def kernel(x, c, ctx, c_ctx, w_mod, b_mod, ln1_g, ln1_b, ln2_g, ln2_b, ev_w_in, ev_w_out, conv_w, conv_b, conv_ln_g, conv_ln_b, lam_q1, lam_k1, lam_q2, lam_k2, diff_norm_g, od_w_in, od_w_out, gmlp_ln_g, gmlp_ln_b, gmlp_ws, gmlp_bs, four_ln_g, four_ln_b, router_w, router_b, w_gate_up, b_gate_up, w_down, b_down):
    b, length, d = x.shape
    depth = w_mod.shape[0]
    alpha = float((2 * depth) ** 0.25)
    assert b + 1 <= SUBLANES and length % (CHUNK * SUBLANES) == 0 and length % GRID_W == 0

    mod = _modulation(c, c_ctx, w_mod, b_mod)
    cos_t, sin_t = _rope_tables(length)
    h = x
    for layer in range(depth):
        j = layer // 2
        m = mod[layer]
        sh1, sc1, g1, sh2, sc2, g2 = [m[:b, i * d:(i + 1) * d][:, None, :] for i in range(6)]
        if layer % 2 == 0:
            lam_init = 0.8 - 0.6 * math.exp(-0.3 * layer)
            w_in = ev_w_in[j].astype(BF16)
            q, k, v, a = _even_in_proj(h, sc1, sh1, w_in, cos_t, sin_t)
            csh1 = jnp.broadcast_to(m[b:b + 1, 0:d][:, None, :], (b, 1, d))
            csc1 = jnp.broadcast_to(m[b:b + 1, d:2 * d][:, None, :], (b, 1, d))
            kv_c = _mod_matmul(ctx, csc1, csh1, w_in[:, QK_W:2 * QK_W + V_W], BF16)
            k_all = jnp.concatenate([k, kv_c[..., :QK_W]], 1)
            v_all = jnp.concatenate([v, kv_c[..., QK_W:]], 1)
            att = _diff_attention(q, k_all, v_all, lam_q1[j], lam_k1[j], lam_q2[j], lam_k2[j],
                                  diff_norm_g[j], lam_init)
            conv = _conformer_conv(a, conv_w[j], conv_b[j], conv_ln_g[j], conv_ln_b[j])
            ya, yb, w_out = conv, att, ev_w_out[j]
        else:
            spatial, zr, zi = _odd_in_proj(h, sc1, sh1, od_w_in[j].astype(BF16), gmlp_ln_g[j], gmlp_ln_b[j],
                                           gmlp_ws[j], gmlp_bs[j], four_ln_g[j], four_ln_b[j])
            ya, yb, w_out = spatial, _length_dft_real(zr, zi), od_w_out[j]
        h1, top_e, gate, pos, counts = _post_mix(ya, yb, w_out.astype(BF16), h, g1, ln1_g[layer], ln1_b[layer],
                                                 sc2, sh2, router_w[layer], router_b[layer], alpha)
        h = _moe_and_norm(h1, top_e[:, :TOP_K], gate, pos[:, :TOP_K], counts[0, :N_EXPERTS], sc2, sh2, g2,
                          w_gate_up, b_gate_up, w_down, b_down, layer, ln2_g[layer], ln2_b[layer], alpha)
    return h
```

```python
import functools
import math

import jax
import jax.numpy as jnp
from jax import lax
from jax.experimental import pallas as pl
from jax.experimental.pallas import tpu as pltpu

F32 = jnp.float32
BF16 = jnp.bfloat16
I32 = jnp.int32
HIGHEST = lax.Precision.HIGHEST

LN_EPS = 1e-5
GRID_W = 64
HEAD_DIM = 64
HEAD_V = 128
N_HEADS = 4
QK_W = N_HEADS * 2 * HEAD_DIM
V_W = N_HEADS * HEAD_V
CONV_W = 512
CONV_TAPS = 31
CONV_HALO = 16
ROPE_BASE = 10000.0
CHUNK = 128
N_GROUPS = 4
GROUP_W = 128
N_EXPERTS = 32
TOP_K = 4
SWIGLU_LIMIT = 7.0
SWIGLU_ALPHA = 1.702
LANES = 128
SUBLANES = 8
EXPERT_ROWS = 256
ISSUE_UNROLL = 8
NEG_BIG = -1e30
VMEM_LIMIT = 56 * 1024 * 1024


def _params(n_axes):
    return pltpu.CompilerParams(dimension_semantics=("arbitrary",) * n_axes,
                                vmem_limit_bytes=VMEM_LIMIT)


def _pick(n, candidates):
    for c in candidates:
        if n % c == 0:
            return c
    return n


def _layer_norm(r, g, b):
    mu = jnp.mean(r, axis=-1, keepdims=True)
    d = r - mu
    var = jnp.mean(d * d, axis=-1, keepdims=True)
    return d * lax.rsqrt(var + LN_EPS) * g + b


def _sigmoid(x):
    return 1.0 / (1.0 + jnp.exp(-x))


def _mod_kernel(cs_ref, w_ref, b_ref, o_ref):
    cs = cs_ref[...]
    a = cs * _sigmoid(cs)
    o_ref[0] = jnp.dot(a, w_ref[0], precision=HIGHEST, preferred_element_type=F32) + b_ref[0]


def _modulation(c, c_ctx, w_mod, b_mod):
    depth, d, n = w_mod.shape
    b = c.shape[0]
    cs = jnp.concatenate([c, c_ctx[None], jnp.zeros((SUBLANES - b - 1, d), F32)], 0)
    tn = _pick(n, (1536, 1024, 512))
    return pl.pallas_call(
        _mod_kernel,
        grid=(depth, n // tn),
        in_specs=[pl.BlockSpec((SUBLANES, d), lambda l, j: (0, 0)),
                  pl.BlockSpec((1, d, tn), lambda l, j: (l, 0, j)),
                  pl.BlockSpec((1, 1, tn), lambda l, j: (l, 0, j))],
        out_specs=pl.BlockSpec((1, SUBLANES, tn), lambda l, j: (l, 0, j)),
        out_shape=jax.ShapeDtypeStruct((depth, SUBLANES, n), F32),
        compiler_params=_params(2),
        name="modulation",
    )(cs, w_mod, b_mod.reshape(depth, 1, n))


def _even_in_kernel(x_ref, sc_ref, sh_ref, w_ref, cos_ref, sin_ref, q_ref, k_ref, v_ref, a_ref):
    u = (x_ref[0] * (1.0 + sc_ref[0]) + sh_ref[0]).astype(BF16)
    z = jnp.dot(u, w_ref[...], preferred_element_type=F32)
    cosv = cos_ref[...]
    sinv = sin_ref[...]
    lane = lax.broadcasted_iota(I32, cosv.shape, 1)
    first = (lane % 32) < 16

    def rope(t):
        partner = jnp.where(first, pltpu.roll(t, LANES - 16, 1), pltpu.roll(t, 16, 1))
        return t * cosv + partner * sinv

    for j in range(QK_W // LANES):
        sl = slice(j * LANES, (j + 1) * LANES)
        rq = rope(z[:, sl]) * (HEAD_DIM ** -0.5 * math.log2(math.e))
        for c in range(2):
            q_ref[c, 0, :, sl] = jnp.where(lane // HEAD_DIM == c, rq, 0.0).astype(BF16)
        k_ref[0, :, sl] = rope(z[:, QK_W + j * LANES:QK_W + (j + 1) * LANES]).astype(BF16)
    v_ref[0] = z[:, 2 * QK_W:2 * QK_W + V_W].astype(BF16)
    a0 = 2 * QK_W + V_W
    a_ref[0] = z[:, a0:a0 + CONV_W] * _sigmoid(z[:, a0 + CONV_W:a0 + 2 * CONV_W])


def _rope_tables(length):
    rows = length // GRID_W
    row = jnp.repeat(jnp.arange(rows, dtype=F32), GRID_W)
    col = jnp.tile(jnp.arange(GRID_W, dtype=F32), rows)
    n_freq = HEAD_DIM // 4
    inv_freq = ROPE_BASE ** (-jnp.arange(n_freq, dtype=F32) / n_freq)
    ar = row[:, None] * inv_freq
    ac = col[:, None] * inv_freq
    cos64 = jnp.concatenate([jnp.cos(ar), jnp.cos(ar), jnp.cos(ac), jnp.cos(ac)], 1)
    sin64 = jnp.concatenate([-jnp.sin(ar), jnp.sin(ar), -jnp.sin(ac), jnp.sin(ac)], 1)
    return jnp.tile(cos64, (1, LANES // HEAD_DIM)), jnp.tile(sin64, (1, LANES // HEAD_DIM))


def _even_in_proj(x, sc, sh, w_bf, cos_t, sin_t):
    b, length, d = x.shape
    n = w_bf.shape[1]
    tm = _pick(length, (512, 256, 128))
    row = lambda bi, i: (bi, i, 0)
    vec = lambda bi, i: (bi, 0, 0)
    return pl.pallas_call(
        _even_in_kernel,
        grid=(b, length // tm),
        in_specs=[pl.BlockSpec((1, tm, d), row),
                  pl.BlockSpec((1, 1, d), vec),
                  pl.BlockSpec((1, 1, d), vec),
                  pl.BlockSpec((d, n), lambda bi, i: (0, 0)),
                  pl.BlockSpec((tm, LANES), lambda bi, i: (i, 0)),
                  pl.BlockSpec((tm, LANES), lambda bi, i: (i, 0))],
        out_specs=[pl.BlockSpec((2, 1, tm, QK_W), lambda bi, i: (0, bi, i, 0)), pl.BlockSpec((1, tm, QK_W), row),
                   pl.BlockSpec((1, tm, V_W), row), pl.BlockSpec((1, tm, CONV_W), row)],
        out_shape=[jax.ShapeDtypeStruct((2, b, length, QK_W), BF16), jax.ShapeDtypeStruct((b, length, QK_W), BF16),
                   jax.ShapeDtypeStruct((b, length, V_W), BF16), jax.ShapeDtypeStruct((b, length, CONV_W), F32)],
        compiler_params=_params(2),
        name="even_in_proj",
    )(x, sc, sh, w_bf, cos_t, sin_t)


def _modmm_kernel(x_ref, sc_ref, sh_ref, w_ref, o_ref):
    u = (x_ref[0] * (1.0 + sc_ref[0]) + sh_ref[0]).astype(BF16)
    o_ref[0] = jnp.dot(u, w_ref[...], preferred_element_type=F32).astype(o_ref.dtype)


def _mod_matmul(x, sc, sh, w_bf, out_dtype):
    b, rows, d = x.shape
    n = w_bf.shape[1]
    tm = _pick(rows, (512, 256, 128))
    return pl.pallas_call(
        _modmm_kernel,
        grid=(b, rows // tm),
        in_specs=[pl.BlockSpec((1, tm, d), lambda bi, i: (bi, i, 0)),
                  pl.BlockSpec((1, 1, d), lambda bi, i: (bi, 0, 0)),
                  pl.BlockSpec((1, 1, d), lambda bi, i: (bi, 0, 0)),
                  pl.BlockSpec((d, n), lambda bi, i: (0, 0))],
        out_specs=pl.BlockSpec((1, tm, n), lambda bi, i: (bi, i, 0)),
        out_shape=jax.ShapeDtypeStruct((b, rows, n), out_dtype),
        compiler_params=_params(2),
        name="context_kv_proj",
    )(x, sc, sh, w_bf)


VT_ROWS = HEAD_V + 16


def _attn_kernel(lq1_ref, lk1_ref, lq2_ref, lk2_ref, g_ref, q_ref, k_ref, vt_ref, o_ref,
                 s0_sc, s1_sc, p0_sc, p1_sc, acc_sc, *, lam_init, kc):
    mq = q_ref.shape[2]
    n = k_ref.shape[1] // kc
    s_sc = (s0_sc, s1_sc)
    p_sc = (p0_sc, p1_sc)

    def scores(j, slot):
        kj = k_ref[0, pl.ds(pl.multiple_of(j * kc, kc), kc), :]
        for c in range(2):
            s_sc[slot][c] = lax.dot_general(kj, q_ref[c, 0], (((1,), (1,)), ((), ())),
                                            preferred_element_type=F32)

    def softmax(slot, m):
        m_out, alpha = [], []
        for c in range(2):
            s = s_sc[slot][c]
            m_new = jnp.maximum(m[c], jnp.max(s, axis=0, keepdims=True))
            p_sc[slot][c] = jnp.exp2((s - m_new).astype(BF16))
            m_out.append(m_new)
            alpha.append(jnp.exp2(m[c] - m_new))
        return tuple(m_out), tuple(alpha)

    def accumulate(j, slot, alpha):
        vt = vt_ref[0, 0, j]
        for c in range(2):
            acc_sc[c] = alpha[c] * acc_sc[c] + jnp.dot(vt, p_sc[slot][c], preferred_element_type=F32)

    def step(j, slot, m, alpha):
        scores(j + 2, slot)
        m, alpha_next = softmax(1 - slot, m)
        accumulate(j, slot, alpha)
        return m, alpha_next

    acc_sc[...] = jnp.zeros(acc_sc.shape, F32)
    m = (jnp.full((1, mq), NEG_BIG, F32),) * 2
    scores(0, 0)
    m, alpha = softmax(0, m)
    scores(1, 1)

    def pair(t, carry):
        m, alpha = step(2 * t, 0, *carry)
        return step(2 * t + 1, 1, m, alpha)

    n_steps = n - 2
    m, alpha = lax.fori_loop(0, n_steps // 2, pair, (m, alpha))
    if n_steps % 2:
        m, alpha = step(n_steps - 1, 0, m, alpha)
    m, alpha_last = softmax((n - 1) % 2, m)
    accumulate(n - 2, (n - 2) % 2, alpha)
    accumulate(n - 1, (n - 1) % 2, alpha_last)
    acc0 = acc_sc[0]
    acc1 = acc_sc[1]
    lam = (jnp.exp(jnp.sum(lq1_ref[...] * lk1_ref[...], keepdims=True))
           - jnp.exp(jnp.sum(lq2_ref[...] * lk2_ref[...], keepdims=True)) + lam_init)
    o = (acc0[:HEAD_V] / acc0[HEAD_V:HEAD_V + 1]
         - lam * (acc1[:HEAD_V] / acc1[HEAD_V:HEAD_V + 1]))
    ms = jnp.mean(o * o, axis=0, keepdims=True)
    o = o * lax.rsqrt(ms + LN_EPS) * g_ref[...] * (1.0 - lam_init)
    o_ref[0] = o.T.astype(o_ref.dtype)


def _diff_attention(q, k_all, v_all, lq1, lk1, lq2, lk2, norm_g, lam_init):
    _, b, length, _ = q.shape
    n_keys = k_all.shape[1]
    mq = _pick(length, (512, 256, 128))
    kc = _pick(n_keys, (1280, 256, 128))
    nc = n_keys // kc
    assert nc >= 3
    vt = v_all.reshape(b, nc, kc, N_HEADS, HEAD_V).transpose(0, 3, 1, 4, 2)
    extra = jnp.zeros((b, N_HEADS, nc, VT_ROWS - HEAD_V, kc), BF16).at[:, :, :, 0, :].set(1.0)
    vt = jnp.concatenate([vt, extra], 3)
    small = lambda bi, h, i: (0, 0)
    return pl.pallas_call(
        functools.partial(_attn_kernel, lam_init=lam_init, kc=kc),
        grid=(b, N_HEADS, length // mq),
        in_specs=[pl.BlockSpec((1, HEAD_DIM), small), pl.BlockSpec((1, HEAD_DIM), small),
                  pl.BlockSpec((1, HEAD_DIM), small), pl.BlockSpec((1, HEAD_DIM), small),
                  pl.BlockSpec((HEAD_V, 1), small),
                  pl.BlockSpec((2, 1, mq, HEAD_V), lambda bi, h, i: (0, bi, i, h)),
                  pl.BlockSpec((1, n_keys, HEAD_V), lambda bi, h, i: (bi, 0, h)),
                  pl.BlockSpec((1, 1, nc, VT_ROWS, kc), lambda bi, h, i: (bi, h, 0, 0, 0))],
        out_specs=pl.BlockSpec((1, mq, HEAD_V), lambda bi, h, i: (bi, i, h)),
        out_shape=jax.ShapeDtypeStruct((b, length, V_W), BF16),
        scratch_shapes=[pltpu.VMEM((2, kc, mq), F32), pltpu.VMEM((2, kc, mq), F32),
                        pltpu.VMEM((2, kc, mq), BF16), pltpu.VMEM((2, kc, mq), BF16),
                        pltpu.VMEM((2, VT_ROWS, mq), F32)],
        compiler_params=_params(3),
        name="diff_attention",
    )(lq1[None], lk1[None], lq2[None], lk2[None], norm_g[:, None], q, k_all, vt)


def _conv_kernel(prev_ref, cur_ref, next_ref, w_ref, b_ref, g_ref, beta_ref, o_ref, win, *, row_chunk):
    i = pl.program_id(1)
    tc = cur_ref.shape[1]
    win[CONV_HALO:CONV_HALO + tc, :] = cur_ref[0]
    win[0:CONV_HALO, :] = jnp.where(i > 0, prev_ref[0], 0.0)
    win[CONV_HALO + tc:, :] = jnp.where(i < pl.num_programs(1) - 1, next_ref[0], 0.0)
    first_tap = CONV_HALO - CONV_TAPS // 2
    for c0 in range(0, tc, row_chunk):
        acc = jnp.zeros((row_chunk, CONV_W), F32)
        for t in range(CONV_TAPS):
            acc = acc + w_ref[t:t + 1, :] * win[c0 + first_tap + t:c0 + first_tap + t + row_chunk, :]
        y = _layer_norm(acc + b_ref[...], g_ref[...], beta_ref[...])
        o_ref[0, c0:c0 + row_chunk, :] = (y * _sigmoid(y)).astype(o_ref.dtype)


def _conformer_conv(a, conv_w, conv_b, ln_g, ln_b):
    b, length, w = a.shape
    tc = _pick(length, (512, 256, 128))
    hb = tc // CONV_HALO
    n_halo = length // CONV_HALO
    w_pad = jnp.concatenate([conv_w, jnp.zeros((1, w), F32)], 0)
    vec = lambda bi, i: (0, 0)
    return pl.pallas_call(
        functools.partial(_conv_kernel, row_chunk=64),
        grid=(b, length // tc),
        in_specs=[pl.BlockSpec((1, CONV_HALO, w), lambda bi, i: (bi, jnp.maximum(i * hb - 1, 0), 0)),
                  pl.BlockSpec((1, tc, w), lambda bi, i: (bi, i, 0)),
                  pl.BlockSpec((1, CONV_HALO, w), lambda bi, i: (bi, jnp.minimum((i + 1) * hb, n_halo - 1), 0)),
                  pl.BlockSpec((CONV_TAPS + 1, w), vec),
                  pl.BlockSpec((1, w), vec), pl.BlockSpec((1, w), vec), pl.BlockSpec((1, w), vec)],
        out_specs=pl.BlockSpec((1, tc, w), lambda bi, i: (bi, i, 0)),
        out_shape=jax.ShapeDtypeStruct((b, length, w), BF16),
        scratch_shapes=[pltpu.VMEM((tc + 2 * CONV_HALO, w), F32)],
        compiler_params=_params(2),
        name="conformer_conv",
    )(a, a, a, w_pad, conv_b[None], ln_g[None], ln_b[None])


def _post_mix_kernel(ya_ref, yb_ref, wa_ref, wb_ref, x_ref, g1_ref, lng_ref, lnb_ref, sc_ref, sh_ref,
                     rw_ref, rb_ref, h_ref, e_ref, gate_ref, pos_ref, cnt_ref, carry, *, alpha):
    first_step = jnp.logical_and(pl.program_id(0) == 0, pl.program_id(1) == 0)

    @pl.when(first_step)
    def _():
        carry[...] = jnp.zeros(carry.shape, F32)

    y = (jnp.dot(ya_ref[0].astype(BF16), wa_ref[...], preferred_element_type=F32)
         + jnp.dot(yb_ref[0].astype(BF16), wb_ref[...], preferred_element_type=F32))
    h = _layer_norm(alpha * x_ref[0] + g1_ref[0] * y, lng_ref[...], lnb_ref[...])
    h_ref[0] = h
    u2 = h * (1.0 + sc_ref[0]) + sh_ref[0]
    logits = jnp.dot(u2, rw_ref[...], precision=HIGHEST, preferred_element_type=F32) + rb_ref[...]

    tm = logits.shape[0]
    lane = lax.broadcasted_iota(I32, logits.shape, 1).astype(F32)
    vals, idxs = [], []
    lg = logits
    for _ in range(TOP_K):
        mx = jnp.max(lg, axis=1, keepdims=True)
        idx = jnp.min(jnp.where(lg == mx, lane, float(LANES)), axis=1, keepdims=True)
        vals.append(mx)
        idxs.append(idx)
        lg = jnp.where(lane == idx, -3e38, lg)
    exps = [jnp.exp(v - vals[0]) for v in vals]
    den = exps[0] + exps[1] + exps[2] + exps[3]

    hot = jnp.zeros(logits.shape, F32)
    for idx in idxs:
        hot = hot + (lane == idx).astype(F32)
    r_i = lax.broadcasted_iota(I32, (tm, tm), 0)
    c_i = lax.broadcasted_iota(I32, (tm, tm), 1)
    tri = (c_i < r_i).astype(BF16)
    before = jnp.dot(tri, hot.astype(BF16), preferred_element_type=F32) + carry[...]
    e_out = jnp.zeros(logits.shape, F32)
    g_out = jnp.zeros(logits.shape, F32)
    p_out = jnp.zeros(logits.shape, F32)
    for j in range(TOP_K):
        pos_j = jnp.sum(jnp.where(lane == idxs[j], before, 0.0), axis=1, keepdims=True)
        sel = lane == float(j)
        e_out = jnp.where(sel, idxs[j], e_out)
        g_out = jnp.where(sel, exps[j] / den, g_out)
        p_out = jnp.where(sel, pos_j, p_out)
    e_ref[...] = e_out.astype(I32)
    gate_ref[...] = g_out
    pos_ref[...] = p_out.astype(I32)
    new_carry = carry[...] + jnp.sum(hot, axis=0, keepdims=True)
    carry[...] = new_carry
    cnt_ref[...] = jnp.broadcast_to(new_carry, cnt_ref.shape).astype(I32)


def _post_mix(ya, yb, w_out_bf, x, g1, ln_g, ln_b, sc2, sh2, router_w, router_b, alpha):
    b, length, d = x.shape
    wa = ya.shape[2]
    tm = _pick(length, (512, 256, 128))
    n_tok = b * length
    rw = jnp.concatenate([router_w, jnp.zeros((d, LANES - N_EXPERTS), F32)], 1)
    rb = jnp.concatenate([router_b, jnp.full((LANES - N_EXPERTS,), NEG_BIG, F32)])[None]
    row = lambda bi, i: (bi, i, 0)
    vec = lambda bi, i: (bi, 0, 0)
    const = lambda bi, i: (0, 0)
    nt = length // tm
    tok = lambda bi, i: (bi * nt + i, 0)
    return pl.pallas_call(
        functools.partial(_post_mix_kernel, alpha=alpha),
        grid=(b, nt),
        in_specs=[pl.BlockSpec((1, tm, wa), row), pl.BlockSpec((1, tm, d - wa), row),
                  pl.BlockSpec((wa, d), const), pl.BlockSpec((d - wa, d), const),
                  pl.BlockSpec((1, tm, d), row), pl.BlockSpec((1, 1, d), vec),
                  pl.BlockSpec((1, d), const), pl.BlockSpec((1, d), const),
                  pl.BlockSpec((1, 1, d), vec), pl.BlockSpec((1, 1, d), vec),
                  pl.BlockSpec((d, LANES), const), pl.BlockSpec((1, LANES), const)],
        out_specs=[pl.BlockSpec((1, tm, d), row), pl.BlockSpec((tm, LANES), tok),
                   pl.BlockSpec((tm, LANES), tok), pl.BlockSpec((tm, LANES), tok),
                   pl.BlockSpec((SUBLANES, LANES), const)],
        out_shape=[jax.ShapeDtypeStruct((b, length, d), F32), jax.ShapeDtypeStruct((n_tok, LANES), I32),
                   jax.ShapeDtypeStruct((n_tok, LANES), F32), jax.ShapeDtypeStruct((n_tok, LANES), I32),
                   jax.ShapeDtypeStruct((SUBLANES, LANES), I32)],
        scratch_shapes=[pltpu.VMEM((1, LANES), F32)],
        compiler_params=_params(2),
        name="out_proj_norm_router",
    )(ya, yb, w_out_bf[:wa], w_out_bf[wa:], x, g1, ln_g[None], ln_b[None], sc2, sh2, rw, rb)


def _dispatch_kernel(zstart_ref, zvalid_ref, dest_ref, h_ref, sc_ref, sh_ref, xs_ref, u_sc, zero_sc, sem, zsem):
    tm = h_ref.shape[0]

    @pl.when(pl.program_id(0) == 0)
    def _():
        zero_sc[...] = jnp.zeros(zero_sc.shape, F32)

        def fill(e, c):
            @pl.when(zvalid_ref[e] > 0)
            def _():
                start = pl.multiple_of(zstart_ref[e], SUBLANES)
                pltpu.make_async_copy(zero_sc, xs_ref.at[pl.ds(start, EXPERT_ROWS)], zsem).start()
            return c

        lax.fori_loop(0, N_EXPERTS, fill, 0)

        def drain(e, c):
            @pl.when(zvalid_ref[e] > 0)
            def _():
                pltpu.make_async_copy(zero_sc, xs_ref.at[pl.ds(0, EXPERT_ROWS)], zsem).wait()
            return c

        lax.fori_loop(0, N_EXPERTS, drain, 0)

    u_sc[...] = h_ref[...] * (1.0 + sc_ref[0]) + sh_ref[0]

    def issue(g, c):
        for rr in range(ISSUE_UNROLL):
            r = g * ISSUE_UNROLL + rr
            for j in range(TOP_K):
                d = dest_ref[0, 0, r * TOP_K + j]
                pltpu.make_async_copy(u_sc.at[pl.ds(r, 1)], xs_ref.at[pl.ds(d, 1)], sem).start()
        return c

    lax.fori_loop(0, tm // ISSUE_UNROLL, issue, 0)
    rows = xs_ref.at[pl.ds(0, tm * TOP_K)]
    pltpu.make_async_copy(rows, rows, sem).wait()


def _dispatch(h_flat, sc2, sh2, dest, zstart, zvalid, n_rows, tokens_per_batch):
    n_tok, d = h_flat.shape
    tm = _pick(tokens_per_batch, (512, 256, 128))
    per_b = tokens_per_batch // tm
    return pl.pallas_call(
        _dispatch_kernel,
        grid_spec=pltpu.PrefetchScalarGridSpec(
            num_scalar_prefetch=2,
            grid=(n_tok // tm,),
            in_specs=[pl.BlockSpec((1, 1, tm * TOP_K), lambda i, zs, zv: (i, 0, 0), memory_space=pltpu.SMEM),
                      pl.BlockSpec((tm, d), lambda i, zs, zv: (i, 0)),
                      pl.BlockSpec((1, 1, d), lambda i, zs, zv: (i // per_b, 0, 0)),
                      pl.BlockSpec((1, 1, d), lambda i, zs, zv: (i // per_b, 0, 0))],
            out_specs=pl.BlockSpec(memory_space=pl.ANY),
            scratch_shapes=[pltpu.VMEM((tm, d), F32), pltpu.VMEM((EXPERT_ROWS, d), F32),
                            pltpu.SemaphoreType.DMA(()), pltpu.SemaphoreType.DMA(())]),
        out_shape=jax.ShapeDtypeStruct((n_rows, d), F32),
        compiler_params=_params(1),
        name="moe_dispatch",
    )(zstart, zvalid, dest.reshape(n_tok // tm, 1, tm * TOP_K), h_flat, sc2, sh2)


def _expert_kernel(blk_e_ref, n_used_ref, x_ref, wgu_ref, bgu_ref, wd_ref, bd_ref, o_ref, wgu_bf, wd_bf):
    i = pl.program_id(0)
    new_expert = jnp.logical_or(i == 0, blk_e_ref[i] != blk_e_ref[jnp.maximum(i - 1, 0)])

    @pl.when(new_expert)
    def _():
        wgu_bf[...] = wgu_ref[0].astype(BF16)
        wd_bf[...] = wd_ref[0].astype(BF16)

    @pl.when(i < n_used_ref[0])
    def _():
        d_e = wd_ref.shape[1]
        gu = jnp.dot(x_ref[...].astype(BF16), wgu_bf[...], preferred_element_type=F32) + bgu_ref[0]
        x_glu = jnp.minimum(gu[:, :d_e], SWIGLU_LIMIT)
        x_lin = jnp.clip(gu[:, d_e:], -SWIGLU_LIMIT, SWIGLU_LIMIT)
        act = x_glu * _sigmoid(SWIGLU_ALPHA * x_glu) * (x_lin + 1.0)
        o_ref[...] = jnp.dot(act.astype(BF16), wd_bf[...], preferred_element_type=F32) + bd_ref[0]


def _experts(xs, blk_e, n_used, wgu, bgu, wd, bd, layer):
    n_rows, d = xs.shape
    depth, n_e, _, n_gu = wgu.shape
    n_blk = n_rows // EXPERT_ROWS
    rows = lambda i, be, nu: (jnp.minimum(i, nu[0] - 1), 0)
    first = layer * n_e
    by_e = lambda i, be, nu: (first + be[i], 0, 0)
    n_e = depth * n_e
    wgu = wgu.reshape(n_e, d, n_gu)
    wd = wd.reshape(n_e, n_gu // 2, d)
    return pl.pallas_call(
        _expert_kernel,
        grid_spec=pltpu.PrefetchScalarGridSpec(
            num_scalar_prefetch=2,
            grid=(n_blk,),
            in_specs=[pl.BlockSpec((EXPERT_ROWS, d), rows),
                      pl.BlockSpec((1, d, n_gu), by_e), pl.BlockSpec((1, 1, n_gu), by_e),
                      pl.BlockSpec((1, n_gu // 2, d), by_e), pl.BlockSpec((1, 1, d), by_e)],
            out_specs=pl.BlockSpec((EXPERT_ROWS, d), rows),
            scratch_shapes=[pltpu.VMEM((d, n_gu), BF16), pltpu.VMEM((n_gu // 2, d), BF16)]),
        out_shape=jax.ShapeDtypeStruct((n_rows, d), F32),
        compiler_params=_params(1),
        name="moe_experts",
    )(blk_e, n_used, xs, wgu, bgu.reshape(n_e, 1, n_gu), wd, bd.reshape(n_e, 1, d))


def _combine_kernel(dest_ref, dest_next_ref, y_ref, h_ref, gate_ref, g2_ref, lng_ref, lnb_ref, o_ref,
                    buf, sem, *, alpha):
    i = pl.program_id(0)
    tm = h_ref.shape[0]

    def fetch(idx_ref, slot):
        def issue(g, c):
            for rr in range(ISSUE_UNROLL):
                r = g * ISSUE_UNROLL + rr
                for j in range(TOP_K):
                    d = idx_ref[0, 0, r * TOP_K + j]
                    pltpu.make_async_copy(y_ref.at[pl.ds(d, 1)], buf.at[slot, j, pl.ds(r, 1)], sem.at[slot]).start()
            return c

        lax.fori_loop(0, tm // ISSUE_UNROLL, issue, 0)

    slot = i % 2

    @pl.when(i == 0)
    def _():
        fetch(dest_ref, 0)

    @pl.when(i + 1 < pl.num_programs(0))
    def _():
        fetch(dest_next_ref, 1 - slot)

    pltpu.make_async_copy(buf.at[slot], buf.at[slot], sem.at[slot]).wait()
    gate = gate_ref[...]
    y2 = gate[:, 0:1] * buf[slot, 0]
    for j in range(1, TOP_K):
        y2 = y2 + gate[:, j:j + 1] * buf[slot, j]
    o_ref[...] = _layer_norm(alpha * h_ref[...] + g2_ref[0] * y2, lng_ref[...], lnb_ref[...])


def _combine(y_rows, dest, h_flat, gate, g2, ln_g, ln_b, alpha, tokens_per_batch):
    n_tok, d = h_flat.shape
    tm = _pick(tokens_per_batch, (256, 128))
    per_b = tokens_per_batch // tm
    n_tiles = n_tok // tm
    dest3 = dest.reshape(n_tiles, 1, tm * TOP_K)
    return pl.pallas_call(
        functools.partial(_combine_kernel, alpha=alpha),
        grid=(n_tiles,),
        in_specs=[pl.BlockSpec((1, 1, tm * TOP_K), lambda i: (i, 0, 0), memory_space=pltpu.SMEM),
                  pl.BlockSpec((1, 1, tm * TOP_K), lambda i: (jnp.minimum(i + 1, n_tiles - 1), 0, 0),
                               memory_space=pltpu.SMEM),
                  pl.BlockSpec(memory_space=pl.ANY),
                  pl.BlockSpec((tm, d), lambda i: (i, 0)),
                  pl.BlockSpec((tm, LANES), lambda i: (i, 0)),
                  pl.BlockSpec((1, 1, d), lambda i: (i // per_b, 0, 0)),
                  pl.BlockSpec((1, d), lambda i: (0, 0)), pl.BlockSpec((1, d), lambda i: (0, 0))],
        out_specs=pl.BlockSpec((tm, d), lambda i: (i, 0)),
        out_shape=jax.ShapeDtypeStruct((n_tok, d), F32),
        scratch_shapes=[pltpu.VMEM((2, TOP_K, tm, d), F32), pltpu.SemaphoreType.DMA((2,))],
        compiler_params=_params(1),
        name="moe_combine_norm",
    )(dest3, dest3, y_rows, h_flat, gate, g2, ln_g[None], ln_b[None])


def _moe_and_norm(h1, top_e, gate, pos, counts, sc2, sh2, g2, wgu, bgu, wd, bd, layer, ln_g, ln_b, alpha):
    b, length, d = h1.shape
    n_tok = b * length
    n_blk = n_tok * TOP_K // EXPERT_ROWS + N_EXPERTS
    padded = (counts + EXPERT_ROWS - 1) // EXPERT_ROWS * EXPERT_ROWS
    pad_end = jnp.cumsum(padded)
    offset = pad_end - padded
    n_used = (pad_end[-1:] // EXPERT_ROWS).astype(I32)
    blk_start = jnp.arange(n_blk, dtype=I32) * EXPERT_ROWS
    blk_e = jnp.minimum(jnp.sum(pad_end[None, :] <= blk_start[:, None], axis=1), N_EXPERTS - 1).astype(I32)
    dest = (offset[top_e] + pos).astype(I32)
    zstart = jnp.maximum(pad_end - EXPERT_ROWS, 0).astype(I32)
    zvalid = (counts > 0).astype(I32)

    h_flat = h1.reshape(n_tok, d)
    xs = _dispatch(h_flat, sc2, sh2, dest, zstart, zvalid, n_blk * EXPERT_ROWS, length)
    y_rows = _experts(xs, blk_e, n_used, wgu, bgu, wd, bd, layer)
    out = _combine(y_rows, dest, h_flat, gate, g2, ln_g, ln_b, alpha, length)
    return out.reshape(b, length, d)


def _odd_in_kernel(x_ref, sc_ref, sh_ref, w_ref, glg_ref, glb_ref, ws_ref, bs_ref, flg_ref, flb_ref, dft_ref,
                   sp_ref, zr_ref, zi_ref):
    u = (x_ref[0] * (1.0 + sc_ref[0]) + sh_ref[0]).astype(BF16)
    z = jnp.dot(u, w_ref[...], preferred_element_type=F32)
    tm = z.shape[0]
    gw = N_GROUPS * GROUP_W
    ug = jax.nn.gelu(z[:, :gw])
    vn = _layer_norm(jax.nn.gelu(z[:, gw:2 * gw]), glg_ref[...], glb_ref[...]).astype(BF16)
    f = z[:, 2 * gw:]
    for g in range(N_GROUPS):
        cols = slice(g * GROUP_W, (g + 1) * GROUP_W)
        for c0 in range(0, tm, CHUNK):
            rows = slice(c0, c0 + CHUNK)
            sv = jnp.dot(ws_ref[g], vn[rows, cols], preferred_element_type=F32) + bs_ref[:, cols]
            sp_ref[0, rows, cols] = (ug[rows, cols] * sv).astype(sp_ref.dtype)
        fn = _layer_norm(f[:, cols], flg_ref[:, cols], flb_ref[:, cols]).astype(BF16)
        zz = jnp.dot(fn, dft_ref[...], preferred_element_type=F32)
        zr_ref[0, :, cols] = zz[:, :GROUP_W]
        zi_ref[0, :, cols] = zz[:, GROUP_W:]


def _odd_in_proj(x, sc, sh, w_bf, gln_g, gln_b, ws, bs, fln_g, fln_b):
    b, length, d = x.shape
    n = w_bf.shape[1]
    gw = N_GROUPS * GROUP_W
    tm = _pick(length, (512, 256, 128))
    kk = jnp.arange(GROUP_W, dtype=I32)
    ang = (2.0 * math.pi / GROUP_W) * ((kk[:, None] * kk[None, :]) % GROUP_W).astype(F32)
    dft = jnp.concatenate([jnp.cos(ang), -jnp.sin(ang)], 1).astype(BF16)
    bs_exp = jnp.repeat(bs.T, GROUP_W, axis=1)
    row = lambda bi, i: (bi, i, 0)
    vec = lambda bi, i: (bi, 0, 0)
    const2 = lambda bi, i: (0, 0)
    return pl.pallas_call(
        _odd_in_kernel,
        grid=(b, length // tm),
        in_specs=[pl.BlockSpec((1, tm, d), row), pl.BlockSpec((1, 1, d), vec), pl.BlockSpec((1, 1, d), vec),
                  pl.BlockSpec((d, n), const2),
                  pl.BlockSpec((1, gw), const2), pl.BlockSpec((1, gw), const2),
                  pl.BlockSpec((N_GROUPS, CHUNK, CHUNK), lambda bi, i: (0, 0, 0)),
                  pl.BlockSpec((CHUNK, gw), const2),
                  pl.BlockSpec((1, gw), const2), pl.BlockSpec((1, gw), const2),
                  pl.BlockSpec((GROUP_W, 2 * GROUP_W), const2)],
        out_specs=[pl.BlockSpec((1, tm, gw), row), pl.BlockSpec((1, tm, gw), row), pl.BlockSpec((1, tm, gw), row)],
        out_shape=[jax.ShapeDtypeStruct((b, length, gw), BF16), jax.ShapeDtypeStruct((b, length, gw), F32),
                   jax.ShapeDtypeStruct((b, length, gw), F32)],
        compiler_params=_params(2),
        name="odd_in_proj",
    )(x, sc, sh, w_bf, gln_g[None], gln_b[None], ws.astype(BF16), bs_exp, fln_g[None], fln_b[None], dft)


def _fft_a_kernel(zr_ref, zi_ref, m_ref, ar_ref, ai_ref):
    l1, nb, w = zr_ref.shape[1:]
    zr = zr_ref[0].reshape(l1 * nb, w).astype(BF16)
    zi = zi_ref[0].reshape(l1 * nb, w).astype(BF16)
    a = jnp.dot(m_ref[...], jnp.concatenate([zr, zi], 0), preferred_element_type=F32)
    ar_ref[0] = a[:l1 * nb].reshape(l1, nb, w)
    ai_ref[0] = a[l1 * nb:].reshape(l1, nb, w)


def _fft_b_kernel(ar_ref, ai_ref, m_ref, o_ref, *, norm):
    nb, l2, w = ar_ref.shape[1:]
    ar = ar_ref[0].reshape(nb * l2, w).astype(BF16)
    ai = ai_ref[0].reshape(nb * l2, w).astype(BF16)
    y = jnp.dot(m_ref[0], jnp.concatenate([ar, ai], 0), preferred_element_type=F32)
    o_ref[0] = (y * norm).reshape(l2, nb, w)


def _length_dft_real(zr, zi):
    b, length, w = zr.shape
    l2 = CHUNK
    l1 = length // l2
    nb = SUBLANES
    eye = jnp.eye(nb, dtype=F32)
    k1 = jnp.arange(l1, dtype=I32)
    ang_a = (2.0 * math.pi / l1) * ((k1[:, None] * k1[None, :]) % l1).astype(F32)
    ca = jnp.kron(jnp.cos(ang_a), eye)
    sa = jnp.kron(jnp.sin(ang_a), eye)
    m_a = jnp.concatenate([jnp.concatenate([ca, sa], 1), jnp.concatenate([-sa, ca], 1)], 0).astype(BF16)
    k2 = jnp.arange(l2, dtype=I32)
    phase = (k2[None, None, :] * (k1[None, :, None] + l1 * k2[:, None, None])) % length
    ang_b = (2.0 * math.pi / length) * phase.astype(F32)

    def expand(t):
        t = t.reshape(l2, l1 // nb, nb, l2).transpose(1, 0, 2, 3)
        return (t[:, :, :, None, :] * eye[None, None, :, :, None]).reshape(l1 // nb, l2 * nb, nb * l2)

    m_b = jnp.concatenate([expand(jnp.cos(ang_b)), expand(jnp.sin(ang_b))], 2).astype(BF16)

    zr4 = zr.reshape(b, l1, l2, w)
    zi4 = zi.reshape(b, l1, l2, w)
    blk_a = pl.BlockSpec((1, l1, nb, w), lambda bi, i: (bi, 0, i, 0))
    ar, ai = pl.pallas_call(
        _fft_a_kernel,
        grid=(b, l2 // nb),
        in_specs=[blk_a, blk_a, pl.BlockSpec((2 * l1 * nb, 2 * l1 * nb), lambda bi, i: (0, 0))],
        out_specs=[blk_a, blk_a],
        out_shape=[jax.ShapeDtypeStruct((b, l1, l2, w), F32)] * 2,
        compiler_params=_params(2),
        name="fourier_stage_a",
    )(zr4, zi4, m_a)
    blk_b = pl.BlockSpec((1, nb, l2, w), lambda bi, i: (bi, i, 0, 0))
    out = pl.pallas_call(
        functools.partial(_fft_b_kernel, norm=float((length * GROUP_W) ** -0.5)),
        grid=(b, l1 // nb),
        in_specs=[blk_b, blk_b, pl.BlockSpec((1, l2 * nb, 2 * nb * l2), lambda bi, i: (i, 0, 0))],
        out_specs=pl.BlockSpec((1, l2, nb, w), lambda bi, i: (bi, 0, i, 0)),
        out_shape=jax.ShapeDtypeStruct((b, l2, l1, w), F32),
        compiler_params=_params(2),
        name="fourier_stage_b",
    )(ar, ai, m_b)
    return out.reshape(b, length, w)


def kernel(x, c, ctx, c_ctx, w_mod, b_mod, ln1_g, ln1_b, ln2_g, ln2_b, ev_w_in, ev_w_out, conv_w, conv_b, conv_ln_g, conv_ln_b, lam_q1, lam_k1, lam_q2, lam_k2, diff_norm_g, od_w_in, od_w_out, gmlp_ln_g, gmlp_ln_b, gmlp_ws, gmlp_bs, four_ln_g, four_ln_b, router_w, router_b, w_gate_up, b_gate_up, w_down, b_down):
    b, length, d = x.shape
    depth = w_mod.shape[0]
    alpha = float((2 * depth) ** 0.25)
    assert b + 1 <= SUBLANES and length % (CHUNK * SUBLANES) == 0 and length % GRID_W == 0

    mod = _modulation(c, c_ctx, w_mod, b_mod)
    cos_t, sin_t = _rope_tables(length)
    h = x
    for layer in range(depth):
        j = layer // 2
        m = mod[layer]
        sh1, sc1, g1, sh2, sc2, g2 = [m[:b, i * d:(i + 1) * d][:, None, :] for i in range(6)]
        if layer % 2 == 0:
            lam_init = 0.8 - 0.6 * math.exp(-0.3 * layer)
            w_in = ev_w_in[j].astype(BF16)
            q, k, v, a = _even_in_proj(h, sc1, sh1, w_in, cos_t, sin_t)
            csh1 = jnp.broadcast_to(m[b:b + 1, 0:d][:, None, :], (b, 1, d))
            csc1 = jnp.broadcast_to(m[b:b + 1, d:2 * d][:, None, :], (b, 1, d))
            kv_c = _mod_matmul(ctx, csc1, csh1, w_in[:, QK_W:2 * QK_W + V_W], BF16)
            k_all = jnp.concatenate([k, kv_c[..., :QK_W]], 1)
            v_all = jnp.concatenate([v, kv_c[..., QK_W:]], 1)
            att = _diff_attention(q, k_all, v_all, lam_q1[j], lam_k1[j], lam_q2[j], lam_k2[j],
                                  diff_norm_g[j], lam_init)
            conv = _conformer_conv(a, conv_w[j], conv_b[j], conv_ln_g[j], conv_ln_b[j])
            ya, yb, w_out = conv, att, ev_w_out[j]
        else:
            spatial, zr, zi = _odd_in_proj(h, sc1, sh1, od_w_in[j].astype(BF16), gmlp_ln_g[j], gmlp_ln_b[j],
                                           gmlp_ws[j], gmlp_bs[j], four_ln_g[j], four_ln_b[j])
            ya, yb, w_out = spatial, _length_dft_real(zr, zi), od_w_out[j]
        h1, top_e, gate, pos, counts = _post_mix(ya, yb, w_out.astype(BF16), h, g1, ln1_g[layer], ln1_b[layer],
                                                 sc2, sh2, router_w[layer], router_b[layer], alpha)
        h = _moe_and_norm(h1, top_e[:, :TOP_K], gate, pos[:, :TOP_K], counts[0, :N_EXPERTS], sc2, sh2, g2,
                          w_gate_up, b_gate_up, w_down, b_down, layer, ln2_g[layer], ln2_b[layer], alpha)
    return h
```

```python
import functools
import math

import jax
import jax.numpy as jnp
from jax import lax
from jax.experimental import pallas as pl
from jax.experimental.pallas import tpu as pltpu

F32 = jnp.float32
BF16 = jnp.bfloat16
I32 = jnp.int32
HIGHEST = lax.Precision.HIGHEST

LN_EPS = 1e-5
GRID_W = 64
HEAD_DIM = 64
HEAD_V = 128
N_HEADS = 4
QK_W = N_HEADS * 2 * HEAD_DIM
V_W = N_HEADS * HEAD_V
CONV_W = 512
CONV_TAPS = 31
CONV_HALO = 16
ROPE_BASE = 10000.0
CHUNK = 128
N_GROUPS = 4
GROUP_W = 128
N_EXPERTS = 32
TOP_K = 4
SWIGLU_LIMIT = 7.0
SWIGLU_ALPHA = 1.702
LANES = 128
SUBLANES = 8
EXPERT_ROWS = 512
ISSUE_UNROLL = 8
NEG_BIG = -1e30
VMEM_LIMIT = 56 * 1024 * 1024


def _params(n_axes):
    return pltpu.CompilerParams(dimension_semantics=("arbitrary",) * n_axes,
                                vmem_limit_bytes=VMEM_LIMIT)


def _pick(n, candidates):
    for c in candidates:
        if n % c == 0:
            return c
    return n


def _layer_norm(r, g, b):
    mu = jnp.mean(r, axis=-1, keepdims=True)
    d = r - mu
    var = jnp.mean(d * d, axis=-1, keepdims=True)
    return d * lax.rsqrt(var + LN_EPS) * g + b


def _sigmoid(x):
    return 1.0 / (1.0 + jnp.exp(-x))


def _mod_kernel(cs_ref, w_ref, b_ref, o_ref):
    cs = cs_ref[...]
    a = cs * _sigmoid(cs)
    o_ref[0] = jnp.dot(a, w_ref[0], precision=HIGHEST, preferred_element_type=F32) + b_ref[0]


def _modulation(c, c_ctx, w_mod, b_mod):
    depth, d, n = w_mod.shape
    b = c.shape[0]
    cs = jnp.concatenate([c, c_ctx[None], jnp.zeros((SUBLANES - b - 1, d), F32)], 0)
    tn = _pick(n, (1536, 1024, 512))
    return pl.pallas_call(
        _mod_kernel,
        grid=(depth, n // tn),
        in_specs=[pl.BlockSpec((SUBLANES, d), lambda l, j: (0, 0)),
                  pl.BlockSpec((1, d, tn), lambda l, j: (l, 0, j)),
                  pl.BlockSpec((1, 1, tn), lambda l, j: (l, 0, j))],
        out_specs=pl.BlockSpec((1, SUBLANES, tn), lambda l, j: (l, 0, j)),
        out_shape=jax.ShapeDtypeStruct((depth, SUBLANES, n), F32),
        compiler_params=_params(2),
        name="modulation",
    )(cs, w_mod, b_mod.reshape(depth, 1, n))


def _even_in_kernel(x_ref, sc_ref, sh_ref, w_ref, cos_ref, sin_ref, q_ref, k_ref, v_ref, a_ref):
    u = (x_ref[0] * (1.0 + sc_ref[0]) + sh_ref[0]).astype(BF16)
    z = jnp.dot(u, w_ref[...], preferred_element_type=F32)
    cosv = cos_ref[...]
    sinv = sin_ref[...]
    lane = lax.broadcasted_iota(I32, cosv.shape, 1)
    first = (lane % 32) < 16

    def rope(t):
        partner = jnp.where(first, pltpu.roll(t, LANES - 16, 1), pltpu.roll(t, 16, 1))
        return t * cosv + partner * sinv

    for j in range(QK_W // LANES):
        sl = slice(j * LANES, (j + 1) * LANES)
        rq = rope(z[:, sl]) * (HEAD_DIM ** -0.5 * math.log2(math.e))
        for c in range(2):
            q_ref[c, 0, :, sl] = jnp.where(lane // HEAD_DIM == c, rq, 0.0).astype(BF16)
        k_ref[0, :, sl] = rope(z[:, QK_W + j * LANES:QK_W + (j + 1) * LANES]).astype(BF16)
    v_ref[0] = z[:, 2 * QK_W:2 * QK_W + V_W].astype(BF16)
    a0 = 2 * QK_W + V_W
    a_ref[0] = z[:, a0:a0 + CONV_W] * _sigmoid(z[:, a0 + CONV_W:a0 + 2 * CONV_W])


def _rope_tables(length):
    rows = length // GRID_W
    row = jnp.repeat(jnp.arange(rows, dtype=F32), GRID_W)
    col = jnp.tile(jnp.arange(GRID_W, dtype=F32), rows)
    n_freq = HEAD_DIM // 4
    inv_freq = ROPE_BASE ** (-jnp.arange(n_freq, dtype=F32) / n_freq)
    ar = row[:, None] * inv_freq
    ac = col[:, None] * inv_freq
    cos64 = jnp.concatenate([jnp.cos(ar), jnp.cos(ar), jnp.cos(ac), jnp.cos(ac)], 1)
    sin64 = jnp.concatenate([-jnp.sin(ar), jnp.sin(ar), -jnp.sin(ac), jnp.sin(ac)], 1)
    return jnp.tile(cos64, (1, LANES // HEAD_DIM)), jnp.tile(sin64, (1, LANES // HEAD_DIM))


def _even_in_proj(x, sc, sh, w_bf, cos_t, sin_t):
    b, length, d = x.shape
    n = w_bf.shape[1]
    tm = _pick(length, (512, 256, 128))
    row = lambda bi, i: (bi, i, 0)
    vec = lambda bi, i: (bi, 0, 0)
    return pl.pallas_call(
        _even_in_kernel,
        grid=(b, length // tm),
        in_specs=[pl.BlockSpec((1, tm, d), row),
                  pl.BlockSpec((1, 1, d), vec),
                  pl.BlockSpec((1, 1, d), vec),
                  pl.BlockSpec((d, n), lambda bi, i: (0, 0)),
                  pl.BlockSpec((tm, LANES), lambda bi, i: (i, 0)),
                  pl.BlockSpec((tm, LANES), lambda bi, i: (i, 0))],
        out_specs=[pl.BlockSpec((2, 1, tm, QK_W), lambda bi, i: (0, bi, i, 0)), pl.BlockSpec((1, tm, QK_W), row),
                   pl.BlockSpec((1, tm, V_W), row), pl.BlockSpec((1, tm, CONV_W), row)],
        out_shape=[jax.ShapeDtypeStruct((2, b, length, QK_W), BF16), jax.ShapeDtypeStruct((b, length, QK_W), BF16),
                   jax.ShapeDtypeStruct((b, length, V_W), BF16), jax.ShapeDtypeStruct((b, length, CONV_W), F32)],
        compiler_params=_params(2),
        name="even_in_proj",
    )(x, sc, sh, w_bf, cos_t, sin_t)


def _modmm_kernel(x_ref, sc_ref, sh_ref, w_ref, o_ref):
    u = (x_ref[0] * (1.0 + sc_ref[0]) + sh_ref[0]).astype(BF16)
    o_ref[0] = jnp.dot(u, w_ref[...], preferred_element_type=F32).astype(o_ref.dtype)


def _mod_matmul(x, sc, sh, w_bf, out_dtype):
    b, rows, d = x.shape
    n = w_bf.shape[1]
    tm = _pick(rows, (512, 256, 128))
    return pl.pallas_call(
        _modmm_kernel,
        grid=(b, rows // tm),
        in_specs=[pl.BlockSpec((1, tm, d), lambda bi, i: (bi, i, 0)),
                  pl.BlockSpec((1, 1, d), lambda bi, i: (bi, 0, 0)),
                  pl.BlockSpec((1, 1, d), lambda bi, i: (bi, 0, 0)),
                  pl.BlockSpec((d, n), lambda bi, i: (0, 0))],
        out_specs=pl.BlockSpec((1, tm, n), lambda bi, i: (bi, i, 0)),
        out_shape=jax.ShapeDtypeStruct((b, rows, n), out_dtype),
        compiler_params=_params(2),
        name="context_kv_proj",
    )(x, sc, sh, w_bf)


VT_ROWS = HEAD_V + 16


def _attn_kernel(lq1_ref, lk1_ref, lq2_ref, lk2_ref, g_ref, q_ref, k_ref, vt_ref, o_ref,
                 s0_sc, s1_sc, p0_sc, p1_sc, acc_sc, *, lam_init, kc):
    mq = q_ref.shape[2]
    n = k_ref.shape[1] // kc
    s_sc = (s0_sc, s1_sc)
    p_sc = (p0_sc, p1_sc)

    def scores(j, slot):
        kj = k_ref[0, pl.ds(pl.multiple_of(j * kc, kc), kc), :]
        for c in range(2):
            s_sc[slot][c] = lax.dot_general(kj, q_ref[c, 0], (((1,), (1,)), ((), ())),
                                            preferred_element_type=F32)

    def softmax(slot, m):
        m_out, alpha = [], []
        for c in range(2):
            s = s_sc[slot][c]
            m_new = jnp.maximum(m[c], jnp.max(s, axis=0, keepdims=True))
            p_sc[slot][c] = jnp.exp2((s - m_new).astype(BF16))
            m_out.append(m_new)
            alpha.append(jnp.exp2(m[c] - m_new))
        return tuple(m_out), tuple(alpha)

    def accumulate(j, slot, alpha):
        vt = vt_ref[0, 0, j]
        for c in range(2):
            acc_sc[c] = alpha[c] * acc_sc[c] + jnp.dot(vt, p_sc[slot][c], preferred_element_type=F32)

    def step(j, slot, m, alpha):
        scores(j + 2, slot)
        m, alpha_next = softmax(1 - slot, m)
        accumulate(j, slot, alpha)
        return m, alpha_next

    acc_sc[...] = jnp.zeros(acc_sc.shape, F32)
    m = (jnp.full((1, mq), NEG_BIG, F32),) * 2
    scores(0, 0)
    m, alpha = softmax(0, m)
    scores(1, 1)

    def pair(t, carry):
        m, alpha = step(2 * t, 0, *carry)
        return step(2 * t + 1, 1, m, alpha)

    n_steps = n - 2
    m, alpha = lax.fori_loop(0, n_steps // 2, pair, (m, alpha))
    if n_steps % 2:
        m, alpha = step(n_steps - 1, 0, m, alpha)
    m, alpha_last = softmax((n - 1) % 2, m)
    accumulate(n - 2, (n - 2) % 2, alpha)
    accumulate(n - 1, (n - 1) % 2, alpha_last)
    acc0 = acc_sc[0]
    acc1 = acc_sc[1]
    lam = (jnp.exp(jnp.sum(lq1_ref[...] * lk1_ref[...], keepdims=True))
           - jnp.exp(jnp.sum(lq2_ref[...] * lk2_ref[...], keepdims=True)) + lam_init)
    o = (acc0[:HEAD_V] / acc0[HEAD_V:HEAD_V + 1]
         - lam * (acc1[:HEAD_V] / acc1[HEAD_V:HEAD_V + 1]))
    ms = jnp.mean(o * o, axis=0, keepdims=True)
    o = o * lax.rsqrt(ms + LN_EPS) * g_ref[...] * (1.0 - lam_init)
    o_ref[0] = o.T.astype(o_ref.dtype)


def _diff_attention(q, k_all, v_all, lq1, lk1, lq2, lk2, norm_g, lam_init):
    _, b, length, _ = q.shape
    n_keys = k_all.shape[1]
    mq = _pick(length, (512, 256, 128))
    kc = next(c for c in (1280, 256, 128) if n_keys % c == 0 and n_keys // c >= 3)
    nc = n_keys // kc
    vt = v_all.reshape(b, nc, kc, N_HEADS, HEAD_V).transpose(0, 3, 1, 4, 2)
    extra = jnp.zeros((b, N_HEADS, nc, VT_ROWS - HEAD_V, kc), BF16).at[:, :, :, 0, :].set(1.0)
    vt = jnp.concatenate([vt, extra], 3)
    small = lambda bi, h, i: (0, 0)
    return pl.pallas_call(
        functools.partial(_attn_kernel, lam_init=lam_init, kc=kc),
        grid=(b, N_HEADS, length // mq),
        in_specs=[pl.BlockSpec((1, HEAD_DIM), small), pl.BlockSpec((1, HEAD_DIM), small),
                  pl.BlockSpec((1, HEAD_DIM), small), pl.BlockSpec((1, HEAD_DIM), small),
                  pl.BlockSpec((HEAD_V, 1), small),
                  pl.BlockSpec((2, 1, mq, HEAD_V), lambda bi, h, i: (0, bi, i, h)),
                  pl.BlockSpec((1, n_keys, HEAD_V), lambda bi, h, i: (bi, 0, h)),
                  pl.BlockSpec((1, 1, nc, VT_ROWS, kc), lambda bi, h, i: (bi, h, 0, 0, 0))],
        out_specs=pl.BlockSpec((1, mq, HEAD_V), lambda bi, h, i: (bi, i, h)),
        out_shape=jax.ShapeDtypeStruct((b, length, V_W), BF16),
        scratch_shapes=[pltpu.VMEM((2, kc, mq), F32), pltpu.VMEM((2, kc, mq), F32),
                        pltpu.VMEM((2, kc, mq), BF16), pltpu.VMEM((2, kc, mq), BF16),
                        pltpu.VMEM((2, VT_ROWS, mq), F32)],
        compiler_params=_params(3),
        name="diff_attention",
    )(lq1[None], lk1[None], lq2[None], lk2[None], norm_g[:, None], q, k_all, vt)


def _conv_kernel(prev_ref, cur_ref, next_ref, w_ref, b_ref, g_ref, beta_ref, o_ref, win, *, row_chunk):
    i = pl.program_id(1)
    tc = cur_ref.shape[1]
    span = tc + 2 * CONV_HALO - SUBLANES
    win[0, CONV_HALO:CONV_HALO + tc, :] = cur_ref[0]
    win[0, 0:CONV_HALO, :] = jnp.where(i > 0, prev_ref[0], 0.0)
    win[0, CONV_HALO + tc:, :] = jnp.where(i < pl.num_programs(1) - 1, next_ref[0], 0.0)
    for s in range(1, SUBLANES):
        win[s, 0:span, :] = win[0, s:s + span, :]
    first_tap = CONV_HALO - CONV_TAPS // 2
    for c0 in range(0, tc, row_chunk):
        acc = jnp.zeros((row_chunk, CONV_W), F32)
        for t in range(CONV_TAPS):
            s = (first_tap + t) % SUBLANES
            r0 = c0 + first_tap + t - s
            acc = acc + w_ref[t:t + 1, :] * win[s, r0:r0 + row_chunk, :]
        y = _layer_norm(acc + b_ref[...], g_ref[...], beta_ref[...])
        o_ref[0, c0:c0 + row_chunk, :] = (y * _sigmoid(y)).astype(o_ref.dtype)


def _conformer_conv(a, conv_w, conv_b, ln_g, ln_b):
    b, length, w = a.shape
    tc = _pick(length, (512, 256, 128))
    hb = tc // CONV_HALO
    n_halo = length // CONV_HALO
    w_pad = jnp.concatenate([conv_w, jnp.zeros((1, w), F32)], 0)
    vec = lambda bi, i: (0, 0)
    return pl.pallas_call(
        functools.partial(_conv_kernel, row_chunk=64),
        grid=(b, length // tc),
        in_specs=[pl.BlockSpec((1, CONV_HALO, w), lambda bi, i: (bi, jnp.maximum(i * hb - 1, 0), 0)),
                  pl.BlockSpec((1, tc, w), lambda bi, i: (bi, i, 0)),
                  pl.BlockSpec((1, CONV_HALO, w), lambda bi, i: (bi, jnp.minimum((i + 1) * hb, n_halo - 1), 0)),
                  pl.BlockSpec((CONV_TAPS + 1, w), vec),
                  pl.BlockSpec((1, w), vec), pl.BlockSpec((1, w), vec), pl.BlockSpec((1, w), vec)],
        out_specs=pl.BlockSpec((1, tc, w), lambda bi, i: (bi, i, 0)),
        out_shape=jax.ShapeDtypeStruct((b, length, w), BF16),
        scratch_shapes=[pltpu.VMEM((SUBLANES, tc + 2 * CONV_HALO, w), F32)],
        compiler_params=_params(2),
        name="conformer_conv",
    )(a, a, a, w_pad, conv_b[None], ln_g[None], ln_b[None])


def _post_mix_kernel(ya_ref, yb_ref, wa_ref, wb_ref, x_ref, g1_ref, lng_ref, lnb_ref, sc_ref, sh_ref,
                     rw_ref, rb_ref, h_ref, e_ref, gate_ref, pos_ref, cnt_ref, carry, *, alpha):
    first_step = jnp.logical_and(pl.program_id(0) == 0, pl.program_id(1) == 0)

    @pl.when(first_step)
    def _():
        carry[...] = jnp.zeros(carry.shape, F32)

    y = (jnp.dot(ya_ref[0].astype(BF16), wa_ref[...], preferred_element_type=F32)
         + jnp.dot(yb_ref[0].astype(BF16), wb_ref[...], preferred_element_type=F32))
    h = _layer_norm(alpha * x_ref[0] + g1_ref[0] * y, lng_ref[...], lnb_ref[...])
    h_ref[0] = h
    u2 = h * (1.0 + sc_ref[0]) + sh_ref[0]
    u_hi = u2.astype(BF16)
    u_lo = (u2 - u_hi.astype(F32)).astype(BF16)
    hh_hl = jnp.dot(u_hi, rw_ref[...], preferred_element_type=F32)
    lh = jnp.dot(u_lo, rw_ref[:, :LANES], preferred_element_type=F32)
    logits = hh_hl[:, :LANES] + hh_hl[:, LANES:] + lh + rb_ref[...]

    tm = logits.shape[0]
    lane = lax.broadcasted_iota(I32, logits.shape, 1).astype(F32)
    vals, idxs = [], []
    lg = logits
    for _ in range(TOP_K):
        mx = jnp.max(lg, axis=1, keepdims=True)
        idx = jnp.min(jnp.where(lg == mx, lane, float(LANES)), axis=1, keepdims=True)
        vals.append(mx)
        idxs.append(idx)
        lg = jnp.where(lane == idx, -3e38, lg)
    exps = [jnp.exp(v - vals[0]) for v in vals]
    den = exps[0] + exps[1] + exps[2] + exps[3]

    hot = jnp.zeros(logits.shape, F32)
    for idx in idxs:
        hot = hot + (lane == idx).astype(F32)
    r_i = lax.broadcasted_iota(I32, (tm, tm), 0)
    c_i = lax.broadcasted_iota(I32, (tm, tm), 1)
    tri = (c_i < r_i).astype(BF16)
    before = jnp.dot(tri, hot.astype(BF16), preferred_element_type=F32) + carry[...]
    e_out = jnp.zeros(logits.shape, F32)
    g_out = jnp.zeros(logits.shape, F32)
    p_out = jnp.zeros(logits.shape, F32)
    for j in range(TOP_K):
        pos_j = jnp.sum(jnp.where(lane == idxs[j], before, 0.0), axis=1, keepdims=True)
        sel = lane == float(j)
        e_out = jnp.where(sel, idxs[j], e_out)
        g_out = jnp.where(sel, exps[j] / den, g_out)
        p_out = jnp.where(sel, pos_j, p_out)
    e_ref[...] = e_out.astype(I32)
    gate_ref[...] = g_out
    pos_ref[...] = p_out.astype(I32)
    new_carry = carry[...] + jnp.sum(hot, axis=0, keepdims=True)
    carry[...] = new_carry
    cnt_ref[...] = jnp.broadcast_to(new_carry, cnt_ref.shape).astype(I32)


def _post_mix(ya, yb, w_out_bf, x, g1, ln_g, ln_b, sc2, sh2, router_w, router_b, alpha):
    b, length, d = x.shape
    wa = ya.shape[2]
    tm = _pick(length, (512, 256, 128))
    n_tok = b * length
    rw = jnp.concatenate([router_w, jnp.zeros((d, LANES - N_EXPERTS), F32)], 1)
    rw_hi = rw.astype(BF16)
    rw = jnp.concatenate([rw_hi, (rw - rw_hi.astype(F32)).astype(BF16)], 1)
    rb =jnp.concatenate([router_b, jnp.full((LANES - N_EXPERTS,), NEG_BIG, F32)])[None]
    row = lambda bi, i: (bi, i, 0)
    vec = lambda bi, i: (bi, 0, 0)
    const = lambda bi, i: (0, 0)
    nt = length // tm
    tok = lambda bi, i: (bi * nt + i, 0)
    return pl.pallas_call(
        functools.partial(_post_mix_kernel, alpha=alpha),
        grid=(b, nt),
        in_specs=[pl.BlockSpec((1, tm, wa), row), pl.BlockSpec((1, tm, d - wa), row),
                  pl.BlockSpec((wa, d), const), pl.BlockSpec((d - wa, d), const),
                  pl.BlockSpec((1, tm, d), row), pl.BlockSpec((1, 1, d), vec),
                  pl.BlockSpec((1, d), const), pl.BlockSpec((1, d), const),
                  pl.BlockSpec((1, 1, d), vec), pl.BlockSpec((1, 1, d), vec),
                  pl.BlockSpec((d, 2 * LANES), const), pl.BlockSpec((1, LANES), const)],
        out_specs=[pl.BlockSpec((1, tm, d), row), pl.BlockSpec((tm, LANES), tok),
                   pl.BlockSpec((tm, LANES), tok), pl.BlockSpec((tm, LANES), tok),
                   pl.BlockSpec((SUBLANES, LANES), const)],
        out_shape=[jax.ShapeDtypeStruct((b, length, d), F32), jax.ShapeDtypeStruct((n_tok, LANES), I32),
                   jax.ShapeDtypeStruct((n_tok, LANES), F32), jax.ShapeDtypeStruct((n_tok, LANES), I32),
                   jax.ShapeDtypeStruct((SUBLANES, LANES), I32)],
        scratch_shapes=[pltpu.VMEM((1, LANES), F32)],
        compiler_params=_params(2),
        name="out_proj_norm_router",
    )(ya, yb, w_out_bf[:wa], w_out_bf[wa:], x, g1, ln_g[None], ln_b[None], sc2, sh2, rw, rb)


def _dispatch_kernel(zstart_ref, zvalid_ref, dest_ref, h_ref, sc_ref, sh_ref, xs_ref, u_sc, zero_sc, sem, zsem):
    tm = h_ref.shape[0]

    @pl.when(pl.program_id(0) == 0)
    def _():
        zero_sc[...] = jnp.zeros(zero_sc.shape, F32)

        def fill(e, c):
            @pl.when(zvalid_ref[e] > 0)
            def _():
                start = pl.multiple_of(zstart_ref[e], SUBLANES)
                pltpu.make_async_copy(zero_sc, xs_ref.at[pl.ds(start, EXPERT_ROWS)], zsem).start()
            return c

        lax.fori_loop(0, N_EXPERTS, fill, 0)

        def drain(e, c):
            @pl.when(zvalid_ref[e] > 0)
            def _():
                pltpu.make_async_copy(zero_sc, xs_ref.at[pl.ds(0, EXPERT_ROWS)], zsem).wait()
            return c

        lax.fori_loop(0, N_EXPERTS, drain, 0)

    u_sc[...] = h_ref[...] * (1.0 + sc_ref[0]) + sh_ref[0]

    def issue(g, c):
        for rr in range(ISSUE_UNROLL):
            r = g * ISSUE_UNROLL + rr
            for j in range(TOP_K):
                d = dest_ref[0, 0, r * TOP_K + j]
                pltpu.make_async_copy(u_sc.at[pl.ds(r, 1)], xs_ref.at[pl.ds(d, 1)], sem).start()
        return c

    lax.fori_loop(0, tm // ISSUE_UNROLL, issue, 0)
    rows = xs_ref.at[pl.ds(0, tm * TOP_K)]
    pltpu.make_async_copy(rows, rows, sem).wait()


def _dispatch(h_flat, sc2, sh2, dest, zstart, zvalid, n_rows, tokens_per_batch):
    n_tok, d = h_flat.shape
    tm = _pick(tokens_per_batch, (512, 256, 128))
    per_b = tokens_per_batch // tm
    return pl.pallas_call(
        _dispatch_kernel,
        grid_spec=pltpu.PrefetchScalarGridSpec(
            num_scalar_prefetch=2,
            grid=(n_tok // tm,),
            in_specs=[pl.BlockSpec((1, 1, tm * TOP_K), lambda i, zs, zv: (i, 0, 0), memory_space=pltpu.SMEM),
                      pl.BlockSpec((tm, d), lambda i, zs, zv: (i, 0)),
                      pl.BlockSpec((1, 1, d), lambda i, zs, zv: (i // per_b, 0, 0)),
                      pl.BlockSpec((1, 1, d), lambda i, zs, zv: (i // per_b, 0, 0))],
            out_specs=pl.BlockSpec(memory_space=pl.ANY),
            scratch_shapes=[pltpu.VMEM((tm, d), F32), pltpu.VMEM((EXPERT_ROWS, d), F32),
                            pltpu.SemaphoreType.DMA(()), pltpu.SemaphoreType.DMA(())]),
        out_shape=jax.ShapeDtypeStruct((n_rows, d), F32),
        compiler_params=_params(1),
        name="moe_dispatch",
    )(zstart, zvalid, dest.reshape(n_tok // tm, 1, tm * TOP_K), h_flat, sc2, sh2)


def _expert_kernel(blk_e_ref, n_used_ref, x_ref, wgu_ref, bgu_ref, wd_ref, bd_ref, o_ref, wgu_bf, wd_bf):
    i = pl.program_id(0)
    new_expert = jnp.logical_or(i == 0, blk_e_ref[i] != blk_e_ref[jnp.maximum(i - 1, 0)])

    @pl.when(new_expert)
    def _():
        wgu_bf[...] = wgu_ref[0].astype(BF16)
        wd_bf[...] = wd_ref[0].astype(BF16)

    @pl.when(i < n_used_ref[0])
    def _():
        d_e = wd_ref.shape[1]
        gu = jnp.dot(x_ref[...].astype(BF16), wgu_bf[...], preferred_element_type=F32) + bgu_ref[0]
        x_glu = jnp.minimum(gu[:, :d_e], SWIGLU_LIMIT)
        x_lin = jnp.clip(gu[:, d_e:], -SWIGLU_LIMIT, SWIGLU_LIMIT)
        act = x_glu * _sigmoid(SWIGLU_ALPHA * x_glu) * (x_lin + 1.0)
        o_ref[...] = jnp.dot(act.astype(BF16), wd_bf[...], preferred_element_type=F32) + bd_ref[0]


def _experts(xs, blk_e, n_used, wgu, bgu, wd, bd, layer):
    n_rows, d = xs.shape
    depth, n_e, _, n_gu = wgu.shape
    n_blk = n_rows // EXPERT_ROWS
    rows = lambda i, be, nu: (jnp.minimum(i, nu[0] - 1), 0)
    first = layer * n_e
    by_e = lambda i, be, nu: (first + be[i], 0, 0)
    n_e = depth * n_e
    wgu = wgu.reshape(n_e, d, n_gu)
    wd = wd.reshape(n_e, n_gu // 2, d)
    return pl.pallas_call(
        _expert_kernel,
        grid_spec=pltpu.PrefetchScalarGridSpec(
            num_scalar_prefetch=2,
            grid=(n_blk,),
            in_specs=[pl.BlockSpec((EXPERT_ROWS, d), rows),
                      pl.BlockSpec((1, d, n_gu), by_e), pl.BlockSpec((1, 1, n_gu), by_e),
                      pl.BlockSpec((1, n_gu // 2, d), by_e), pl.BlockSpec((1, 1, d), by_e)],
            out_specs=pl.BlockSpec((EXPERT_ROWS, d), rows),
            scratch_shapes=[pltpu.VMEM((d, n_gu), BF16), pltpu.VMEM((n_gu // 2, d), BF16)]),
        out_shape=jax.ShapeDtypeStruct((n_rows, d), F32),
        compiler_params=_params(1),
        name="moe_experts",
    )(blk_e, n_used, xs, wgu, bgu.reshape(n_e, 1, n_gu), wd, bd.reshape(n_e, 1, d))


def _combine_kernel(dest_ref, dest_next_ref, y_ref, h_ref, gate_ref, g2_ref, lng_ref, lnb_ref, o_ref,
                    buf, sem, *, alpha):
    i = pl.program_id(0)
    tm = h_ref.shape[0]

    def fetch(idx_ref, slot):
        def issue(g, c):
            for rr in range(ISSUE_UNROLL):
                r = g * ISSUE_UNROLL + rr
                for j in range(TOP_K):
                    d = idx_ref[0, 0, r * TOP_K + j]
                    pltpu.make_async_copy(y_ref.at[pl.ds(d, 1)], buf.at[slot, j, pl.ds(r, 1)], sem.at[slot]).start()
            return c

        lax.fori_loop(0, tm // ISSUE_UNROLL, issue, 0)

    slot = i % 2

    @pl.when(i == 0)
    def _():
        fetch(dest_ref, 0)

    @pl.when(i + 1 < pl.num_programs(0))
    def _():
        fetch(dest_next_ref, 1 - slot)

    pltpu.make_async_copy(buf.at[slot], buf.at[slot], sem.at[slot]).wait()
    gate = gate_ref[...]
    y2 = gate[:, 0:1] * buf[slot, 0]
    for j in range(1, TOP_K):
        y2 = y2 + gate[:, j:j + 1] * buf[slot, j]
    o_ref[...] = _layer_norm(alpha * h_ref[...] + g2_ref[0] * y2, lng_ref[...], lnb_ref[...])


def _combine(y_rows, dest, h_flat, gate, g2, ln_g, ln_b, alpha, tokens_per_batch):
    n_tok, d = h_flat.shape
    tm = _pick(tokens_per_batch, (256, 128))
    per_b = tokens_per_batch // tm
    n_tiles = n_tok // tm
    dest3 = dest.reshape(n_tiles, 1, tm * TOP_K)
    return pl.pallas_call(
        functools.partial(_combine_kernel, alpha=alpha),
        grid=(n_tiles,),
        in_specs=[pl.BlockSpec((1, 1, tm * TOP_K), lambda i: (i, 0, 0), memory_space=pltpu.SMEM),
                  pl.BlockSpec((1, 1, tm * TOP_K), lambda i: (jnp.minimum(i + 1, n_tiles - 1), 0, 0),
                               memory_space=pltpu.SMEM),
                  pl.BlockSpec(memory_space=pl.ANY),
                  pl.BlockSpec((tm, d), lambda i: (i, 0)),
                  pl.BlockSpec((tm, LANES), lambda i: (i, 0)),
                  pl.BlockSpec((1, 1, d), lambda i: (i // per_b, 0, 0)),
                  pl.BlockSpec((1, d), lambda i: (0, 0)), pl.BlockSpec((1, d), lambda i: (0, 0))],
        out_specs=pl.BlockSpec((tm, d), lambda i: (i, 0)),
        out_shape=jax.ShapeDtypeStruct((n_tok, d), F32),
        scratch_shapes=[pltpu.VMEM((2, TOP_K, tm, d), F32), pltpu.SemaphoreType.DMA((2,))],
        compiler_params=_params(1),
        name="moe_combine_norm",
    )(dest3, dest3, y_rows, h_flat, gate, g2, ln_g[None], ln_b[None])


def _moe_and_norm(h1, top_e, gate, pos, counts, sc2, sh2, g2, wgu, bgu, wd, bd, layer, ln_g, ln_b, alpha):
    b, length, d = h1.shape
    n_tok = b * length
    n_blk = n_tok * TOP_K // EXPERT_ROWS + N_EXPERTS
    padded = (counts + EXPERT_ROWS - 1) // EXPERT_ROWS * EXPERT_ROWS
    pad_end = jnp.cumsum(padded)
    offset = pad_end - padded
    n_used = (pad_end[-1:] // EXPERT_ROWS).astype(I32)
    blk_start = jnp.arange(n_blk, dtype=I32) * EXPERT_ROWS
    blk_e = jnp.minimum(jnp.sum(pad_end[None, :] <= blk_start[:, None], axis=1), N_EXPERTS - 1).astype(I32)
    dest = (offset[top_e] + pos).astype(I32)
    zstart = jnp.maximum(pad_end - EXPERT_ROWS, 0).astype(I32)
    zvalid = (counts > 0).astype(I32)

    h_flat = h1.reshape(n_tok, d)
    xs = _dispatch(h_flat, sc2, sh2, dest, zstart, zvalid, n_blk * EXPERT_ROWS, length)
    y_rows = _experts(xs, blk_e, n_used, wgu, bgu, wd, bd, layer)
    out = _combine(y_rows, dest, h_flat, gate, g2, ln_g, ln_b, alpha, length)
    return out.reshape(b, length, d)


def _odd_in_kernel(x_ref, sc_ref, sh_ref, w_ref, glg_ref, glb_ref, ws_ref, bs_ref, flg_ref, flb_ref, dft_ref,
                   sp_ref, zr_ref, zi_ref):
    u = (x_ref[0] * (1.0 + sc_ref[0]) + sh_ref[0]).astype(BF16)
    z = jnp.dot(u, w_ref[...], preferred_element_type=F32)
    tm = z.shape[0]
    gw = N_GROUPS * GROUP_W
    ug = jax.nn.gelu(z[:, :gw])
    vn = _layer_norm(jax.nn.gelu(z[:, gw:2 * gw]), glg_ref[...], glb_ref[...]).astype(BF16)
    f = z[:, 2 * gw:]
    for g in range(N_GROUPS):
        cols = slice(g * GROUP_W, (g + 1) * GROUP_W)
        for c0 in range(0, tm, CHUNK):
            rows = slice(c0, c0 + CHUNK)
            sv = jnp.dot(ws_ref[g], vn[rows, cols], preferred_element_type=F32) + bs_ref[:, cols]
            sp_ref[0, rows, cols] = (ug[rows, cols] * sv).astype(sp_ref.dtype)
        fn = _layer_norm(f[:, cols], flg_ref[:, cols], flb_ref[:, cols]).astype(BF16)
        zz = jnp.dot(fn, dft_ref[...], preferred_element_type=F32)
        zr_ref[0, :, cols] = zz[:, :GROUP_W]
        zi_ref[0, :, cols] = zz[:, GROUP_W:]


def _odd_in_proj(x, sc, sh, w_bf, gln_g, gln_b, ws, bs, fln_g, fln_b):
    b, length, d = x.shape
    n = w_bf.shape[1]
    gw = N_GROUPS * GROUP_W
    tm = _pick(length, (512, 256, 128))
    kk = jnp.arange(GROUP_W, dtype=I32)
    ang = (2.0 * math.pi / GROUP_W) * ((kk[:, None] * kk[None, :]) % GROUP_W).astype(F32)
    dft = jnp.concatenate([jnp.cos(ang), -jnp.sin(ang)], 1).astype(BF16)
    bs_exp = jnp.repeat(bs.T, GROUP_W, axis=1)
    row = lambda bi, i: (bi, i, 0)
    vec = lambda bi, i: (bi, 0, 0)
    const2 = lambda bi, i: (0, 0)
    return pl.pallas_call(
        _odd_in_kernel,
        grid=(b, length // tm),
        in_specs=[pl.BlockSpec((1, tm, d), row), pl.BlockSpec((1, 1, d), vec), pl.BlockSpec((1, 1, d), vec),
                  pl.BlockSpec((d, n), const2),
                  pl.BlockSpec((1, gw), const2), pl.BlockSpec((1, gw), const2),
                  pl.BlockSpec((N_GROUPS, CHUNK, CHUNK), lambda bi, i: (0, 0, 0)),
                  pl.BlockSpec((CHUNK, gw), const2),
                  pl.BlockSpec((1, gw), const2), pl.BlockSpec((1, gw), const2),
                  pl.BlockSpec((GROUP_W, 2 * GROUP_W), const2)],
        out_specs=[pl.BlockSpec((1, tm, gw), row), pl.BlockSpec((1, tm, gw), row), pl.BlockSpec((1, tm, gw), row)],
        out_shape=[jax.ShapeDtypeStruct((b, length, gw), BF16), jax.ShapeDtypeStruct((b, length, gw), F32),
                   jax.ShapeDtypeStruct((b, length, gw), F32)],
        compiler_params=_params(2),
        name="odd_in_proj",
    )(x, sc, sh, w_bf, gln_g[None], gln_b[None], ws.astype(BF16), bs_exp, fln_g[None], fln_b[None], dft)


def _fft_a_kernel(zr_ref, zi_ref, m_ref, ar_ref, ai_ref):
    l1, nb, w = zr_ref.shape[1:]
    zr = zr_ref[0].reshape(l1 * nb, w).astype(BF16)
    zi = zi_ref[0].reshape(l1 * nb, w).astype(BF16)
    a = jnp.dot(m_ref[...], jnp.concatenate([zr, zi], 0), preferred_element_type=F32)
    ar_ref[0] = a[:l1 * nb].reshape(l1, nb, w)
    ai_ref[0] = a[l1 * nb:].reshape(l1, nb, w)


def _fft_b_kernel(ar_ref, ai_ref, m_ref, o_ref, *, norm):
    nb, l2, w = ar_ref.shape[1:]
    ar = ar_ref[0].reshape(nb * l2, w).astype(BF16)
    ai = ai_ref[0].reshape(nb * l2, w).astype(BF16)
    y = jnp.dot(m_ref[0], jnp.concatenate([ar, ai], 0), preferred_element_type=F32)
    o_ref[0] = (y * norm).reshape(l2, nb, w)


def _length_dft_real(zr, zi):
    b, length, w = zr.shape
    l2 = CHUNK
    l1 = length // l2
    nb = SUBLANES
    eye = jnp.eye(nb, dtype=F32)
    k1 = jnp.arange(l1, dtype=I32)
    ang_a = (2.0 * math.pi / l1) * ((k1[:, None] * k1[None, :]) % l1).astype(F32)
    ca = jnp.kron(jnp.cos(ang_a), eye)
    sa = jnp.kron(jnp.sin(ang_a), eye)
    m_a = jnp.concatenate([jnp.concatenate([ca, sa], 1), jnp.concatenate([-sa, ca], 1)], 0).astype(BF16)
    k2 = jnp.arange(l2, dtype=I32)
    phase = (k2[None, None, :] * (k1[None, :, None] + l1 * k2[:, None, None])) % length
    ang_b = (2.0 * math.pi / length) * phase.astype(F32)

    def expand(t):
        t = t.reshape(l2, l1 // nb, nb, l2).transpose(1, 0, 2, 3)
        return (t[:, :, :, None, :] * eye[None, None, :, :, None]).reshape(l1 // nb, l2 * nb, nb * l2)

    m_b = jnp.concatenate([expand(jnp.cos(ang_b)), expand(jnp.sin(ang_b))], 2).astype(BF16)

    zr4 = zr.reshape(b, l1, l2, w)
    zi4 = zi.reshape(b, l1, l2, w)
    blk_a = pl.BlockSpec((1, l1, nb, w), lambda bi, i: (bi, 0, i, 0))
    ar, ai = pl.pallas_call(
        _fft_a_kernel,
        grid=(b, l2 // nb),
        in_specs=[blk_a, blk_a, pl.BlockSpec((2 * l1 * nb, 2 * l1 * nb), lambda bi, i: (0, 0))],
        out_specs=[blk_a, blk_a],
        out_shape=[jax.ShapeDtypeStruct((b, l1, l2, w), F32)] * 2,
        compiler_params=_params(2),
        name="fourier_stage_a",
    )(zr4, zi4, m_a)
    blk_b = pl.BlockSpec((1, nb, l2, w), lambda bi, i: (bi, i, 0, 0))
    out = pl.pallas_call(
        functools.partial(_fft_b_kernel, norm=float((length * GROUP_W) ** -0.5)),
        grid=(b, l1 // nb),
        in_specs=[blk_b, blk_b, pl.BlockSpec((1, l2 * nb, 2 * nb * l2), lambda bi, i: (i, 0, 0))],
        out_specs=pl.BlockSpec((1, l2, nb, w), lambda bi, i: (bi, 0, i, 0)),
        out_shape=jax.ShapeDtypeStruct((b, l2, l1, w), F32),
        compiler_params=_params(2),
        name="fourier_stage_b",
    )(ar, ai, m_b)
    return out.reshape(b, length, w)


def kernel(x, c, ctx, c_ctx, w_mod, b_mod, ln1_g, ln1_b, ln2_g, ln2_b, ev_w_in, ev_w_out, conv_w, conv_b, conv_ln_g, conv_ln_b, lam_q1, lam_k1, lam_q2, lam_k2, diff_norm_g, od_w_in, od_w_out, gmlp_ln_g, gmlp_ln_b, gmlp_ws, gmlp_bs, four_ln_g, four_ln_b, router_w, router_b, w_gate_up, b_gate_up, w_down, b_down):
    b, length, d = x.shape
    depth = w_mod.shape[0]
    alpha = float((2 * depth) ** 0.25)
    assert b + 1 <= SUBLANES and length % (CHUNK * SUBLANES) == 0 and length % GRID_W == 0

    mod = _modulation(c, c_ctx, w_mod, b_mod)
    cos_t, sin_t = _rope_tables(length)
    h = x
    for layer in range(depth):
        j = layer // 2
        m = mod[layer]
        sh1, sc1, g1, sh2, sc2, g2 = [m[:b, i * d:(i + 1) * d][:, None, :] for i in range(6)]
        if layer % 2 == 0:
            lam_init = 0.8 - 0.6 * math.exp(-0.3 * layer)
            w_in = ev_w_in[j].astype(BF16)
            q, k, v, a = _even_in_proj(h, sc1, sh1, w_in, cos_t, sin_t)
            csh1 = jnp.broadcast_to(m[b:b + 1, 0:d][:, None, :], (b, 1, d))
            csc1 = jnp.broadcast_to(m[b:b + 1, d:2 * d][:, None, :], (b, 1, d))
            kv_c = _mod_matmul(ctx, csc1, csh1, w_in[:, QK_W:2 * QK_W + V_W], BF16)
            k_all = jnp.concatenate([k, kv_c[..., :QK_W]], 1)
            v_all = jnp.concatenate([v, kv_c[..., QK_W:]], 1)
            att = _diff_attention(q, k_all, v_all, lam_q1[j], lam_k1[j], lam_q2[j], lam_k2[j],
                                  diff_norm_g[j], lam_init)
            conv = _conformer_conv(a, conv_w[j], conv_b[j], conv_ln_g[j], conv_ln_b[j])
            ya, yb, w_out = conv, att, ev_w_out[j]
        else:
            spatial, zr, zi = _odd_in_proj(h, sc1, sh1, od_w_in[j].astype(BF16), gmlp_ln_g[j], gmlp_ln_b[j],
                                           gmlp_ws[j], gmlp_bs[j], four_ln_g[j], four_ln_b[j])
            ya, yb, w_out = spatial, _length_dft_real(zr, zi), od_w_out[j]
        h1, top_e, gate, pos, counts = _post_mix(ya, yb, w_out.astype(BF16), h, g1, ln1_g[layer], ln1_b[layer],
                                                 sc2, sh2, router_w[layer], router_b[layer], alpha)
        h = _moe_and_norm(h1, top_e[:, :TOP_K], gate, pos[:, :TOP_K], counts[0, :N_EXPERTS], sc2, sh2, g2,
                          w_gate_up, b_gate_up, w_down, b_down, layer, ln2_g[layer], ln2_b[layer], alpha)
    return h
```

```python
import functools
import math

import jax
import jax.numpy as jnp
from jax import lax
from jax.experimental import pallas as pl
from jax.experimental.pallas import tpu as pltpu

F32 = jnp.float32
BF16 = jnp.bfloat16
I32 = jnp.int32
HIGHEST = lax.Precision.HIGHEST

LN_EPS = 1e-5
GRID_W = 64
HEAD_DIM = 64
HEAD_V = 128
N_HEADS = 4
QK_W = N_HEADS * 2 * HEAD_DIM
V_W = N_HEADS * HEAD_V
CONV_W = 512
CONV_TAPS = 31
CONV_HALO = 16
ROPE_BASE = 10000.0
CHUNK = 128
N_GROUPS = 4
GROUP_W = 128
N_EXPERTS = 32
TOP_K = 4
SWIGLU_LIMIT = 7.0
SWIGLU_ALPHA = 1.702
LANES = 128
SUBLANES = 8
EXPERT_ROWS = 512
ISSUE_UNROLL = 8
NEG_BIG = -1e30
VMEM_LIMIT = 56 * 1024 * 1024


def _params(n_axes):
    return pltpu.CompilerParams(dimension_semantics=("arbitrary",) * n_axes,
                                vmem_limit_bytes=VMEM_LIMIT)


def _pick(n, candidates):
    for c in candidates:
        if n % c == 0:
            return c
    return n


def _layer_norm(r, g, b):
    mu = jnp.mean(r, axis=-1, keepdims=True)
    d = r - mu
    var = jnp.mean(d * d, axis=-1, keepdims=True)
    return d * lax.rsqrt(var + LN_EPS) * g + b


def _sigmoid(x):
    return 1.0 / (1.0 + jnp.exp(-x))


def _mod_kernel(cs_ref, w_ref, b_ref, o_ref):
    cs = cs_ref[...]
    a = cs * _sigmoid(cs)
    o_ref[0] = jnp.dot(a, w_ref[0], precision=HIGHEST, preferred_element_type=F32) + b_ref[0]


def _modulation(c, c_ctx, w_mod, b_mod):
    depth, d, n = w_mod.shape
    b = c.shape[0]
    cs = jnp.concatenate([c, c_ctx[None], jnp.zeros((SUBLANES - b - 1, d), F32)], 0)
    tn = _pick(n, (1536, 1024, 512))
    return pl.pallas_call(
        _mod_kernel,
        grid=(depth, n // tn),
        in_specs=[pl.BlockSpec((SUBLANES, d), lambda l, j: (0, 0)),
                  pl.BlockSpec((1, d, tn), lambda l, j: (l, 0, j)),
                  pl.BlockSpec((1, 1, tn), lambda l, j: (l, 0, j))],
        out_specs=pl.BlockSpec((1, SUBLANES, tn), lambda l, j: (l, 0, j)),
        out_shape=jax.ShapeDtypeStruct((depth, SUBLANES, n), F32),
        compiler_params=_params(2),
        name="modulation",
    )(cs, w_mod, b_mod.reshape(depth, 1, n))


def _even_in_kernel(x_ref, sc_ref, sh_ref, w_ref, cos_ref, sin_ref, q_ref, k_ref, v_ref, a_ref):
    u = (x_ref[0] * (1.0 + sc_ref[0]) + sh_ref[0]).astype(BF16)
    z = jnp.dot(u, w_ref[...], preferred_element_type=F32)
    cosv = cos_ref[...]
    sinv = sin_ref[...]
    lane = lax.broadcasted_iota(I32, cosv.shape, 1)
    first = (lane % 32) < 16

    def rope(t):
        partner = jnp.where(first, pltpu.roll(t, LANES - 16, 1), pltpu.roll(t, 16, 1))
        return t * cosv + partner * sinv

    for j in range(QK_W // LANES):
        sl = slice(j * LANES, (j + 1) * LANES)
        rq = rope(z[:, sl]) * (HEAD_DIM ** -0.5 * math.log2(math.e))
        for c in range(2):
            q_ref[c, 0, :, sl] = jnp.where(lane // HEAD_DIM == c, rq, 0.0).astype(BF16)
        k_ref[0, :, sl] = rope(z[:, QK_W + j * LANES:QK_W + (j + 1) * LANES]).astype(BF16)
    v_ref[0] = z[:, 2 * QK_W:2 * QK_W + V_W].astype(BF16)
    a0 = 2 * QK_W + V_W
    a_ref[0] = z[:, a0:a0 + CONV_W] * _sigmoid(z[:, a0 + CONV_W:a0 + 2 * CONV_W])


def _rope_tables(length):
    rows = length // GRID_W
    row = jnp.repeat(jnp.arange(rows, dtype=F32), GRID_W)
    col = jnp.tile(jnp.arange(GRID_W, dtype=F32), rows)
    n_freq = HEAD_DIM // 4
    inv_freq = ROPE_BASE ** (-jnp.arange(n_freq, dtype=F32) / n_freq)
    ar = row[:, None] * inv_freq
    ac = col[:, None] * inv_freq
    cos64 = jnp.concatenate([jnp.cos(ar), jnp.cos(ar), jnp.cos(ac), jnp.cos(ac)], 1)
    sin64 = jnp.concatenate([-jnp.sin(ar), jnp.sin(ar), -jnp.sin(ac), jnp.sin(ac)], 1)
    return jnp.tile(cos64, (1, LANES // HEAD_DIM)), jnp.tile(sin64, (1, LANES // HEAD_DIM))


def _even_in_proj(x, sc, sh, w_bf, cos_t, sin_t):
    b, length, d = x.shape
    n = w_bf.shape[1]
    tm = _pick(length, (512, 256, 128))
    row = lambda bi, i: (bi, i, 0)
    vec = lambda bi, i: (bi, 0, 0)
    return pl.pallas_call(
        _even_in_kernel,
        grid=(b, length // tm),
        in_specs=[pl.BlockSpec((1, tm, d), row),
                  pl.BlockSpec((1, 1, d), vec),
                  pl.BlockSpec((1, 1, d), vec),
                  pl.BlockSpec((d, n), lambda bi, i: (0, 0)),
                  pl.BlockSpec((tm, LANES), lambda bi, i: (i, 0)),
                  pl.BlockSpec((tm, LANES), lambda bi, i: (i, 0))],
        out_specs=[pl.BlockSpec((2, 1, tm, QK_W), lambda bi, i: (0, bi, i, 0)), pl.BlockSpec((1, tm, QK_W), row),
                   pl.BlockSpec((1, tm, V_W), row), pl.BlockSpec((1, tm, CONV_W), row)],
        out_shape=[jax.ShapeDtypeStruct((2, b, length, QK_W), BF16), jax.ShapeDtypeStruct((b, length, QK_W), BF16),
                   jax.ShapeDtypeStruct((b, length, V_W), BF16), jax.ShapeDtypeStruct((b, length, CONV_W), F32)],
        compiler_params=_params(2),
        name="even_in_proj",
    )(x, sc, sh, w_bf, cos_t, sin_t)


def _modmm_kernel(x_ref, sc_ref, sh_ref, w_ref, o_ref):
    u = (x_ref[0] * (1.0 + sc_ref[0]) + sh_ref[0]).astype(BF16)
    o_ref[0] = jnp.dot(u, w_ref[...], preferred_element_type=F32).astype(o_ref.dtype)


def _mod_matmul(x, sc, sh, w_bf, out_dtype):
    b, rows, d = x.shape
    n = w_bf.shape[1]
    tm = _pick(rows, (512, 256, 128))
    return pl.pallas_call(
        _modmm_kernel,
        grid=(b, rows // tm),
        in_specs=[pl.BlockSpec((1, tm, d), lambda bi, i: (bi, i, 0)),
                  pl.BlockSpec((1, 1, d), lambda bi, i: (bi, 0, 0)),
                  pl.BlockSpec((1, 1, d), lambda bi, i: (bi, 0, 0)),
                  pl.BlockSpec((d, n), lambda bi, i: (0, 0))],
        out_specs=pl.BlockSpec((1, tm, n), lambda bi, i: (bi, i, 0)),
        out_shape=jax.ShapeDtypeStruct((b, rows, n), out_dtype),
        compiler_params=_params(2),
        name="context_kv_proj",
    )(x, sc, sh, w_bf)


VT_ROWS = HEAD_V + 16


def _attn_kernel(lq1_ref, lk1_ref, lq2_ref, lk2_ref, g_ref, q_ref, k_ref, vt_ref, o_ref,
                 s0_sc, s1_sc, p0_sc, p1_sc, acc_sc, *, lam_init, kc):
    mq = q_ref.shape[2]
    n = k_ref.shape[1] // kc
    s_sc = (s0_sc, s1_sc)
    p_sc = (p0_sc, p1_sc)

    def scores(j, slot):
        kj = k_ref[0, pl.ds(pl.multiple_of(j * kc, kc), kc), :]
        for c in range(2):
            s_sc[slot][c] = lax.dot_general(kj, q_ref[c, 0], (((1,), (1,)), ((), ())),
                                            preferred_element_type=F32)

    def softmax(slot, m):
        m_out, alpha = [], []
        for c in range(2):
            s = s_sc[slot][c]
            m_new = jnp.maximum(m[c], jnp.max(s, axis=0, keepdims=True))
            p_sc[slot][c] = jnp.exp2((s - m_new).astype(BF16))
            m_out.append(m_new)
            alpha.append(jnp.exp2(m[c] - m_new))
        return tuple(m_out), tuple(alpha)

    def accumulate(j, slot, alpha):
        vt = vt_ref[0, 0, j]
        for c in range(2):
            acc_sc[c] = alpha[c] * acc_sc[c] + jnp.dot(vt, p_sc[slot][c], preferred_element_type=F32)

    def step(j, slot, m, alpha):
        scores(j + 2, slot)
        m, alpha_next = softmax(1 - slot, m)
        accumulate(j, slot, alpha)
        return m, alpha_next

    acc_sc[...] = jnp.zeros(acc_sc.shape, F32)
    m = (jnp.full((1, mq), NEG_BIG, F32),) * 2
    scores(0, 0)
    m, alpha = softmax(0, m)
    scores(1, 1)

    def pair(t, carry):
        m, alpha = step(2 * t, 0, *carry)
        return step(2 * t + 1, 1, m, alpha)

    n_steps = n - 2
    m, alpha = lax.fori_loop(0, n_steps // 2, pair, (m, alpha))
    if n_steps % 2:
        m, alpha = step(n_steps - 1, 0, m, alpha)
    m, alpha_last = softmax((n - 1) % 2, m)
    accumulate(n - 2, (n - 2) % 2, alpha)
    accumulate(n - 1, (n - 1) % 2, alpha_last)
    acc0 = acc_sc[0]
    acc1 = acc_sc[1]
    lam = (jnp.exp(jnp.sum(lq1_ref[...] * lk1_ref[...], keepdims=True))
           - jnp.exp(jnp.sum(lq2_ref[...] * lk2_ref[...], keepdims=True)) + lam_init)
    o = (acc0[:HEAD_V] / acc0[HEAD_V:HEAD_V + 1]
         - lam * (acc1[:HEAD_V] / acc1[HEAD_V:HEAD_V + 1]))
    ms = jnp.mean(o * o, axis=0, keepdims=True)
    o = o * lax.rsqrt(ms + LN_EPS) * g_ref[...] * (1.0 - lam_init)
    o_ref[0] = o.T.astype(o_ref.dtype)


def _diff_attention(q, k_all, v_all, lq1, lk1, lq2, lk2, norm_g, lam_init):
    _, b, length, _ = q.shape
    n_keys = k_all.shape[1]
    mq = _pick(length, (1024, 512, 256, 128))
    kc = next(c for c in (1280, 256, 128) if n_keys % c == 0 and n_keys // c >= 3)
    nc = n_keys // kc
    vt = v_all.reshape(b, nc, kc, N_HEADS, HEAD_V).transpose(0, 3, 1, 4, 2)
    extra = jnp.zeros((b, N_HEADS, nc, VT_ROWS - HEAD_V, kc), BF16).at[:, :, :, 0, :].set(1.0)
    vt = jnp.concatenate([vt, extra], 3)
    small = lambda bi, h, i: (0, 0)
    return pl.pallas_call(
        functools.partial(_attn_kernel, lam_init=lam_init, kc=kc),
        grid=(b, N_HEADS, length // mq),
        in_specs=[pl.BlockSpec((1, HEAD_DIM), small), pl.BlockSpec((1, HEAD_DIM), small),
                  pl.BlockSpec((1, HEAD_DIM), small), pl.BlockSpec((1, HEAD_DIM), small),
                  pl.BlockSpec((HEAD_V, 1), small),
                  pl.BlockSpec((2, 1, mq, HEAD_V), lambda bi, h, i: (0, bi, i, h)),
                  pl.BlockSpec((1, n_keys, HEAD_V), lambda bi, h, i: (bi, 0, h)),
                  pl.BlockSpec((1, 1, nc, VT_ROWS, kc), lambda bi, h, i: (bi, h, 0, 0, 0))],
        out_specs=pl.BlockSpec((1, mq, HEAD_V), lambda bi, h, i: (bi, i, h)),
        out_shape=jax.ShapeDtypeStruct((b, length, V_W), BF16),
        scratch_shapes=[pltpu.VMEM((2, kc, mq), F32), pltpu.VMEM((2, kc, mq), F32),
                        pltpu.VMEM((2, kc, mq), BF16), pltpu.VMEM((2, kc, mq), BF16),
                        pltpu.VMEM((2, VT_ROWS, mq), F32)],
        compiler_params=_params(3),
        name="diff_attention",
    )(lq1[None], lk1[None], lq2[None], lk2[None], norm_g[:, None], q, k_all, vt)


def _conv_kernel(prev_ref, cur_ref, next_ref, w_ref, b_ref, g_ref, beta_ref, o_ref, win, *, row_chunk):
    i = pl.program_id(1)
    tc = cur_ref.shape[1]
    span = tc + 2 * CONV_HALO - SUBLANES
    win[0, CONV_HALO:CONV_HALO + tc, :] = cur_ref[0]
    win[0, 0:CONV_HALO, :] = jnp.where(i > 0, prev_ref[0], 0.0)
    win[0, CONV_HALO + tc:, :] = jnp.where(i < pl.num_programs(1) - 1, next_ref[0], 0.0)
    for s in range(1, SUBLANES):
        win[s, 0:span, :] = win[0, s:s + span, :]
    first_tap = CONV_HALO - CONV_TAPS // 2
    for c0 in range(0, tc, row_chunk):
        acc = jnp.zeros((row_chunk, CONV_W), F32)
        for t in range(CONV_TAPS):
            s = (first_tap + t) % SUBLANES
            r0 = c0 + first_tap + t - s
            acc = acc + w_ref[t:t + 1, :] * win[s, r0:r0 + row_chunk, :]
        y = _layer_norm(acc + b_ref[...], g_ref[...], beta_ref[...])
        o_ref[0, c0:c0 + row_chunk, :] = (y * _sigmoid(y)).astype(o_ref.dtype)


def _conformer_conv(a, conv_w, conv_b, ln_g, ln_b):
    b, length, w = a.shape
    tc = _pick(length, (512, 256, 128))
    hb = tc // CONV_HALO
    n_halo = length // CONV_HALO
    w_pad = jnp.concatenate([conv_w, jnp.zeros((1, w), F32)], 0)
    vec = lambda bi, i: (0, 0)
    return pl.pallas_call(
        functools.partial(_conv_kernel, row_chunk=64),
        grid=(b, length // tc),
        in_specs=[pl.BlockSpec((1, CONV_HALO, w), lambda bi, i: (bi, jnp.maximum(i * hb - 1, 0), 0)),
                  pl.BlockSpec((1, tc, w), lambda bi, i: (bi, i, 0)),
                  pl.BlockSpec((1, CONV_HALO, w), lambda bi, i: (bi, jnp.minimum((i + 1) * hb, n_halo - 1), 0)),
                  pl.BlockSpec((CONV_TAPS + 1, w), vec),
                  pl.BlockSpec((1, w), vec), pl.BlockSpec((1, w), vec), pl.BlockSpec((1, w), vec)],
        out_specs=pl.BlockSpec((1, tc, w), lambda bi, i: (bi, i, 0)),
        out_shape=jax.ShapeDtypeStruct((b, length, w), BF16),
        scratch_shapes=[pltpu.VMEM((SUBLANES, tc + 2 * CONV_HALO, w), F32)],
        compiler_params=_params(2),
        name="conformer_conv",
    )(a, a, a, w_pad, conv_b[None], ln_g[None], ln_b[None])


def _post_mix_kernel(ya_ref, yb_ref, wa_ref, wb_ref, x_ref, g1_ref, lng_ref, lnb_ref, sc_ref, sh_ref,
                     rw_ref, rb_ref, h_ref, e_ref, gate_ref, pos_ref, cnt_ref, carry, *, alpha):
    first_step = jnp.logical_and(pl.program_id(0) == 0, pl.program_id(1) == 0)

    @pl.when(first_step)
    def _():
        carry[...] = jnp.zeros(carry.shape, F32)

    y = (jnp.dot(ya_ref[0].astype(BF16), wa_ref[...], preferred_element_type=F32)
         + jnp.dot(yb_ref[0].astype(BF16), wb_ref[...], preferred_element_type=F32))
    h = _layer_norm(alpha * x_ref[0] + g1_ref[0] * y, lng_ref[...], lnb_ref[...])
    h_ref[0] = h
    u2 = h * (1.0 + sc_ref[0]) + sh_ref[0]
    u_hi = u2.astype(BF16)
    u_lo = (u2 - u_hi.astype(F32)).astype(BF16)
    hh_hl = jnp.dot(u_hi, rw_ref[...], preferred_element_type=F32)
    lh = jnp.dot(u_lo, rw_ref[:, :LANES], preferred_element_type=F32)
    logits = hh_hl[:, :LANES] + hh_hl[:, LANES:] + lh + rb_ref[...]

    tm = logits.shape[0]
    lane = lax.broadcasted_iota(I32, logits.shape, 1).astype(F32)
    vals, idxs = [], []
    lg = logits
    for _ in range(TOP_K):
        mx = jnp.max(lg, axis=1, keepdims=True)
        idx = jnp.min(jnp.where(lg == mx, lane, float(LANES)), axis=1, keepdims=True)
        vals.append(mx)
        idxs.append(idx)
        lg = jnp.where(lane == idx, -3e38, lg)
    exps = [jnp.exp(v - vals[0]) for v in vals]
    den = exps[0] + exps[1] + exps[2] + exps[3]

    hot = jnp.zeros(logits.shape, F32)
    for idx in idxs:
        hot = hot + (lane == idx).astype(F32)
    r_i = lax.broadcasted_iota(I32, (tm, tm), 0)
    c_i = lax.broadcasted_iota(I32, (tm, tm), 1)
    tri = (c_i < r_i).astype(BF16)
    before = jnp.dot(tri, hot.astype(BF16), preferred_element_type=F32) + carry[...]
    e_out = jnp.zeros(logits.shape, F32)
    g_out = jnp.zeros(logits.shape, F32)
    p_out = jnp.zeros(logits.shape, F32)
    for j in range(TOP_K):
        pos_j = jnp.sum(jnp.where(lane == idxs[j], before, 0.0), axis=1, keepdims=True)
        sel = lane == float(j)
        e_out = jnp.where(sel, idxs[j], e_out)
        g_out = jnp.where(sel, exps[j] / den, g_out)
        p_out = jnp.where(sel, pos_j, p_out)
    e_ref[...] = e_out.T[:SUBLANES].astype(I32)
    gate_ref[...] = g_out
    pos_ref[...] = p_out.T[:SUBLANES].astype(I32)
    new_carry = carry[...] + jnp.sum(hot, axis=0, keepdims=True)
    carry[...] = new_carry
    cnt_ref[...] = jnp.broadcast_to(new_carry, cnt_ref.shape).astype(I32)


def _post_mix(ya, yb, w_out_bf, x, g1, ln_g, ln_b, sc2, sh2, router_w, router_b, alpha):
    b, length, d = x.shape
    wa = ya.shape[2]
    tm = _pick(length, (512, 256, 128))
    n_tok = b * length
    rw = jnp.concatenate([router_w, jnp.zeros((d, LANES - N_EXPERTS), F32)], 1)
    rw_hi = rw.astype(BF16)
    rw = jnp.concatenate([rw_hi, (rw - rw_hi.astype(F32)).astype(BF16)], 1)
    rb =jnp.concatenate([router_b, jnp.full((LANES - N_EXPERTS,), NEG_BIG, F32)])[None]
    row = lambda bi, i: (bi, i, 0)
    vec = lambda bi, i: (bi, 0, 0)
    const = lambda bi, i: (0, 0)
    nt = length // tm
    tok = lambda bi, i: (bi * nt + i, 0)
    return pl.pallas_call(
        functools.partial(_post_mix_kernel, alpha=alpha),
        grid=(b, nt),
        in_specs=[pl.BlockSpec((1, tm, wa), row), pl.BlockSpec((1, tm, d - wa), row),
                  pl.BlockSpec((wa, d), const), pl.BlockSpec((d - wa, d), const),
                  pl.BlockSpec((1, tm, d), row), pl.BlockSpec((1, 1, d), vec),
                  pl.BlockSpec((1, d), const), pl.BlockSpec((1, d), const),
                  pl.BlockSpec((1, 1, d), vec), pl.BlockSpec((1, 1, d), vec),
                  pl.BlockSpec((d, 2 * LANES), const), pl.BlockSpec((1, LANES), const)],
        out_specs=[pl.BlockSpec((1, tm, d), row), pl.BlockSpec((SUBLANES, tm), lambda bi, i: (0, bi * nt + i)),
                   pl.BlockSpec((tm, LANES), tok), pl.BlockSpec((SUBLANES, tm), lambda bi, i: (0, bi * nt + i)),
                   pl.BlockSpec((SUBLANES, LANES), const)],
        out_shape=[jax.ShapeDtypeStruct((b, length, d), F32), jax.ShapeDtypeStruct((SUBLANES, n_tok), I32),
                   jax.ShapeDtypeStruct((n_tok, LANES), F32), jax.ShapeDtypeStruct((SUBLANES, n_tok), I32),
                   jax.ShapeDtypeStruct((SUBLANES, LANES), I32)],
        scratch_shapes=[pltpu.VMEM((1, LANES), F32)],
        compiler_params=_params(2),
        name="out_proj_norm_router",
    )(ya, yb, w_out_bf[:wa], w_out_bf[wa:], x, g1, ln_g[None], ln_b[None], sc2, sh2, rw, rb)


def _tile_indices(dest, tm):
    n_tok = dest.shape[1]
    return dest.reshape(TOP_K, n_tok // tm, tm).transpose(1, 0, 2).reshape(n_tok // tm, 1, TOP_K * tm)


def _dispatch_kernel(zstart_ref, zvalid_ref, dest_ref, h_ref, sc_ref, sh_ref, xs_ref, u_sc, zero_sc, sem, zsem):
    tm = h_ref.shape[0]

    @pl.when(pl.program_id(0) == 0)
    def _():
        zero_sc[...] = jnp.zeros(zero_sc.shape, F32)

        def fill(e, c):
            @pl.when(zvalid_ref[e] > 0)
            def _():
                start = pl.multiple_of(zstart_ref[e], SUBLANES)
                pltpu.make_async_copy(zero_sc, xs_ref.at[pl.ds(start, EXPERT_ROWS)], zsem).start()
            return c

        lax.fori_loop(0, N_EXPERTS, fill, 0)

        def drain(e, c):
            @pl.when(zvalid_ref[e] > 0)
            def _():
                pltpu.make_async_copy(zero_sc, xs_ref.at[pl.ds(0, EXPERT_ROWS)], zsem).wait()
            return c

        lax.fori_loop(0, N_EXPERTS, drain, 0)

    u_sc[...] = h_ref[...] * (1.0 + sc_ref[0]) + sh_ref[0]

    def issue(g, c):
        for rr in range(ISSUE_UNROLL):
            r = g * ISSUE_UNROLL + rr
            for j in range(TOP_K):
                d = dest_ref[0, 0, j * tm + r]
                pltpu.make_async_copy(u_sc.at[pl.ds(r, 1)], xs_ref.at[pl.ds(d, 1)], sem).start()
        return c

    lax.fori_loop(0, tm // ISSUE_UNROLL, issue, 0)
    rows = xs_ref.at[pl.ds(0, tm * TOP_K)]
    pltpu.make_async_copy(rows, rows, sem).wait()


def _dispatch(h_flat, sc2, sh2, dest, zstart, zvalid, n_rows, tokens_per_batch):
    n_tok, d = h_flat.shape
    tm = _pick(tokens_per_batch, (512, 256, 128))
    per_b = tokens_per_batch // tm
    return pl.pallas_call(
        _dispatch_kernel,
        grid_spec=pltpu.PrefetchScalarGridSpec(
            num_scalar_prefetch=2,
            grid=(n_tok // tm,),
            in_specs=[pl.BlockSpec((1, 1, tm * TOP_K), lambda i, zs, zv: (i, 0, 0), memory_space=pltpu.SMEM),
                      pl.BlockSpec((tm, d), lambda i, zs, zv: (i, 0)),
                      pl.BlockSpec((1, 1, d), lambda i, zs, zv: (i // per_b, 0, 0)),
                      pl.BlockSpec((1, 1, d), lambda i, zs, zv: (i // per_b, 0, 0))],
            out_specs=pl.BlockSpec(memory_space=pl.ANY),
            scratch_shapes=[pltpu.VMEM((tm, d), F32), pltpu.VMEM((EXPERT_ROWS, d), F32),
                            pltpu.SemaphoreType.DMA(()), pltpu.SemaphoreType.DMA(())]),
        out_shape=jax.ShapeDtypeStruct((n_rows, d), F32),
        compiler_params=_params(1),
        name="moe_dispatch",
    )(zstart, zvalid, _tile_indices(dest, tm), h_flat, sc2, sh2)


def _expert_kernel(blk_e_ref, n_used_ref, x_ref, wgu_ref, bgu_ref, wd_ref, bd_ref, o_ref, wgu_bf, wd_bf):
    i = pl.program_id(0)
    new_expert = jnp.logical_or(i == 0, blk_e_ref[i] != blk_e_ref[jnp.maximum(i - 1, 0)])

    @pl.when(new_expert)
    def _():
        wgu_bf[...] = wgu_ref[0].astype(BF16)
        wd_bf[...] = wd_ref[0].astype(BF16)

    @pl.when(i < n_used_ref[0])
    def _():
        d_e = wd_ref.shape[1]
        gu = jnp.dot(x_ref[...].astype(BF16), wgu_bf[...], preferred_element_type=F32) + bgu_ref[0]
        x_glu = jnp.minimum(gu[:, :d_e], SWIGLU_LIMIT)
        x_lin = jnp.clip(gu[:, d_e:], -SWIGLU_LIMIT, SWIGLU_LIMIT)
        act = x_glu * _sigmoid(SWIGLU_ALPHA * x_glu) * (x_lin + 1.0)
        o_ref[...] = jnp.dot(act.astype(BF16), wd_bf[...], preferred_element_type=F32) + bd_ref[0]


def _experts(xs, blk_e, n_used, wgu, bgu, wd, bd, layer):
    n_rows, d = xs.shape
    depth, n_e, _, n_gu = wgu.shape
    n_blk = n_rows // EXPERT_ROWS
    rows = lambda i, be, nu: (jnp.minimum(i, nu[0] - 1), 0)
    first = layer * n_e
    by_e = lambda i, be, nu: (first + be[i], 0, 0)
    n_e = depth * n_e
    wgu = wgu.reshape(n_e, d, n_gu)
    wd = wd.reshape(n_e, n_gu // 2, d)
    return pl.pallas_call(
        _expert_kernel,
        grid_spec=pltpu.PrefetchScalarGridSpec(
            num_scalar_prefetch=2,
            grid=(n_blk,),
            in_specs=[pl.BlockSpec((EXPERT_ROWS, d), rows),
                      pl.BlockSpec((1, d, n_gu), by_e), pl.BlockSpec((1, 1, n_gu), by_e),
                      pl.BlockSpec((1, n_gu // 2, d), by_e), pl.BlockSpec((1, 1, d), by_e)],
            out_specs=pl.BlockSpec((EXPERT_ROWS, d), rows),
            scratch_shapes=[pltpu.VMEM((d, n_gu), BF16), pltpu.VMEM((n_gu // 2, d), BF16)]),
        out_shape=jax.ShapeDtypeStruct((n_rows, d), F32),
        compiler_params=_params(1),
        name="moe_experts",
    )(blk_e, n_used, xs, wgu, bgu.reshape(n_e, 1, n_gu), wd, bd.reshape(n_e, 1, d))


def _combine_kernel(dest_ref, dest_next_ref, y_ref, h_ref, gate_ref, g2_ref, lng_ref, lnb_ref, o_ref,
                    buf, sem, *, alpha):
    i = pl.program_id(0)
    tm = h_ref.shape[0]

    def fetch(idx_ref, slot):
        def issue(g, c):
            for rr in range(ISSUE_UNROLL):
                r = g * ISSUE_UNROLL + rr
                for j in range(TOP_K):
                    d = idx_ref[0, 0, j * tm + r]
                    pltpu.make_async_copy(y_ref.at[pl.ds(d, 1)], buf.at[slot, j, pl.ds(r, 1)], sem.at[slot]).start()
            return c

        lax.fori_loop(0, tm // ISSUE_UNROLL, issue, 0)

    slot = i % 2

    @pl.when(i == 0)
    def _():
        fetch(dest_ref, 0)

    @pl.when(i + 1 < pl.num_programs(0))
    def _():
        fetch(dest_next_ref, 1 - slot)

    pltpu.make_async_copy(buf.at[slot], buf.at[slot], sem.at[slot]).wait()
    gate = gate_ref[...]
    y2 = gate[:, 0:1] * buf[slot, 0]
    for j in range(1, TOP_K):
        y2 = y2 + gate[:, j:j + 1] * buf[slot, j]
    o_ref[...] = _layer_norm(alpha * h_ref[...] + g2_ref[0] * y2, lng_ref[...], lnb_ref[...])


def _combine(y_rows, dest, h_flat, gate, g2, ln_g, ln_b, alpha, tokens_per_batch):
    n_tok, d = h_flat.shape
    tm = _pick(tokens_per_batch, (256, 128))
    per_b = tokens_per_batch // tm
    n_tiles = n_tok // tm
    dest3 = _tile_indices(dest, tm)
    return pl.pallas_call(
        functools.partial(_combine_kernel, alpha=alpha),
        grid=(n_tiles,),
        in_specs=[pl.BlockSpec((1, 1, tm * TOP_K), lambda i: (i, 0, 0), memory_space=pltpu.SMEM),
                  pl.BlockSpec((1, 1, tm * TOP_K), lambda i: (jnp.minimum(i + 1, n_tiles - 1), 0, 0),
                               memory_space=pltpu.SMEM),
                  pl.BlockSpec(memory_space=pl.ANY),
                  pl.BlockSpec((tm, d), lambda i: (i, 0)),
                  pl.BlockSpec((tm, LANES), lambda i: (i, 0)),
                  pl.BlockSpec((1, 1, d), lambda i: (i // per_b, 0, 0)),
                  pl.BlockSpec((1, d), lambda i: (0, 0)), pl.BlockSpec((1, d), lambda i: (0, 0))],
        out_specs=pl.BlockSpec((tm, d), lambda i: (i, 0)),
        out_shape=jax.ShapeDtypeStruct((n_tok, d), F32),
        scratch_shapes=[pltpu.VMEM((2, TOP_K, tm, d), F32), pltpu.SemaphoreType.DMA((2,))],
        compiler_params=_params(1),
        name="moe_combine_norm",
    )(dest3, dest3, y_rows, h_flat, gate, g2, ln_g[None], ln_b[None])


def _moe_and_norm(h1, top_e, gate, pos, counts, sc2, sh2, g2, wgu, bgu, wd, bd, layer, ln_g, ln_b, alpha):
    b, length, d = h1.shape
    n_tok = b * length
    n_blk = n_tok * TOP_K // EXPERT_ROWS + N_EXPERTS
    padded = (counts + EXPERT_ROWS - 1) // EXPERT_ROWS * EXPERT_ROWS
    pad_end = jnp.cumsum(padded)
    offset = pad_end - padded
    n_used = (pad_end[-1:] // EXPERT_ROWS).astype(I32)
    blk_start = jnp.arange(n_blk, dtype=I32) * EXPERT_ROWS
    blk_e = jnp.minimum(jnp.sum(pad_end[None, :] <= blk_start[:, None], axis=1), N_EXPERTS - 1).astype(I32)
    dest = (offset[top_e] + pos).astype(I32)
    zstart = jnp.maximum(pad_end - EXPERT_ROWS, 0).astype(I32)
    zvalid = (counts > 0).astype(I32)

    h_flat = h1.reshape(n_tok, d)
    xs = _dispatch(h_flat, sc2, sh2, dest, zstart, zvalid, n_blk * EXPERT_ROWS, length)
    y_rows = _experts(xs, blk_e, n_used, wgu, bgu, wd, bd, layer)
    out = _combine(y_rows, dest, h_flat, gate, g2, ln_g, ln_b, alpha, length)
    return out.reshape(b, length, d)


def _odd_in_kernel(x_ref, sc_ref, sh_ref, w_ref, glg_ref, glb_ref, ws_ref, bs_ref, flg_ref, flb_ref, dft_ref,
                   sp_ref, zr_ref, zi_ref):
    u = (x_ref[0] * (1.0 + sc_ref[0]) + sh_ref[0]).astype(BF16)
    z = jnp.dot(u, w_ref[...], preferred_element_type=F32)
    tm = z.shape[0]
    gw = N_GROUPS * GROUP_W
    ug = jax.nn.gelu(z[:, :gw])
    vn = _layer_norm(jax.nn.gelu(z[:, gw:2 * gw]), glg_ref[...], glb_ref[...]).astype(BF16)
    f = z[:, 2 * gw:]
    for g in range(N_GROUPS):
        cols = slice(g * GROUP_W, (g + 1) * GROUP_W)
        for c0 in range(0, tm, CHUNK):
            rows = slice(c0, c0 + CHUNK)
            sv = jnp.dot(ws_ref[g], vn[rows, cols], preferred_element_type=F32) + bs_ref[:, cols]
            sp_ref[0, rows, cols] = (ug[rows, cols] * sv).astype(sp_ref.dtype)
        fn = _layer_norm(f[:, cols], flg_ref[:, cols], flb_ref[:, cols]).astype(BF16)
        zz = jnp.dot(fn, dft_ref[...], preferred_element_type=F32)
        zr_ref[0, :, cols] = zz[:, :GROUP_W]
        zi_ref[0, :, cols] = zz[:, GROUP_W:]


def _odd_in_proj(x, sc, sh, w_bf, gln_g, gln_b, ws, bs, fln_g, fln_b):
    b, length, d = x.shape
    n = w_bf.shape[1]
    gw = N_GROUPS * GROUP_W
    tm = _pick(length, (512, 256, 128))
    kk = jnp.arange(GROUP_W, dtype=I32)
    ang = (2.0 * math.pi / GROUP_W) * ((kk[:, None] * kk[None, :]) % GROUP_W).astype(F32)
    dft = jnp.concatenate([jnp.cos(ang), -jnp.sin(ang)], 1).astype(BF16)
    bs_exp = jnp.repeat(bs.T, GROUP_W, axis=1)
    row = lambda bi, i: (bi, i, 0)
    vec = lambda bi, i: (bi, 0, 0)
    const2 = lambda bi, i: (0, 0)
    return pl.pallas_call(
        _odd_in_kernel,
        grid=(b, length // tm),
        in_specs=[pl.BlockSpec((1, tm, d), row), pl.BlockSpec((1, 1, d), vec), pl.BlockSpec((1, 1, d), vec),
                  pl.BlockSpec((d, n), const2),
                  pl.BlockSpec((1, gw), const2), pl.BlockSpec((1, gw), const2),
                  pl.BlockSpec((N_GROUPS, CHUNK, CHUNK), lambda bi, i: (0, 0, 0)),
                  pl.BlockSpec((CHUNK, gw), const2),
                  pl.BlockSpec((1, gw), const2), pl.BlockSpec((1, gw), const2),
                  pl.BlockSpec((GROUP_W, 2 * GROUP_W), const2)],
        out_specs=[pl.BlockSpec((1, tm, gw), row), pl.BlockSpec((1, tm, gw), row), pl.BlockSpec((1, tm, gw), row)],
        out_shape=[jax.ShapeDtypeStruct((b, length, gw), BF16), jax.ShapeDtypeStruct((b, length, gw), F32),
                   jax.ShapeDtypeStruct((b, length, gw), F32)],
        compiler_params=_params(2),
        name="odd_in_proj",
    )(x, sc, sh, w_bf, gln_g[None], gln_b[None], ws.astype(BF16), bs_exp, fln_g[None], fln_b[None], dft)


def _fft_a_kernel(zr_ref, zi_ref, c_ref, s_ref, ar_ref, ai_ref):
    l1, nb, w = zr_ref.shape[1:]
    zr = zr_ref[0].reshape(l1 * nb, w).astype(BF16)
    zi = zi_ref[0].reshape(l1 * nb, w).astype(BF16)
    c = c_ref[...]
    s = s_ref[...]
    ar = jnp.dot(c, zr, preferred_element_type=F32) + jnp.dot(s, zi, preferred_element_type=F32)
    ai = jnp.dot(c, zi, preferred_element_type=F32) - jnp.dot(s, zr, preferred_element_type=F32)
    ar_ref[0] = ar.reshape(l1, nb, w)
    ai_ref[0] = ai.reshape(l1, nb, w)


def _fft_b_kernel(ar_ref, ai_ref, c_ref, s_ref, o_ref, *, norm):
    nb, l2, w = ar_ref.shape[1:]
    ar = ar_ref[0].reshape(nb * l2, w).astype(BF16)
    ai = ai_ref[0].reshape(nb * l2, w).astype(BF16)
    y = jnp.dot(c_ref[0], ar, preferred_element_type=F32) + jnp.dot(s_ref[0], ai, preferred_element_type=F32)
    o_ref[0] = (y * norm).reshape(l2, nb, w)


def _length_dft_real(zr, zi):
    b, length, w = zr.shape
    l2 = CHUNK
    l1 = length // l2
    nb = SUBLANES
    k1 = jnp.arange(l1, dtype=I32)
    ang_a = (2.0 * math.pi / l1) * ((k1[:, None] * k1[None, :]) % l1).astype(F32)
    r = lax.broadcasted_iota(I32, (l1 * nb, l1 * nb), 0)
    c = lax.broadcasted_iota(I32, (l1 * nb, l1 * nb), 1)
    same = r % nb == c % nb
    rep = (lax.broadcasted_iota(I32, (l1 * nb, l1), 0) // nb == lax.broadcasted_iota(I32, (l1 * nb, l1), 1))
    rep = rep.astype(BF16)

    def expand_a(t):
        t = jnp.dot(jnp.dot(rep, t.astype(BF16), preferred_element_type=F32).astype(BF16), rep.T,
                    preferred_element_type=F32)
        return jnp.where(same, t, 0.0).astype(BF16)

    cos_a = expand_a(jnp.cos(ang_a))
    sin_a = expand_a(jnp.sin(ang_a))
    shape_t = (l1 // nb, l2 * nb, l2)
    kb = lax.broadcasted_iota(I32, shape_t, 0)
    r = lax.broadcasted_iota(I32, shape_t, 1)
    n2 = lax.broadcasted_iota(I32, shape_t, 2)
    ang_b = (2.0 * math.pi / length) * ((n2 * (kb * nb + r % nb + l1 * (r // nb))) % length).astype(F32)
    shape_b = (l1 // nb, l2 * nb, nb * l2)
    same = lax.broadcasted_iota(I32, shape_b, 1) % nb == lax.broadcasted_iota(I32, shape_b, 2) // l2

    def expand_b(t):
        return jnp.where(same, jnp.tile(t, (1, 1, nb)), 0.0).astype(BF16)

    cos_t, sin_t = lax.optimization_barrier((jnp.cos(ang_b), jnp.sin(ang_b)))
    cos_b = expand_b(cos_t)
    sin_b = expand_b(sin_t)

    zr4 = zr.reshape(b, l1, l2, w)
    zi4 = zi.reshape(b, l1, l2, w)
    blk_a = pl.BlockSpec((1, l1, nb, w), lambda bi, i: (bi, 0, i, 0))
    ar, ai = pl.pallas_call(
        _fft_a_kernel,
        grid=(b, l2 // nb),
        in_specs=[blk_a, blk_a, pl.BlockSpec((l1 * nb, l1 * nb), lambda bi, i: (0, 0)),
                  pl.BlockSpec((l1 * nb, l1 * nb), lambda bi, i: (0, 0))],
        out_specs=[blk_a, blk_a],
        out_shape=[jax.ShapeDtypeStruct((b, l1, l2, w), F32)] * 2,
        compiler_params=_params(2),
        name="fourier_stage_a",
    )(zr4, zi4, cos_a, sin_a)
    blk_b = pl.BlockSpec((1, nb, l2, w), lambda bi, i: (bi, i, 0, 0))
    out = pl.pallas_call(
        functools.partial(_fft_b_kernel, norm=float((length * GROUP_W) ** -0.5)),
        grid=(b, l1 // nb),
        in_specs=[blk_b, blk_b, pl.BlockSpec((1, l2 * nb, nb * l2), lambda bi, i: (i, 0, 0)),
                  pl.BlockSpec((1, l2 * nb, nb * l2), lambda bi, i: (i, 0, 0))],
        out_specs=pl.BlockSpec((1, l2, nb, w), lambda bi, i: (bi, 0, i, 0)),
        out_shape=jax.ShapeDtypeStruct((b, l2, l1, w), F32),
        compiler_params=_params(2),
        name="fourier_stage_b",
    )(ar, ai, cos_b, sin_b)
    return out.reshape(b, length, w)


def kernel(x, c, ctx, c_ctx, w_mod, b_mod, ln1_g, ln1_b, ln2_g, ln2_b, ev_w_in, ev_w_out, conv_w, conv_b, conv_ln_g, conv_ln_b, lam_q1, lam_k1, lam_q2, lam_k2, diff_norm_g, od_w_in, od_w_out, gmlp_ln_g, gmlp_ln_b, gmlp_ws, gmlp_bs, four_ln_g, four_ln_b, router_w, router_b, w_gate_up, b_gate_up, w_down, b_down):
    b, length, d = x.shape
    depth = w_mod.shape[0]
    alpha = float((2 * depth) ** 0.25)
    assert b + 1 <= SUBLANES and length % (CHUNK * SUBLANES) == 0 and length % GRID_W == 0

    mod = _modulation(c, c_ctx, w_mod, b_mod)
    cos_t, sin_t = _rope_tables(length)
    h = x
    for layer in range(depth):
        j = layer // 2
        m = mod[layer]
        sh1, sc1, g1, sh2, sc2, g2 = [m[:b, i * d:(i + 1) * d][:, None, :] for i in range(6)]
        if layer % 2 == 0:
            lam_init = 0.8 - 0.6 * math.exp(-0.3 * layer)
            w_in = ev_w_in[j].astype(BF16)
            q, k, v, a = _even_in_proj(h, sc1, sh1, w_in, cos_t, sin_t)
            csh1 = jnp.broadcast_to(m[b:b + 1, 0:d][:, None, :], (b, 1, d))
            csc1 = jnp.broadcast_to(m[b:b + 1, d:2 * d][:, None, :], (b, 1, d))
            kv_c = _mod_matmul(ctx, csc1, csh1, w_in[:, QK_W:2 * QK_W + V_W], BF16)
            k_all = jnp.concatenate([k, kv_c[..., :QK_W]], 1)
            v_all = jnp.concatenate([v, kv_c[..., QK_W:]], 1)
            att = _diff_attention(q, k_all, v_all, lam_q1[j], lam_k1[j], lam_q2[j], lam_k2[j],
                                  diff_norm_g[j], lam_init)
            conv = _conformer_conv(a, conv_w[j], conv_b[j], conv_ln_g[j], conv_ln_b[j])
            ya, yb, w_out = conv, att, ev_w_out[j]
        else:
            spatial, zr, zi = _odd_in_proj(h, sc1, sh1, od_w_in[j].astype(BF16), gmlp_ln_g[j], gmlp_ln_b[j],
                                           gmlp_ws[j], gmlp_bs[j], four_ln_g[j], four_ln_b[j])
            ya, yb, w_out = spatial, _length_dft_real(zr, zi), od_w_out[j]
        h1, top_e, gate, pos, counts = _post_mix(ya, yb, w_out.astype(BF16), h, g1, ln1_g[layer], ln1_b[layer],
                                                 sc2, sh2, router_w[layer], router_b[layer], alpha)
        h = _moe_and_norm(h1, top_e[:TOP_K], gate, pos[:TOP_K], counts[0, :N_EXPERTS], sc2, sh2, g2,
                          w_gate_up, b_gate_up, w_down, b_down, layer, ln2_g[layer], ln2_b[layer], alpha)
    return h
```

```python
import functools
import math

import jax
import jax.numpy as jnp
from jax import lax
from jax.experimental import pallas as pl
from jax.experimental.pallas import tpu as pltpu

F32 = jnp.float32
BF16 = jnp.bfloat16
I32 = jnp.int32
HIGHEST = lax.Precision.HIGHEST

LN_EPS = 1e-5
GRID_W = 64
HEAD_DIM = 64
HEAD_V = 128
N_HEADS = 4
QK_W = N_HEADS * 2 * HEAD_DIM
V_W = N_HEADS * HEAD_V
CONV_W = 512
CONV_TAPS = 31
CONV_HALO = 16
ROPE_BASE = 10000.0
CHUNK = 128
N_GROUPS = 4
GROUP_W = 128
N_EXPERTS = 32
TOP_K = 4
SWIGLU_LIMIT = 7.0
SWIGLU_ALPHA = 1.702
LANES = 128
SUBLANES = 8
EXPERT_ROWS = 512
ISSUE_UNROLL = 8
NEG_BIG = -1e30
VMEM_LIMIT = 56 * 1024 * 1024


def _params(n_axes):
    return pltpu.CompilerParams(dimension_semantics=("arbitrary",) * n_axes,
                                vmem_limit_bytes=VMEM_LIMIT)


def _pick(n, candidates):
    for c in candidates:
        if n % c == 0:
            return c
    return n


def _layer_norm(r, g, b):
    mu = jnp.mean(r, axis=-1, keepdims=True)
    d = r - mu
    var = jnp.mean(d * d, axis=-1, keepdims=True)
    return d * lax.rsqrt(var + LN_EPS) * g + b


def _sigmoid(x):
    return 1.0 / (1.0 + jnp.exp(-x))


def _mod_kernel(cs_ref, w_ref, b_ref, o_ref):
    cs = cs_ref[...]
    a = cs * _sigmoid(cs)
    o_ref[0] = jnp.dot(a, w_ref[0], precision=HIGHEST, preferred_element_type=F32) + b_ref[0]


def _modulation(c, c_ctx, w_mod, b_mod):
    depth, d, n = w_mod.shape
    b = c.shape[0]
    cs = jnp.concatenate([c, c_ctx[None], jnp.zeros((SUBLANES - b - 1, d), F32)], 0)
    tn = _pick(n, (1536, 1024, 512))
    return pl.pallas_call(
        _mod_kernel,
        grid=(depth, n // tn),
        in_specs=[pl.BlockSpec((SUBLANES, d), lambda l, j: (0, 0)),
                  pl.BlockSpec((1, d, tn), lambda l, j: (l, 0, j)),
                  pl.BlockSpec((1, 1, tn), lambda l, j: (l, 0, j))],
        out_specs=pl.BlockSpec((1, SUBLANES, tn), lambda l, j: (l, 0, j)),
        out_shape=jax.ShapeDtypeStruct((depth, SUBLANES, n), F32),
        compiler_params=_params(2),
        name="modulation",
    )(cs, w_mod, b_mod.reshape(depth, 1, n))


def _even_in_kernel(x_ref, sc_ref, sh_ref, w_ref, cos_ref, sin_ref, q_ref, k_ref, v_ref, a_ref):
    u = (x_ref[0] * (1.0 + sc_ref[0]) + sh_ref[0]).astype(BF16)
    z = jnp.dot(u, w_ref[...], preferred_element_type=F32)
    cosv = cos_ref[...]
    sinv = sin_ref[...]
    lane = lax.broadcasted_iota(I32, cosv.shape, 1)
    first = (lane % 32) < 16

    def rope(t):
        partner = jnp.where(first, pltpu.roll(t, LANES - 16, 1), pltpu.roll(t, 16, 1))
        return t * cosv + partner * sinv

    for j in range(QK_W // LANES):
        sl = slice(j * LANES, (j + 1) * LANES)
        rq = rope(z[:, sl]) * (HEAD_DIM ** -0.5 * math.log2(math.e))
        for c in range(2):
            q_ref[c, 0, :, sl] = jnp.where(lane // HEAD_DIM == c, rq, 0.0).astype(BF16)
        k_ref[0, :, sl] = rope(z[:, QK_W + j * LANES:QK_W + (j + 1) * LANES]).astype(BF16)
    v_ref[0] = z[:, 2 * QK_W:2 * QK_W + V_W].astype(BF16)
    a0 = 2 * QK_W + V_W
    a_ref[0] = z[:, a0:a0 + CONV_W] * _sigmoid(z[:, a0 + CONV_W:a0 + 2 * CONV_W])


def _rope_tables(length):
    rows = length // GRID_W
    row = jnp.repeat(jnp.arange(rows, dtype=F32), GRID_W)
    col = jnp.tile(jnp.arange(GRID_W, dtype=F32), rows)
    n_freq = HEAD_DIM // 4
    inv_freq = ROPE_BASE ** (-jnp.arange(n_freq, dtype=F32) / n_freq)
    ar = row[:, None] * inv_freq
    ac = col[:, None] * inv_freq
    cos64 = jnp.concatenate([jnp.cos(ar), jnp.cos(ar), jnp.cos(ac), jnp.cos(ac)], 1)
    sin64 = jnp.concatenate([-jnp.sin(ar), jnp.sin(ar), -jnp.sin(ac), jnp.sin(ac)], 1)
    return jnp.tile(cos64, (1, LANES // HEAD_DIM)), jnp.tile(sin64, (1, LANES // HEAD_DIM))


def _even_in_proj(x, sc, sh, w_bf, cos_t, sin_t):
    b, length, d = x.shape
    n = w_bf.shape[1]
    tm = _pick(length, (512, 256, 128))
    row = lambda bi, i: (bi, i, 0)
    vec = lambda bi, i: (bi, 0, 0)
    return pl.pallas_call(
        _even_in_kernel,
        grid=(b, length // tm),
        in_specs=[pl.BlockSpec((1, tm, d), row),
                  pl.BlockSpec((1, 1, d), vec),
                  pl.BlockSpec((1, 1, d), vec),
                  pl.BlockSpec((d, n), lambda bi, i: (0, 0)),
                  pl.BlockSpec((tm, LANES), lambda bi, i: (i, 0)),
                  pl.BlockSpec((tm, LANES), lambda bi, i: (i, 0))],
        out_specs=[pl.BlockSpec((2, 1, tm, QK_W), lambda bi, i: (0, bi, i, 0)), pl.BlockSpec((1, tm, QK_W), row),
                   pl.BlockSpec((1, tm, V_W), row), pl.BlockSpec((1, tm, CONV_W), row)],
        out_shape=[jax.ShapeDtypeStruct((2, b, length, QK_W), BF16), jax.ShapeDtypeStruct((b, length, QK_W), BF16),
                   jax.ShapeDtypeStruct((b, length, V_W), BF16), jax.ShapeDtypeStruct((b, length, CONV_W), F32)],
        compiler_params=_params(2),
        name="even_in_proj",
    )(x, sc, sh, w_bf, cos_t, sin_t)


def _modmm_kernel(x_ref, sc_ref, sh_ref, w_ref, o_ref):
    u = (x_ref[0] * (1.0 + sc_ref[0]) + sh_ref[0]).astype(BF16)
    o_ref[0] = jnp.dot(u, w_ref[...], preferred_element_type=F32).astype(o_ref.dtype)


def _mod_matmul(x, sc, sh, w_bf, out_dtype):
    b, rows, d = x.shape
    n = w_bf.shape[1]
    tm = _pick(rows, (512, 256, 128))
    return pl.pallas_call(
        _modmm_kernel,
        grid=(b, rows // tm),
        in_specs=[pl.BlockSpec((1, tm, d), lambda bi, i: (bi, i, 0)),
                  pl.BlockSpec((1, 1, d), lambda bi, i: (bi, 0, 0)),
                  pl.BlockSpec((1, 1, d), lambda bi, i: (bi, 0, 0)),
                  pl.BlockSpec((d, n), lambda bi, i: (0, 0))],
        out_specs=pl.BlockSpec((1, tm, n), lambda bi, i: (bi, i, 0)),
        out_shape=jax.ShapeDtypeStruct((b, rows, n), out_dtype),
        compiler_params=_params(2),
        name="context_kv_proj",
    )(x, sc, sh, w_bf)


VT_ROWS = HEAD_V + 16


def _attn_kernel(lq1_ref, lk1_ref, lq2_ref, lk2_ref, g_ref, q_ref, k_ref, vt_ref, o_ref,
                 s0_sc, s1_sc, p0_sc, p1_sc, acc_sc, *, lam_init, kc):
    mq = q_ref.shape[2]
    n = k_ref.shape[1] // kc
    s_sc = (s0_sc, s1_sc)
    p_sc = (p0_sc, p1_sc)

    def scores(j, slot):
        kj = k_ref[0, pl.ds(pl.multiple_of(j * kc, kc), kc), :]
        for c in range(2):
            s_sc[slot][c] = lax.dot_general(kj, q_ref[c, 0], (((1,), (1,)), ((), ())),
                                            preferred_element_type=F32)

    def softmax(slot, m):
        m_out, alpha = [], []
        for c in range(2):
            s = s_sc[slot][c]
            m_new = jnp.maximum(m[c], jnp.max(s, axis=0, keepdims=True))
            p_sc[slot][c] = jnp.exp2((s - m_new).astype(BF16))
            m_out.append(m_new)
            alpha.append(jnp.exp2(m[c] - m_new))
        return tuple(m_out), tuple(alpha)

    def accumulate(j, slot, alpha):
        vt = vt_ref[0, 0, j]
        for c in range(2):
            acc_sc[c] = alpha[c] * acc_sc[c] + jnp.dot(vt, p_sc[slot][c], preferred_element_type=F32)

    def step(j, slot, m, alpha):
        scores(j + 2, slot)
        m, alpha_next = softmax(1 - slot, m)
        accumulate(j, slot, alpha)
        return m, alpha_next

    acc_sc[...] = jnp.zeros(acc_sc.shape, F32)
    m = (jnp.full((1, mq), NEG_BIG, F32),) * 2
    scores(0, 0)
    m, alpha = softmax(0, m)
    scores(1, 1)

    def pair(t, carry):
        m, alpha = step(2 * t, 0, *carry)
        return step(2 * t + 1, 1, m, alpha)

    n_steps = n - 2
    m, alpha = lax.fori_loop(0, n_steps // 2, pair, (m, alpha))
    if n_steps % 2:
        m, alpha = step(n_steps - 1, 0, m, alpha)
    m, alpha_last = softmax((n - 1) % 2, m)
    accumulate(n - 2, (n - 2) % 2, alpha)
    accumulate(n - 1, (n - 1) % 2, alpha_last)
    acc0 = acc_sc[0]
    acc1 = acc_sc[1]
    lam = (jnp.exp(jnp.sum(lq1_ref[...] * lk1_ref[...], keepdims=True))
           - jnp.exp(jnp.sum(lq2_ref[...] * lk2_ref[...], keepdims=True)) + lam_init)
    o = (acc0[:HEAD_V] / acc0[HEAD_V:HEAD_V + 1]
         - lam * (acc1[:HEAD_V] / acc1[HEAD_V:HEAD_V + 1]))
    ms = jnp.mean(o * o, axis=0, keepdims=True)
    o = o * lax.rsqrt(ms + LN_EPS) * g_ref[...] * (1.0 - lam_init)
    o_ref[0] = o.T.astype(o_ref.dtype)


def _diff_attention(q, k_all, v_all, lq1, lk1, lq2, lk2, norm_g, lam_init):
    _, b, length, _ = q.shape
    n_keys = k_all.shape[1]
    mq = _pick(length, (1024, 512, 256, 128))
    kc = next(c for c in (1280, 256, 128) if n_keys % c == 0 and n_keys // c >= 3)
    nc = n_keys // kc
    vt = v_all.reshape(b, nc, kc, N_HEADS, HEAD_V).transpose(0, 3, 1, 4, 2)
    extra = jnp.zeros((b, N_HEADS, nc, VT_ROWS - HEAD_V, kc), BF16).at[:, :, :, 0, :].set(1.0)
    vt = jnp.concatenate([vt, extra], 3)
    small = lambda bi, h, i: (0, 0)
    return pl.pallas_call(
        functools.partial(_attn_kernel, lam_init=lam_init, kc=kc),
        grid=(b, N_HEADS, length // mq),
        in_specs=[pl.BlockSpec((1, HEAD_DIM), small), pl.BlockSpec((1, HEAD_DIM), small),
                  pl.BlockSpec((1, HEAD_DIM), small), pl.BlockSpec((1, HEAD_DIM), small),
                  pl.BlockSpec((HEAD_V, 1), small),
                  pl.BlockSpec((2, 1, mq, HEAD_V), lambda bi, h, i: (0, bi, i, h)),
                  pl.BlockSpec((1, n_keys, HEAD_V), lambda bi, h, i: (bi, 0, h)),
                  pl.BlockSpec((1, 1, nc, VT_ROWS, kc), lambda bi, h, i: (bi, h, 0, 0, 0))],
        out_specs=pl.BlockSpec((1, mq, HEAD_V), lambda bi, h, i: (bi, i, h)),
        out_shape=jax.ShapeDtypeStruct((b, length, V_W), BF16),
        scratch_shapes=[pltpu.VMEM((2, kc, mq), F32), pltpu.VMEM((2, kc, mq), F32),
                        pltpu.VMEM((2, kc, mq), BF16), pltpu.VMEM((2, kc, mq), BF16),
                        pltpu.VMEM((2, VT_ROWS, mq), F32)],
        compiler_params=_params(3),
        name="diff_attention",
    )(lq1[None], lk1[None], lq2[None], lk2[None], norm_g[:, None], q, k_all, vt)


def _conv_kernel(prev_ref, cur_ref, next_ref, w_ref, b_ref, g_ref, beta_ref, o_ref, win, *, row_chunk):
    i = pl.program_id(1)
    tc = cur_ref.shape[1]
    span = tc + 2 * CONV_HALO - SUBLANES
    win[0, CONV_HALO:CONV_HALO + tc, :] = cur_ref[0]
    win[0, 0:CONV_HALO, :] = jnp.where(i > 0, prev_ref[0], 0.0)
    win[0, CONV_HALO + tc:, :] = jnp.where(i < pl.num_programs(1) - 1, next_ref[0], 0.0)
    for s in range(1, SUBLANES):
        win[s, 0:span, :] = win[0, s:s + span, :]
    first_tap = CONV_HALO - CONV_TAPS // 2
    for c0 in range(0, tc, row_chunk):
        acc = jnp.zeros((row_chunk, CONV_W), F32)
        for t in range(CONV_TAPS):
            s = (first_tap + t) % SUBLANES
            r0 = c0 + first_tap + t - s
            acc = acc + w_ref[t:t + 1, :] * win[s, r0:r0 + row_chunk, :]
        y = _layer_norm(acc + b_ref[...], g_ref[...], beta_ref[...])
        o_ref[0, c0:c0 + row_chunk, :] = (y * _sigmoid(y)).astype(o_ref.dtype)


def _conformer_conv(a, conv_w, conv_b, ln_g, ln_b):
    b, length, w = a.shape
    tc = _pick(length, (512, 256, 128))
    hb = tc // CONV_HALO
    n_halo = length // CONV_HALO
    w_pad = jnp.concatenate([conv_w, jnp.zeros((1, w), F32)], 0)
    vec = lambda bi, i: (0, 0)
    return pl.pallas_call(
        functools.partial(_conv_kernel, row_chunk=64),
        grid=(b, length // tc),
        in_specs=[pl.BlockSpec((1, CONV_HALO, w), lambda bi, i: (bi, jnp.maximum(i * hb - 1, 0), 0)),
                  pl.BlockSpec((1, tc, w), lambda bi, i: (bi, i, 0)),
                  pl.BlockSpec((1, CONV_HALO, w), lambda bi, i: (bi, jnp.minimum((i + 1) * hb, n_halo - 1), 0)),
                  pl.BlockSpec((CONV_TAPS + 1, w), vec),
                  pl.BlockSpec((1, w), vec), pl.BlockSpec((1, w), vec), pl.BlockSpec((1, w), vec)],
        out_specs=pl.BlockSpec((1, tc, w), lambda bi, i: (bi, i, 0)),
        out_shape=jax.ShapeDtypeStruct((b, length, w), BF16),
        scratch_shapes=[pltpu.VMEM((SUBLANES, tc + 2 * CONV_HALO, w), F32)],
        compiler_params=_params(2),
        name="conformer_conv",
    )(a, a, a, w_pad, conv_b[None], ln_g[None], ln_b[None])


def _post_mix_kernel(ya_ref, yb_ref, wa_ref, wb_ref, x_ref, g1_ref, lng_ref, lnb_ref, sc_ref, sh_ref,
                     rw_ref, rb_ref, h_ref, e_ref, gate_ref, pos_ref, cnt_ref, carry, *, alpha):
    first_step = jnp.logical_and(pl.program_id(0) == 0, pl.program_id(1) == 0)

    @pl.when(first_step)
    def _():
        carry[...] = jnp.zeros(carry.shape, F32)

    y = (jnp.dot(ya_ref[0].astype(BF16), wa_ref[...], preferred_element_type=F32)
         + jnp.dot(yb_ref[0].astype(BF16), wb_ref[...], preferred_element_type=F32))
    h = _layer_norm(alpha * x_ref[0] + g1_ref[0] * y, lng_ref[...], lnb_ref[...])
    h_ref[0] = h
    u2 = h * (1.0 + sc_ref[0]) + sh_ref[0]
    u_hi = u2.astype(BF16)
    u_lo = (u2 - u_hi.astype(F32)).astype(BF16)
    hh_hl = jnp.dot(u_hi, rw_ref[...], preferred_element_type=F32)
    lh = jnp.dot(u_lo, rw_ref[:, :LANES], preferred_element_type=F32)
    logits = hh_hl[:, :LANES] + hh_hl[:, LANES:] + lh + rb_ref[...]

    tm = logits.shape[0]
    lane = lax.broadcasted_iota(I32, logits.shape, 1).astype(F32)
    vals, idxs = [], []
    lg = logits
    for _ in range(TOP_K):
        mx = jnp.max(lg, axis=1, keepdims=True)
        idx = jnp.min(jnp.where(lg == mx, lane, float(LANES)), axis=1, keepdims=True)
        vals.append(mx)
        idxs.append(idx)
        lg = jnp.where(lane == idx, -3e38, lg)
    exps = [jnp.exp(v - vals[0]) for v in vals]
    den = exps[0] + exps[1] + exps[2] + exps[3]

    hot = jnp.zeros(logits.shape, F32)
    for idx in idxs:
        hot = hot + (lane == idx).astype(F32)
    r_i = lax.broadcasted_iota(I32, (tm, tm), 0)
    c_i = lax.broadcasted_iota(I32, (tm, tm), 1)
    tri = (c_i < r_i).astype(BF16)
    before = jnp.dot(tri, hot.astype(BF16), preferred_element_type=F32) + carry[...]
    e_out = jnp.zeros(logits.shape, F32)
    g_out = jnp.zeros(logits.shape, F32)
    p_out = jnp.zeros(logits.shape, F32)
    for j in range(TOP_K):
        pos_j = jnp.sum(jnp.where(lane == idxs[j], before, 0.0), axis=1, keepdims=True)
        sel = lane == float(j)
        e_out = jnp.where(sel, idxs[j], e_out)
        g_out = jnp.where(sel, exps[j] / den, g_out)
        p_out = jnp.where(sel, pos_j, p_out)
    e_ref[...] = e_out.T[:SUBLANES].astype(I32)
    gate_ref[...] = g_out
    pos_ref[...] = p_out.T[:SUBLANES].astype(I32)
    new_carry = carry[...] + jnp.sum(hot, axis=0, keepdims=True)
    carry[...] = new_carry
    cnt_ref[...] = jnp.broadcast_to(new_carry, cnt_ref.shape).astype(I32)


def _post_mix(ya, yb, w_out_bf, x, g1, ln_g, ln_b, sc2, sh2, router_w, router_b, alpha):
    b, length, d = x.shape
    wa = ya.shape[2]
    tm = _pick(length, (512, 256, 128))
    n_tok = b * length
    rw = jnp.concatenate([router_w, jnp.zeros((d, LANES - N_EXPERTS), F32)], 1)
    rw_hi = rw.astype(BF16)
    rw = jnp.concatenate([rw_hi, (rw - rw_hi.astype(F32)).astype(BF16)], 1)
    rb =jnp.concatenate([router_b, jnp.full((LANES - N_EXPERTS,), NEG_BIG, F32)])[None]
    row = lambda bi, i: (bi, i, 0)
    vec = lambda bi, i: (bi, 0, 0)
    const = lambda bi, i: (0, 0)
    nt = length // tm
    tok = lambda bi, i: (bi * nt + i, 0)
    return pl.pallas_call(
        functools.partial(_post_mix_kernel, alpha=alpha),
        grid=(b, nt),
        in_specs=[pl.BlockSpec((1, tm, wa), row), pl.BlockSpec((1, tm, d - wa), row),
                  pl.BlockSpec((wa, d), const), pl.BlockSpec((d - wa, d), const),
                  pl.BlockSpec((1, tm, d), row), pl.BlockSpec((1, 1, d), vec),
                  pl.BlockSpec((1, d), const), pl.BlockSpec((1, d), const),
                  pl.BlockSpec((1, 1, d), vec), pl.BlockSpec((1, 1, d), vec),
                  pl.BlockSpec((d, 2 * LANES), const), pl.BlockSpec((1, LANES), const)],
        out_specs=[pl.BlockSpec((1, tm, d), row), pl.BlockSpec((SUBLANES, tm), lambda bi, i: (0, bi * nt + i)),
                   pl.BlockSpec((tm, LANES), tok), pl.BlockSpec((SUBLANES, tm), lambda bi, i: (0, bi * nt + i)),
                   pl.BlockSpec((SUBLANES, LANES), const)],
        out_shape=[jax.ShapeDtypeStruct((b, length, d), F32), jax.ShapeDtypeStruct((SUBLANES, n_tok), I32),
                   jax.ShapeDtypeStruct((n_tok, LANES), F32), jax.ShapeDtypeStruct((SUBLANES, n_tok), I32),
                   jax.ShapeDtypeStruct((SUBLANES, LANES), I32)],
        scratch_shapes=[pltpu.VMEM((1, LANES), F32)],
        compiler_params=_params(2),
        name="out_proj_norm_router",
    )(ya, yb, w_out_bf[:wa], w_out_bf[wa:], x, g1, ln_g[None], ln_b[None], sc2, sh2, rw, rb)


def _tile_indices(dest, tm):
    n_tok = dest.shape[1]
    return dest.reshape(TOP_K, n_tok // tm, tm).transpose(1, 0, 2).reshape(n_tok // tm, 1, TOP_K * tm)


def _dispatch_kernel(zstart_ref, zvalid_ref, dest_ref, h_ref, sc_ref, sh_ref, xs_ref, u_sc, zero_sc, sem, zsem):
    tm = h_ref.shape[0]

    @pl.when(pl.program_id(0) == 0)
    def _():
        zero_sc[...] = jnp.zeros(zero_sc.shape, F32)

        def fill(e, c):
            @pl.when(zvalid_ref[e] > 0)
            def _():
                start = pl.multiple_of(zstart_ref[e], SUBLANES)
                pltpu.make_async_copy(zero_sc, xs_ref.at[pl.ds(start, EXPERT_ROWS)], zsem).start()
            return c

        lax.fori_loop(0, N_EXPERTS, fill, 0)

        def drain(e, c):
            @pl.when(zvalid_ref[e] > 0)
            def _():
                pltpu.make_async_copy(zero_sc, xs_ref.at[pl.ds(0, EXPERT_ROWS)], zsem).wait()
            return c

        lax.fori_loop(0, N_EXPERTS, drain, 0)

    u_sc[...] = h_ref[...] * (1.0 + sc_ref[0]) + sh_ref[0]

    def issue(g, c):
        for rr in range(ISSUE_UNROLL):
            r = g * ISSUE_UNROLL + rr
            for j in range(TOP_K):
                d = dest_ref[0, 0, j * tm + r]
                pltpu.make_async_copy(u_sc.at[pl.ds(r, 1)], xs_ref.at[pl.ds(d, 1)], sem).start()
        return c

    lax.fori_loop(0, tm // ISSUE_UNROLL, issue, 0)
    rows = xs_ref.at[pl.ds(0, tm * TOP_K)]
    pltpu.make_async_copy(rows, rows, sem).wait()


def _dispatch(h_flat, sc2, sh2, dest, zstart, zvalid, n_rows, tokens_per_batch):
    n_tok, d = h_flat.shape
    tm = _pick(tokens_per_batch, (512, 256, 128))
    per_b = tokens_per_batch // tm
    return pl.pallas_call(
        _dispatch_kernel,
        grid_spec=pltpu.PrefetchScalarGridSpec(
            num_scalar_prefetch=2,
            grid=(n_tok // tm,),
            in_specs=[pl.BlockSpec((1, 1, tm * TOP_K), lambda i, zs, zv: (i, 0, 0), memory_space=pltpu.SMEM),
                      pl.BlockSpec((tm, d), lambda i, zs, zv: (i, 0)),
                      pl.BlockSpec((1, 1, d), lambda i, zs, zv: (i // per_b, 0, 0)),
                      pl.BlockSpec((1, 1, d), lambda i, zs, zv: (i // per_b, 0, 0))],
            out_specs=pl.BlockSpec(memory_space=pl.ANY),
            scratch_shapes=[pltpu.VMEM((tm, d), F32), pltpu.VMEM((EXPERT_ROWS, d), F32),
                            pltpu.SemaphoreType.DMA(()), pltpu.SemaphoreType.DMA(())]),
        out_shape=jax.ShapeDtypeStruct((n_rows, d), F32),
        compiler_params=_params(1),
        name="moe_dispatch",
    )(zstart, zvalid, _tile_indices(dest, tm), h_flat, sc2, sh2)


def _expert_kernel(blk_e_ref, n_used_ref, x_ref, wgu_ref, bgu_ref, wd_ref, bd_ref, o_ref, wgu_bf, wd_bf):
    i = pl.program_id(0)
    new_expert = jnp.logical_or(i == 0, blk_e_ref[i] != blk_e_ref[jnp.maximum(i - 1, 0)])

    @pl.when(new_expert)
    def _():
        wgu_bf[...] = wgu_ref[0].astype(BF16)
        wd_bf[...] = wd_ref[0].astype(BF16)

    @pl.when(i < n_used_ref[0])
    def _():
        d_e = wd_ref.shape[1]
        gu = jnp.dot(x_ref[...].astype(BF16), wgu_bf[...], preferred_element_type=F32) + bgu_ref[0]
        x_glu = jnp.minimum(gu[:, :d_e], SWIGLU_LIMIT)
        x_lin = jnp.clip(gu[:, d_e:], -SWIGLU_LIMIT, SWIGLU_LIMIT)
        act = x_glu * _sigmoid(SWIGLU_ALPHA * x_glu) * (x_lin + 1.0)
        o_ref[...] = jnp.dot(act.astype(BF16), wd_bf[...], preferred_element_type=F32) + bd_ref[0]


def _experts(xs, blk_e, n_used, wgu, bgu, wd, bd, layer):
    n_rows, d = xs.shape
    depth, n_e, _, n_gu = wgu.shape
    n_blk = n_rows // EXPERT_ROWS
    rows = lambda i, be, nu: (jnp.minimum(i, nu[0] - 1), 0)
    first = layer * n_e
    by_e = lambda i, be, nu: (first + be[i], 0, 0)
    n_e = depth * n_e
    wgu = wgu.reshape(n_e, d, n_gu)
    wd = wd.reshape(n_e, n_gu // 2, d)
    return pl.pallas_call(
        _expert_kernel,
        grid_spec=pltpu.PrefetchScalarGridSpec(
            num_scalar_prefetch=2,
            grid=(n_blk,),
            in_specs=[pl.BlockSpec((EXPERT_ROWS, d), rows),
                      pl.BlockSpec((1, d, n_gu), by_e), pl.BlockSpec((1, 1, n_gu), by_e),
                      pl.BlockSpec((1, n_gu // 2, d), by_e), pl.BlockSpec((1, 1, d), by_e)],
            out_specs=pl.BlockSpec((EXPERT_ROWS, d), rows),
            scratch_shapes=[pltpu.VMEM((d, n_gu), BF16), pltpu.VMEM((n_gu // 2, d), BF16)]),
        out_shape=jax.ShapeDtypeStruct((n_rows, d), F32),
        compiler_params=_params(1),
        name="moe_experts",
    )(blk_e, n_used, xs, wgu, bgu.reshape(n_e, 1, n_gu), wd, bd.reshape(n_e, 1, d))


def _combine_kernel(dest_ref, dest_next_ref, y_ref, h_ref, gate_ref, g2_ref, lng_ref, lnb_ref, o_ref,
                    buf, sem, *, alpha):
    i = pl.program_id(0)
    tm = h_ref.shape[0]

    def fetch(idx_ref, slot):
        def issue(g, c):
            for rr in range(ISSUE_UNROLL):
                r = g * ISSUE_UNROLL + rr
                for j in range(TOP_K):
                    d = idx_ref[0, 0, j * tm + r]
                    pltpu.make_async_copy(y_ref.at[pl.ds(d, 1)], buf.at[slot, j, pl.ds(r, 1)], sem.at[slot]).start()
            return c

        lax.fori_loop(0, tm // ISSUE_UNROLL, issue, 0)

    slot = i % 2

    @pl.when(i == 0)
    def _():
        fetch(dest_ref, 0)

    @pl.when(i + 1 < pl.num_programs(0))
    def _():
        fetch(dest_next_ref, 1 - slot)

    pltpu.make_async_copy(buf.at[slot], buf.at[slot], sem.at[slot]).wait()
    gate = gate_ref[...]
    y2 = gate[:, 0:1] * buf[slot, 0]
    for j in range(1, TOP_K):
        y2 = y2 + gate[:, j:j + 1] * buf[slot, j]
    o_ref[...] = _layer_norm(alpha * h_ref[...] + g2_ref[0] * y2, lng_ref[...], lnb_ref[...])


def _combine(y_rows, dest, h_flat, gate, g2, ln_g, ln_b, alpha, tokens_per_batch):
    n_tok, d = h_flat.shape
    tm = _pick(tokens_per_batch, (256, 128))
    per_b = tokens_per_batch // tm
    n_tiles = n_tok // tm
    dest3 = _tile_indices(dest, tm)
    return pl.pallas_call(
        functools.partial(_combine_kernel, alpha=alpha),
        grid=(n_tiles,),
        in_specs=[pl.BlockSpec((1, 1, tm * TOP_K), lambda i: (i, 0, 0), memory_space=pltpu.SMEM),
                  pl.BlockSpec((1, 1, tm * TOP_K), lambda i: (jnp.minimum(i + 1, n_tiles - 1), 0, 0),
                               memory_space=pltpu.SMEM),
                  pl.BlockSpec(memory_space=pl.ANY),
                  pl.BlockSpec((tm, d), lambda i: (i, 0)),
                  pl.BlockSpec((tm, LANES), lambda i: (i, 0)),
                  pl.BlockSpec((1, 1, d), lambda i: (i // per_b, 0, 0)),
                  pl.BlockSpec((1, d), lambda i: (0, 0)), pl.BlockSpec((1, d), lambda i: (0, 0))],
        out_specs=pl.BlockSpec((tm, d), lambda i: (i, 0)),
        out_shape=jax.ShapeDtypeStruct((n_tok, d), F32),
        scratch_shapes=[pltpu.VMEM((2, TOP_K, tm, d), F32), pltpu.SemaphoreType.DMA((2,))],
        compiler_params=_params(1),
        name="moe_combine_norm",
    )(dest3, dest3, y_rows, h_flat, gate, g2, ln_g[None], ln_b[None])


def _moe_and_norm(h1, top_e, gate, pos, counts, sc2, sh2, g2, wgu, bgu, wd, bd, layer, ln_g, ln_b, alpha):
    b, length, d = h1.shape
    n_tok = b * length
    n_blk = n_tok * TOP_K // EXPERT_ROWS + N_EXPERTS
    padded = (counts + EXPERT_ROWS - 1) // EXPERT_ROWS * EXPERT_ROWS
    pad_end = jnp.cumsum(padded)
    offset = pad_end - padded
    n_used = (pad_end[-1:] // EXPERT_ROWS).astype(I32)
    blk_start = jnp.arange(n_blk, dtype=I32) * EXPERT_ROWS
    blk_e = jnp.minimum(jnp.sum(pad_end[None, :] <= blk_start[:, None], axis=1), N_EXPERTS - 1).astype(I32)
    dest = pos
    for e in range(N_EXPERTS):
        dest = dest + jnp.where(top_e == e, offset[e], 0)
    dest = dest.astype(I32)
    zstart = jnp.maximum(pad_end - EXPERT_ROWS, 0).astype(I32)
    zvalid = (counts > 0).astype(I32)

    h_flat = h1.reshape(n_tok, d)
    xs = _dispatch(h_flat, sc2, sh2, dest, zstart, zvalid, n_blk * EXPERT_ROWS, length)
    y_rows = _experts(xs, blk_e, n_used, wgu, bgu, wd, bd, layer)
    out = _combine(y_rows, dest, h_flat, gate, g2, ln_g, ln_b, alpha, length)
    return out.reshape(b, length, d)


def _odd_in_kernel(x_ref, sc_ref, sh_ref, w_ref, glg_ref, glb_ref, ws_ref, bs_ref, flg_ref, flb_ref, dft_ref,
                   sp_ref, zr_ref, zi_ref):
    u = (x_ref[0] * (1.0 + sc_ref[0]) + sh_ref[0]).astype(BF16)
    z = jnp.dot(u, w_ref[...], preferred_element_type=F32)
    tm = z.shape[0]
    gw = N_GROUPS * GROUP_W
    ug = jax.nn.gelu(z[:, :gw])
    vn = _layer_norm(jax.nn.gelu(z[:, gw:2 * gw]), glg_ref[...], glb_ref[...]).astype(BF16)
    f = z[:, 2 * gw:]
    for g in range(N_GROUPS):
        cols = slice(g * GROUP_W, (g + 1) * GROUP_W)
        for c0 in range(0, tm, CHUNK):
            rows = slice(c0, c0 + CHUNK)
            sv = jnp.dot(ws_ref[g], vn[rows, cols], preferred_element_type=F32) + bs_ref[:, cols]
            sp_ref[0, rows, cols] = (ug[rows, cols] * sv).astype(sp_ref.dtype)
        fn = _layer_norm(f[:, cols], flg_ref[:, cols], flb_ref[:, cols]).astype(BF16)
        zz = jnp.dot(fn, dft_ref[...], preferred_element_type=F32)
        zr_ref[0, :, cols] = zz[:, :GROUP_W]
        zi_ref[0, :, cols] = zz[:, GROUP_W:]


def _odd_in_proj(x, sc, sh, w_bf, gln_g, gln_b, ws, bs, fln_g, fln_b):
    b, length, d = x.shape
    n = w_bf.shape[1]
    gw = N_GROUPS * GROUP_W
    tm = _pick(length, (512, 256, 128))
    kk = jnp.arange(GROUP_W, dtype=I32)
    ang = (2.0 * math.pi / GROUP_W) * ((kk[:, None] * kk[None, :]) % GROUP_W).astype(F32)
    dft = jnp.concatenate([jnp.cos(ang), -jnp.sin(ang)], 1).astype(BF16)
    bs_exp = jnp.repeat(bs.T, GROUP_W, axis=1)
    row = lambda bi, i: (bi, i, 0)
    vec = lambda bi, i: (bi, 0, 0)
    const2 = lambda bi, i: (0, 0)
    return pl.pallas_call(
        _odd_in_kernel,
        grid=(b, length // tm),
        in_specs=[pl.BlockSpec((1, tm, d), row), pl.BlockSpec((1, 1, d), vec), pl.BlockSpec((1, 1, d), vec),
                  pl.BlockSpec((d, n), const2),
                  pl.BlockSpec((1, gw), const2), pl.BlockSpec((1, gw), const2),
                  pl.BlockSpec((N_GROUPS, CHUNK, CHUNK), lambda bi, i: (0, 0, 0)),
                  pl.BlockSpec((CHUNK, gw), const2),
                  pl.BlockSpec((1, gw), const2), pl.BlockSpec((1, gw), const2),
                  pl.BlockSpec((GROUP_W, 2 * GROUP_W), const2)],
        out_specs=[pl.BlockSpec((1, tm, gw), row), pl.BlockSpec((1, tm, gw), row), pl.BlockSpec((1, tm, gw), row)],
        out_shape=[jax.ShapeDtypeStruct((b, length, gw), BF16), jax.ShapeDtypeStruct((b, length, gw), F32),
                   jax.ShapeDtypeStruct((b, length, gw), F32)],
        compiler_params=_params(2),
        name="odd_in_proj",
    )(x, sc, sh, w_bf, gln_g[None], gln_b[None], ws.astype(BF16), bs_exp, fln_g[None], fln_b[None], dft)


def _fft_a_kernel(zr_ref, zi_ref, c_ref, s_ref, ar_ref, ai_ref):
    l1, nb, w = zr_ref.shape[1:]
    zr = zr_ref[0].reshape(l1 * nb, w).astype(BF16)
    zi = zi_ref[0].reshape(l1 * nb, w).astype(BF16)
    c = c_ref[...]
    s = s_ref[...]
    ar = jnp.dot(c, zr, preferred_element_type=F32) + jnp.dot(s, zi, preferred_element_type=F32)
    ai = jnp.dot(c, zi, preferred_element_type=F32) - jnp.dot(s, zr, preferred_element_type=F32)
    ar_ref[0] = ar.reshape(l1, nb, w)
    ai_ref[0] = ai.reshape(l1, nb, w)


def _fft_b_kernel(ar_ref, ai_ref, c_ref, s_ref, o_ref, *, norm):
    nb, l2, w = ar_ref.shape[1:]
    ar = ar_ref[0].reshape(nb * l2, w).astype(BF16)
    ai = ai_ref[0].reshape(nb * l2, w).astype(BF16)
    y = jnp.dot(c_ref[0], ar, preferred_element_type=F32) + jnp.dot(s_ref[0], ai, preferred_element_type=F32)
    o_ref[0] = (y * norm).reshape(l2, nb, w)


def _length_dft_real(zr, zi):
    b, length, w = zr.shape
    l2 = CHUNK
    l1 = length // l2
    nb = SUBLANES
    k1 = jnp.arange(l1, dtype=I32)
    ang_a = (2.0 * math.pi / l1) * ((k1[:, None] * k1[None, :]) % l1).astype(F32)
    r = lax.broadcasted_iota(I32, (l1 * nb, l1 * nb), 0)
    c = lax.broadcasted_iota(I32, (l1 * nb, l1 * nb), 1)
    same = r % nb == c % nb
    rep = (lax.broadcasted_iota(I32, (l1 * nb, l1), 0) // nb == lax.broadcasted_iota(I32, (l1 * nb, l1), 1))
    rep = rep.astype(BF16)

    def expand_a(t):
        t = jnp.dot(jnp.dot(rep, t.astype(BF16), preferred_element_type=F32).astype(BF16), rep.T,
                    preferred_element_type=F32)
        return jnp.where(same, t, 0.0).astype(BF16)

    cos_a = expand_a(jnp.cos(ang_a))
    sin_a = expand_a(jnp.sin(ang_a))
    shape_t = (l1 // nb, l2 * nb, l2)
    kb = lax.broadcasted_iota(I32, shape_t, 0)
    r = lax.broadcasted_iota(I32, shape_t, 1)
    n2 = lax.broadcasted_iota(I32, shape_t, 2)
    ang_b = (2.0 * math.pi / length) * ((n2 * (kb * nb + r % nb + l1 * (r // nb))) % length).astype(F32)
    shape_b = (l1 // nb, l2 * nb, nb * l2)
    same = lax.broadcasted_iota(I32, shape_b, 1) % nb == lax.broadcasted_iota(I32, shape_b, 2) // l2

    def expand_b(t):
        return jnp.where(same, jnp.concatenate([t] * nb, axis=2), 0.0).astype(BF16)

    cos_t, sin_t = lax.optimization_barrier((jnp.cos(ang_b), jnp.sin(ang_b)))
    cos_b = expand_b(cos_t)
    sin_b = expand_b(sin_t)

    zr4 = zr.reshape(b, l1, l2, w)
    zi4 = zi.reshape(b, l1, l2, w)
    blk_a = pl.BlockSpec((1, l1, nb, w), lambda bi, i: (bi, 0, i, 0))
    ar, ai = pl.pallas_call(
        _fft_a_kernel,
        grid=(b, l2 // nb),
        in_specs=[blk_a, blk_a, pl.BlockSpec((l1 * nb, l1 * nb), lambda bi, i: (0, 0)),
                  pl.BlockSpec((l1 * nb, l1 * nb), lambda bi, i: (0, 0))],
        out_specs=[blk_a, blk_a],
        out_shape=[jax.ShapeDtypeStruct((b, l1, l2, w), F32)] * 2,
        compiler_params=_params(2),
        name="fourier_stage_a",
    )(zr4, zi4, cos_a, sin_a)
    blk_b = pl.BlockSpec((1, nb, l2, w), lambda bi, i: (bi, i, 0, 0))
    out = pl.pallas_call(
        functools.partial(_fft_b_kernel, norm=float((length * GROUP_W) ** -0.5)),
        grid=(b, l1 // nb),
        in_specs=[blk_b, blk_b, pl.BlockSpec((1, l2 * nb, nb * l2), lambda bi, i: (i, 0, 0)),
                  pl.BlockSpec((1, l2 * nb, nb * l2), lambda bi, i: (i, 0, 0))],
        out_specs=pl.BlockSpec((1, l2, nb, w), lambda bi, i: (bi, 0, i, 0)),
        out_shape=jax.ShapeDtypeStruct((b, l2, l1, w), F32),
        compiler_params=_params(2),
        name="fourier_stage_b",
    )(ar, ai, cos_b, sin_b)
    return out.reshape(b, length, w)


def kernel(x, c, ctx, c_ctx, w_mod, b_mod, ln1_g, ln1_b, ln2_g, ln2_b, ev_w_in, ev_w_out, conv_w, conv_b, conv_ln_g, conv_ln_b, lam_q1, lam_k1, lam_q2, lam_k2, diff_norm_g, od_w_in, od_w_out, gmlp_ln_g, gmlp_ln_b, gmlp_ws, gmlp_bs, four_ln_g, four_ln_b, router_w, router_b, w_gate_up, b_gate_up, w_down, b_down):
    b, length, d = x.shape
    depth = w_mod.shape[0]
    alpha = float((2 * depth) ** 0.25)
    assert b + 1 <= SUBLANES and length % (CHUNK * SUBLANES) == 0 and length % GRID_W == 0

    mod = _modulation(c, c_ctx, w_mod, b_mod)
    cos_t, sin_t = _rope_tables(length)
    h = x
    for layer in range(depth):
        j = layer // 2
        m = mod[layer]
        sh1, sc1, g1, sh2, sc2, g2 = [m[:b, i * d:(i + 1) * d][:, None, :] for i in range(6)]
        if layer % 2 == 0:
            lam_init = 0.8 - 0.6 * math.exp(-0.3 * layer)
            w_in = ev_w_in[j].astype(BF16)
            q, k, v, a = _even_in_proj(h, sc1, sh1, w_in, cos_t, sin_t)
            csh1 = jnp.broadcast_to(m[b:b + 1, 0:d][:, None, :], (b, 1, d))
            csc1 = jnp.broadcast_to(m[b:b + 1, d:2 * d][:, None, :], (b, 1, d))
            kv_c = _mod_matmul(ctx, csc1, csh1, w_in[:, QK_W:2 * QK_W + V_W], BF16)
            k_all = jnp.concatenate([k, kv_c[..., :QK_W]], 1)
            v_all = jnp.concatenate([v, kv_c[..., QK_W:]], 1)
            att = _diff_attention(q, k_all, v_all, lam_q1[j], lam_k1[j], lam_q2[j], lam_k2[j],
                                  diff_norm_g[j], lam_init)
            conv = _conformer_conv(a, conv_w[j], conv_b[j], conv_ln_g[j], conv_ln_b[j])
            ya, yb, w_out = conv, att, ev_w_out[j]
        else:
            spatial, zr, zi = _odd_in_proj(h, sc1, sh1, od_w_in[j].astype(BF16), gmlp_ln_g[j], gmlp_ln_b[j],
                                           gmlp_ws[j], gmlp_bs[j], four_ln_g[j], four_ln_b[j])
            ya, yb, w_out = spatial, _length_dft_real(zr, zi), od_w_out[j]
        h1, top_e, gate, pos, counts = _post_mix(ya, yb, w_out.astype(BF16), h, g1, ln1_g[layer], ln1_b[layer],
                                                 sc2, sh2, router_w[layer], router_b[layer], alpha)
        h = _moe_and_norm(h1, top_e[:TOP_K], gate, pos[:TOP_K], counts[0, :N_EXPERTS], sc2, sh2, g2,
                          w_gate_up, b_gate_up, w_down, b_down, layer, ln2_g[layer], ln2_b[layer], alpha)
    return h
```

```python
import functools
import math

import jax
import jax.numpy as jnp
from jax import lax
from jax.experimental import pallas as pl
from jax.experimental.pallas import tpu as pltpu

F32 = jnp.float32
BF16 = jnp.bfloat16
I32 = jnp.int32
HIGHEST = lax.Precision.HIGHEST

LN_EPS = 1e-5
GRID_W = 64
HEAD_DIM = 64
HEAD_V = 128
N_HEADS = 4
QK_W = N_HEADS * 2 * HEAD_DIM
V_W = N_HEADS * HEAD_V
CONV_W = 512
CONV_TAPS = 31
CONV_HALO = 16
ROPE_BASE = 10000.0
CHUNK = 128
N_GROUPS = 4
GROUP_W = 128
N_EXPERTS = 32
TOP_K = 4
SWIGLU_LIMIT = 7.0
SWIGLU_ALPHA = 1.702
LANES = 128
SUBLANES = 8
EXPERT_ROWS = 512
ISSUE_UNROLL = 8
NEG_BIG = -1e30
VMEM_LIMIT = 56 * 1024 * 1024


def _params(n_axes):
    return pltpu.CompilerParams(dimension_semantics=("arbitrary",) * n_axes,
                                vmem_limit_bytes=VMEM_LIMIT)


def _pick(n, candidates):
    for c in candidates:
        if n % c == 0:
            return c
    return n


def _layer_norm(r, g, b):
    mu = jnp.mean(r, axis=-1, keepdims=True)
    d = r - mu
    var = jnp.mean(d * d, axis=-1, keepdims=True)
    return d * lax.rsqrt(var + LN_EPS) * g + b


def _sigmoid(x):
    return 1.0 / (1.0 + jnp.exp(-x))


def _mod_kernel(cs_ref, w_ref, b_ref, o_ref):
    cs = cs_ref[...]
    a = cs * _sigmoid(cs)
    o_ref[0] = jnp.dot(a, w_ref[0], precision=HIGHEST, preferred_element_type=F32) + b_ref[0]


def _modulation(c, c_ctx, w_mod, b_mod):
    depth, d, n = w_mod.shape
    b = c.shape[0]
    cs = jnp.concatenate([c, c_ctx[None], jnp.zeros((SUBLANES - b - 1, d), F32)], 0)
    tn = _pick(n, (1536, 1024, 512))
    return pl.pallas_call(
        _mod_kernel,
        grid=(depth, n // tn),
        in_specs=[pl.BlockSpec((SUBLANES, d), lambda l, j: (0, 0)),
                  pl.BlockSpec((1, d, tn), lambda l, j: (l, 0, j)),
                  pl.BlockSpec((1, 1, tn), lambda l, j: (l, 0, j))],
        out_specs=pl.BlockSpec((1, SUBLANES, tn), lambda l, j: (l, 0, j)),
        out_shape=jax.ShapeDtypeStruct((depth, SUBLANES, n), F32),
        compiler_params=_params(2),
        name="modulation",
    )(cs, w_mod, b_mod.reshape(depth, 1, n))


KV_TILE = 256
VT_ROWS = HEAD_V + 16


def _store_values_transposed(v, vt_ref):
    extra = (lax.broadcasted_iota(I32, (VT_ROWS - HEAD_V, v.shape[0]), 0) == 0).astype(BF16)
    for h in range(N_HEADS):
        vt_ref[0, h, 0, :HEAD_V, :] = v[:, h * HEAD_V:(h + 1) * HEAD_V].T.astype(BF16)
        vt_ref[0, h, 0, HEAD_V:, :] = extra


def _even_in_kernel(x_ref, sc_ref, sh_ref, w_ref, cos_ref, sin_ref, q_ref, k_ref, vt_ref, a_ref):
    u = (x_ref[0] * (1.0 + sc_ref[0]) + sh_ref[0]).astype(BF16)
    z = jnp.dot(u, w_ref[...], preferred_element_type=F32)
    cosv = cos_ref[...]
    sinv = sin_ref[...]
    lane = lax.broadcasted_iota(I32, cosv.shape, 1)
    first = (lane % 32) < 16

    def rope(t):
        partner = jnp.where(first, pltpu.roll(t, LANES - 16, 1), pltpu.roll(t, 16, 1))
        return t * cosv + partner * sinv

    for j in range(QK_W // LANES):
        sl = slice(j * LANES, (j + 1) * LANES)
        rq = rope(z[:, sl]) * (HEAD_DIM ** -0.5 * math.log2(math.e))
        for c in range(2):
            q_ref[c, 0, :, sl] = jnp.where(lane // HEAD_DIM == c, rq, 0.0).astype(BF16)
        k_ref[0, :, sl] = rope(z[:, QK_W + j * LANES:QK_W + (j + 1) * LANES]).astype(BF16)
    _store_values_transposed(z[:, 2 * QK_W:2 * QK_W + V_W], vt_ref)
    a0 = 2 * QK_W + V_W
    a_ref[0] = z[:, a0:a0 + CONV_W] * _sigmoid(z[:, a0 + CONV_W:a0 + 2 * CONV_W])


def _rope_tables(length):
    rows = length // GRID_W
    row = jnp.repeat(jnp.arange(rows, dtype=F32), GRID_W)
    col = jnp.tile(jnp.arange(GRID_W, dtype=F32), rows)
    n_freq = HEAD_DIM // 4
    inv_freq = ROPE_BASE ** (-jnp.arange(n_freq, dtype=F32) / n_freq)
    ar = row[:, None] * inv_freq
    ac = col[:, None] * inv_freq
    cos64 = jnp.concatenate([jnp.cos(ar), jnp.cos(ar), jnp.cos(ac), jnp.cos(ac)], 1)
    sin64 = jnp.concatenate([-jnp.sin(ar), jnp.sin(ar), -jnp.sin(ac), jnp.sin(ac)], 1)
    return jnp.tile(cos64, (1, LANES // HEAD_DIM)), jnp.tile(sin64, (1, LANES // HEAD_DIM))


def _key_chunk(n_keys):
    return next(c for c in (1280, 256) if n_keys % c == 0 and n_keys // c >= 3)


def _kv_specs(b, n_keys, kc, first_tile):
    per_chunk = kc // KV_TILE
    k_spec = pl.BlockSpec((1, KV_TILE, QK_W), lambda bi, i: (bi, first_tile + i, 0))
    vt_spec = pl.BlockSpec((1, N_HEADS, 1, VT_ROWS, KV_TILE),
                           lambda bi, i: (bi, 0, (first_tile + i) // per_chunk, 0, (first_tile + i) % per_chunk))
    shapes = [jax.ShapeDtypeStruct((b, n_keys, QK_W), BF16),
              jax.ShapeDtypeStruct((b, N_HEADS, n_keys // kc, VT_ROWS, kc), BF16)]
    return k_spec, vt_spec, shapes


def _even_in_proj(x, sc, sh, w_bf, cos_t, sin_t, n_keys):
    b, length, d = x.shape
    n = w_bf.shape[1]
    tm = KV_TILE
    row = lambda bi, i: (bi, i, 0)
    vec = lambda bi, i: (bi, 0, 0)
    k_spec, vt_spec, kv_shapes = _kv_specs(b, n_keys, _key_chunk(n_keys), 0)
    return pl.pallas_call(
        _even_in_kernel,
        grid=(b, length // tm),
        in_specs=[pl.BlockSpec((1, tm, d), row),
                  pl.BlockSpec((1, 1, d), vec),
                  pl.BlockSpec((1, 1, d), vec),
                  pl.BlockSpec((d, n), lambda bi, i: (0, 0)),
                  pl.BlockSpec((tm, LANES), lambda bi, i: (i, 0)),
                  pl.BlockSpec((tm, LANES), lambda bi, i: (i, 0))],
        out_specs=[pl.BlockSpec((2, 1, tm, QK_W), lambda bi, i: (0, bi, i, 0)), k_spec, vt_spec,
                   pl.BlockSpec((1, tm, CONV_W), row)],
        out_shape=[jax.ShapeDtypeStruct((2, b, length, QK_W), BF16)] + kv_shapes
                  + [jax.ShapeDtypeStruct((b, length, CONV_W), F32)],
        compiler_params=_params(2),
        name="even_in_proj",
    )(x, sc, sh, w_bf, cos_t, sin_t)


def _ctx_kv_kernel(x_ref, sc_ref, sh_ref, w_ref, k_in, vt_in, k_ref, vt_ref):
    del k_in, vt_in
    u = (x_ref[0] * (1.0 + sc_ref[0]) + sh_ref[0]).astype(BF16)
    z = jnp.dot(u, w_ref[...], preferred_element_type=F32)
    k_ref[0] = z[:, :QK_W].astype(BF16)
    _store_values_transposed(z[:, QK_W:], vt_ref)


def _context_kv(ctx, sc, sh, w_bf, k_all, vt, first_key):
    b, rows, d = ctx.shape
    n = w_bf.shape[1]
    n_keys = k_all.shape[1]
    k_spec, vt_spec, kv_shapes = _kv_specs(b, n_keys, vt.shape[4], first_key // KV_TILE)
    return pl.pallas_call(
        _ctx_kv_kernel,
        grid=(b, rows // KV_TILE),
        in_specs=[pl.BlockSpec((1, KV_TILE, d), lambda bi, i: (bi, i, 0)),
                  pl.BlockSpec((1, 1, d), lambda bi, i: (bi, 0, 0)),
                  pl.BlockSpec((1, 1, d), lambda bi, i: (bi, 0, 0)),
                  pl.BlockSpec((d, n), lambda bi, i: (0, 0)),
                  pl.BlockSpec(memory_space=pl.ANY), pl.BlockSpec(memory_space=pl.ANY)],
        out_specs=[k_spec, vt_spec],
        out_shape=kv_shapes,
        input_output_aliases={4: 0, 5: 1},
        compiler_params=_params(2),
        name="context_kv_proj",
    )(ctx, sc, sh, w_bf, k_all, vt)


def _attn_kernel(lq1_ref, lk1_ref, lq2_ref, lk2_ref, g_ref, q_ref, k_ref, vt_ref, o_ref,
                 s0_sc, s1_sc, p0_sc, p1_sc, acc_sc, *, lam_init, kc):
    mq = q_ref.shape[2]
    n = k_ref.shape[1] // kc
    s_sc = (s0_sc, s1_sc)
    p_sc = (p0_sc, p1_sc)

    def scores(j, slot):
        kj = k_ref[0, pl.ds(pl.multiple_of(j * kc, kc), kc), :]
        for c in range(2):
            s_sc[slot][c] = lax.dot_general(kj, q_ref[c, 0], (((1,), (1,)), ((), ())),
                                            preferred_element_type=F32)

    def softmax(slot, m):
        m_out, alpha = [], []
        for c in range(2):
            s = s_sc[slot][c]
            m_new = jnp.maximum(m[c], jnp.max(s, axis=0, keepdims=True))
            p_sc[slot][c] = jnp.exp2((s - m_new).astype(BF16))
            m_out.append(m_new)
            alpha.append(jnp.exp2(m[c] - m_new))
        return tuple(m_out), tuple(alpha)

    def accumulate(j, slot, alpha):
        vt = vt_ref[0, 0, j]
        for c in range(2):
            acc_sc[c] = alpha[c] * acc_sc[c] + jnp.dot(vt, p_sc[slot][c], preferred_element_type=F32)

    def step(j, slot, m, alpha):
        scores(j + 2, slot)
        m, alpha_next = softmax(1 - slot, m)
        accumulate(j, slot, alpha)
        return m, alpha_next

    acc_sc[...] = jnp.zeros(acc_sc.shape, F32)
    m = (jnp.full((1, mq), NEG_BIG, F32),) * 2
    scores(0, 0)
    m, alpha = softmax(0, m)
    scores(1, 1)

    def pair(t, carry):
        m, alpha = step(2 * t, 0, *carry)
        return step(2 * t + 1, 1, m, alpha)

    n_steps = n - 2
    m, alpha = lax.fori_loop(0, n_steps // 2, pair, (m, alpha))
    if n_steps % 2:
        m, alpha = step(n_steps - 1, 0, m, alpha)
    m, alpha_last = softmax((n - 1) % 2, m)
    accumulate(n - 2, (n - 2) % 2, alpha)
    accumulate(n - 1, (n - 1) % 2, alpha_last)
    acc0 = acc_sc[0]
    acc1 = acc_sc[1]
    lam = (jnp.exp(jnp.sum(lq1_ref[...] * lk1_ref[...], keepdims=True))
           - jnp.exp(jnp.sum(lq2_ref[...] * lk2_ref[...], keepdims=True)) + lam_init)
    o = (acc0[:HEAD_V] / acc0[HEAD_V:HEAD_V + 1]
         - lam * (acc1[:HEAD_V] / acc1[HEAD_V:HEAD_V + 1]))
    ms = jnp.mean(o * o, axis=0, keepdims=True)
    o = o * lax.rsqrt(ms + LN_EPS) * g_ref[...] * (1.0 - lam_init)
    o_ref[0] = o.T.astype(o_ref.dtype)


def _diff_attention(q, k_all, vt, lq1, lk1, lq2, lk2, norm_g, lam_init):
    _, b, length, _ = q.shape
    n_keys = k_all.shape[1]
    mq = _pick(length, (1024, 512, 256, 128))
    nc, kc = vt.shape[2], vt.shape[4]
    small = lambda bi, h, i: (0, 0)
    return pl.pallas_call(
        functools.partial(_attn_kernel, lam_init=lam_init, kc=kc),
        grid=(b, N_HEADS, length // mq),
        in_specs=[pl.BlockSpec((1, HEAD_DIM), small), pl.BlockSpec((1, HEAD_DIM), small),
                  pl.BlockSpec((1, HEAD_DIM), small), pl.BlockSpec((1, HEAD_DIM), small),
                  pl.BlockSpec((HEAD_V, 1), small),
                  pl.BlockSpec((2, 1, mq, HEAD_V), lambda bi, h, i: (0, bi, i, h)),
                  pl.BlockSpec((1, n_keys, HEAD_V), lambda bi, h, i: (bi, 0, h)),
                  pl.BlockSpec((1, 1, nc, VT_ROWS, kc), lambda bi, h, i: (bi, h, 0, 0, 0))],
        out_specs=pl.BlockSpec((1, mq, HEAD_V), lambda bi, h, i: (bi, i, h)),
        out_shape=jax.ShapeDtypeStruct((b, length, V_W), BF16),
        scratch_shapes=[pltpu.VMEM((2, kc, mq), F32), pltpu.VMEM((2, kc, mq), F32),
                        pltpu.VMEM((2, kc, mq), BF16), pltpu.VMEM((2, kc, mq), BF16),
                        pltpu.VMEM((2, VT_ROWS, mq), F32)],
        compiler_params=_params(3),
        name="diff_attention",
    )(lq1[None], lk1[None], lq2[None], lk2[None], norm_g[:, None], q, k_all, vt)


def _conv_kernel(prev_ref, cur_ref, next_ref, w_ref, b_ref, g_ref, beta_ref, o_ref, win, *, row_chunk):
    i = pl.program_id(1)
    tc = cur_ref.shape[1]
    span = tc + 2 * CONV_HALO - SUBLANES
    win[0, CONV_HALO:CONV_HALO + tc, :] = cur_ref[0]
    win[0, 0:CONV_HALO, :] = jnp.where(i > 0, prev_ref[0], 0.0)
    win[0, CONV_HALO + tc:, :] = jnp.where(i < pl.num_programs(1) - 1, next_ref[0], 0.0)
    for s in range(1, SUBLANES):
        win[s, 0:span, :] = win[0, s:s + span, :]
    first_tap = CONV_HALO - CONV_TAPS // 2
    for c0 in range(0, tc, row_chunk):
        acc = jnp.zeros((row_chunk, CONV_W), F32)
        for t in range(CONV_TAPS):
            s = (first_tap + t) % SUBLANES
            r0 = c0 + first_tap + t - s
            acc = acc + w_ref[t:t + 1, :] * win[s, r0:r0 + row_chunk, :]
        y = _layer_norm(acc + b_ref[...], g_ref[...], beta_ref[...])
        o_ref[0, c0:c0 + row_chunk, :] = (y * _sigmoid(y)).astype(o_ref.dtype)


def _conformer_conv(a, conv_w, conv_b, ln_g, ln_b):
    b, length, w = a.shape
    tc = _pick(length, (512, 256, 128))
    hb = tc // CONV_HALO
    n_halo = length // CONV_HALO
    w_pad = jnp.concatenate([conv_w, jnp.zeros((1, w), F32)], 0)
    vec = lambda bi, i: (0, 0)
    return pl.pallas_call(
        functools.partial(_conv_kernel, row_chunk=64),
        grid=(b, length // tc),
        in_specs=[pl.BlockSpec((1, CONV_HALO, w), lambda bi, i: (bi, jnp.maximum(i * hb - 1, 0), 0)),
                  pl.BlockSpec((1, tc, w), lambda bi, i: (bi, i, 0)),
                  pl.BlockSpec((1, CONV_HALO, w), lambda bi, i: (bi, jnp.minimum((i + 1) * hb, n_halo - 1), 0)),
                  pl.BlockSpec((CONV_TAPS + 1, w), vec),
                  pl.BlockSpec((1, w), vec), pl.BlockSpec((1, w), vec), pl.BlockSpec((1, w), vec)],
        out_specs=pl.BlockSpec((1, tc, w), lambda bi, i: (bi, i, 0)),
        out_shape=jax.ShapeDtypeStruct((b, length, w), BF16),
        scratch_shapes=[pltpu.VMEM((SUBLANES, tc + 2 * CONV_HALO, w), F32)],
        compiler_params=_params(2),
        name="conformer_conv",
    )(a, a, a, w_pad, conv_b[None], ln_g[None], ln_b[None])


def _post_mix_kernel(ya_ref, yb_ref, wa_ref, wb_ref, x_ref, g1_ref, lng_ref, lnb_ref, sc_ref, sh_ref,
                     rw_ref, rb_ref, h_ref, e_ref, gate_ref, pos_ref, cnt_ref, carry, *, alpha):
    first_step = jnp.logical_and(pl.program_id(0) == 0, pl.program_id(1) == 0)

    @pl.when(first_step)
    def _():
        carry[...] = jnp.zeros(carry.shape, F32)

    y = (jnp.dot(ya_ref[0].astype(BF16), wa_ref[...], preferred_element_type=F32)
         + jnp.dot(yb_ref[0].astype(BF16), wb_ref[...], preferred_element_type=F32))
    h = _layer_norm(alpha * x_ref[0] + g1_ref[0] * y, lng_ref[...], lnb_ref[...])
    h_ref[0] = h
    u2 = h * (1.0 + sc_ref[0]) + sh_ref[0]
    u_hi = u2.astype(BF16)
    u_lo = (u2 - u_hi.astype(F32)).astype(BF16)
    hh_hl = jnp.dot(u_hi, rw_ref[...], preferred_element_type=F32)
    lh = jnp.dot(u_lo, rw_ref[:, :LANES], preferred_element_type=F32)
    logits = hh_hl[:, :LANES] + hh_hl[:, LANES:] + lh + rb_ref[...]

    tm = logits.shape[0]
    lane = lax.broadcasted_iota(I32, logits.shape, 1).astype(F32)
    vals, idxs = [], []
    lg = logits
    for _ in range(TOP_K):
        mx = jnp.max(lg, axis=1, keepdims=True)
        idx = jnp.min(jnp.where(lg == mx, lane, float(LANES)), axis=1, keepdims=True)
        vals.append(mx)
        idxs.append(idx)
        lg = jnp.where(lane == idx, -3e38, lg)
    exps = [jnp.exp(v - vals[0]) for v in vals]
    den = exps[0] + exps[1] + exps[2] + exps[3]

    hot = jnp.zeros(logits.shape, F32)
    for idx in idxs:
        hot = hot + (lane == idx).astype(F32)
    r_i = lax.broadcasted_iota(I32, (tm, tm), 0)
    c_i = lax.broadcasted_iota(I32, (tm, tm), 1)
    tri = (c_i < r_i).astype(BF16)
    before = jnp.dot(tri, hot.astype(BF16), preferred_element_type=F32) + carry[...]
    e_out = jnp.zeros(logits.shape, F32)
    g_out = jnp.zeros(logits.shape, F32)
    p_out = jnp.zeros(logits.shape, F32)
    for j in range(TOP_K):
        pos_j = jnp.sum(jnp.where(lane == idxs[j], before, 0.0), axis=1, keepdims=True)
        sel = lane == float(j)
        e_out = jnp.where(sel, idxs[j], e_out)
        g_out = jnp.where(sel, exps[j] / den, g_out)
        p_out = jnp.where(sel, pos_j, p_out)
    e_ref[...] = e_out.T[:SUBLANES].astype(I32)
    gate_ref[...] = g_out
    pos_ref[...] = p_out.T[:SUBLANES].astype(I32)
    new_carry = carry[...] + jnp.sum(hot, axis=0, keepdims=True)
    carry[...] = new_carry
    cnt_ref[...] = jnp.broadcast_to(new_carry, cnt_ref.shape).astype(I32)


def _post_mix(ya, yb, w_out_bf, x, g1, ln_g, ln_b, sc2, sh2, router_w, router_b, alpha):
    b, length, d = x.shape
    wa = ya.shape[2]
    tm = _pick(length, (512, 256, 128))
    n_tok = b * length
    rw = jnp.concatenate([router_w, jnp.zeros((d, LANES - N_EXPERTS), F32)], 1)
    rw_hi = rw.astype(BF16)
    rw = jnp.concatenate([rw_hi, (rw - rw_hi.astype(F32)).astype(BF16)], 1)
    rb =jnp.concatenate([router_b, jnp.full((LANES - N_EXPERTS,), NEG_BIG, F32)])[None]
    row = lambda bi, i: (bi, i, 0)
    vec = lambda bi, i: (bi, 0, 0)
    const = lambda bi, i: (0, 0)
    nt = length // tm
    tok = lambda bi, i: (bi * nt + i, 0)
    return pl.pallas_call(
        functools.partial(_post_mix_kernel, alpha=alpha),
        grid=(b, nt),
        in_specs=[pl.BlockSpec((1, tm, wa), row), pl.BlockSpec((1, tm, d - wa), row),
                  pl.BlockSpec((wa, d), const), pl.BlockSpec((d - wa, d), const),
                  pl.BlockSpec((1, tm, d), row), pl.BlockSpec((1, 1, d), vec),
                  pl.BlockSpec((1, d), const), pl.BlockSpec((1, d), const),
                  pl.BlockSpec((1, 1, d), vec), pl.BlockSpec((1, 1, d), vec),
                  pl.BlockSpec((d, 2 * LANES), const), pl.BlockSpec((1, LANES), const)],
        out_specs=[pl.BlockSpec((1, tm, d), row), pl.BlockSpec((SUBLANES, tm), lambda bi, i: (0, bi * nt + i)),
                   pl.BlockSpec((tm, LANES), tok), pl.BlockSpec((SUBLANES, tm), lambda bi, i: (0, bi * nt + i)),
                   pl.BlockSpec((SUBLANES, LANES), const)],
        out_shape=[jax.ShapeDtypeStruct((b, length, d), F32), jax.ShapeDtypeStruct((SUBLANES, n_tok), I32),
                   jax.ShapeDtypeStruct((n_tok, LANES), F32), jax.ShapeDtypeStruct((SUBLANES, n_tok), I32),
                   jax.ShapeDtypeStruct((SUBLANES, LANES), I32)],
        scratch_shapes=[pltpu.VMEM((1, LANES), F32)],
        compiler_params=_params(2),
        name="out_proj_norm_router",
    )(ya, yb, w_out_bf[:wa], w_out_bf[wa:], x, g1, ln_g[None], ln_b[None], sc2, sh2, rw, rb)


def _tile_indices(dest, tm):
    n_tok = dest.shape[1]
    return dest.reshape(TOP_K, n_tok // tm, tm).transpose(1, 0, 2).reshape(n_tok // tm, 1, TOP_K * tm)


def _dispatch_kernel(zstart_ref, zvalid_ref, dest_ref, h_ref, sc_ref, sh_ref, xs_ref, u_sc, zero_sc, sem, zsem):
    tm = h_ref.shape[0]

    @pl.when(pl.program_id(0) == 0)
    def _():
        zero_sc[...] = jnp.zeros(zero_sc.shape, F32)

        def fill(e, c):
            @pl.when(zvalid_ref[e] > 0)
            def _():
                start = pl.multiple_of(zstart_ref[e], SUBLANES)
                pltpu.make_async_copy(zero_sc, xs_ref.at[pl.ds(start, EXPERT_ROWS)], zsem).start()
            return c

        lax.fori_loop(0, N_EXPERTS, fill, 0)

        def drain(e, c):
            @pl.when(zvalid_ref[e] > 0)
            def _():
                pltpu.make_async_copy(zero_sc, xs_ref.at[pl.ds(0, EXPERT_ROWS)], zsem).wait()
            return c

        lax.fori_loop(0, N_EXPERTS, drain, 0)

    u_sc[...] = h_ref[...] * (1.0 + sc_ref[0]) + sh_ref[0]

    def issue(g, c):
        for rr in range(ISSUE_UNROLL):
            r = g * ISSUE_UNROLL + rr
            for j in range(TOP_K):
                d = dest_ref[0, 0, j * tm + r]
                pltpu.make_async_copy(u_sc.at[pl.ds(r, 1)], xs_ref.at[pl.ds(d, 1)], sem).start()
        return c

    lax.fori_loop(0, tm // ISSUE_UNROLL, issue, 0)
    rows = xs_ref.at[pl.ds(0, tm * TOP_K)]
    pltpu.make_async_copy(rows, rows, sem).wait()


def _dispatch(h_flat, sc2, sh2, dest, zstart, zvalid, n_rows, tokens_per_batch):
    n_tok, d = h_flat.shape
    tm = _pick(tokens_per_batch, (512, 256, 128))
    per_b = tokens_per_batch // tm
    return pl.pallas_call(
        _dispatch_kernel,
        grid_spec=pltpu.PrefetchScalarGridSpec(
            num_scalar_prefetch=2,
            grid=(n_tok // tm,),
            in_specs=[pl.BlockSpec((1, 1, tm * TOP_K), lambda i, zs, zv: (i, 0, 0), memory_space=pltpu.SMEM),
                      pl.BlockSpec((tm, d), lambda i, zs, zv: (i, 0)),
                      pl.BlockSpec((1, 1, d), lambda i, zs, zv: (i // per_b, 0, 0)),
                      pl.BlockSpec((1, 1, d), lambda i, zs, zv: (i // per_b, 0, 0))],
            out_specs=pl.BlockSpec(memory_space=pl.ANY),
            scratch_shapes=[pltpu.VMEM((tm, d), F32), pltpu.VMEM((EXPERT_ROWS, d), F32),
                            pltpu.SemaphoreType.DMA(()), pltpu.SemaphoreType.DMA(())]),
        out_shape=jax.ShapeDtypeStruct((n_rows, d), F32),
        compiler_params=_params(1),
        name="moe_dispatch",
    )(zstart, zvalid, _tile_indices(dest, tm), h_flat, sc2, sh2)


def _expert_kernel(blk_e_ref, n_used_ref, x_ref, wgu_ref, bgu_ref, wd_ref, bd_ref, o_ref, wgu_bf, wd_bf):
    i = pl.program_id(0)
    new_expert = jnp.logical_or(i == 0, blk_e_ref[i] != blk_e_ref[jnp.maximum(i - 1, 0)])

    @pl.when(new_expert)
    def _():
        wgu_bf[...] = wgu_ref[0].astype(BF16)
        wd_bf[...] = wd_ref[0].astype(BF16)

    @pl.when(i < n_used_ref[0])
    def _():
        d_e = wd_ref.shape[1]
        gu = jnp.dot(x_ref[...].astype(BF16), wgu_bf[...], preferred_element_type=F32) + bgu_ref[0]
        x_glu = jnp.minimum(gu[:, :d_e], SWIGLU_LIMIT)
        x_lin = jnp.clip(gu[:, d_e:], -SWIGLU_LIMIT, SWIGLU_LIMIT)
        act = x_glu * _sigmoid(SWIGLU_ALPHA * x_glu) * (x_lin + 1.0)
        o_ref[...] = jnp.dot(act.astype(BF16), wd_bf[...], preferred_element_type=F32) + bd_ref[0]


def _experts(xs, blk_e, n_used, wgu, bgu, wd, bd, layer):
    n_rows, d = xs.shape
    depth, n_e, _, n_gu = wgu.shape
    n_blk = n_rows // EXPERT_ROWS
    rows = lambda i, be, nu: (jnp.minimum(i, nu[0] - 1), 0)
    first = layer * n_e
    by_e = lambda i, be, nu: (first + be[i], 0, 0)
    n_e = depth * n_e
    wgu = wgu.reshape(n_e, d, n_gu)
    wd = wd.reshape(n_e, n_gu // 2, d)
    return pl.pallas_call(
        _expert_kernel,
        grid_spec=pltpu.PrefetchScalarGridSpec(
            num_scalar_prefetch=2,
            grid=(n_blk,),
            in_specs=[pl.BlockSpec((EXPERT_ROWS, d), rows),
                      pl.BlockSpec((1, d, n_gu), by_e), pl.BlockSpec((1, 1, n_gu), by_e),
                      pl.BlockSpec((1, n_gu // 2, d), by_e), pl.BlockSpec((1, 1, d), by_e)],
            out_specs=pl.BlockSpec((EXPERT_ROWS, d), rows),
            scratch_shapes=[pltpu.VMEM((d, n_gu), BF16), pltpu.VMEM((n_gu // 2, d), BF16)]),
        out_shape=jax.ShapeDtypeStruct((n_rows, d), F32),
        compiler_params=_params(1),
        name="moe_experts",
    )(blk_e, n_used, xs, wgu, bgu.reshape(n_e, 1, n_gu), wd, bd.reshape(n_e, 1, d))


def _combine_kernel(dest_ref, dest_next_ref, y_ref, h_ref, gate_ref, g2_ref, lng_ref, lnb_ref, o_ref,
                    buf, sem, *, alpha):
    i = pl.program_id(0)
    tm = h_ref.shape[0]

    def fetch(idx_ref, slot):
        def issue(g, c):
            for rr in range(ISSUE_UNROLL):
                r = g * ISSUE_UNROLL + rr
                for j in range(TOP_K):
                    d = idx_ref[0, 0, j * tm + r]
                    pltpu.make_async_copy(y_ref.at[pl.ds(d, 1)], buf.at[slot, j, pl.ds(r, 1)], sem.at[slot]).start()
            return c

        lax.fori_loop(0, tm // ISSUE_UNROLL, issue, 0)

    slot = i % 2

    @pl.when(i == 0)
    def _():
        fetch(dest_ref, 0)

    @pl.when(i + 1 < pl.num_programs(0))
    def _():
        fetch(dest_next_ref, 1 - slot)

    pltpu.make_async_copy(buf.at[slot], buf.at[slot], sem.at[slot]).wait()
    gate = gate_ref[...]
    y2 = gate[:, 0:1] * buf[slot, 0]
    for j in range(1, TOP_K):
        y2 = y2 + gate[:, j:j + 1] * buf[slot, j]
    o_ref[...] = _layer_norm(alpha * h_ref[...] + g2_ref[0] * y2, lng_ref[...], lnb_ref[...])


def _combine(y_rows, dest, h_flat, gate, g2, ln_g, ln_b, alpha, tokens_per_batch):
    n_tok, d = h_flat.shape
    tm = _pick(tokens_per_batch, (256, 128))
    per_b = tokens_per_batch // tm
    n_tiles = n_tok // tm
    dest3 = _tile_indices(dest, tm)
    return pl.pallas_call(
        functools.partial(_combine_kernel, alpha=alpha),
        grid=(n_tiles,),
        in_specs=[pl.BlockSpec((1, 1, tm * TOP_K), lambda i: (i, 0, 0), memory_space=pltpu.SMEM),
                  pl.BlockSpec((1, 1, tm * TOP_K), lambda i: (jnp.minimum(i + 1, n_tiles - 1), 0, 0),
                               memory_space=pltpu.SMEM),
                  pl.BlockSpec(memory_space=pl.ANY),
                  pl.BlockSpec((tm, d), lambda i: (i, 0)),
                  pl.BlockSpec((tm, LANES), lambda i: (i, 0)),
                  pl.BlockSpec((1, 1, d), lambda i: (i // per_b, 0, 0)),
                  pl.BlockSpec((1, d), lambda i: (0, 0)), pl.BlockSpec((1, d), lambda i: (0, 0))],
        out_specs=pl.BlockSpec((tm, d), lambda i: (i, 0)),
        out_shape=jax.ShapeDtypeStruct((n_tok, d), F32),
        scratch_shapes=[pltpu.VMEM((2, TOP_K, tm, d), F32), pltpu.SemaphoreType.DMA((2,))],
        compiler_params=_params(1),
        name="moe_combine_norm",
    )(dest3, dest3, y_rows, h_flat, gate, g2, ln_g[None], ln_b[None])


def _moe_and_norm(h1, top_e, gate, pos, counts, sc2, sh2, g2, wgu, bgu, wd, bd, layer, ln_g, ln_b, alpha):
    b, length, d = h1.shape
    n_tok = b * length
    n_blk = n_tok * TOP_K // EXPERT_ROWS + N_EXPERTS
    padded = (counts + EXPERT_ROWS - 1) // EXPERT_ROWS * EXPERT_ROWS
    pad_end = jnp.cumsum(padded)
    offset = pad_end - padded
    n_used = (pad_end[-1:] // EXPERT_ROWS).astype(I32)
    blk_start = jnp.arange(n_blk, dtype=I32) * EXPERT_ROWS
    blk_e = jnp.minimum(jnp.sum(pad_end[None, :] <= blk_start[:, None], axis=1), N_EXPERTS - 1).astype(I32)
    dest = pos
    for e in range(N_EXPERTS):
        dest = dest + jnp.where(top_e == e, offset[e], 0)
    dest = dest.astype(I32)
    zstart = jnp.maximum(pad_end - EXPERT_ROWS, 0).astype(I32)
    zvalid = (counts > 0).astype(I32)

    h_flat = h1.reshape(n_tok, d)
    xs = _dispatch(h_flat, sc2, sh2, dest, zstart, zvalid, n_blk * EXPERT_ROWS, length)
    y_rows = _experts(xs, blk_e, n_used, wgu, bgu, wd, bd, layer)
    out = _combine(y_rows, dest, h_flat, gate, g2, ln_g, ln_b, alpha, length)
    return out.reshape(b, length, d)


def _odd_in_kernel(x_ref, sc_ref, sh_ref, w_ref, glg_ref, glb_ref, ws_ref, bs_ref, flg_ref, flb_ref, dft_ref,
                   sp_ref, zr_ref, zi_ref):
    u = (x_ref[0] * (1.0 + sc_ref[0]) + sh_ref[0]).astype(BF16)
    z = jnp.dot(u, w_ref[...], preferred_element_type=F32)
    tm = z.shape[0]
    gw = N_GROUPS * GROUP_W
    ug = jax.nn.gelu(z[:, :gw])
    vn = _layer_norm(jax.nn.gelu(z[:, gw:2 * gw]), glg_ref[...], glb_ref[...]).astype(BF16)
    f = z[:, 2 * gw:]
    for g in range(N_GROUPS):
        cols = slice(g * GROUP_W, (g + 1) * GROUP_W)
        for c0 in range(0, tm, CHUNK):
            rows = slice(c0, c0 + CHUNK)
            sv = jnp.dot(ws_ref[g], vn[rows, cols], preferred_element_type=F32) + bs_ref[:, cols]
            sp_ref[0, rows, cols] = (ug[rows, cols] * sv).astype(sp_ref.dtype)
        fn = _layer_norm(f[:, cols], flg_ref[:, cols], flb_ref[:, cols]).astype(BF16)
        zz = jnp.dot(fn, dft_ref[...], preferred_element_type=F32)
        zr_ref[0, :, cols] = zz[:, :GROUP_W]
        zi_ref[0, :, cols] = zz[:, GROUP_W:]


def _odd_in_proj(x, sc, sh, w_bf, gln_g, gln_b, ws, bs, fln_g, fln_b):
    b, length, d = x.shape
    n = w_bf.shape[1]
    gw = N_GROUPS * GROUP_W
    tm = _pick(length, (512, 256, 128))
    kk = jnp.arange(GROUP_W, dtype=I32)
    ang = (2.0 * math.pi / GROUP_W) * ((kk[:, None] * kk[None, :]) % GROUP_W).astype(F32)
    dft = jnp.concatenate([jnp.cos(ang), -jnp.sin(ang)], 1).astype(BF16)
    bs_exp = jnp.repeat(bs.T, GROUP_W, axis=1)
    row = lambda bi, i: (bi, i, 0)
    vec = lambda bi, i: (bi, 0, 0)
    const2 = lambda bi, i: (0, 0)
    return pl.pallas_call(
        _odd_in_kernel,
        grid=(b, length // tm),
        in_specs=[pl.BlockSpec((1, tm, d), row), pl.BlockSpec((1, 1, d), vec), pl.BlockSpec((1, 1, d), vec),
                  pl.BlockSpec((d, n), const2),
                  pl.BlockSpec((1, gw), const2), pl.BlockSpec((1, gw), const2),
                  pl.BlockSpec((N_GROUPS, CHUNK, CHUNK), lambda bi, i: (0, 0, 0)),
                  pl.BlockSpec((CHUNK, gw), const2),
                  pl.BlockSpec((1, gw), const2), pl.BlockSpec((1, gw), const2),
                  pl.BlockSpec((GROUP_W, 2 * GROUP_W), const2)],
        out_specs=[pl.BlockSpec((1, tm, gw), row), pl.BlockSpec((1, tm, gw), row), pl.BlockSpec((1, tm, gw), row)],
        out_shape=[jax.ShapeDtypeStruct((b, length, gw), BF16), jax.ShapeDtypeStruct((b, length, gw), F32),
                   jax.ShapeDtypeStruct((b, length, gw), F32)],
        compiler_params=_params(2),
        name="odd_in_proj",
    )(x, sc, sh, w_bf, gln_g[None], gln_b[None], ws.astype(BF16), bs_exp, fln_g[None], fln_b[None], dft)


def _fft_a_kernel(zr_ref, zi_ref, c_ref, s_ref, ar_ref, ai_ref):
    l1, nb, w = zr_ref.shape[1:]
    zr = zr_ref[0].reshape(l1 * nb, w).astype(BF16)
    zi = zi_ref[0].reshape(l1 * nb, w).astype(BF16)
    c = c_ref[...]
    s = s_ref[...]
    ar = jnp.dot(c, zr, preferred_element_type=F32) + jnp.dot(s, zi, preferred_element_type=F32)
    ai = jnp.dot(c, zi, preferred_element_type=F32) - jnp.dot(s, zr, preferred_element_type=F32)
    ar_ref[0] = ar.reshape(l1, nb, w)
    ai_ref[0] = ai.reshape(l1, nb, w)


def _fft_b_kernel(ar_ref, ai_ref, c_ref, s_ref, o_ref, *, norm):
    nb, l2, w = ar_ref.shape[1:]
    ar = ar_ref[0].reshape(nb * l2, w).astype(BF16)
    ai = ai_ref[0].reshape(nb * l2, w).astype(BF16)
    y = jnp.dot(c_ref[0], ar, preferred_element_type=F32) + jnp.dot(s_ref[0], ai, preferred_element_type=F32)
    o_ref[0] = (y * norm).reshape(l2, nb, w)


def _length_dft_real(zr, zi):
    b, length, w = zr.shape
    l2 = CHUNK
    l1 = length // l2
    nb = SUBLANES
    k1 = jnp.arange(l1, dtype=I32)
    ang_a = (2.0 * math.pi / l1) * ((k1[:, None] * k1[None, :]) % l1).astype(F32)
    r = lax.broadcasted_iota(I32, (l1 * nb, l1 * nb), 0)
    c = lax.broadcasted_iota(I32, (l1 * nb, l1 * nb), 1)
    same = r % nb == c % nb
    rep = (lax.broadcasted_iota(I32, (l1 * nb, l1), 0) // nb == lax.broadcasted_iota(I32, (l1 * nb, l1), 1))
    rep = rep.astype(BF16)

    def expand_a(t):
        t = jnp.dot(jnp.dot(rep, t.astype(BF16), preferred_element_type=F32).astype(BF16), rep.T,
                    preferred_element_type=F32)
        return jnp.where(same, t, 0.0).astype(BF16)

    cos_a = expand_a(jnp.cos(ang_a))
    sin_a = expand_a(jnp.sin(ang_a))
    shape_t = (l1 // nb, l2 * nb, l2)
    kb = lax.broadcasted_iota(I32, shape_t, 0)
    r = lax.broadcasted_iota(I32, shape_t, 1)
    n2 = lax.broadcasted_iota(I32, shape_t, 2)
    ang_b = (2.0 * math.pi / length) * ((n2 * (kb * nb + r % nb + l1 * (r // nb))) % length).astype(F32)
    shape_b = (l1 // nb, l2 * nb, nb * l2)
    same = lax.broadcasted_iota(I32, shape_b, 1) % nb == lax.broadcasted_iota(I32, shape_b, 2) // l2

    def expand_b(t):
        return jnp.where(same, jnp.concatenate([t] * nb, axis=2), 0.0).astype(BF16)

    cos_t, sin_t = lax.optimization_barrier((jnp.cos(ang_b), jnp.sin(ang_b)))
    cos_b = expand_b(cos_t)
    sin_b = expand_b(sin_t)

    zr4 = zr.reshape(b, l1, l2, w)
    zi4 = zi.reshape(b, l1, l2, w)
    blk_a = pl.BlockSpec((1, l1, nb, w), lambda bi, i: (bi, 0, i, 0))
    ar, ai = pl.pallas_call(
        _fft_a_kernel,
        grid=(b, l2 // nb),
        in_specs=[blk_a, blk_a, pl.BlockSpec((l1 * nb, l1 * nb), lambda bi, i: (0, 0)),
                  pl.BlockSpec((l1 * nb, l1 * nb), lambda bi, i: (0, 0))],
        out_specs=[blk_a, blk_a],
        out_shape=[jax.ShapeDtypeStruct((b, l1, l2, w), F32)] * 2,
        compiler_params=_params(2),
        name="fourier_stage_a",
    )(zr4, zi4, cos_a, sin_a)
    blk_b = pl.BlockSpec((1, nb, l2, w), lambda bi, i: (bi, i, 0, 0))
    out = pl.pallas_call(
        functools.partial(_fft_b_kernel, norm=float((length * GROUP_W) ** -0.5)),
        grid=(b, l1 // nb),
        in_specs=[blk_b, blk_b, pl.BlockSpec((1, l2 * nb, nb * l2), lambda bi, i: (i, 0, 0)),
                  pl.BlockSpec((1, l2 * nb, nb * l2), lambda bi, i: (i, 0, 0))],
        out_specs=pl.BlockSpec((1, l2, nb, w), lambda bi, i: (bi, 0, i, 0)),
        out_shape=jax.ShapeDtypeStruct((b, l2, l1, w), F32),
        compiler_params=_params(2),
        name="fourier_stage_b",
    )(ar, ai, cos_b, sin_b)
    return out.reshape(b, length, w)


def kernel(x, c, ctx, c_ctx, w_mod, b_mod, ln1_g, ln1_b, ln2_g, ln2_b, ev_w_in, ev_w_out, conv_w, conv_b, conv_ln_g, conv_ln_b, lam_q1, lam_k1, lam_q2, lam_k2, diff_norm_g, od_w_in, od_w_out, gmlp_ln_g, gmlp_ln_b, gmlp_ws, gmlp_bs, four_ln_g, four_ln_b, router_w, router_b, w_gate_up, b_gate_up, w_down, b_down):
    b, length, d = x.shape
    depth = w_mod.shape[0]
    alpha = float((2 * depth) ** 0.25)
    assert b + 1 <= SUBLANES and length % (CHUNK * SUBLANES) == 0 and length % GRID_W == 0
    assert length % KV_TILE == 0 and ctx.shape[1] % KV_TILE == 0

    mod = _modulation(c, c_ctx, w_mod, b_mod)
    cos_t, sin_t = _rope_tables(length)
    h = x
    for layer in range(depth):
        j = layer // 2
        m = mod[layer]
        sh1, sc1, g1, sh2, sc2, g2 = [m[:b, i * d:(i + 1) * d][:, None, :] for i in range(6)]
        if layer % 2 == 0:
            lam_init = 0.8 - 0.6 * math.exp(-0.3 * layer)
            w_in = ev_w_in[j].astype(BF16)
            q, k_all, vt, a = _even_in_proj(h, sc1, sh1, w_in, cos_t, sin_t, length + ctx.shape[1])
            csh1 = jnp.broadcast_to(m[b:b + 1, 0:d][:, None, :], (b, 1, d))
            csc1 = jnp.broadcast_to(m[b:b + 1, d:2 * d][:, None, :], (b, 1, d))
            k_all, vt = _context_kv(ctx, csc1, csh1, w_in[:, QK_W:2 * QK_W + V_W], k_all, vt, length)
            att = _diff_attention(q, k_all, vt, lam_q1[j], lam_k1[j], lam_q2[j], lam_k2[j],
                                  diff_norm_g[j], lam_init)
            conv = _conformer_conv(a, conv_w[j], conv_b[j], conv_ln_g[j], conv_ln_b[j])
            ya, yb, w_out = conv, att, ev_w_out[j]
        else:
            spatial, zr, zi = _odd_in_proj(h, sc1, sh1, od_w_in[j].astype(BF16), gmlp_ln_g[j], gmlp_ln_b[j],
                                           gmlp_ws[j], gmlp_bs[j], four_ln_g[j], four_ln_b[j])
            ya, yb, w_out = spatial, _length_dft_real(zr, zi), od_w_out[j]
        h1, top_e, gate, pos, counts = _post_mix(ya, yb, w_out.astype(BF16), h, g1, ln1_g[layer], ln1_b[layer],
                                                 sc2, sh2, router_w[layer], router_b[layer], alpha)
        h = _moe_and_norm(h1, top_e[:TOP_K], gate, pos[:TOP_K], counts[0, :N_EXPERTS], sc2, sh2, g2,
                          w_gate_up, b_gate_up, w_down, b_down, layer, ln2_g[layer], ln2_b[layer], alpha)
    return h
```

```python
import functools
import math

import jax
import jax.numpy as jnp
from jax import lax
from jax.experimental import pallas as pl
from jax.experimental.pallas import tpu as pltpu

F32 = jnp.float32
BF16 = jnp.bfloat16
I32 = jnp.int32
HIGHEST = lax.Precision.HIGHEST

LN_EPS = 1e-5
GRID_W = 64
HEAD_DIM = 64
HEAD_V = 128
N_HEADS = 4
QK_W = N_HEADS * 2 * HEAD_DIM
V_W = N_HEADS * HEAD_V
CONV_W = 512
CONV_TAPS = 31
CONV_HALO = 16
ROPE_BASE = 10000.0
CHUNK = 128
N_GROUPS = 4
GROUP_W = 128
N_EXPERTS = 32
TOP_K = 4
SWIGLU_LIMIT = 7.0
SWIGLU_ALPHA = 1.702
LANES = 128
SUBLANES = 8
EXPERT_ROWS = 512
ISSUE_UNROLL = 8
NEG_BIG = -1e30
VMEM_LIMIT = 56 * 1024 * 1024


def _params(n_axes):
    return pltpu.CompilerParams(dimension_semantics=("arbitrary",) * n_axes,
                                vmem_limit_bytes=VMEM_LIMIT)


def _pick(n, candidates):
    for c in candidates:
        if n % c == 0:
            return c
    return n


def _layer_norm(r, g, b):
    mu = jnp.mean(r, axis=-1, keepdims=True)
    d = r - mu
    var = jnp.mean(d * d, axis=-1, keepdims=True)
    return d * lax.rsqrt(var + LN_EPS) * g + b


def _sigmoid(x):
    return 1.0 / (1.0 + jnp.exp(-x))


def _mod_kernel(cs_ref, w_ref, b_ref, o_ref):
    cs = cs_ref[...]
    a = cs * _sigmoid(cs)
    o_ref[0] = jnp.dot(a, w_ref[0], precision=HIGHEST, preferred_element_type=F32) + b_ref[0]


def _modulation(c, c_ctx, w_mod, b_mod):
    depth, d, n = w_mod.shape
    b = c.shape[0]
    cs = jnp.concatenate([c, c_ctx[None], jnp.zeros((SUBLANES - b - 1, d), F32)], 0)
    tn = _pick(n, (1536, 1024, 512))
    return pl.pallas_call(
        _mod_kernel,
        grid=(depth, n // tn),
        in_specs=[pl.BlockSpec((SUBLANES, d), lambda l, j: (0, 0)),
                  pl.BlockSpec((1, d, tn), lambda l, j: (l, 0, j)),
                  pl.BlockSpec((1, 1, tn), lambda l, j: (l, 0, j))],
        out_specs=pl.BlockSpec((1, SUBLANES, tn), lambda l, j: (l, 0, j)),
        out_shape=jax.ShapeDtypeStruct((depth, SUBLANES, n), F32),
        compiler_params=_params(2),
        name="modulation",
    )(cs, w_mod, b_mod.reshape(depth, 1, n))


KV_TILE = 256
VT_ROWS = HEAD_V + 16


def _store_values_transposed(v, vt_ref):
    extra = (lax.broadcasted_iota(I32, (VT_ROWS - HEAD_V, v.shape[0]), 0) == 0).astype(BF16)
    for h in range(N_HEADS):
        vt_ref[0, h, 0, :HEAD_V, :] = v[:, h * HEAD_V:(h + 1) * HEAD_V].T.astype(BF16)
        vt_ref[0, h, 0, HEAD_V:, :] = extra


def _even_in_kernel(x_ref, sc_ref, sh_ref, w_ref, cos_ref, sin_ref, q_ref, k_ref, vt_ref, a_ref):
    u = (x_ref[0] * (1.0 + sc_ref[0]) + sh_ref[0]).astype(BF16)
    z = jnp.dot(u, w_ref[...], preferred_element_type=F32)
    cosv = cos_ref[...]
    sinv = sin_ref[...]
    lane = lax.broadcasted_iota(I32, cosv.shape, 1)
    first = (lane % 32) < 16

    def rope(t):
        partner = jnp.where(first, pltpu.roll(t, LANES - 16, 1), pltpu.roll(t, 16, 1))
        return t * cosv + partner * sinv

    for j in range(QK_W // LANES):
        sl = slice(j * LANES, (j + 1) * LANES)
        rq = rope(z[:, sl]) * (HEAD_DIM ** -0.5 * math.log2(math.e))
        for c in range(2):
            q_ref[c, 0, :, sl] = jnp.where(lane // HEAD_DIM == c, rq, 0.0).astype(BF16)
        k_ref[0, :, sl] = rope(z[:, QK_W + j * LANES:QK_W + (j + 1) * LANES]).astype(BF16)
    _store_values_transposed(z[:, 2 * QK_W:2 * QK_W + V_W], vt_ref)
    a0 = 2 * QK_W + V_W
    a_ref[0] = z[:, a0:a0 + CONV_W] * _sigmoid(z[:, a0 + CONV_W:a0 + 2 * CONV_W])


def _rope_tables(length):
    rows = length // GRID_W
    row = jnp.repeat(jnp.arange(rows, dtype=F32), GRID_W)
    col = jnp.tile(jnp.arange(GRID_W, dtype=F32), rows)
    n_freq = HEAD_DIM // 4
    inv_freq = ROPE_BASE ** (-jnp.arange(n_freq, dtype=F32) / n_freq)
    ar = row[:, None] * inv_freq
    ac = col[:, None] * inv_freq
    cos64 = jnp.concatenate([jnp.cos(ar), jnp.cos(ar), jnp.cos(ac), jnp.cos(ac)], 1)
    sin64 = jnp.concatenate([-jnp.sin(ar), jnp.sin(ar), -jnp.sin(ac), jnp.sin(ac)], 1)
    return jnp.tile(cos64, (1, LANES // HEAD_DIM)), jnp.tile(sin64, (1, LANES // HEAD_DIM))


def _key_chunk(n_keys):
    return next(c for c in (1280, 256) if n_keys % c == 0 and n_keys // c >= 3)


def _kv_specs(b, n_keys, kc, first_tile):
    per_chunk = kc // KV_TILE
    k_spec = pl.BlockSpec((1, KV_TILE, QK_W), lambda bi, i: (bi, first_tile + i, 0))
    vt_spec = pl.BlockSpec((1, N_HEADS, 1, VT_ROWS, KV_TILE),
                           lambda bi, i: (bi, 0, (first_tile + i) // per_chunk, 0, (first_tile + i) % per_chunk))
    shapes = [jax.ShapeDtypeStruct((b, n_keys, QK_W), BF16),
              jax.ShapeDtypeStruct((b, N_HEADS, n_keys // kc, VT_ROWS, kc), BF16)]
    return k_spec, vt_spec, shapes


def _even_in_proj(x, sc, sh, w_bf, cos_t, sin_t, n_keys):
    b, length, d = x.shape
    n = w_bf.shape[1]
    tm = KV_TILE
    row = lambda bi, i: (bi, i, 0)
    vec = lambda bi, i: (bi, 0, 0)
    k_spec, vt_spec, kv_shapes = _kv_specs(b, n_keys, _key_chunk(n_keys), 0)
    return pl.pallas_call(
        _even_in_kernel,
        grid=(b, length // tm),
        in_specs=[pl.BlockSpec((1, tm, d), row),
                  pl.BlockSpec((1, 1, d), vec),
                  pl.BlockSpec((1, 1, d), vec),
                  pl.BlockSpec((d, n), lambda bi, i: (0, 0)),
                  pl.BlockSpec((tm, LANES), lambda bi, i: (i, 0)),
                  pl.BlockSpec((tm, LANES), lambda bi, i: (i, 0))],
        out_specs=[pl.BlockSpec((2, 1, tm, QK_W), lambda bi, i: (0, bi, i, 0)), k_spec, vt_spec,
                   pl.BlockSpec((1, tm, CONV_W), row)],
        out_shape=[jax.ShapeDtypeStruct((2, b, length, QK_W), BF16)] + kv_shapes
                  + [jax.ShapeDtypeStruct((b, length, CONV_W), F32)],
        compiler_params=_params(2),
        name="even_in_proj",
    )(x, sc, sh, w_bf, cos_t, sin_t)


def _ctx_kv_kernel(x_ref, sc_ref, sh_ref, w_ref, k_in, vt_in, k_ref, vt_ref):
    del k_in, vt_in
    u = (x_ref[0] * (1.0 + sc_ref[0]) + sh_ref[0]).astype(BF16)
    z = jnp.dot(u, w_ref[...], preferred_element_type=F32)
    k_ref[0] = z[:, :QK_W].astype(BF16)
    _store_values_transposed(z[:, QK_W:], vt_ref)


def _context_kv(ctx, sc, sh, w_bf, k_all, vt, first_key):
    b, rows, d = ctx.shape
    n = w_bf.shape[1]
    n_keys = k_all.shape[1]
    k_spec, vt_spec, kv_shapes = _kv_specs(b, n_keys, vt.shape[4], first_key // KV_TILE)
    return pl.pallas_call(
        _ctx_kv_kernel,
        grid=(b, rows // KV_TILE),
        in_specs=[pl.BlockSpec((1, KV_TILE, d), lambda bi, i: (bi, i, 0)),
                  pl.BlockSpec((1, 1, d), lambda bi, i: (bi, 0, 0)),
                  pl.BlockSpec((1, 1, d), lambda bi, i: (bi, 0, 0)),
                  pl.BlockSpec((d, n), lambda bi, i: (0, 0)),
                  pl.BlockSpec(memory_space=pl.ANY), pl.BlockSpec(memory_space=pl.ANY)],
        out_specs=[k_spec, vt_spec],
        out_shape=kv_shapes,
        input_output_aliases={4: 0, 5: 1},
        compiler_params=_params(2),
        name="context_kv_proj",
    )(ctx, sc, sh, w_bf, k_all, vt)


def _attn_kernel(lq1_ref, lk1_ref, lq2_ref, lk2_ref, g_ref, q_ref, k_ref, vt_ref, o_ref,
                 s0_sc, s1_sc, p0_sc, p1_sc, acc_sc, *, lam_init, kc):
    mq = q_ref.shape[2]
    n = k_ref.shape[1] // kc
    s_sc = (s0_sc, s1_sc)
    p_sc = (p0_sc, p1_sc)

    def scores(j, slot):
        kj = k_ref[0, pl.ds(pl.multiple_of(j * kc, kc), kc), :]
        for c in range(2):
            s_sc[slot][c] = lax.dot_general(kj, q_ref[c, 0], (((1,), (1,)), ((), ())),
                                            preferred_element_type=F32)

    def softmax(slot, m):
        m_out, alpha = [], []
        for c in range(2):
            s = s_sc[slot][c]
            m_new = jnp.maximum(m[c], jnp.max(s, axis=0, keepdims=True))
            p_sc[slot][c] = jnp.exp2((s - m_new).astype(BF16))
            m_out.append(m_new)
            alpha.append(jnp.exp2(m[c] - m_new))
        return tuple(m_out), tuple(alpha)

    def accumulate(j, slot, alpha):
        vt = vt_ref[0, 0, j]
        for c in range(2):
            acc_sc[c] = alpha[c] * acc_sc[c] + jnp.dot(vt, p_sc[slot][c], preferred_element_type=F32)

    def step(j, slot, m, alpha):
        scores(j + 2, slot)
        m, alpha_next = softmax(1 - slot, m)
        accumulate(j, slot, alpha)
        return m, alpha_next

    acc_sc[...] = jnp.zeros(acc_sc.shape, F32)
    m = (jnp.full((1, mq), NEG_BIG, F32),) * 2
    scores(0, 0)
    m, alpha = softmax(0, m)
    scores(1, 1)

    def pair(t, carry):
        m, alpha = step(2 * t, 0, *carry)
        return step(2 * t + 1, 1, m, alpha)

    n_steps = n - 2
    m, alpha = lax.fori_loop(0, n_steps // 2, pair, (m, alpha))
    if n_steps % 2:
        m, alpha = step(n_steps - 1, 0, m, alpha)
    m, alpha_last = softmax((n - 1) % 2, m)
    accumulate(n - 2, (n - 2) % 2, alpha)
    accumulate(n - 1, (n - 1) % 2, alpha_last)
    acc0 = acc_sc[0]
    acc1 = acc_sc[1]
    lam = (jnp.exp(jnp.sum(lq1_ref[...] * lk1_ref[...], keepdims=True))
           - jnp.exp(jnp.sum(lq2_ref[...] * lk2_ref[...], keepdims=True)) + lam_init)
    o = (acc0[:HEAD_V] / acc0[HEAD_V:HEAD_V + 1]
         - lam * (acc1[:HEAD_V] / acc1[HEAD_V:HEAD_V + 1]))
    ms = jnp.mean(o * o, axis=0, keepdims=True)
    o = o * lax.rsqrt(ms + LN_EPS) * g_ref[...] * (1.0 - lam_init)
    o_ref[0] = o.T.astype(o_ref.dtype)


def _diff_attention(q, k_all, vt, lq1, lk1, lq2, lk2, norm_g, lam_init):
    _, b, length, _ = q.shape
    n_keys = k_all.shape[1]
    mq = _pick(length, (1024, 512, 256, 128))
    nc, kc = vt.shape[2], vt.shape[4]
    small = lambda bi, h, i: (0, 0)
    return pl.pallas_call(
        functools.partial(_attn_kernel, lam_init=lam_init, kc=kc),
        grid=(b, N_HEADS, length // mq),
        in_specs=[pl.BlockSpec((1, HEAD_DIM), small), pl.BlockSpec((1, HEAD_DIM), small),
                  pl.BlockSpec((1, HEAD_DIM), small), pl.BlockSpec((1, HEAD_DIM), small),
                  pl.BlockSpec((HEAD_V, 1), small),
                  pl.BlockSpec((2, 1, mq, HEAD_V), lambda bi, h, i: (0, bi, i, h)),
                  pl.BlockSpec((1, n_keys, HEAD_V), lambda bi, h, i: (bi, 0, h)),
                  pl.BlockSpec((1, 1, nc, VT_ROWS, kc), lambda bi, h, i: (bi, h, 0, 0, 0))],
        out_specs=pl.BlockSpec((1, mq, HEAD_V), lambda bi, h, i: (bi, i, h)),
        out_shape=jax.ShapeDtypeStruct((b, length, V_W), BF16),
        scratch_shapes=[pltpu.VMEM((2, kc, mq), F32), pltpu.VMEM((2, kc, mq), F32),
                        pltpu.VMEM((2, kc, mq), BF16), pltpu.VMEM((2, kc, mq), BF16),
                        pltpu.VMEM((2, VT_ROWS, mq), F32)],
        compiler_params=_params(3),
        name="diff_attention",
    )(lq1[None], lk1[None], lq2[None], lk2[None], norm_g[:, None], q, k_all, vt)


def _conv_kernel(prev_ref, cur_ref, next_ref, w_ref, b_ref, g_ref, beta_ref, o_ref, win, *, row_chunk):
    i = pl.program_id(1)
    tc = cur_ref.shape[1]
    span = tc + 2 * CONV_HALO - SUBLANES
    win[0, CONV_HALO:CONV_HALO + tc, :] = cur_ref[0]
    win[0, 0:CONV_HALO, :] = jnp.where(i > 0, prev_ref[0], 0.0)
    win[0, CONV_HALO + tc:, :] = jnp.where(i < pl.num_programs(1) - 1, next_ref[0], 0.0)
    for s in range(1, SUBLANES):
        win[s, 0:span, :] = win[0, s:s + span, :]
    first_tap = CONV_HALO - CONV_TAPS // 2
    for c0 in range(0, tc, row_chunk):
        acc = jnp.zeros((row_chunk, CONV_W), F32)
        for t in range(CONV_TAPS):
            s = (first_tap + t) % SUBLANES
            r0 = c0 + first_tap + t - s
            acc = acc + w_ref[t:t + 1, :] * win[s, r0:r0 + row_chunk, :]
        y = _layer_norm(acc + b_ref[...], g_ref[...], beta_ref[...])
        o_ref[0, c0:c0 + row_chunk, :] = (y * _sigmoid(y)).astype(o_ref.dtype)


def _conformer_conv(a, conv_w, conv_b, ln_g, ln_b):
    b, length, w = a.shape
    tc = _pick(length, (512, 256, 128))
    hb = tc // CONV_HALO
    n_halo = length // CONV_HALO
    w_pad = jnp.concatenate([conv_w, jnp.zeros((1, w), F32)], 0)
    vec = lambda bi, i: (0, 0)
    return pl.pallas_call(
        functools.partial(_conv_kernel, row_chunk=64),
        grid=(b, length // tc),
        in_specs=[pl.BlockSpec((1, CONV_HALO, w), lambda bi, i: (bi, jnp.maximum(i * hb - 1, 0), 0)),
                  pl.BlockSpec((1, tc, w), lambda bi, i: (bi, i, 0)),
                  pl.BlockSpec((1, CONV_HALO, w), lambda bi, i: (bi, jnp.minimum((i + 1) * hb, n_halo - 1), 0)),
                  pl.BlockSpec((CONV_TAPS + 1, w), vec),
                  pl.BlockSpec((1, w), vec), pl.BlockSpec((1, w), vec), pl.BlockSpec((1, w), vec)],
        out_specs=pl.BlockSpec((1, tc, w), lambda bi, i: (bi, i, 0)),
        out_shape=jax.ShapeDtypeStruct((b, length, w), BF16),
        scratch_shapes=[pltpu.VMEM((SUBLANES, tc + 2 * CONV_HALO, w), F32)],
        compiler_params=_params(2),
        name="conformer_conv",
    )(a, a, a, w_pad, conv_b[None], ln_g[None], ln_b[None])


def _post_mix_kernel(ya_ref, yb_ref, wa_ref, wb_ref, x_ref, g1_ref, lng_ref, lnb_ref, sc_ref, sh_ref,
                     rw_ref, rb_ref, h_ref, e_ref, gate_ref, pos_ref, cnt_ref, carry, *, alpha):
    first_step = jnp.logical_and(pl.program_id(0) == 0, pl.program_id(1) == 0)

    @pl.when(first_step)
    def _():
        carry[...] = jnp.zeros(carry.shape, F32)

    y = (jnp.dot(ya_ref[0].astype(BF16), wa_ref[...], preferred_element_type=F32)
         + jnp.dot(yb_ref[0].astype(BF16), wb_ref[...], preferred_element_type=F32))
    h = _layer_norm(alpha * x_ref[0] + g1_ref[0] * y, lng_ref[...], lnb_ref[...])
    h_ref[0] = h
    u2 = h * (1.0 + sc_ref[0]) + sh_ref[0]
    u_hi = u2.astype(BF16)
    u_lo = (u2 - u_hi.astype(F32)).astype(BF16)
    hh_hl = jnp.dot(u_hi, rw_ref[...], preferred_element_type=F32)
    lh = jnp.dot(u_lo, rw_ref[:, :LANES], preferred_element_type=F32)
    logits = hh_hl[:, :LANES] + hh_hl[:, LANES:] + lh + rb_ref[...]

    tm = logits.shape[0]
    lane = lax.broadcasted_iota(I32, logits.shape, 1).astype(F32)
    vals, idxs = [], []
    lg = logits
    for _ in range(TOP_K):
        mx = jnp.max(lg, axis=1, keepdims=True)
        idx = jnp.min(jnp.where(lg == mx, lane, float(LANES)), axis=1, keepdims=True)
        vals.append(mx)
        idxs.append(idx)
        lg = jnp.where(lane == idx, -3e38, lg)
    exps = [jnp.exp(v - vals[0]) for v in vals]
    den = exps[0] + exps[1] + exps[2] + exps[3]

    hot = jnp.zeros(logits.shape, F32)
    for idx in idxs:
        hot = hot + (lane == idx).astype(F32)
    r_i = lax.broadcasted_iota(I32, (tm, tm), 0)
    c_i = lax.broadcasted_iota(I32, (tm, tm), 1)
    tri = (c_i < r_i).astype(BF16)
    before = jnp.dot(tri, hot.astype(BF16), preferred_element_type=F32) + carry[...]
    e_out = jnp.zeros(logits.shape, F32)
    g_out = jnp.zeros(logits.shape, F32)
    p_out = jnp.zeros(logits.shape, F32)
    for j in range(TOP_K):
        pos_j = jnp.sum(jnp.where(lane == idxs[j], before, 0.0), axis=1, keepdims=True)
        sel = lane == float(j)
        e_out = jnp.where(sel, idxs[j], e_out)
        g_out = jnp.where(sel, exps[j] / den, g_out)
        p_out = jnp.where(sel, pos_j, p_out)
    e_ref[...] = e_out.T[:SUBLANES].astype(I32)
    gate_ref[...] = g_out
    pos_ref[...] = p_out.T[:SUBLANES].astype(I32)
    new_carry = carry[...] + jnp.sum(hot, axis=0, keepdims=True)
    carry[...] = new_carry
    cnt_ref[...] = jnp.broadcast_to(new_carry, cnt_ref.shape).astype(I32)


def _post_mix(ya, yb, w_out_bf, x, g1, ln_g, ln_b, sc2, sh2, router_w, router_b, alpha):
    b, length, d = x.shape
    wa = ya.shape[2]
    tm = _pick(length, (512, 256, 128))
    n_tok = b * length
    rw = jnp.concatenate([router_w, jnp.zeros((d, LANES - N_EXPERTS), F32)], 1)
    rw_hi = rw.astype(BF16)
    rw = jnp.concatenate([rw_hi, (rw - rw_hi.astype(F32)).astype(BF16)], 1)
    rb =jnp.concatenate([router_b, jnp.full((LANES - N_EXPERTS,), NEG_BIG, F32)])[None]
    row = lambda bi, i: (bi, i, 0)
    vec = lambda bi, i: (bi, 0, 0)
    const = lambda bi, i: (0, 0)
    nt = length // tm
    tok = lambda bi, i: (bi * nt + i, 0)
    return pl.pallas_call(
        functools.partial(_post_mix_kernel, alpha=alpha),
        grid=(b, nt),
        in_specs=[pl.BlockSpec((1, tm, wa), row), pl.BlockSpec((1, tm, d - wa), row),
                  pl.BlockSpec((wa, d), const), pl.BlockSpec((d - wa, d), const),
                  pl.BlockSpec((1, tm, d), row), pl.BlockSpec((1, 1, d), vec),
                  pl.BlockSpec((1, d), const), pl.BlockSpec((1, d), const),
                  pl.BlockSpec((1, 1, d), vec), pl.BlockSpec((1, 1, d), vec),
                  pl.BlockSpec((d, 2 * LANES), const), pl.BlockSpec((1, LANES), const)],
        out_specs=[pl.BlockSpec((1, tm, d), row), pl.BlockSpec((SUBLANES, tm), lambda bi, i: (0, bi * nt + i)),
                   pl.BlockSpec((tm, LANES), tok), pl.BlockSpec((SUBLANES, tm), lambda bi, i: (0, bi * nt + i)),
                   pl.BlockSpec((SUBLANES, LANES), const)],
        out_shape=[jax.ShapeDtypeStruct((b, length, d), F32), jax.ShapeDtypeStruct((SUBLANES, n_tok), I32),
                   jax.ShapeDtypeStruct((n_tok, LANES), F32), jax.ShapeDtypeStruct((SUBLANES, n_tok), I32),
                   jax.ShapeDtypeStruct((SUBLANES, LANES), I32)],
        scratch_shapes=[pltpu.VMEM((1, LANES), F32)],
        compiler_params=_params(2),
        name="out_proj_norm_router",
    )(ya, yb, w_out_bf[:wa], w_out_bf[wa:], x, g1, ln_g[None], ln_b[None], sc2, sh2, rw, rb)


def _tile_indices(dest, tm):
    n_tok = dest.shape[1]
    return dest.reshape(TOP_K, n_tok // tm, tm).transpose(1, 0, 2).reshape(n_tok // tm, 1, TOP_K * tm)


def _dispatch_kernel(zstart_ref, zvalid_ref, dest_ref, h_ref, sc_ref, sh_ref, xs_ref, u_sc, zero_sc, sem, zsem):
    tm = h_ref.shape[0]

    @pl.when(pl.program_id(0) == 0)
    def _():
        zero_sc[...] = jnp.zeros(zero_sc.shape, F32)

        def fill(e, c):
            @pl.when(zvalid_ref[e] > 0)
            def _():
                start = pl.multiple_of(zstart_ref[e], SUBLANES)
                pltpu.make_async_copy(zero_sc, xs_ref.at[pl.ds(start, EXPERT_ROWS)], zsem).start()
            return c

        lax.fori_loop(0, N_EXPERTS, fill, 0)

        def drain(e, c):
            @pl.when(zvalid_ref[e] > 0)
            def _():
                pltpu.make_async_copy(zero_sc, xs_ref.at[pl.ds(0, EXPERT_ROWS)], zsem).wait()
            return c

        lax.fori_loop(0, N_EXPERTS, drain, 0)

    u_sc[...] = h_ref[...] * (1.0 + sc_ref[0]) + sh_ref[0]

    def issue(g, c):
        for rr in range(ISSUE_UNROLL):
            r = g * ISSUE_UNROLL + rr
            for j in range(TOP_K):
                d = dest_ref[0, 0, j * tm + r]
                pltpu.make_async_copy(u_sc.at[pl.ds(r, 1)], xs_ref.at[pl.ds(d, 1)], sem).start()
        return c

    lax.fori_loop(0, tm // ISSUE_UNROLL, issue, 0)
    rows = xs_ref.at[pl.ds(0, tm * TOP_K)]
    pltpu.make_async_copy(rows, rows, sem).wait()


def _dispatch(h_flat, sc2, sh2, dest, zstart, zvalid, n_rows, tokens_per_batch):
    n_tok, d = h_flat.shape
    tm = _pick(tokens_per_batch, (512, 256, 128))
    per_b = tokens_per_batch // tm
    return pl.pallas_call(
        _dispatch_kernel,
        grid_spec=pltpu.PrefetchScalarGridSpec(
            num_scalar_prefetch=2,
            grid=(n_tok // tm,),
            in_specs=[pl.BlockSpec((1, 1, tm * TOP_K), lambda i, zs, zv: (i, 0, 0), memory_space=pltpu.SMEM),
                      pl.BlockSpec((tm, d), lambda i, zs, zv: (i, 0)),
                      pl.BlockSpec((1, 1, d), lambda i, zs, zv: (i // per_b, 0, 0)),
                      pl.BlockSpec((1, 1, d), lambda i, zs, zv: (i // per_b, 0, 0))],
            out_specs=pl.BlockSpec(memory_space=pl.ANY),
            scratch_shapes=[pltpu.VMEM((tm, d), F32), pltpu.VMEM((EXPERT_ROWS, d), F32),
                            pltpu.SemaphoreType.DMA(()), pltpu.SemaphoreType.DMA(())]),
        out_shape=jax.ShapeDtypeStruct((n_rows, d), F32),
        compiler_params=_params(1),
        name="moe_dispatch",
    )(zstart, zvalid, _tile_indices(dest, tm), h_flat, sc2, sh2)


def _expert_kernel(blk_e_ref, n_used_ref, x_ref, wgu_ref, bgu_ref, wd_ref, bd_ref, o_ref, wgu_bf, wd_bf):
    i = pl.program_id(0)
    new_expert = jnp.logical_or(i == 0, blk_e_ref[i] != blk_e_ref[jnp.maximum(i - 1, 0)])

    @pl.when(new_expert)
    def _():
        wgu_bf[...] = wgu_ref[0].astype(BF16)
        wd_bf[...] = wd_ref[0].astype(BF16)

    @pl.when(i < n_used_ref[0])
    def _():
        d_e = wd_ref.shape[1]
        gu = jnp.dot(x_ref[...].astype(BF16), wgu_bf[...], preferred_element_type=F32) + bgu_ref[0]
        x_glu = jnp.minimum(gu[:, :d_e], SWIGLU_LIMIT)
        x_lin = jnp.clip(gu[:, d_e:], -SWIGLU_LIMIT, SWIGLU_LIMIT)
        act = x_glu * _sigmoid(SWIGLU_ALPHA * x_glu) * (x_lin + 1.0)
        o_ref[...] = jnp.dot(act.astype(BF16), wd_bf[...], preferred_element_type=F32) + bd_ref[0]


def _experts(xs, blk_e, n_used, wgu, bgu, wd, bd, layer):
    n_rows, d = xs.shape
    depth, n_e, _, n_gu = wgu.shape
    n_blk = n_rows // EXPERT_ROWS
    rows = lambda i, be, nu: (jnp.minimum(i, nu[0] - 1), 0)
    first = layer * n_e
    by_e = lambda i, be, nu: (first + be[i], 0, 0)
    n_e = depth * n_e
    wgu = wgu.reshape(n_e, d, n_gu)
    wd = wd.reshape(n_e, n_gu // 2, d)
    return pl.pallas_call(
        _expert_kernel,
        grid_spec=pltpu.PrefetchScalarGridSpec(
            num_scalar_prefetch=2,
            grid=(n_blk,),
            in_specs=[pl.BlockSpec((EXPERT_ROWS, d), rows),
                      pl.BlockSpec((1, d, n_gu), by_e), pl.BlockSpec((1, 1, n_gu), by_e),
                      pl.BlockSpec((1, n_gu // 2, d), by_e), pl.BlockSpec((1, 1, d), by_e)],
            out_specs=pl.BlockSpec((EXPERT_ROWS, d), rows),
            scratch_shapes=[pltpu.VMEM((d, n_gu), BF16), pltpu.VMEM((n_gu // 2, d), BF16)]),
        out_shape=jax.ShapeDtypeStruct((n_rows, d), F32),
        compiler_params=_params(1),
        name="moe_experts",
    )(blk_e, n_used, xs, wgu, bgu.reshape(n_e, 1, n_gu), wd, bd.reshape(n_e, 1, d))


def _combine_kernel(dest_ref, dest_next_ref, y_ref, h_ref, gate_ref, g2_ref, lng_ref, lnb_ref, o_ref,
                    buf, sem, *, alpha):
    i = pl.program_id(0)
    tm = h_ref.shape[0]

    def fetch(idx_ref, slot):
        def issue(g, c):
            for rr in range(ISSUE_UNROLL):
                r = g * ISSUE_UNROLL + rr
                for j in range(TOP_K):
                    d = idx_ref[0, 0, j * tm + r]
                    pltpu.make_async_copy(y_ref.at[pl.ds(d, 1)], buf.at[slot, j, pl.ds(r, 1)], sem.at[slot]).start()
            return c

        lax.fori_loop(0, tm // ISSUE_UNROLL, issue, 0)

    slot = i % 2

    @pl.when(i == 0)
    def _():
        fetch(dest_ref, 0)

    @pl.when(i + 1 < pl.num_programs(0))
    def _():
        fetch(dest_next_ref, 1 - slot)

    pltpu.make_async_copy(buf.at[slot], buf.at[slot], sem.at[slot]).wait()
    gate = gate_ref[...]
    y2 = gate[:, 0:1] * buf[slot, 0]
    for j in range(1, TOP_K):
        y2 = y2 + gate[:, j:j + 1] * buf[slot, j]
    o_ref[...] = _layer_norm(alpha * h_ref[...] + g2_ref[0] * y2, lng_ref[...], lnb_ref[...])


def _combine(y_rows, dest, h_flat, gate, g2, ln_g, ln_b, alpha, tokens_per_batch):
    n_tok, d = h_flat.shape
    tm = _pick(tokens_per_batch, (256, 128))
    per_b = tokens_per_batch // tm
    n_tiles = n_tok // tm
    dest3 = _tile_indices(dest, tm)
    return pl.pallas_call(
        functools.partial(_combine_kernel, alpha=alpha),
        grid=(n_tiles,),
        in_specs=[pl.BlockSpec((1, 1, tm * TOP_K), lambda i: (i, 0, 0), memory_space=pltpu.SMEM),
                  pl.BlockSpec((1, 1, tm * TOP_K), lambda i: (jnp.minimum(i + 1, n_tiles - 1), 0, 0),
                               memory_space=pltpu.SMEM),
                  pl.BlockSpec(memory_space=pl.ANY),
                  pl.BlockSpec((tm, d), lambda i: (i, 0)),
                  pl.BlockSpec((tm, LANES), lambda i: (i, 0)),
                  pl.BlockSpec((1, 1, d), lambda i: (i // per_b, 0, 0)),
                  pl.BlockSpec((1, d), lambda i: (0, 0)), pl.BlockSpec((1, d), lambda i: (0, 0))],
        out_specs=pl.BlockSpec((tm, d), lambda i: (i, 0)),
        out_shape=jax.ShapeDtypeStruct((n_tok, d), F32),
        scratch_shapes=[pltpu.VMEM((2, TOP_K, tm, d), F32), pltpu.SemaphoreType.DMA((2,))],
        compiler_params=_params(1),
        name="moe_combine_norm",
    )(dest3, dest3, y_rows, h_flat, gate, g2, ln_g[None], ln_b[None])


def _moe_and_norm(h1, top_e, gate, pos, counts, sc2, sh2, g2, wgu, bgu, wd, bd, layer, ln_g, ln_b, alpha):
    b, length, d = h1.shape
    n_tok = b * length
    n_blk = n_tok * TOP_K // EXPERT_ROWS + N_EXPERTS
    padded = (counts + EXPERT_ROWS - 1) // EXPERT_ROWS * EXPERT_ROWS
    pad_end = jnp.cumsum(padded)
    offset = pad_end - padded
    n_used = (pad_end[-1:] // EXPERT_ROWS).astype(I32)
    blk_start = jnp.arange(n_blk, dtype=I32) * EXPERT_ROWS
    blk_e = jnp.minimum(jnp.sum(pad_end[None, :] <= blk_start[:, None], axis=1), N_EXPERTS - 1).astype(I32)
    dest = pos
    for e in range(N_EXPERTS):
        dest = dest + jnp.where(top_e == e, offset[e], 0)
    dest = dest.astype(I32)
    zstart = jnp.maximum(pad_end - EXPERT_ROWS, 0).astype(I32)
    zvalid = (counts > 0).astype(I32)

    h_flat = h1.reshape(n_tok, d)
    xs = _dispatch(h_flat, sc2, sh2, dest, zstart, zvalid, n_blk * EXPERT_ROWS, length)
    y_rows = _experts(xs, blk_e, n_used, wgu, bgu, wd, bd, layer)
    out = _combine(y_rows, dest, h_flat, gate, g2, ln_g, ln_b, alpha, length)
    return out.reshape(b, length, d)


def _odd_in_kernel(x_ref, sc_ref, sh_ref, w_ref, glg_ref, glb_ref, ws_ref, bs_ref, flg_ref, flb_ref, dft_ref,
                   sp_ref, zr_ref, zi_ref):
    u = (x_ref[0] * (1.0 + sc_ref[0]) + sh_ref[0]).astype(BF16)
    z = jnp.dot(u, w_ref[...], preferred_element_type=F32)
    tm = z.shape[0]
    gw = N_GROUPS * GROUP_W
    ug = jax.nn.gelu(z[:, :gw])
    vn = _layer_norm(jax.nn.gelu(z[:, gw:2 * gw]), glg_ref[...], glb_ref[...]).astype(BF16)
    f = z[:, 2 * gw:]
    for g in range(N_GROUPS):
        cols = slice(g * GROUP_W, (g + 1) * GROUP_W)
        for c0 in range(0, tm, CHUNK):
            rows = slice(c0, c0 + CHUNK)
            sv = jnp.dot(ws_ref[g], vn[rows, cols], preferred_element_type=F32) + bs_ref[:, cols]
            sp_ref[0, rows, cols] = (ug[rows, cols] * sv).astype(sp_ref.dtype)
        fn = _layer_norm(f[:, cols], flg_ref[:, cols], flb_ref[:, cols]).astype(BF16)
        zz = jnp.dot(fn, dft_ref[...], preferred_element_type=F32)
        zr_ref[0, :, cols] = zz[:, :GROUP_W]
        zi_ref[0, :, cols] = zz[:, GROUP_W:]


def _odd_in_proj(x, sc, sh, w_bf, gln_g, gln_b, ws, bs, fln_g, fln_b):
    b, length, d = x.shape
    n = w_bf.shape[1]
    gw = N_GROUPS * GROUP_W
    tm = _pick(length, (512, 256, 128))
    kk = jnp.arange(GROUP_W, dtype=I32)
    ang = (2.0 * math.pi / GROUP_W) * ((kk[:, None] * kk[None, :]) % GROUP_W).astype(F32)
    dft = jnp.concatenate([jnp.cos(ang), -jnp.sin(ang)], 1).astype(BF16)
    bs_exp = jnp.repeat(bs.T, GROUP_W, axis=1)
    row = lambda bi, i: (bi, i, 0)
    vec = lambda bi, i: (bi, 0, 0)
    const2 = lambda bi, i: (0, 0)
    return pl.pallas_call(
        _odd_in_kernel,
        grid=(b, length // tm),
        in_specs=[pl.BlockSpec((1, tm, d), row), pl.BlockSpec((1, 1, d), vec), pl.BlockSpec((1, 1, d), vec),
                  pl.BlockSpec((d, n), const2),
                  pl.BlockSpec((1, gw), const2), pl.BlockSpec((1, gw), const2),
                  pl.BlockSpec((N_GROUPS, CHUNK, CHUNK), lambda bi, i: (0, 0, 0)),
                  pl.BlockSpec((CHUNK, gw), const2),
                  pl.BlockSpec((1, gw), const2), pl.BlockSpec((1, gw), const2),
                  pl.BlockSpec((GROUP_W, 2 * GROUP_W), const2)],
        out_specs=[pl.BlockSpec((1, tm, gw), row), pl.BlockSpec((1, tm, gw), row), pl.BlockSpec((1, tm, gw), row)],
        out_shape=[jax.ShapeDtypeStruct((b, length, gw), BF16), jax.ShapeDtypeStruct((b, length, gw), F32),
                   jax.ShapeDtypeStruct((b, length, gw), F32)],
        compiler_params=_params(2),
        name="odd_in_proj",
    )(x, sc, sh, w_bf, gln_g[None], gln_b[None], ws.astype(BF16), bs_exp, fln_g[None], fln_b[None], dft)


def _fft_a_kernel(zr_ref, zi_ref, cms_ref, c_ref, s_ref, ar_ref, ai_ref):
    l1, nb, w = zr_ref.shape[1:]
    zr = zr_ref[0].reshape(l1 * nb, w)
    zi = zi_ref[0].reshape(l1 * nb, w)
    k1 = jnp.dot(cms_ref[...], zr.astype(BF16), preferred_element_type=F32)
    k2 = jnp.dot(c_ref[...], (zi - zr).astype(BF16), preferred_element_type=F32)
    k3 = jnp.dot(s_ref[...], (zr + zi).astype(BF16), preferred_element_type=F32)
    ar_ref[0] = (k1 + k3).reshape(l1, nb, w)
    ai_ref[0] = (k1 + k2).reshape(l1, nb, w)


def _fft_b_kernel(ar_ref, ai_ref, c_ref, s_ref, mask_ref, o_ref, *, norm):
    nb, l2, w = ar_ref.shape[1:]
    ar = ar_ref[0].reshape(nb * l2, w).astype(BF16)
    ai = ai_ref[0].reshape(nb * l2, w).astype(BF16)
    mask = mask_ref[...]
    c = jnp.concatenate([c_ref[0]] * nb, axis=1) * mask
    s = jnp.concatenate([s_ref[0]] * nb, axis=1) * mask
    y = jnp.dot(c, ar, preferred_element_type=F32) + jnp.dot(s, ai, preferred_element_type=F32)
    o_ref[0] = (y * norm).reshape(l2, nb, w)


def _length_dft_real(zr, zi):
    b, length, w = zr.shape
    l2 = CHUNK
    l1 = length // l2
    nb = SUBLANES
    k1 = jnp.arange(l1, dtype=I32)
    ang_a = (2.0 * math.pi / l1) * ((k1[:, None] * k1[None, :]) % l1).astype(F32)
    r = lax.broadcasted_iota(I32, (l1 * nb, l1 * nb), 0)
    c = lax.broadcasted_iota(I32, (l1 * nb, l1 * nb), 1)
    same = r % nb == c % nb
    rep = (lax.broadcasted_iota(I32, (l1 * nb, l1), 0) // nb == lax.broadcasted_iota(I32, (l1 * nb, l1), 1))
    rep = rep.astype(BF16)

    def expand_a(t):
        t = jnp.dot(jnp.dot(rep, t.astype(BF16), preferred_element_type=F32).astype(BF16), rep.T,
                    preferred_element_type=F32)
        return jnp.where(same, t, 0.0).astype(BF16)

    cos_a = expand_a(jnp.cos(ang_a))
    sin_a = expand_a(jnp.sin(ang_a))
    cms_a = expand_a(jnp.cos(ang_a) - jnp.sin(ang_a))
    shape_t = (l1 // nb, l2 * nb, l2)
    kb = lax.broadcasted_iota(I32, shape_t, 0)
    r = lax.broadcasted_iota(I32, shape_t, 1)
    n2 = lax.broadcasted_iota(I32, shape_t, 2)
    ang_b = (2.0 * math.pi / length) * ((n2 * (kb * nb + r % nb + l1 * (r // nb))) % length).astype(F32)
    cos_b = jnp.cos(ang_b).astype(BF16)
    sin_b = jnp.sin(ang_b).astype(BF16)
    shape_m = (l2 * nb, nb * l2)
    mask_b = (lax.broadcasted_iota(I32, shape_m, 0) % nb == lax.broadcasted_iota(I32, shape_m, 1) // l2)
    mask_b = mask_b.astype(BF16)

    zr4 = zr.reshape(b, l1, l2, w)
    zi4 = zi.reshape(b, l1, l2, w)
    blk_a = pl.BlockSpec((1, l1, nb, w), lambda bi, i: (bi, 0, i, 0))
    ar, ai = pl.pallas_call(
        _fft_a_kernel,
        grid=(b, l2 // nb),
        in_specs=[blk_a, blk_a] + [pl.BlockSpec((l1 * nb, l1 * nb), lambda bi, i: (0, 0))] * 3,
        out_specs=[blk_a, blk_a],
        out_shape=[jax.ShapeDtypeStruct((b, l1, l2, w), F32)] * 2,
        compiler_params=_params(2),
        name="fourier_stage_a",
    )(zr4, zi4, cms_a, cos_a, sin_a)
    blk_b = pl.BlockSpec((1, nb, l2, w), lambda bi, i: (bi, i, 0, 0))
    out = pl.pallas_call(
        functools.partial(_fft_b_kernel, norm=float((length * GROUP_W) ** -0.5)),
        grid=(b, l1 // nb),
        in_specs=[blk_b, blk_b, pl.BlockSpec((1, l2 * nb, l2), lambda bi, i: (i, 0, 0)),
                  pl.BlockSpec((1, l2 * nb, l2), lambda bi, i: (i, 0, 0)),
                  pl.BlockSpec((l2 * nb, nb * l2), lambda bi, i: (0, 0))],
        out_specs=pl.BlockSpec((1, l2, nb, w), lambda bi, i: (bi, 0, i, 0)),
        out_shape=jax.ShapeDtypeStruct((b, l2, l1, w), F32),
        compiler_params=_params(2),
        name="fourier_stage_b",
    )(ar, ai, cos_b, sin_b, mask_b)
    return out.reshape(b, length, w)


def kernel(x, c, ctx, c_ctx, w_mod, b_mod, ln1_g, ln1_b, ln2_g, ln2_b, ev_w_in, ev_w_out, conv_w, conv_b, conv_ln_g, conv_ln_b, lam_q1, lam_k1, lam_q2, lam_k2, diff_norm_g, od_w_in, od_w_out, gmlp_ln_g, gmlp_ln_b, gmlp_ws, gmlp_bs, four_ln_g, four_ln_b, router_w, router_b, w_gate_up, b_gate_up, w_down, b_down):
    b, length, d = x.shape
    depth = w_mod.shape[0]
    alpha = float((2 * depth) ** 0.25)
    assert b + 1 <= SUBLANES and length % (CHUNK * SUBLANES) == 0 and length % GRID_W == 0
    assert length % KV_TILE == 0 and ctx.shape[1] % KV_TILE == 0

    mod = _modulation(c, c_ctx, w_mod, b_mod)
    cos_t, sin_t = _rope_tables(length)
    h = x
    for layer in range(depth):
        j = layer // 2
        m = mod[layer]
        sh1, sc1, g1, sh2, sc2, g2 = [m[:b, i * d:(i + 1) * d][:, None, :] for i in range(6)]
        if layer % 2 == 0:
            lam_init = 0.8 - 0.6 * math.exp(-0.3 * layer)
            w_in = ev_w_in[j].astype(BF16)
            q, k_all, vt, a = _even_in_proj(h, sc1, sh1, w_in, cos_t, sin_t, length + ctx.shape[1])
            csh1 = jnp.broadcast_to(m[b:b + 1, 0:d][:, None, :], (b, 1, d))
            csc1 = jnp.broadcast_to(m[b:b + 1, d:2 * d][:, None, :], (b, 1, d))
            k_all, vt = _context_kv(ctx, csc1, csh1, w_in[:, QK_W:2 * QK_W + V_W], k_all, vt, length)
            att = _diff_attention(q, k_all, vt, lam_q1[j], lam_k1[j], lam_q2[j], lam_k2[j],
                                  diff_norm_g[j], lam_init)
            conv = _conformer_conv(a, conv_w[j], conv_b[j], conv_ln_g[j], conv_ln_b[j])
            ya, yb, w_out = conv, att, ev_w_out[j]
        else:
            spatial, zr, zi = _odd_in_proj(h, sc1, sh1, od_w_in[j].astype(BF16), gmlp_ln_g[j], gmlp_ln_b[j],
                                           gmlp_ws[j], gmlp_bs[j], four_ln_g[j], four_ln_b[j])
            ya, yb, w_out = spatial, _length_dft_real(zr, zi), od_w_out[j]
        h1, top_e, gate, pos, counts = _post_mix(ya, yb, w_out.astype(BF16), h, g1, ln1_g[layer], ln1_b[layer],
                                                 sc2, sh2, router_w[layer], router_b[layer], alpha)
        h = _moe_and_norm(h1, top_e[:TOP_K], gate, pos[:TOP_K], counts[0, :N_EXPERTS], sc2, sh2, g2,
                          w_gate_up, b_gate_up, w_down, b_down, layer, ln2_g[layer], ln2_b[layer], alpha)
    return h
```

```python
import functools
import math

import jax
import jax.numpy as jnp
from jax import lax
from jax.experimental import pallas as pl
from jax.experimental.pallas import tpu as pltpu

F32 = jnp.float32
BF16 = jnp.bfloat16
I32 = jnp.int32
HIGHEST = lax.Precision.HIGHEST

LN_EPS = 1e-5
GRID_W = 64
HEAD_DIM = 64
HEAD_V = 128
N_HEADS = 4
QK_W = N_HEADS * 2 * HEAD_DIM
V_W = N_HEADS * HEAD_V
CONV_W = 512
CONV_TAPS = 31
CONV_HALO = 16
ROPE_BASE = 10000.0
CHUNK = 128
N_GROUPS = 4
GROUP_W = 128
N_EXPERTS = 32
TOP_K = 4
SWIGLU_LIMIT = 7.0
SWIGLU_ALPHA = 1.702
LANES = 128
SUBLANES = 8
EXPERT_ROWS = 512
ISSUE_UNROLL = 8
NEG_BIG = -1e30
VMEM_LIMIT = 56 * 1024 * 1024


def _params(n_axes):
    return pltpu.CompilerParams(dimension_semantics=("arbitrary",) * n_axes,
                                vmem_limit_bytes=VMEM_LIMIT)


def _pick(n, candidates):
    for c in candidates:
        if n % c == 0:
            return c
    return n


def _layer_norm(r, g, b):
    mu = jnp.mean(r, axis=-1, keepdims=True)
    d = r - mu
    var = jnp.mean(d * d, axis=-1, keepdims=True)
    return d * lax.rsqrt(var + LN_EPS) * g + b


def _sigmoid(x):
    return 1.0 / (1.0 + jnp.exp(-x))


def _mod_kernel(cs_ref, w_ref, b_ref, o_ref):
    cs = cs_ref[...]
    a = cs * _sigmoid(cs)
    o_ref[0] = jnp.dot(a, w_ref[0], precision=HIGHEST, preferred_element_type=F32) + b_ref[0]


def _modulation(c, c_ctx, w_mod, b_mod):
    depth, d, n = w_mod.shape
    b = c.shape[0]
    cs = jnp.concatenate([c, c_ctx[None], jnp.zeros((SUBLANES - b - 1, d), F32)], 0)
    tn = _pick(n, (1536, 1024, 512))
    return pl.pallas_call(
        _mod_kernel,
        grid=(depth, n // tn),
        in_specs=[pl.BlockSpec((SUBLANES, d), lambda l, j: (0, 0)),
                  pl.BlockSpec((1, d, tn), lambda l, j: (l, 0, j)),
                  pl.BlockSpec((1, 1, tn), lambda l, j: (l, 0, j))],
        out_specs=pl.BlockSpec((1, SUBLANES, tn), lambda l, j: (l, 0, j)),
        out_shape=jax.ShapeDtypeStruct((depth, SUBLANES, n), F32),
        compiler_params=_params(2),
        name="modulation",
    )(cs, w_mod, b_mod.reshape(depth, 1, n))


KV_TILE = 256
VT_ROWS = HEAD_V + 16


def _store_values_transposed(v, vt_ref):
    extra = (lax.broadcasted_iota(I32, (VT_ROWS - HEAD_V, v.shape[0]), 0) == 0).astype(BF16)
    for h in range(N_HEADS):
        vt_ref[0, h, 0, :HEAD_V, :] = v[:, h * HEAD_V:(h + 1) * HEAD_V].T.astype(BF16)
        vt_ref[0, h, 0, HEAD_V:, :] = extra


def _even_in_kernel(x_ref, sc_ref, sh_ref, w_ref, cos_ref, sin_ref, q_ref, k_ref, vt_ref, a_ref):
    u = (x_ref[0] * (1.0 + sc_ref[0]) + sh_ref[0]).astype(BF16)
    z = jnp.dot(u, w_ref[...], preferred_element_type=F32)
    cosv = cos_ref[...]
    sinv = sin_ref[...]
    lane = lax.broadcasted_iota(I32, cosv.shape, 1)
    first = (lane % 32) < 16

    def rope(t):
        partner = jnp.where(first, pltpu.roll(t, LANES - 16, 1), pltpu.roll(t, 16, 1))
        return t * cosv + partner * sinv

    for j in range(QK_W // LANES):
        sl = slice(j * LANES, (j + 1) * LANES)
        rq = rope(z[:, sl]) * (HEAD_DIM ** -0.5 * math.log2(math.e))
        for c in range(2):
            q_ref[c, 0, :, sl] = jnp.where(lane // HEAD_DIM == c, rq, 0.0).astype(BF16)
        k_ref[0, :, sl] = rope(z[:, QK_W + j * LANES:QK_W + (j + 1) * LANES]).astype(BF16)
    _store_values_transposed(z[:, 2 * QK_W:2 * QK_W + V_W], vt_ref)
    a0 = 2 * QK_W + V_W
    a_ref[0] = z[:, a0:a0 + CONV_W] * _sigmoid(z[:, a0 + CONV_W:a0 + 2 * CONV_W])


def _rope_tables(length):
    rows = length // GRID_W
    row = jnp.repeat(jnp.arange(rows, dtype=F32), GRID_W)
    col = jnp.tile(jnp.arange(GRID_W, dtype=F32), rows)
    n_freq = HEAD_DIM // 4
    inv_freq = ROPE_BASE ** (-jnp.arange(n_freq, dtype=F32) / n_freq)
    ar = row[:, None] * inv_freq
    ac = col[:, None] * inv_freq
    cos64 = jnp.concatenate([jnp.cos(ar), jnp.cos(ar), jnp.cos(ac), jnp.cos(ac)], 1)
    sin64 = jnp.concatenate([-jnp.sin(ar), jnp.sin(ar), -jnp.sin(ac), jnp.sin(ac)], 1)
    return jnp.tile(cos64, (1, LANES // HEAD_DIM)), jnp.tile(sin64, (1, LANES // HEAD_DIM))


def _key_chunk(n_keys):
    return next(c for c in (1280, 256) if n_keys % c == 0 and n_keys // c >= 3)


def _kv_specs(b, n_keys, kc, first_tile):
    per_chunk = kc // KV_TILE
    k_spec = pl.BlockSpec((1, KV_TILE, QK_W), lambda bi, i: (bi, first_tile + i, 0))
    vt_spec = pl.BlockSpec((1, N_HEADS, 1, VT_ROWS, KV_TILE),
                           lambda bi, i: (bi, 0, (first_tile + i) // per_chunk, 0, (first_tile + i) % per_chunk))
    shapes = [jax.ShapeDtypeStruct((b, n_keys, QK_W), BF16),
              jax.ShapeDtypeStruct((b, N_HEADS, n_keys // kc, VT_ROWS, kc), BF16)]
    return k_spec, vt_spec, shapes


def _even_in_proj(x, sc, sh, w_bf, cos_t, sin_t, n_keys):
    b, length, d = x.shape
    n = w_bf.shape[1]
    tm = KV_TILE
    row = lambda bi, i: (bi, i, 0)
    vec = lambda bi, i: (bi, 0, 0)
    k_spec, vt_spec, kv_shapes = _kv_specs(b, n_keys, _key_chunk(n_keys), 0)
    return pl.pallas_call(
        _even_in_kernel,
        grid=(b, length // tm),
        in_specs=[pl.BlockSpec((1, tm, d), row),
                  pl.BlockSpec((1, 1, d), vec),
                  pl.BlockSpec((1, 1, d), vec),
                  pl.BlockSpec((d, n), lambda bi, i: (0, 0)),
                  pl.BlockSpec((tm, LANES), lambda bi, i: (i, 0)),
                  pl.BlockSpec((tm, LANES), lambda bi, i: (i, 0))],
        out_specs=[pl.BlockSpec((2, 1, tm, QK_W), lambda bi, i: (0, bi, i, 0)), k_spec, vt_spec,
                   pl.BlockSpec((1, tm, CONV_W), row)],
        out_shape=[jax.ShapeDtypeStruct((2, b, length, QK_W), BF16)] + kv_shapes
                  + [jax.ShapeDtypeStruct((b, length, CONV_W), F32)],
        compiler_params=_params(2),
        name="even_in_proj",
    )(x, sc, sh, w_bf, cos_t, sin_t)


def _ctx_kv_kernel(x_ref, sc_ref, sh_ref, w_ref, k_in, vt_in, k_ref, vt_ref):
    del k_in, vt_in
    u = (x_ref[0] * (1.0 + sc_ref[0]) + sh_ref[0]).astype(BF16)
    z = jnp.dot(u, w_ref[...], preferred_element_type=F32)
    k_ref[0] = z[:, :QK_W].astype(BF16)
    _store_values_transposed(z[:, QK_W:], vt_ref)


def _context_kv(ctx, sc, sh, w_bf, k_all, vt, first_key):
    b, rows, d = ctx.shape
    n = w_bf.shape[1]
    n_keys = k_all.shape[1]
    k_spec, vt_spec, kv_shapes = _kv_specs(b, n_keys, vt.shape[4], first_key // KV_TILE)
    return pl.pallas_call(
        _ctx_kv_kernel,
        grid=(b, rows // KV_TILE),
        in_specs=[pl.BlockSpec((1, KV_TILE, d), lambda bi, i: (bi, i, 0)),
                  pl.BlockSpec((1, 1, d), lambda bi, i: (bi, 0, 0)),
                  pl.BlockSpec((1, 1, d), lambda bi, i: (bi, 0, 0)),
                  pl.BlockSpec((d, n), lambda bi, i: (0, 0)),
                  pl.BlockSpec(memory_space=pl.ANY), pl.BlockSpec(memory_space=pl.ANY)],
        out_specs=[k_spec, vt_spec],
        out_shape=kv_shapes,
        input_output_aliases={4: 0, 5: 1},
        compiler_params=_params(2),
        name="context_kv_proj",
    )(ctx, sc, sh, w_bf, k_all, vt)


def _attn_kernel(lq1_ref, lk1_ref, lq2_ref, lk2_ref, g_ref, q_ref, k_ref, vt_ref, o_ref,
                 s0_sc, s1_sc, p0_sc, p1_sc, acc_sc, *, lam_init, kc):
    mq = q_ref.shape[2]
    n = k_ref.shape[1] // kc
    s_sc = (s0_sc, s1_sc)
    p_sc = (p0_sc, p1_sc)

    def scores(j, slot):
        kj = k_ref[0, pl.ds(pl.multiple_of(j * kc, kc), kc), :]
        for c in range(2):
            s_sc[slot][c] = lax.dot_general(kj, q_ref[c, 0], (((1,), (1,)), ((), ())),
                                            preferred_element_type=F32)

    def softmax(slot, m):
        m_out, alpha = [], []
        for c in range(2):
            s = s_sc[slot][c]
            m_new = jnp.maximum(m[c], jnp.max(s, axis=0, keepdims=True))
            p_sc[slot][c] = jnp.exp2((s - m_new).astype(BF16))
            m_out.append(m_new)
            alpha.append(jnp.exp2(m[c] - m_new))
        return tuple(m_out), tuple(alpha)

    def accumulate(j, slot, alpha):
        vt = vt_ref[0, 0, j]
        for c in range(2):
            acc_sc[c] = alpha[c] * acc_sc[c] + jnp.dot(vt, p_sc[slot][c], preferred_element_type=F32)

    def step(j, slot, m, alpha):
        scores(j + 2, slot)
        m, alpha_next = softmax(1 - slot, m)
        accumulate(j, slot, alpha)
        return m, alpha_next

    acc_sc[...] = jnp.zeros(acc_sc.shape, F32)
    m = (jnp.full((1, mq), NEG_BIG, F32),) * 2
    scores(0, 0)
    m, alpha = softmax(0, m)
    scores(1, 1)

    def pair(t, carry):
        m, alpha = step(2 * t, 0, *carry)
        return step(2 * t + 1, 1, m, alpha)

    n_steps = n - 2
    m, alpha = lax.fori_loop(0, n_steps // 2, pair, (m, alpha))
    if n_steps % 2:
        m, alpha = step(n_steps - 1, 0, m, alpha)
    m, alpha_last = softmax((n - 1) % 2, m)
    accumulate(n - 2, (n - 2) % 2, alpha)
    accumulate(n - 1, (n - 1) % 2, alpha_last)
    acc0 = acc_sc[0]
    acc1 = acc_sc[1]
    lam = (jnp.exp(jnp.sum(lq1_ref[...] * lk1_ref[...], keepdims=True))
           - jnp.exp(jnp.sum(lq2_ref[...] * lk2_ref[...], keepdims=True)) + lam_init)
    o = (acc0[:HEAD_V] / acc0[HEAD_V:HEAD_V + 1]
         - lam * (acc1[:HEAD_V] / acc1[HEAD_V:HEAD_V + 1]))
    ms = jnp.mean(o * o, axis=0, keepdims=True)
    o = o * lax.rsqrt(ms + LN_EPS) * g_ref[...] * (1.0 - lam_init)
    o_ref[0] = o.T.astype(o_ref.dtype)


def _diff_attention(q, k_all, vt, lq1, lk1, lq2, lk2, norm_g, lam_init):
    _, b, length, _ = q.shape
    n_keys = k_all.shape[1]
    mq = _pick(length, (1024, 512, 256, 128))
    nc, kc = vt.shape[2], vt.shape[4]
    small = lambda bi, h, i: (0, 0)
    return pl.pallas_call(
        functools.partial(_attn_kernel, lam_init=lam_init, kc=kc),
        grid=(b, N_HEADS, length // mq),
        in_specs=[pl.BlockSpec((1, HEAD_DIM), small), pl.BlockSpec((1, HEAD_DIM), small),
                  pl.BlockSpec((1, HEAD_DIM), small), pl.BlockSpec((1, HEAD_DIM), small),
                  pl.BlockSpec((HEAD_V, 1), small),
                  pl.BlockSpec((2, 1, mq, HEAD_V), lambda bi, h, i: (0, bi, i, h)),
                  pl.BlockSpec((1, n_keys, HEAD_V), lambda bi, h, i: (bi, 0, h)),
                  pl.BlockSpec((1, 1, nc, VT_ROWS, kc), lambda bi, h, i: (bi, h, 0, 0, 0))],
        out_specs=pl.BlockSpec((1, mq, HEAD_V), lambda bi, h, i: (bi, i, h)),
        out_shape=jax.ShapeDtypeStruct((b, length, V_W), BF16),
        scratch_shapes=[pltpu.VMEM((2, kc, mq), F32), pltpu.VMEM((2, kc, mq), F32),
                        pltpu.VMEM((2, kc, mq), BF16), pltpu.VMEM((2, kc, mq), BF16),
                        pltpu.VMEM((2, VT_ROWS, mq), F32)],
        compiler_params=_params(3),
        name="diff_attention",
    )(lq1[None], lk1[None], lq2[None], lk2[None], norm_g[:, None], q, k_all, vt)


def _conv_kernel(prev_ref, cur_ref, next_ref, w_ref, b_ref, g_ref, beta_ref, o_ref, win, *, row_chunk):
    i = pl.program_id(1)
    tc = cur_ref.shape[1]
    span = tc + 2 * CONV_HALO - SUBLANES
    win[0, CONV_HALO:CONV_HALO + tc, :] = cur_ref[0]
    win[0, 0:CONV_HALO, :] = jnp.where(i > 0, prev_ref[0], 0.0)
    win[0, CONV_HALO + tc:, :] = jnp.where(i < pl.num_programs(1) - 1, next_ref[0], 0.0)
    for s in range(1, SUBLANES):
        win[s, 0:span, :] = win[0, s:s + span, :]
    first_tap = CONV_HALO - CONV_TAPS // 2
    for c0 in range(0, tc, row_chunk):
        acc = jnp.zeros((row_chunk, CONV_W), F32)
        for t in range(CONV_TAPS):
            s = (first_tap + t) % SUBLANES
            r0 = c0 + first_tap + t - s
            acc = acc + w_ref[t:t + 1, :] * win[s, r0:r0 + row_chunk, :]
        y = _layer_norm(acc + b_ref[...], g_ref[...], beta_ref[...])
        o_ref[0, c0:c0 + row_chunk, :] = (y * _sigmoid(y)).astype(o_ref.dtype)


def _conformer_conv(a, conv_w, conv_b, ln_g, ln_b):
    b, length, w = a.shape
    tc = _pick(length, (512, 256, 128))
    hb = tc // CONV_HALO
    n_halo = length // CONV_HALO
    w_pad = jnp.concatenate([conv_w, jnp.zeros((1, w), F32)], 0)
    vec = lambda bi, i: (0, 0)
    return pl.pallas_call(
        functools.partial(_conv_kernel, row_chunk=64),
        grid=(b, length // tc),
        in_specs=[pl.BlockSpec((1, CONV_HALO, w), lambda bi, i: (bi, jnp.maximum(i * hb - 1, 0), 0)),
                  pl.BlockSpec((1, tc, w), lambda bi, i: (bi, i, 0)),
                  pl.BlockSpec((1, CONV_HALO, w), lambda bi, i: (bi, jnp.minimum((i + 1) * hb, n_halo - 1), 0)),
                  pl.BlockSpec((CONV_TAPS + 1, w), vec),
                  pl.BlockSpec((1, w), vec), pl.BlockSpec((1, w), vec), pl.BlockSpec((1, w), vec)],
        out_specs=pl.BlockSpec((1, tc, w), lambda bi, i: (bi, i, 0)),
        out_shape=jax.ShapeDtypeStruct((b, length, w), BF16),
        scratch_shapes=[pltpu.VMEM((SUBLANES, tc + 2 * CONV_HALO, w), F32)],
        compiler_params=_params(2),
        name="conformer_conv",
    )(a, a, a, w_pad, conv_b[None], ln_g[None], ln_b[None])


def _post_mix_kernel(ya_ref, yb_ref, wa_ref, wb_ref, x_ref, g1_ref, lng_ref, lnb_ref, sc_ref, sh_ref,
                     rw_ref, rb_ref, h_ref, e_ref, gate_ref, pos_ref, cnt_ref, carry, *, alpha):
    first_step = jnp.logical_and(pl.program_id(0) == 0, pl.program_id(1) == 0)

    @pl.when(first_step)
    def _():
        carry[...] = jnp.zeros(carry.shape, F32)

    y = (jnp.dot(ya_ref[0].astype(BF16), wa_ref[...], preferred_element_type=F32)
         + jnp.dot(yb_ref[0].astype(BF16), wb_ref[...], preferred_element_type=F32))
    h = _layer_norm(alpha * x_ref[0] + g1_ref[0] * y, lng_ref[...], lnb_ref[...])
    h_ref[0] = h
    u2 = h * (1.0 + sc_ref[0]) + sh_ref[0]
    u_hi = u2.astype(BF16)
    u_lo = (u2 - u_hi.astype(F32)).astype(BF16)
    hh_hl = jnp.dot(u_hi, rw_ref[...], preferred_element_type=F32)
    lh = jnp.dot(u_lo, rw_ref[:, :LANES], preferred_element_type=F32)
    logits = hh_hl[:, :LANES] + hh_hl[:, LANES:] + lh + rb_ref[...]

    tm = logits.shape[0]
    lane = lax.broadcasted_iota(I32, logits.shape, 1).astype(F32)
    vals, idxs = [], []
    lg = logits
    for _ in range(TOP_K):
        mx = jnp.max(lg, axis=1, keepdims=True)
        idx = jnp.min(jnp.where(lg == mx, lane, float(LANES)), axis=1, keepdims=True)
        vals.append(mx)
        idxs.append(idx)
        lg = jnp.where(lane == idx, -3e38, lg)
    exps = [jnp.exp(v - vals[0]) for v in vals]
    den = exps[0] + exps[1] + exps[2] + exps[3]

    hot = jnp.zeros(logits.shape, F32)
    for idx in idxs:
        hot = hot + (lane == idx).astype(F32)
    r_i = lax.broadcasted_iota(I32, (tm, tm), 0)
    c_i = lax.broadcasted_iota(I32, (tm, tm), 1)
    tri = (c_i < r_i).astype(BF16)
    before = jnp.dot(tri, hot.astype(BF16), preferred_element_type=F32) + carry[...]
    e_out = jnp.zeros(logits.shape, F32)
    g_out = jnp.zeros(logits.shape, F32)
    p_out = jnp.zeros(logits.shape, F32)
    for j in range(TOP_K):
        pos_j = jnp.sum(jnp.where(lane == idxs[j], before, 0.0), axis=1, keepdims=True)
        sel = lane == float(j)
        e_out = jnp.where(sel, idxs[j], e_out)
        g_out = jnp.where(sel, exps[j] / den, g_out)
        p_out = jnp.where(sel, pos_j, p_out)
    e_ref[...] = e_out.T[:SUBLANES].astype(I32)
    gate_ref[...] = g_out
    pos_ref[...] = p_out.T[:SUBLANES].astype(I32)
    new_carry = carry[...] + jnp.sum(hot, axis=0, keepdims=True)
    carry[...] = new_carry
    cnt_ref[...] = jnp.broadcast_to(new_carry, cnt_ref.shape).astype(I32)


def _post_mix(ya, yb, w_out_bf, x, g1, ln_g, ln_b, sc2, sh2, router_w, router_b, alpha):
    b, length, d = x.shape
    wa = ya.shape[2]
    tm = _pick(length, (512, 256, 128))
    n_tok = b * length
    rw = jnp.concatenate([router_w, jnp.zeros((d, LANES - N_EXPERTS), F32)], 1)
    rw_hi = rw.astype(BF16)
    rw = jnp.concatenate([rw_hi, (rw - rw_hi.astype(F32)).astype(BF16)], 1)
    rb =jnp.concatenate([router_b, jnp.full((LANES - N_EXPERTS,), NEG_BIG, F32)])[None]
    row = lambda bi, i: (bi, i, 0)
    vec = lambda bi, i: (bi, 0, 0)
    const = lambda bi, i: (0, 0)
    nt = length // tm
    tok = lambda bi, i: (bi * nt + i, 0)
    return pl.pallas_call(
        functools.partial(_post_mix_kernel, alpha=alpha),
        grid=(b, nt),
        in_specs=[pl.BlockSpec((1, tm, wa), row), pl.BlockSpec((1, tm, d - wa), row),
                  pl.BlockSpec((wa, d), const), pl.BlockSpec((d - wa, d), const),
                  pl.BlockSpec((1, tm, d), row), pl.BlockSpec((1, 1, d), vec),
                  pl.BlockSpec((1, d), const), pl.BlockSpec((1, d), const),
                  pl.BlockSpec((1, 1, d), vec), pl.BlockSpec((1, 1, d), vec),
                  pl.BlockSpec((d, 2 * LANES), const), pl.BlockSpec((1, LANES), const)],
        out_specs=[pl.BlockSpec((1, tm, d), row), pl.BlockSpec((SUBLANES, tm), lambda bi, i: (0, bi * nt + i)),
                   pl.BlockSpec((tm, LANES), tok), pl.BlockSpec((SUBLANES, tm), lambda bi, i: (0, bi * nt + i)),
                   pl.BlockSpec((SUBLANES, LANES), const)],
        out_shape=[jax.ShapeDtypeStruct((b, length, d), F32), jax.ShapeDtypeStruct((SUBLANES, n_tok), I32),
                   jax.ShapeDtypeStruct((n_tok, LANES), F32), jax.ShapeDtypeStruct((SUBLANES, n_tok), I32),
                   jax.ShapeDtypeStruct((SUBLANES, LANES), I32)],
        scratch_shapes=[pltpu.VMEM((1, LANES), F32)],
        compiler_params=_params(2),
        name="out_proj_norm_router",
    )(ya, yb, w_out_bf[:wa], w_out_bf[wa:], x, g1, ln_g[None], ln_b[None], sc2, sh2, rw, rb)


def _tile_indices(dest, tm):
    n_tok = dest.shape[1]
    return dest.reshape(TOP_K, n_tok // tm, tm).transpose(1, 0, 2).reshape(n_tok // tm, 1, TOP_K * tm)


def _dispatch_kernel(zstart_ref, zvalid_ref, dest_ref, h_ref, sc_ref, sh_ref, xs_ref, u_sc, zero_sc, sem, zsem):
    tm = h_ref.shape[0]

    @pl.when(pl.program_id(0) == 0)
    def _():
        zero_sc[...] = jnp.zeros(zero_sc.shape, F32)

        def fill(e, c):
            @pl.when(zvalid_ref[e] > 0)
            def _():
                start = pl.multiple_of(zstart_ref[e], SUBLANES)
                pltpu.make_async_copy(zero_sc, xs_ref.at[pl.ds(start, EXPERT_ROWS)], zsem).start()
            return c

        lax.fori_loop(0, N_EXPERTS, fill, 0)

        def drain(e, c):
            @pl.when(zvalid_ref[e] > 0)
            def _():
                pltpu.make_async_copy(zero_sc, xs_ref.at[pl.ds(0, EXPERT_ROWS)], zsem).wait()
            return c

        lax.fori_loop(0, N_EXPERTS, drain, 0)

    u_sc[...] = h_ref[...] * (1.0 + sc_ref[0]) + sh_ref[0]

    def issue(g, c):
        for rr in range(ISSUE_UNROLL):
            r = g * ISSUE_UNROLL + rr
            for j in range(TOP_K):
                d = dest_ref[0, 0, j * tm + r]
                pltpu.make_async_copy(u_sc.at[pl.ds(r, 1)], xs_ref.at[pl.ds(d, 1)], sem).start()
        return c

    lax.fori_loop(0, tm // ISSUE_UNROLL, issue, 0)
    rows = xs_ref.at[pl.ds(0, tm * TOP_K)]
    pltpu.make_async_copy(rows, rows, sem).wait()


def _dispatch(h_flat, sc2, sh2, dest, zstart, zvalid, n_rows, tokens_per_batch):
    n_tok, d = h_flat.shape
    tm = _pick(tokens_per_batch, (512, 256, 128))
    per_b = tokens_per_batch // tm
    return pl.pallas_call(
        _dispatch_kernel,
        grid_spec=pltpu.PrefetchScalarGridSpec(
            num_scalar_prefetch=2,
            grid=(n_tok // tm,),
            in_specs=[pl.BlockSpec((1, 1, tm * TOP_K), lambda i, zs, zv: (i, 0, 0), memory_space=pltpu.SMEM),
                      pl.BlockSpec((tm, d), lambda i, zs, zv: (i, 0)),
                      pl.BlockSpec((1, 1, d), lambda i, zs, zv: (i // per_b, 0, 0)),
                      pl.BlockSpec((1, 1, d), lambda i, zs, zv: (i // per_b, 0, 0))],
            out_specs=pl.BlockSpec(memory_space=pl.ANY),
            scratch_shapes=[pltpu.VMEM((tm, d), F32), pltpu.VMEM((EXPERT_ROWS, d), F32),
                            pltpu.SemaphoreType.DMA(()), pltpu.SemaphoreType.DMA(())]),
        out_shape=jax.ShapeDtypeStruct((n_rows, d), F32),
        compiler_params=_params(1),
        name="moe_dispatch",
    )(zstart, zvalid, _tile_indices(dest, tm), h_flat, sc2, sh2)


def _expert_kernel(blk_e_ref, n_used_ref, x_ref, wgu_ref, bgu_ref, wd_ref, bd_ref, o_ref, wgu_bf, wd_bf):
    i = pl.program_id(0)
    new_expert = jnp.logical_or(i == 0, blk_e_ref[i] != blk_e_ref[jnp.maximum(i - 1, 0)])

    @pl.when(new_expert)
    def _():
        wgu_bf[...] = wgu_ref[0].astype(BF16)
        wd_bf[...] = wd_ref[0].astype(BF16)

    @pl.when(i < n_used_ref[0])
    def _():
        d_e = wd_ref.shape[1]
        gu = jnp.dot(x_ref[...].astype(BF16), wgu_bf[...], preferred_element_type=F32) + bgu_ref[0]
        x_glu = jnp.minimum(gu[:, :d_e], SWIGLU_LIMIT)
        x_lin = jnp.clip(gu[:, d_e:], -SWIGLU_LIMIT, SWIGLU_LIMIT)
        act = x_glu * _sigmoid(SWIGLU_ALPHA * x_glu) * (x_lin + 1.0)
        o_ref[...] = jnp.dot(act.astype(BF16), wd_bf[...], preferred_element_type=F32) + bd_ref[0]


def _experts(xs, blk_e, n_used, wgu, bgu, wd, bd, layer):
    n_rows, d = xs.shape
    depth, n_e, _, n_gu = wgu.shape
    n_blk = n_rows // EXPERT_ROWS
    rows = lambda i, be, nu: (jnp.minimum(i, nu[0] - 1), 0)
    first = layer * n_e
    by_e = lambda i, be, nu: (first + be[i], 0, 0)
    n_e = depth * n_e
    wgu = wgu.reshape(n_e, d, n_gu)
    wd = wd.reshape(n_e, n_gu // 2, d)
    return pl.pallas_call(
        _expert_kernel,
        grid_spec=pltpu.PrefetchScalarGridSpec(
            num_scalar_prefetch=2,
            grid=(n_blk,),
            in_specs=[pl.BlockSpec((EXPERT_ROWS, d), rows),
                      pl.BlockSpec((1, d, n_gu), by_e), pl.BlockSpec((1, 1, n_gu), by_e),
                      pl.BlockSpec((1, n_gu // 2, d), by_e), pl.BlockSpec((1, 1, d), by_e)],
            out_specs=pl.BlockSpec((EXPERT_ROWS, d), rows),
            scratch_shapes=[pltpu.VMEM((d, n_gu), BF16), pltpu.VMEM((n_gu // 2, d), BF16)]),
        out_shape=jax.ShapeDtypeStruct((n_rows, d), F32),
        compiler_params=_params(1),
        name="moe_experts",
    )(blk_e, n_used, xs, wgu, bgu.reshape(n_e, 1, n_gu), wd, bd.reshape(n_e, 1, d))


def _combine_kernel(dest_ref, dest_next_ref, y_ref, h_ref, gate_ref, g2_ref, lng_ref, lnb_ref, o_ref,
                    buf, sem, *, alpha):
    i = pl.program_id(0)
    tm = h_ref.shape[0]

    def fetch(idx_ref, slot):
        def issue(g, c):
            for rr in range(ISSUE_UNROLL):
                r = g * ISSUE_UNROLL + rr
                for j in range(TOP_K):
                    d = idx_ref[0, 0, j * tm + r]
                    pltpu.make_async_copy(y_ref.at[pl.ds(d, 1)], buf.at[slot, j, pl.ds(r, 1)], sem.at[slot]).start()
            return c

        lax.fori_loop(0, tm // ISSUE_UNROLL, issue, 0)

    def fetch_unrolled(idx_ref, slot):
        for r in range(tm):
            for j in range(TOP_K):
                d = idx_ref[0, 0, j * tm + r]
                pltpu.make_async_copy(y_ref.at[pl.ds(d, 1)], buf.at[slot, j, pl.ds(r, 1)], sem.at[slot]).start()

    slot = i % 2
    more = i + 1 < pl.num_programs(0)

    @pl.when(i == 0)
    def _():
        fetch(dest_ref, 0)

    pl.when(jnp.logical_and(more, slot == 0))(lambda: fetch_unrolled(dest_next_ref, 1))
    pl.when(jnp.logical_and(more, slot == 1))(lambda: fetch_unrolled(dest_next_ref, 0))

    pltpu.make_async_copy(buf.at[slot], buf.at[slot], sem.at[slot]).wait()
    gate = gate_ref[...]
    y2 = gate[:, 0:1] * buf[slot, 0]
    for j in range(1, TOP_K):
        y2 = y2 + gate[:, j:j + 1] * buf[slot, j]
    o_ref[...] = _layer_norm(alpha * h_ref[...] + g2_ref[0] * y2, lng_ref[...], lnb_ref[...])


def _combine(y_rows, dest, h_flat, gate, g2, ln_g, ln_b, alpha, tokens_per_batch):
    n_tok, d = h_flat.shape
    tm = _pick(tokens_per_batch, (256, 128))
    per_b = tokens_per_batch // tm
    n_tiles = n_tok // tm
    dest3 = _tile_indices(dest, tm)
    return pl.pallas_call(
        functools.partial(_combine_kernel, alpha=alpha),
        grid=(n_tiles,),
        in_specs=[pl.BlockSpec((1, 1, tm * TOP_K), lambda i: (i, 0, 0), memory_space=pltpu.SMEM),
                  pl.BlockSpec((1, 1, tm * TOP_K), lambda i: (jnp.minimum(i + 1, n_tiles - 1), 0, 0),
                               memory_space=pltpu.SMEM),
                  pl.BlockSpec(memory_space=pl.ANY),
                  pl.BlockSpec((tm, d), lambda i: (i, 0)),
                  pl.BlockSpec((tm, LANES), lambda i: (i, 0)),
                  pl.BlockSpec((1, 1, d), lambda i: (i // per_b, 0, 0)),
                  pl.BlockSpec((1, d), lambda i: (0, 0)), pl.BlockSpec((1, d), lambda i: (0, 0))],
        out_specs=pl.BlockSpec((tm, d), lambda i: (i, 0)),
        out_shape=jax.ShapeDtypeStruct((n_tok, d), F32),
        scratch_shapes=[pltpu.VMEM((2, TOP_K, tm, d), F32), pltpu.SemaphoreType.DMA((2,))],
        compiler_params=_params(1),
        name="moe_combine_norm",
    )(dest3, dest3, y_rows, h_flat, gate, g2, ln_g[None], ln_b[None])


def _moe_and_norm(h1, top_e, gate, pos, counts, sc2, sh2, g2, wgu, bgu, wd, bd, layer, ln_g, ln_b, alpha):
    b, length, d = h1.shape
    n_tok = b * length
    n_blk = n_tok * TOP_K // EXPERT_ROWS + N_EXPERTS
    padded = (counts + EXPERT_ROWS - 1) // EXPERT_ROWS * EXPERT_ROWS
    pad_end = jnp.cumsum(padded)
    offset = pad_end - padded
    n_used = (pad_end[-1:] // EXPERT_ROWS).astype(I32)
    blk_start = jnp.arange(n_blk, dtype=I32) * EXPERT_ROWS
    blk_e = jnp.minimum(jnp.sum(pad_end[None, :] <= blk_start[:, None], axis=1), N_EXPERTS - 1).astype(I32)
    dest = pos
    for e in range(N_EXPERTS):
        dest = dest + jnp.where(top_e == e, offset[e], 0)
    dest = dest.astype(I32)
    zstart = jnp.maximum(pad_end - EXPERT_ROWS, 0).astype(I32)
    zvalid = (counts > 0).astype(I32)

    h_flat = h1.reshape(n_tok, d)
    xs = _dispatch(h_flat, sc2, sh2, dest, zstart, zvalid, n_blk * EXPERT_ROWS, length)
    y_rows = _experts(xs, blk_e, n_used, wgu, bgu, wd, bd, layer)
    out = _combine(y_rows, dest, h_flat, gate, g2, ln_g, ln_b, alpha, length)
    return out.reshape(b, length, d)


def _odd_in_kernel(x_ref, sc_ref, sh_ref, w_ref, glg_ref, glb_ref, ws_ref, bs_ref, flg_ref, flb_ref, dft_ref,
                   sp_ref, zr_ref, zi_ref):
    u = (x_ref[0] * (1.0 + sc_ref[0]) + sh_ref[0]).astype(BF16)
    z = jnp.dot(u, w_ref[...], preferred_element_type=F32)
    tm = z.shape[0]
    gw = N_GROUPS * GROUP_W
    ug = jax.nn.gelu(z[:, :gw])
    vn = _layer_norm(jax.nn.gelu(z[:, gw:2 * gw]), glg_ref[...], glb_ref[...]).astype(BF16)
    f = z[:, 2 * gw:]
    for g in range(N_GROUPS):
        cols = slice(g * GROUP_W, (g + 1) * GROUP_W)
        for c0 in range(0, tm, CHUNK):
            rows = slice(c0, c0 + CHUNK)
            sv = jnp.dot(ws_ref[g], vn[rows, cols], preferred_element_type=F32) + bs_ref[:, cols]
            sp_ref[0, rows, cols] = (ug[rows, cols] * sv).astype(sp_ref.dtype)
        fn = _layer_norm(f[:, cols], flg_ref[:, cols], flb_ref[:, cols]).astype(BF16)
        zz = jnp.dot(fn, dft_ref[...], preferred_element_type=F32)
        zr_ref[0, :, cols] = zz[:, :GROUP_W]
        zi_ref[0, :, cols] = zz[:, GROUP_W:]


def _odd_in_proj(x, sc, sh, w_bf, gln_g, gln_b, ws, bs, fln_g, fln_b):
    b, length, d = x.shape
    n = w_bf.shape[1]
    gw = N_GROUPS * GROUP_W
    tm = _pick(length, (512, 256, 128))
    kk = jnp.arange(GROUP_W, dtype=I32)
    ang = (2.0 * math.pi / GROUP_W) * ((kk[:, None] * kk[None, :]) % GROUP_W).astype(F32)
    dft = jnp.concatenate([jnp.cos(ang), -jnp.sin(ang)], 1).astype(BF16)
    bs_exp = jnp.repeat(bs.T, GROUP_W, axis=1)
    row = lambda bi, i: (bi, i, 0)
    vec = lambda bi, i: (bi, 0, 0)
    const2 = lambda bi, i: (0, 0)
    return pl.pallas_call(
        _odd_in_kernel,
        grid=(b, length // tm),
        in_specs=[pl.BlockSpec((1, tm, d), row), pl.BlockSpec((1, 1, d), vec), pl.BlockSpec((1, 1, d), vec),
                  pl.BlockSpec((d, n), const2),
                  pl.BlockSpec((1, gw), const2), pl.BlockSpec((1, gw), const2),
                  pl.BlockSpec((N_GROUPS, CHUNK, CHUNK), lambda bi, i: (0, 0, 0)),
                  pl.BlockSpec((CHUNK, gw), const2),
                  pl.BlockSpec((1, gw), const2), pl.BlockSpec((1, gw), const2),
                  pl.BlockSpec((GROUP_W, 2 * GROUP_W), const2)],
        out_specs=[pl.BlockSpec((1, tm, gw), row), pl.BlockSpec((1, tm, gw), row), pl.BlockSpec((1, tm, gw), row)],
        out_shape=[jax.ShapeDtypeStruct((b, length, gw), BF16), jax.ShapeDtypeStruct((b, length, gw), F32),
                   jax.ShapeDtypeStruct((b, length, gw), F32)],
        compiler_params=_params(2),
        name="odd_in_proj",
    )(x, sc, sh, w_bf, gln_g[None], gln_b[None], ws.astype(BF16), bs_exp, fln_g[None], fln_b[None], dft)


def _fft_a_kernel(zr_ref, zi_ref, cms_ref, c_ref, s_ref, ar_ref, ai_ref):
    l1, nb, w = zr_ref.shape[1:]
    zr = zr_ref[0].reshape(l1 * nb, w)
    zi = zi_ref[0].reshape(l1 * nb, w)
    k1 = jnp.dot(cms_ref[...], zr.astype(BF16), preferred_element_type=F32)
    k2 = jnp.dot(c_ref[...], (zi - zr).astype(BF16), preferred_element_type=F32)
    k3 = jnp.dot(s_ref[...], (zr + zi).astype(BF16), preferred_element_type=F32)
    ar_ref[0] = (k1 + k3).reshape(l1, nb, w)
    ai_ref[0] = (k1 + k2).reshape(l1, nb, w)


def _fft_b_kernel(ar_ref, ai_ref, c_ref, s_ref, mask_ref, o_ref, *, norm):
    nb, l2, w = ar_ref.shape[1:]
    ar = ar_ref[0].reshape(nb * l2, w).astype(BF16)
    ai = ai_ref[0].reshape(nb * l2, w).astype(BF16)
    mask = mask_ref[...]
    c = jnp.concatenate([c_ref[0]] * nb, axis=1) * mask
    s = jnp.concatenate([s_ref[0]] * nb, axis=1) * mask
    y = jnp.dot(c, ar, preferred_element_type=F32) + jnp.dot(s, ai, preferred_element_type=F32)
    o_ref[0] = (y * norm).reshape(l2, nb, w)


def _length_dft_real(zr, zi):
    b, length, w = zr.shape
    l2 = CHUNK
    l1 = length // l2
    nb = SUBLANES
    k1 = jnp.arange(l1, dtype=I32)
    ang_a = (2.0 * math.pi / l1) * ((k1[:, None] * k1[None, :]) % l1).astype(F32)
    r = lax.broadcasted_iota(I32, (l1 * nb, l1 * nb), 0)
    c = lax.broadcasted_iota(I32, (l1 * nb, l1 * nb), 1)
    same = r % nb == c % nb
    rep = (lax.broadcasted_iota(I32, (l1 * nb, l1), 0) // nb == lax.broadcasted_iota(I32, (l1 * nb, l1), 1))
    rep = rep.astype(BF16)

    def expand_a(t):
        t = jnp.dot(jnp.dot(rep, t.astype(BF16), preferred_element_type=F32).astype(BF16), rep.T,
                    preferred_element_type=F32)
        return jnp.where(same, t, 0.0).astype(BF16)

    cos_a = expand_a(jnp.cos(ang_a))
    sin_a = expand_a(jnp.sin(ang_a))
    cms_a = expand_a(jnp.cos(ang_a) - jnp.sin(ang_a))
    shape_t = (l1 // nb, l2 * nb, l2)
    kb = lax.broadcasted_iota(I32, shape_t, 0)
    r = lax.broadcasted_iota(I32, shape_t, 1)
    n2 = lax.broadcasted_iota(I32, shape_t, 2)
    ang_b = (2.0 * math.pi / length) * ((n2 * (kb * nb + r % nb + l1 * (r // nb))) % length).astype(F32)
    cos_b = jnp.cos(ang_b).astype(BF16)
    sin_b = jnp.sin(ang_b).astype(BF16)
    shape_m = (l2 * nb, nb * l2)
    mask_b = (lax.broadcasted_iota(I32, shape_m, 0) % nb == lax.broadcasted_iota(I32, shape_m, 1) // l2)
    mask_b = mask_b.astype(BF16)

    zr4 = zr.reshape(b, l1, l2, w)
    zi4 = zi.reshape(b, l1, l2, w)
    blk_a = pl.BlockSpec((1, l1, nb, w), lambda bi, i: (bi, 0, i, 0))
    ar, ai = pl.pallas_call(
        _fft_a_kernel,
        grid=(b, l2 // nb),
        in_specs=[blk_a, blk_a] + [pl.BlockSpec((l1 * nb, l1 * nb), lambda bi, i: (0, 0))] * 3,
        out_specs=[blk_a, blk_a],
        out_shape=[jax.ShapeDtypeStruct((b, l1, l2, w), F32)] * 2,
        compiler_params=_params(2),
        name="fourier_stage_a",
    )(zr4, zi4, cms_a, cos_a, sin_a)
    blk_b = pl.BlockSpec((1, nb, l2, w), lambda bi, i: (bi, i, 0, 0))
    out = pl.pallas_call(
        functools.partial(_fft_b_kernel, norm=float((length * GROUP_W) ** -0.5)),
        grid=(b, l1 // nb),
        in_specs=[blk_b, blk_b, pl.BlockSpec((1, l2 * nb, l2), lambda bi, i: (i, 0, 0)),
                  pl.BlockSpec((1, l2 * nb, l2), lambda bi, i: (i, 0, 0)),
                  pl.BlockSpec((l2 * nb, nb * l2), lambda bi, i: (0, 0))],
        out_specs=pl.BlockSpec((1, l2, nb, w), lambda bi, i: (bi, 0, i, 0)),
        out_shape=jax.ShapeDtypeStruct((b, l2, l1, w), F32),
        compiler_params=_params(2),
        name="fourier_stage_b",
    )(ar, ai, cos_b, sin_b, mask_b)
    return out.reshape(b, length, w)


def kernel(x, c, ctx, c_ctx, w_mod, b_mod, ln1_g, ln1_b, ln2_g, ln2_b, ev_w_in, ev_w_out, conv_w, conv_b, conv_ln_g, conv_ln_b, lam_q1, lam_k1, lam_q2, lam_k2, diff_norm_g, od_w_in, od_w_out, gmlp_ln_g, gmlp_ln_b, gmlp_ws, gmlp_bs, four_ln_g, four_ln_b, router_w, router_b, w_gate_up, b_gate_up, w_down, b_down):
    b, length, d = x.shape
    depth = w_mod.shape[0]
    alpha = float((2 * depth) ** 0.25)
    assert b + 1 <= SUBLANES and length % (CHUNK * SUBLANES) == 0 and length % GRID_W == 0
    assert length % KV_TILE == 0 and ctx.shape[1] % KV_TILE == 0

    mod = _modulation(c, c_ctx, w_mod, b_mod)
    cos_t, sin_t = _rope_tables(length)
    h = x
    for layer in range(depth):
        j = layer // 2
        m = mod[layer]
        sh1, sc1, g1, sh2, sc2, g2 = [m[:b, i * d:(i + 1) * d][:, None, :] for i in range(6)]
        if layer % 2 == 0:
            lam_init = 0.8 - 0.6 * math.exp(-0.3 * layer)
            w_in = ev_w_in[j].astype(BF16)
            q, k_all, vt, a = _even_in_proj(h, sc1, sh1, w_in, cos_t, sin_t, length + ctx.shape[1])
            csh1 = jnp.broadcast_to(m[b:b + 1, 0:d][:, None, :], (b, 1, d))
            csc1 = jnp.broadcast_to(m[b:b + 1, d:2 * d][:, None, :], (b, 1, d))
            k_all, vt = _context_kv(ctx, csc1, csh1, w_in[:, QK_W:2 * QK_W + V_W], k_all, vt, length)
            att = _diff_attention(q, k_all, vt, lam_q1[j], lam_k1[j], lam_q2[j], lam_k2[j],
                                  diff_norm_g[j], lam_init)
            conv = _conformer_conv(a, conv_w[j], conv_b[j], conv_ln_g[j], conv_ln_b[j])
            ya, yb, w_out = conv, att, ev_w_out[j]
        else:
            spatial, zr, zi = _odd_in_proj(h, sc1, sh1, od_w_in[j].astype(BF16), gmlp_ln_g[j], gmlp_ln_b[j],
                                           gmlp_ws[j], gmlp_bs[j], four_ln_g[j], four_ln_b[j])
            ya, yb, w_out = spatial, _length_dft_real(zr, zi), od_w_out[j]
        h1, top_e, gate, pos, counts = _post_mix(ya, yb, w_out.astype(BF16), h, g1, ln1_g[layer], ln1_b[layer],
                                                 sc2, sh2, router_w[layer], router_b[layer], alpha)
        h = _moe_and_norm(h1, top_e[:TOP_K], gate, pos[:TOP_K], counts[0, :N_EXPERTS], sc2, sh2, g2,
                          w_gate_up, b_gate_up, w_down, b_down, layer, ln2_g[layer], ln2_b[layer], alpha)
    return h
```

```python
import functools
import math

import jax
import jax.numpy as jnp
from jax import lax
from jax.experimental import pallas as pl
from jax.experimental.pallas import tpu as pltpu

F32 = jnp.float32
BF16 = jnp.bfloat16
I32 = jnp.int32
HIGHEST = lax.Precision.HIGHEST

LN_EPS = 1e-5
GRID_W = 64
HEAD_DIM = 64
HEAD_V = 128
N_HEADS = 4
QK_W = N_HEADS * 2 * HEAD_DIM
V_W = N_HEADS * HEAD_V
CONV_W = 512
CONV_TAPS = 31
CONV_HALO = 16
ROPE_BASE = 10000.0
CHUNK = 128
N_GROUPS = 4
GROUP_W = 128
N_EXPERTS = 32
TOP_K = 4
SWIGLU_LIMIT = 7.0
SWIGLU_ALPHA = 1.702
LANES = 128
SUBLANES = 8
EXPERT_ROWS = 512
ISSUE_UNROLL = 8
NEG_BIG = -1e30
VMEM_LIMIT = 56 * 1024 * 1024


def _params(n_axes):
    return pltpu.CompilerParams(dimension_semantics=("arbitrary",) * n_axes,
                                vmem_limit_bytes=VMEM_LIMIT)


def _pick(n, candidates):
    for c in candidates:
        if n % c == 0:
            return c
    return n


def _layer_norm(r, g, b):
    mu = jnp.mean(r, axis=-1, keepdims=True)
    d = r - mu
    var = jnp.mean(d * d, axis=-1, keepdims=True)
    return d * lax.rsqrt(var + LN_EPS) * g + b


def _sigmoid(x):
    return 1.0 / (1.0 + jnp.exp(-x))


def _mod_kernel(cs_ref, w_ref, b_ref, o_ref):
    cs = cs_ref[...]
    a = cs * _sigmoid(cs)
    o_ref[0] = jnp.dot(a, w_ref[0], precision=HIGHEST, preferred_element_type=F32) + b_ref[0]


def _modulation(c, c_ctx, w_mod, b_mod):
    depth, d, n = w_mod.shape
    b = c.shape[0]
    cs = jnp.concatenate([c, c_ctx[None], jnp.zeros((SUBLANES - b - 1, d), F32)], 0)
    tn = _pick(n, (1536, 1024, 512))
    return pl.pallas_call(
        _mod_kernel,
        grid=(depth, n // tn),
        in_specs=[pl.BlockSpec((SUBLANES, d), lambda l, j: (0, 0)),
                  pl.BlockSpec((1, d, tn), lambda l, j: (l, 0, j)),
                  pl.BlockSpec((1, 1, tn), lambda l, j: (l, 0, j))],
        out_specs=pl.BlockSpec((1, SUBLANES, tn), lambda l, j: (l, 0, j)),
        out_shape=jax.ShapeDtypeStruct((depth, SUBLANES, n), F32),
        compiler_params=_params(2),
        name="modulation",
    )(cs, w_mod, b_mod.reshape(depth, 1, n))


KV_TILE = 256
VT_ROWS = HEAD_V + 16


def _store_values_transposed(v, vt_ref):
    extra = (lax.broadcasted_iota(I32, (VT_ROWS - HEAD_V, v.shape[0]), 0) == 0).astype(BF16)
    for h in range(N_HEADS):
        vt_ref[0, h, 0, :HEAD_V, :] = v[:, h * HEAD_V:(h + 1) * HEAD_V].T.astype(BF16)
        vt_ref[0, h, 0, HEAD_V:, :] = extra


def _even_in_kernel(x_ref, sc_ref, sh_ref, w_ref, cos_ref, sin_ref, q_ref, k_ref, vt_ref, a_ref):
    u = (x_ref[0] * (1.0 + sc_ref[0]) + sh_ref[0]).astype(BF16)
    z = jnp.dot(u, w_ref[...], preferred_element_type=F32)
    cosv = cos_ref[...]
    sinv = sin_ref[...]
    lane = lax.broadcasted_iota(I32, cosv.shape, 1)
    first = (lane % 32) < 16

    def rope(t):
        partner = jnp.where(first, pltpu.roll(t, LANES - 16, 1), pltpu.roll(t, 16, 1))
        return t * cosv + partner * sinv

    for j in range(QK_W // LANES):
        sl = slice(j * LANES, (j + 1) * LANES)
        rq = rope(z[:, sl]) * (HEAD_DIM ** -0.5 * math.log2(math.e))
        for c in range(2):
            q_ref[c, 0, :, sl] = jnp.where(lane // HEAD_DIM == c, rq, 0.0).astype(BF16)
        k_ref[0, :, sl] = rope(z[:, QK_W + j * LANES:QK_W + (j + 1) * LANES]).astype(BF16)
    _store_values_transposed(z[:, 2 * QK_W:2 * QK_W + V_W], vt_ref)
    a0 = 2 * QK_W + V_W
    a_ref[0] = z[:, a0:a0 + CONV_W] * _sigmoid(z[:, a0 + CONV_W:a0 + 2 * CONV_W])


def _rope_tables(length):
    rows = length // GRID_W
    row = jnp.repeat(jnp.arange(rows, dtype=F32), GRID_W)
    col = jnp.tile(jnp.arange(GRID_W, dtype=F32), rows)
    n_freq = HEAD_DIM // 4
    inv_freq = ROPE_BASE ** (-jnp.arange(n_freq, dtype=F32) / n_freq)
    ar = row[:, None] * inv_freq
    ac = col[:, None] * inv_freq
    cos64 = jnp.concatenate([jnp.cos(ar), jnp.cos(ar), jnp.cos(ac), jnp.cos(ac)], 1)
    sin64 = jnp.concatenate([-jnp.sin(ar), jnp.sin(ar), -jnp.sin(ac), jnp.sin(ac)], 1)
    return jnp.tile(cos64, (1, LANES // HEAD_DIM)), jnp.tile(sin64, (1, LANES // HEAD_DIM))


def _key_chunk(n_keys):
    return next(c for c in (1280, 256) if n_keys % c == 0 and n_keys // c >= 3)


def _kv_specs(b, n_keys, kc, first_tile):
    per_chunk = kc // KV_TILE
    k_spec = pl.BlockSpec((1, KV_TILE, QK_W), lambda bi, i: (bi, first_tile + i, 0))
    vt_spec = pl.BlockSpec((1, N_HEADS, 1, VT_ROWS, KV_TILE),
                           lambda bi, i: (bi, 0, (first_tile + i) // per_chunk, 0, (first_tile + i) % per_chunk))
    shapes = [jax.ShapeDtypeStruct((b, n_keys, QK_W), BF16),
              jax.ShapeDtypeStruct((b, N_HEADS, n_keys // kc, VT_ROWS, kc), BF16)]
    return k_spec, vt_spec, shapes


def _even_in_proj(x, sc, sh, w_bf, cos_t, sin_t, n_keys):
    b, length, d = x.shape
    n = w_bf.shape[1]
    tm = KV_TILE
    row = lambda bi, i: (bi, i, 0)
    vec = lambda bi, i: (bi, 0, 0)
    k_spec, vt_spec, kv_shapes = _kv_specs(b, n_keys, _key_chunk(n_keys), 0)
    return pl.pallas_call(
        _even_in_kernel,
        grid=(b, length // tm),
        in_specs=[pl.BlockSpec((1, tm, d), row),
                  pl.BlockSpec((1, 1, d), vec),
                  pl.BlockSpec((1, 1, d), vec),
                  pl.BlockSpec((d, n), lambda bi, i: (0, 0)),
                  pl.BlockSpec((tm, LANES), lambda bi, i: (i, 0)),
                  pl.BlockSpec((tm, LANES), lambda bi, i: (i, 0))],
        out_specs=[pl.BlockSpec((2, 1, tm, QK_W), lambda bi, i: (0, bi, i, 0)), k_spec, vt_spec,
                   pl.BlockSpec((1, tm, CONV_W), row)],
        out_shape=[jax.ShapeDtypeStruct((2, b, length, QK_W), BF16)] + kv_shapes
                  + [jax.ShapeDtypeStruct((b, length, CONV_W), F32)],
        compiler_params=_params(2),
        name="even_in_proj",
    )(x, sc, sh, w_bf, cos_t, sin_t)


def _ctx_kv_kernel(x_ref, sc_ref, sh_ref, w_ref, k_in, vt_in, k_ref, vt_ref):
    del k_in, vt_in
    u = (x_ref[0] * (1.0 + sc_ref[0]) + sh_ref[0]).astype(BF16)
    z = jnp.dot(u, w_ref[...], preferred_element_type=F32)
    k_ref[0] = z[:, :QK_W].astype(BF16)
    _store_values_transposed(z[:, QK_W:], vt_ref)


def _context_kv(ctx, sc, sh, w_bf, k_all, vt, first_key):
    b, rows, d = ctx.shape
    n = w_bf.shape[1]
    n_keys = k_all.shape[1]
    k_spec, vt_spec, kv_shapes = _kv_specs(b, n_keys, vt.shape[4], first_key // KV_TILE)
    return pl.pallas_call(
        _ctx_kv_kernel,
        grid=(b, rows // KV_TILE),
        in_specs=[pl.BlockSpec((1, KV_TILE, d), lambda bi, i: (bi, i, 0)),
                  pl.BlockSpec((1, 1, d), lambda bi, i: (bi, 0, 0)),
                  pl.BlockSpec((1, 1, d), lambda bi, i: (bi, 0, 0)),
                  pl.BlockSpec((d, n), lambda bi, i: (0, 0)),
                  pl.BlockSpec(memory_space=pl.ANY), pl.BlockSpec(memory_space=pl.ANY)],
        out_specs=[k_spec, vt_spec],
        out_shape=kv_shapes,
        input_output_aliases={4: 0, 5: 1},
        compiler_params=_params(2),
        name="context_kv_proj",
    )(ctx, sc, sh, w_bf, k_all, vt)


def _attn_kernel(lq1_ref, lk1_ref, lq2_ref, lk2_ref, g_ref, q_ref, k_ref, vt_ref, o_ref,
                 s0_sc, s1_sc, p0_sc, p1_sc, acc_sc, *, lam_init, kc):
    mq = q_ref.shape[2]
    n = k_ref.shape[1] // kc
    s_sc = (s0_sc, s1_sc)
    p_sc = (p0_sc, p1_sc)

    def scores(j, slot):
        kj = k_ref[0, pl.ds(pl.multiple_of(j * kc, kc), kc), :]
        for c in range(2):
            s_sc[slot][c] = lax.dot_general(kj, q_ref[c, 0], (((1,), (1,)), ((), ())),
                                            preferred_element_type=F32)

    def softmax(slot, m):
        m_out, alpha = [], []
        for c in range(2):
            s = s_sc[slot][c]
            m_new = jnp.maximum(m[c], jnp.max(s, axis=0, keepdims=True))
            p_sc[slot][c] = jnp.exp2((s - m_new).astype(BF16))
            m_out.append(m_new)
            alpha.append(jnp.exp2(m[c] - m_new))
        return tuple(m_out), tuple(alpha)

    def accumulate(j, slot, alpha):
        vt = vt_ref[0, 0, j]
        for c in range(2):
            acc_sc[c] = alpha[c] * acc_sc[c] + jnp.dot(vt, p_sc[slot][c], preferred_element_type=F32)

    def step(j, slot, m, alpha):
        scores(j + 2, slot)
        m, alpha_next = softmax(1 - slot, m)
        accumulate(j, slot, alpha)
        return m, alpha_next

    acc_sc[...] = jnp.zeros(acc_sc.shape, F32)
    m = (jnp.full((1, mq), NEG_BIG, F32),) * 2
    scores(0, 0)
    m, alpha = softmax(0, m)
    scores(1, 1)

    def pair(t, carry):
        m, alpha = step(2 * t, 0, *carry)
        return step(2 * t + 1, 1, m, alpha)

    n_steps = n - 2
    m, alpha = lax.fori_loop(0, n_steps // 2, pair, (m, alpha))
    if n_steps % 2:
        m, alpha = step(n_steps - 1, 0, m, alpha)
    m, alpha_last = softmax((n - 1) % 2, m)
    accumulate(n - 2, (n - 2) % 2, alpha)
    accumulate(n - 1, (n - 1) % 2, alpha_last)
    acc0 = acc_sc[0]
    acc1 = acc_sc[1]
    lam = (jnp.exp(jnp.sum(lq1_ref[...] * lk1_ref[...], keepdims=True))
           - jnp.exp(jnp.sum(lq2_ref[...] * lk2_ref[...], keepdims=True)) + lam_init)
    o = (acc0[:HEAD_V] / acc0[HEAD_V:HEAD_V + 1]
         - lam * (acc1[:HEAD_V] / acc1[HEAD_V:HEAD_V + 1]))
    ms = jnp.mean(o * o, axis=0, keepdims=True)
    o = o * lax.rsqrt(ms + LN_EPS) * g_ref[...] * (1.0 - lam_init)
    o_ref[0] = o.T.astype(o_ref.dtype)


def _diff_attention(q, k_all, vt, lq1, lk1, lq2, lk2, norm_g, lam_init):
    _, b, length, _ = q.shape
    n_keys = k_all.shape[1]
    mq = _pick(length, (1024, 512, 256, 128))
    nc, kc = vt.shape[2], vt.shape[4]
    small = lambda bi, h, i: (0, 0)
    return pl.pallas_call(
        functools.partial(_attn_kernel, lam_init=lam_init, kc=kc),
        grid=(b, N_HEADS, length // mq),
        in_specs=[pl.BlockSpec((1, HEAD_DIM), small), pl.BlockSpec((1, HEAD_DIM), small),
                  pl.BlockSpec((1, HEAD_DIM), small), pl.BlockSpec((1, HEAD_DIM), small),
                  pl.BlockSpec((HEAD_V, 1), small),
                  pl.BlockSpec((2, 1, mq, HEAD_V), lambda bi, h, i: (0, bi, i, h)),
                  pl.BlockSpec((1, n_keys, HEAD_V), lambda bi, h, i: (bi, 0, h)),
                  pl.BlockSpec((1, 1, nc, VT_ROWS, kc), lambda bi, h, i: (bi, h, 0, 0, 0))],
        out_specs=pl.BlockSpec((1, mq, HEAD_V), lambda bi, h, i: (bi, i, h)),
        out_shape=jax.ShapeDtypeStruct((b, length, V_W), BF16),
        scratch_shapes=[pltpu.VMEM((2, kc, mq), F32), pltpu.VMEM((2, kc, mq), F32),
                        pltpu.VMEM((2, kc, mq), BF16), pltpu.VMEM((2, kc, mq), BF16),
                        pltpu.VMEM((2, VT_ROWS, mq), F32)],
        compiler_params=_params(3),
        name="diff_attention",
    )(lq1[None], lk1[None], lq2[None], lk2[None], norm_g[:, None], q, k_all, vt)


def _conv_kernel(prev_ref, cur_ref, next_ref, w_ref, b_ref, g_ref, beta_ref, o_ref, win, *, row_chunk):
    i = pl.program_id(1)
    tc = cur_ref.shape[1]
    span = tc + 2 * CONV_HALO - SUBLANES
    win[0, CONV_HALO:CONV_HALO + tc, :] = cur_ref[0]
    win[0, 0:CONV_HALO, :] = jnp.where(i > 0, prev_ref[0], 0.0)
    win[0, CONV_HALO + tc:, :] = jnp.where(i < pl.num_programs(1) - 1, next_ref[0], 0.0)
    for s in range(1, SUBLANES):
        win[s, 0:span, :] = win[0, s:s + span, :]
    first_tap = CONV_HALO - CONV_TAPS // 2
    for c0 in range(0, tc, row_chunk):
        acc = jnp.zeros((row_chunk, CONV_W), F32)
        for t in range(CONV_TAPS):
            s = (first_tap + t) % SUBLANES
            r0 = c0 + first_tap + t - s
            acc = acc + w_ref[t:t + 1, :] * win[s, r0:r0 + row_chunk, :]
        y = _layer_norm(acc + b_ref[...], g_ref[...], beta_ref[...])
        o_ref[0, c0:c0 + row_chunk, :] = (y * _sigmoid(y)).astype(o_ref.dtype)


def _conformer_conv(a, conv_w, conv_b, ln_g, ln_b):
    b, length, w = a.shape
    tc = _pick(length, (512, 256, 128))
    hb = tc // CONV_HALO
    n_halo = length // CONV_HALO
    w_pad = jnp.concatenate([conv_w, jnp.zeros((1, w), F32)], 0)
    vec = lambda bi, i: (0, 0)
    return pl.pallas_call(
        functools.partial(_conv_kernel, row_chunk=64),
        grid=(b, length // tc),
        in_specs=[pl.BlockSpec((1, CONV_HALO, w), lambda bi, i: (bi, jnp.maximum(i * hb - 1, 0), 0)),
                  pl.BlockSpec((1, tc, w), lambda bi, i: (bi, i, 0)),
                  pl.BlockSpec((1, CONV_HALO, w), lambda bi, i: (bi, jnp.minimum((i + 1) * hb, n_halo - 1), 0)),
                  pl.BlockSpec((CONV_TAPS + 1, w), vec),
                  pl.BlockSpec((1, w), vec), pl.BlockSpec((1, w), vec), pl.BlockSpec((1, w), vec)],
        out_specs=pl.BlockSpec((1, tc, w), lambda bi, i: (bi, i, 0)),
        out_shape=jax.ShapeDtypeStruct((b, length, w), BF16),
        scratch_shapes=[pltpu.VMEM((SUBLANES, tc + 2 * CONV_HALO, w), F32)],
        compiler_params=_params(2),
        name="conformer_conv",
    )(a, a, a, w_pad, conv_b[None], ln_g[None], ln_b[None])


def _post_mix_kernel(ya_ref, yb_ref, wa_ref, wb_ref, x_ref, g1_ref, lng_ref, lnb_ref, sc_ref, sh_ref,
                     rw_ref, rb_ref, h_ref, e_ref, gate_ref, pos_ref, cnt_ref, carry, *, alpha):
    first_step = jnp.logical_and(pl.program_id(0) == 0, pl.program_id(1) == 0)

    @pl.when(first_step)
    def _():
        carry[...] = jnp.zeros(carry.shape, F32)

    y = (jnp.dot(ya_ref[0].astype(BF16), wa_ref[...], preferred_element_type=F32)
         + jnp.dot(yb_ref[0].astype(BF16), wb_ref[...], preferred_element_type=F32))
    h = _layer_norm(alpha * x_ref[0] + g1_ref[0] * y, lng_ref[...], lnb_ref[...])
    h_ref[0] = h
    u2 = h * (1.0 + sc_ref[0]) + sh_ref[0]
    u_hi = u2.astype(BF16)
    u_lo = (u2 - u_hi.astype(F32)).astype(BF16)
    hh_hl = jnp.dot(u_hi, rw_ref[...], preferred_element_type=F32)
    lh = jnp.dot(u_lo, rw_ref[:, :LANES], preferred_element_type=F32)
    logits = hh_hl[:, :LANES] + hh_hl[:, LANES:] + lh + rb_ref[...]

    tm = logits.shape[0]
    lane = lax.broadcasted_iota(I32, logits.shape, 1).astype(F32)
    vals, idxs = [], []
    lg = logits
    for _ in range(TOP_K):
        mx = jnp.max(lg, axis=1, keepdims=True)
        idx = jnp.min(jnp.where(lg == mx, lane, float(LANES)), axis=1, keepdims=True)
        vals.append(mx)
        idxs.append(idx)
        lg = jnp.where(lane == idx, -3e38, lg)
    exps = [jnp.exp(v - vals[0]) for v in vals]
    den = exps[0] + exps[1] + exps[2] + exps[3]

    hot = jnp.zeros(logits.shape, F32)
    for idx in idxs:
        hot = hot + (lane == idx).astype(F32)
    r_i = lax.broadcasted_iota(I32, (tm, tm), 0)
    c_i = lax.broadcasted_iota(I32, (tm, tm), 1)
    tri = (c_i < r_i).astype(BF16)
    before = jnp.dot(tri, hot.astype(BF16), preferred_element_type=F32) + carry[...]
    e_out = jnp.zeros(logits.shape, F32)
    g_out = jnp.zeros(logits.shape, F32)
    p_out = jnp.zeros(logits.shape, F32)
    for j in range(TOP_K):
        pos_j = jnp.sum(jnp.where(lane == idxs[j], before, 0.0), axis=1, keepdims=True)
        sel = lane == float(j)
        e_out = jnp.where(sel, idxs[j], e_out)
        g_out = jnp.where(sel, exps[j] / den, g_out)
        p_out = jnp.where(sel, pos_j, p_out)
    e_ref[...] = e_out.T[:SUBLANES].astype(I32)
    gate_ref[...] = g_out
    pos_ref[...] = p_out.T[:SUBLANES].astype(I32)
    new_carry = carry[...] + jnp.sum(hot, axis=0, keepdims=True)
    carry[...] = new_carry
    cnt_ref[...] = jnp.broadcast_to(new_carry, cnt_ref.shape).astype(I32)


def _post_mix(ya, yb, w_out_bf, x, g1, ln_g, ln_b, sc2, sh2, router_w, router_b, alpha):
    b, length, d = x.shape
    wa = ya.shape[2]
    tm = _pick(length, (512, 256, 128))
    n_tok = b * length
    rw = jnp.concatenate([router_w, jnp.zeros((d, LANES - N_EXPERTS), F32)], 1)
    rw_hi = rw.astype(BF16)
    rw = jnp.concatenate([rw_hi, (rw - rw_hi.astype(F32)).astype(BF16)], 1)
    rb =jnp.concatenate([router_b, jnp.full((LANES - N_EXPERTS,), NEG_BIG, F32)])[None]
    row = lambda bi, i: (bi, i, 0)
    vec = lambda bi, i: (bi, 0, 0)
    const = lambda bi, i: (0, 0)
    nt = length // tm
    tok = lambda bi, i: (bi * nt + i, 0)
    return pl.pallas_call(
        functools.partial(_post_mix_kernel, alpha=alpha),
        grid=(b, nt),
        in_specs=[pl.BlockSpec((1, tm, wa), row), pl.BlockSpec((1, tm, d - wa), row),
                  pl.BlockSpec((wa, d), const), pl.BlockSpec((d - wa, d), const),
                  pl.BlockSpec((1, tm, d), row), pl.BlockSpec((1, 1, d), vec),
                  pl.BlockSpec((1, d), const), pl.BlockSpec((1, d), const),
                  pl.BlockSpec((1, 1, d), vec), pl.BlockSpec((1, 1, d), vec),
                  pl.BlockSpec((d, 2 * LANES), const), pl.BlockSpec((1, LANES), const)],
        out_specs=[pl.BlockSpec((1, tm, d), row), pl.BlockSpec((SUBLANES, tm), lambda bi, i: (0, bi * nt + i)),
                   pl.BlockSpec((tm, LANES), tok), pl.BlockSpec((SUBLANES, tm), lambda bi, i: (0, bi * nt + i)),
                   pl.BlockSpec((SUBLANES, LANES), const)],
        out_shape=[jax.ShapeDtypeStruct((b, length, d), F32), jax.ShapeDtypeStruct((SUBLANES, n_tok), I32),
                   jax.ShapeDtypeStruct((n_tok, LANES), F32), jax.ShapeDtypeStruct((SUBLANES, n_tok), I32),
                   jax.ShapeDtypeStruct((SUBLANES, LANES), I32)],
        scratch_shapes=[pltpu.VMEM((1, LANES), F32)],
        compiler_params=_params(2),
        name="out_proj_norm_router",
    )(ya, yb, w_out_bf[:wa], w_out_bf[wa:], x, g1, ln_g[None], ln_b[None], sc2, sh2, rw, rb)


def _tile_indices(dest, tm):
    n_tok = dest.shape[1]
    return dest.reshape(TOP_K, n_tok // tm, tm).transpose(1, 0, 2).reshape(n_tok // tm, 1, TOP_K * tm)


def _dispatch_kernel(zstart_ref, zvalid_ref, dest_ref, h_ref, sc_ref, sh_ref, xs_ref, u_sc, zero_sc, sem, zsem):
    tm = h_ref.shape[0]

    @pl.when(pl.program_id(0) == 0)
    def _():
        zero_sc[...] = jnp.zeros(zero_sc.shape, F32)

        def fill(e, c):
            @pl.when(zvalid_ref[e] > 0)
            def _():
                start = pl.multiple_of(zstart_ref[e], SUBLANES)
                pltpu.make_async_copy(zero_sc, xs_ref.at[pl.ds(start, EXPERT_ROWS)], zsem).start()
            return c

        lax.fori_loop(0, N_EXPERTS, fill, 0)

        def drain(e, c):
            @pl.when(zvalid_ref[e] > 0)
            def _():
                pltpu.make_async_copy(zero_sc, xs_ref.at[pl.ds(0, EXPERT_ROWS)], zsem).wait()
            return c

        lax.fori_loop(0, N_EXPERTS, drain, 0)

    u_sc[...] = h_ref[...] * (1.0 + sc_ref[0]) + sh_ref[0]

    for r in range(tm):
        for j in range(TOP_K):
            d = dest_ref[0, 0, j * tm + r]
            pltpu.make_async_copy(u_sc.at[pl.ds(r, 1)], xs_ref.at[pl.ds(d, 1)], sem).start()
    rows = xs_ref.at[pl.ds(0, tm * TOP_K)]
    pltpu.make_async_copy(rows, rows, sem).wait()


def _dispatch(h_flat, sc2, sh2, dest, zstart, zvalid, n_rows, tokens_per_batch):
    n_tok, d = h_flat.shape
    tm = _pick(tokens_per_batch, (256, 128))
    per_b = tokens_per_batch // tm
    return pl.pallas_call(
        _dispatch_kernel,
        grid_spec=pltpu.PrefetchScalarGridSpec(
            num_scalar_prefetch=2,
            grid=(n_tok // tm,),
            in_specs=[pl.BlockSpec((1, 1, tm * TOP_K), lambda i, zs, zv: (i, 0, 0), memory_space=pltpu.SMEM),
                      pl.BlockSpec((tm, d), lambda i, zs, zv: (i, 0)),
                      pl.BlockSpec((1, 1, d), lambda i, zs, zv: (i // per_b, 0, 0)),
                      pl.BlockSpec((1, 1, d), lambda i, zs, zv: (i // per_b, 0, 0))],
            out_specs=pl.BlockSpec(memory_space=pl.ANY),
            scratch_shapes=[pltpu.VMEM((tm, d), F32), pltpu.VMEM((EXPERT_ROWS, d), F32),
                            pltpu.SemaphoreType.DMA(()), pltpu.SemaphoreType.DMA(())]),
        out_shape=jax.ShapeDtypeStruct((n_rows, d), F32),
        compiler_params=_params(1),
        name="moe_dispatch",
    )(zstart, zvalid, _tile_indices(dest, tm), h_flat, sc2, sh2)


def _expert_kernel(blk_e_ref, n_used_ref, x_ref, wgu_ref, bgu_ref, wd_ref, bd_ref, o_ref, wgu_bf, wd_bf):
    i = pl.program_id(0)
    new_expert = jnp.logical_or(i == 0, blk_e_ref[i] != blk_e_ref[jnp.maximum(i - 1, 0)])

    @pl.when(new_expert)
    def _():
        wgu_bf[...] = wgu_ref[0].astype(BF16)
        wd_bf[...] = wd_ref[0].astype(BF16)

    @pl.when(i < n_used_ref[0])
    def _():
        d_e = wd_ref.shape[1]
        gu = jnp.dot(x_ref[...].astype(BF16), wgu_bf[...], preferred_element_type=F32) + bgu_ref[0]
        x_glu = jnp.minimum(gu[:, :d_e], SWIGLU_LIMIT)
        x_lin = jnp.clip(gu[:, d_e:], -SWIGLU_LIMIT, SWIGLU_LIMIT)
        act = x_glu * _sigmoid(SWIGLU_ALPHA * x_glu) * (x_lin + 1.0)
        o_ref[...] = jnp.dot(act.astype(BF16), wd_bf[...], preferred_element_type=F32) + bd_ref[0]


def _experts(xs, blk_e, n_used, wgu, bgu, wd, bd, layer):
    n_rows, d = xs.shape
    depth, n_e, _, n_gu = wgu.shape
    n_blk = n_rows // EXPERT_ROWS
    rows = lambda i, be, nu: (jnp.minimum(i, nu[0] - 1), 0)
    first = layer * n_e
    by_e = lambda i, be, nu: (first + be[i], 0, 0)
    n_e = depth * n_e
    wgu = wgu.reshape(n_e, d, n_gu)
    wd = wd.reshape(n_e, n_gu // 2, d)
    return pl.pallas_call(
        _expert_kernel,
        grid_spec=pltpu.PrefetchScalarGridSpec(
            num_scalar_prefetch=2,
            grid=(n_blk,),
            in_specs=[pl.BlockSpec((EXPERT_ROWS, d), rows),
                      pl.BlockSpec((1, d, n_gu), by_e), pl.BlockSpec((1, 1, n_gu), by_e),
                      pl.BlockSpec((1, n_gu // 2, d), by_e), pl.BlockSpec((1, 1, d), by_e)],
            out_specs=pl.BlockSpec((EXPERT_ROWS, d), rows),
            scratch_shapes=[pltpu.VMEM((d, n_gu), BF16), pltpu.VMEM((n_gu // 2, d), BF16)]),
        out_shape=jax.ShapeDtypeStruct((n_rows, d), F32),
        compiler_params=_params(1),
        name="moe_experts",
    )(blk_e, n_used, xs, wgu, bgu.reshape(n_e, 1, n_gu), wd, bd.reshape(n_e, 1, d))


def _combine_kernel(dest_ref, dest_next_ref, y_ref, h_ref, gate_ref, g2_ref, lng_ref, lnb_ref, o_ref,
                    buf, sem, *, alpha):
    i = pl.program_id(0)
    tm = h_ref.shape[0]

    def fetch(idx_ref, slot):
        def issue(g, c):
            for rr in range(ISSUE_UNROLL):
                r = g * ISSUE_UNROLL + rr
                for j in range(TOP_K):
                    d = idx_ref[0, 0, j * tm + r]
                    pltpu.make_async_copy(y_ref.at[pl.ds(d, 1)], buf.at[slot, j, pl.ds(r, 1)], sem.at[slot]).start()
            return c

        lax.fori_loop(0, tm // ISSUE_UNROLL, issue, 0)

    def fetch_unrolled(idx_ref, slot):
        for r in range(tm):
            for j in range(TOP_K):
                d = idx_ref[0, 0, j * tm + r]
                pltpu.make_async_copy(y_ref.at[pl.ds(d, 1)], buf.at[slot, j, pl.ds(r, 1)], sem.at[slot]).start()

    slot = i % 2
    more = i + 1 < pl.num_programs(0)

    @pl.when(i == 0)
    def _():
        fetch(dest_ref, 0)

    pl.when(jnp.logical_and(more, slot == 0))(lambda: fetch_unrolled(dest_next_ref, 1))
    pl.when(jnp.logical_and(more, slot == 1))(lambda: fetch_unrolled(dest_next_ref, 0))

    pltpu.make_async_copy(buf.at[slot], buf.at[slot], sem.at[slot]).wait()
    gate = gate_ref[...]
    y2 = gate[:, 0:1] * buf[slot, 0]
    for j in range(1, TOP_K):
        y2 = y2 + gate[:, j:j + 1] * buf[slot, j]
    o_ref[...] = _layer_norm(alpha * h_ref[...] + g2_ref[0] * y2, lng_ref[...], lnb_ref[...])


def _combine(y_rows, dest, h_flat, gate, g2, ln_g, ln_b, alpha, tokens_per_batch):
    n_tok, d = h_flat.shape
    tm = _pick(tokens_per_batch, (256, 128))
    per_b = tokens_per_batch // tm
    n_tiles = n_tok // tm
    dest3 = _tile_indices(dest, tm)
    return pl.pallas_call(
        functools.partial(_combine_kernel, alpha=alpha),
        grid=(n_tiles,),
        in_specs=[pl.BlockSpec((1, 1, tm * TOP_K), lambda i: (i, 0, 0), memory_space=pltpu.SMEM),
                  pl.BlockSpec((1, 1, tm * TOP_K), lambda i: (jnp.minimum(i + 1, n_tiles - 1), 0, 0),
                               memory_space=pltpu.SMEM),
                  pl.BlockSpec(memory_space=pl.ANY),
                  pl.BlockSpec((tm, d), lambda i: (i, 0)),
                  pl.BlockSpec((tm, LANES), lambda i: (i, 0)),
                  pl.BlockSpec((1, 1, d), lambda i: (i // per_b, 0, 0)),
                  pl.BlockSpec((1, d), lambda i: (0, 0)), pl.BlockSpec((1, d), lambda i: (0, 0))],
        out_specs=pl.BlockSpec((tm, d), lambda i: (i, 0)),
        out_shape=jax.ShapeDtypeStruct((n_tok, d), F32),
        scratch_shapes=[pltpu.VMEM((2, TOP_K, tm, d), F32), pltpu.SemaphoreType.DMA((2,))],
        compiler_params=_params(1),
        name="moe_combine_norm",
    )(dest3, dest3, y_rows, h_flat, gate, g2, ln_g[None], ln_b[None])


def _moe_and_norm(h1, top_e, gate, pos, counts, sc2, sh2, g2, wgu, bgu, wd, bd, layer, ln_g, ln_b, alpha):
    b, length, d = h1.shape
    n_tok = b * length
    n_blk = n_tok * TOP_K // EXPERT_ROWS + N_EXPERTS
    padded = (counts + EXPERT_ROWS - 1) // EXPERT_ROWS * EXPERT_ROWS
    pad_end = jnp.cumsum(padded)
    offset = pad_end - padded
    n_used = (pad_end[-1:] // EXPERT_ROWS).astype(I32)
    blk_start = jnp.arange(n_blk, dtype=I32) * EXPERT_ROWS
    blk_e = jnp.minimum(jnp.sum(pad_end[None, :] <= blk_start[:, None], axis=1), N_EXPERTS - 1).astype(I32)
    dest = pos
    for e in range(N_EXPERTS):
        dest = dest + jnp.where(top_e == e, offset[e], 0)
    dest = dest.astype(I32)
    zstart = jnp.maximum(pad_end - EXPERT_ROWS, 0).astype(I32)
    zvalid = (counts > 0).astype(I32)

    h_flat = h1.reshape(n_tok, d)
    xs = _dispatch(h_flat, sc2, sh2, dest, zstart, zvalid, n_blk * EXPERT_ROWS, length)
    y_rows = _experts(xs, blk_e, n_used, wgu, bgu, wd, bd, layer)
    out = _combine(y_rows, dest, h_flat, gate, g2, ln_g, ln_b, alpha, length)
    return out.reshape(b, length, d)


def _odd_in_kernel(x_ref, sc_ref, sh_ref, w_ref, glg_ref, glb_ref, ws_ref, bs_ref, flg_ref, flb_ref, dft_ref,
                   sp_ref, zr_ref, zi_ref):
    u = (x_ref[0] * (1.0 + sc_ref[0]) + sh_ref[0]).astype(BF16)
    z = jnp.dot(u, w_ref[...], preferred_element_type=F32)
    tm = z.shape[0]
    gw = N_GROUPS * GROUP_W
    ug = jax.nn.gelu(z[:, :gw])
    vn = _layer_norm(jax.nn.gelu(z[:, gw:2 * gw]), glg_ref[...], glb_ref[...]).astype(BF16)
    f = z[:, 2 * gw:]
    for g in range(N_GROUPS):
        cols = slice(g * GROUP_W, (g + 1) * GROUP_W)
        for c0 in range(0, tm, CHUNK):
            rows = slice(c0, c0 + CHUNK)
            sv = jnp.dot(ws_ref[g], vn[rows, cols], preferred_element_type=F32) + bs_ref[:, cols]
            sp_ref[0, rows, cols] = (ug[rows, cols] * sv).astype(sp_ref.dtype)
        fn = _layer_norm(f[:, cols], flg_ref[:, cols], flb_ref[:, cols]).astype(BF16)
        zz = jnp.dot(fn, dft_ref[...], preferred_element_type=F32)
        zr_ref[0, :, cols] = zz[:, :GROUP_W]
        zi_ref[0, :, cols] = zz[:, GROUP_W:]


def _odd_in_proj(x, sc, sh, w_bf, gln_g, gln_b, ws, bs, fln_g, fln_b):
    b, length, d = x.shape
    n = w_bf.shape[1]
    gw = N_GROUPS * GROUP_W
    tm = _pick(length, (512, 256, 128))
    kk = jnp.arange(GROUP_W, dtype=I32)
    ang = (2.0 * math.pi / GROUP_W) * ((kk[:, None] * kk[None, :]) % GROUP_W).astype(F32)
    dft = jnp.concatenate([jnp.cos(ang), -jnp.sin(ang)], 1).astype(BF16)
    bs_exp = jnp.repeat(bs.T, GROUP_W, axis=1)
    row = lambda bi, i: (bi, i, 0)
    vec = lambda bi, i: (bi, 0, 0)
    const2 = lambda bi, i: (0, 0)
    return pl.pallas_call(
        _odd_in_kernel,
        grid=(b, length // tm),
        in_specs=[pl.BlockSpec((1, tm, d), row), pl.BlockSpec((1, 1, d), vec), pl.BlockSpec((1, 1, d), vec),
                  pl.BlockSpec((d, n), const2),
                  pl.BlockSpec((1, gw), const2), pl.BlockSpec((1, gw), const2),
                  pl.BlockSpec((N_GROUPS, CHUNK, CHUNK), lambda bi, i: (0, 0, 0)),
                  pl.BlockSpec((CHUNK, gw), const2),
                  pl.BlockSpec((1, gw), const2), pl.BlockSpec((1, gw), const2),
                  pl.BlockSpec((GROUP_W, 2 * GROUP_W), const2)],
        out_specs=[pl.BlockSpec((1, tm, gw), row), pl.BlockSpec((1, tm, gw), row), pl.BlockSpec((1, tm, gw), row)],
        out_shape=[jax.ShapeDtypeStruct((b, length, gw), BF16), jax.ShapeDtypeStruct((b, length, gw), F32),
                   jax.ShapeDtypeStruct((b, length, gw), F32)],
        compiler_params=_params(2),
        name="odd_in_proj",
    )(x, sc, sh, w_bf, gln_g[None], gln_b[None], ws.astype(BF16), bs_exp, fln_g[None], fln_b[None], dft)


def _fft_a_kernel(zr_ref, zi_ref, cms_ref, c_ref, s_ref, ar_ref, ai_ref):
    l1, nb, w = zr_ref.shape[1:]
    zr = zr_ref[0].reshape(l1 * nb, w)
    zi = zi_ref[0].reshape(l1 * nb, w)
    k1 = jnp.dot(cms_ref[...], zr.astype(BF16), preferred_element_type=F32)
    k2 = jnp.dot(c_ref[...], (zi - zr).astype(BF16), preferred_element_type=F32)
    k3 = jnp.dot(s_ref[...], (zr + zi).astype(BF16), preferred_element_type=F32)
    ar_ref[0] = (k1 + k3).reshape(l1, nb, w)
    ai_ref[0] = (k1 + k2).reshape(l1, nb, w)


def _fft_b_kernel(ar_ref, ai_ref, c_ref, s_ref, mask_ref, o_ref, *, norm):
    nb, l2, w = ar_ref.shape[1:]
    ar = ar_ref[0].reshape(nb * l2, w).astype(BF16)
    ai = ai_ref[0].reshape(nb * l2, w).astype(BF16)
    mask = mask_ref[...]
    c = jnp.concatenate([c_ref[0]] * nb, axis=1) * mask
    s = jnp.concatenate([s_ref[0]] * nb, axis=1) * mask
    y = jnp.dot(c, ar, preferred_element_type=F32) + jnp.dot(s, ai, preferred_element_type=F32)
    o_ref[0] = (y * norm).reshape(l2, nb, w)


def _length_dft_real(zr, zi):
    b, length, w = zr.shape
    l2 = CHUNK
    l1 = length // l2
    nb = SUBLANES
    k1 = jnp.arange(l1, dtype=I32)
    ang_a = (2.0 * math.pi / l1) * ((k1[:, None] * k1[None, :]) % l1).astype(F32)
    r = lax.broadcasted_iota(I32, (l1 * nb, l1 * nb), 0)
    c = lax.broadcasted_iota(I32, (l1 * nb, l1 * nb), 1)
    same = r % nb == c % nb
    rep = (lax.broadcasted_iota(I32, (l1 * nb, l1), 0) // nb == lax.broadcasted_iota(I32, (l1 * nb, l1), 1))
    rep = rep.astype(BF16)

    def expand_a(t):
        t = jnp.dot(jnp.dot(rep, t.astype(BF16), preferred_element_type=F32).astype(BF16), rep.T,
                    preferred_element_type=F32)
        return jnp.where(same, t, 0.0).astype(BF16)

    cos_a = expand_a(jnp.cos(ang_a))
    sin_a = expand_a(jnp.sin(ang_a))
    cms_a = expand_a(jnp.cos(ang_a) - jnp.sin(ang_a))
    shape_t = (l1 // nb, l2 * nb, l2)
    kb = lax.broadcasted_iota(I32, shape_t, 0)
    r = lax.broadcasted_iota(I32, shape_t, 1)
    n2 = lax.broadcasted_iota(I32, shape_t, 2)
    ang_b = (2.0 * math.pi / length) * ((n2 * (kb * nb + r % nb + l1 * (r // nb))) % length).astype(F32)
    cos_b = jnp.cos(ang_b).astype(BF16)
    sin_b = jnp.sin(ang_b).astype(BF16)
    shape_m = (l2 * nb, nb * l2)
    mask_b = (lax.broadcasted_iota(I32, shape_m, 0) % nb == lax.broadcasted_iota(I32, shape_m, 1) // l2)
    mask_b = mask_b.astype(BF16)

    zr4 = zr.reshape(b, l1, l2, w)
    zi4 = zi.reshape(b, l1, l2, w)
    blk_a = pl.BlockSpec((1, l1, nb, w), lambda bi, i: (bi, 0, i, 0))
    ar, ai = pl.pallas_call(
        _fft_a_kernel,
        grid=(b, l2 // nb),
        in_specs=[blk_a, blk_a] + [pl.BlockSpec((l1 * nb, l1 * nb), lambda bi, i: (0, 0))] * 3,
        out_specs=[blk_a, blk_a],
        out_shape=[jax.ShapeDtypeStruct((b, l1, l2, w), F32)] * 2,
        compiler_params=_params(2),
        name="fourier_stage_a",
    )(zr4, zi4, cms_a, cos_a, sin_a)
    blk_b = pl.BlockSpec((1, nb, l2, w), lambda bi, i: (bi, i, 0, 0))
    out = pl.pallas_call(
        functools.partial(_fft_b_kernel, norm=float((length * GROUP_W) ** -0.5)),
        grid=(b, l1 // nb),
        in_specs=[blk_b, blk_b, pl.BlockSpec((1, l2 * nb, l2), lambda bi, i: (i, 0, 0)),
                  pl.BlockSpec((1, l2 * nb, l2), lambda bi, i: (i, 0, 0)),
                  pl.BlockSpec((l2 * nb, nb * l2), lambda bi, i: (0, 0))],
        out_specs=pl.BlockSpec((1, l2, nb, w), lambda bi, i: (bi, 0, i, 0)),
        out_shape=jax.ShapeDtypeStruct((b, l2, l1, w), F32),
        compiler_params=_params(2),
        name="fourier_stage_b",
    )(ar, ai, cos_b, sin_b, mask_b)
    return out.reshape(b, length, w)


def kernel(x, c, ctx, c_ctx, w_mod, b_mod, ln1_g, ln1_b, ln2_g, ln2_b, ev_w_in, ev_w_out, conv_w, conv_b, conv_ln_g, conv_ln_b, lam_q1, lam_k1, lam_q2, lam_k2, diff_norm_g, od_w_in, od_w_out, gmlp_ln_g, gmlp_ln_b, gmlp_ws, gmlp_bs, four_ln_g, four_ln_b, router_w, router_b, w_gate_up, b_gate_up, w_down, b_down):
    b, length, d = x.shape
    depth = w_mod.shape[0]
    alpha = float((2 * depth) ** 0.25)
    assert b + 1 <= SUBLANES and length % (CHUNK * SUBLANES) == 0 and length % GRID_W == 0
    assert length % KV_TILE == 0 and ctx.shape[1] % KV_TILE == 0

    mod = _modulation(c, c_ctx, w_mod, b_mod)
    cos_t, sin_t = _rope_tables(length)
    h = x
    for layer in range(depth):
        j = layer // 2
        m = mod[layer]
        sh1, sc1, g1, sh2, sc2, g2 = [m[:b, i * d:(i + 1) * d][:, None, :] for i in range(6)]
        if layer % 2 == 0:
            lam_init = 0.8 - 0.6 * math.exp(-0.3 * layer)
            w_in = ev_w_in[j].astype(BF16)
            q, k_all, vt, a = _even_in_proj(h, sc1, sh1, w_in, cos_t, sin_t, length + ctx.shape[1])
            csh1 = jnp.broadcast_to(m[b:b + 1, 0:d][:, None, :], (b, 1, d))
            csc1 = jnp.broadcast_to(m[b:b + 1, d:2 * d][:, None, :], (b, 1, d))
            k_all, vt = _context_kv(ctx, csc1, csh1, w_in[:, QK_W:2 * QK_W + V_W], k_all, vt, length)
            att = _diff_attention(q, k_all, vt, lam_q1[j], lam_k1[j], lam_q2[j], lam_k2[j],
                                  diff_norm_g[j], lam_init)
            conv = _conformer_conv(a, conv_w[j], conv_b[j], conv_ln_g[j], conv_ln_b[j])
            ya, yb, w_out = conv, att, ev_w_out[j]
        else:
            spatial, zr, zi = _odd_in_proj(h, sc1, sh1, od_w_in[j].astype(BF16), gmlp_ln_g[j], gmlp_ln_b[j],
                                           gmlp_ws[j], gmlp_bs[j], four_ln_g[j], four_ln_b[j])
            ya, yb, w_out = spatial, _length_dft_real(zr, zi), od_w_out[j]
        h1, top_e, gate, pos, counts = _post_mix(ya, yb, w_out.astype(BF16), h, g1, ln1_g[layer], ln1_b[layer],
                                                 sc2, sh2, router_w[layer], router_b[layer], alpha)
        h = _moe_and_norm(h1, top_e[:TOP_K], gate, pos[:TOP_K], counts[0, :N_EXPERTS], sc2, sh2, g2,
                          w_gate_up, b_gate_up, w_down, b_down, layer, ln2_g[layer], ln2_b[layer], alpha)
    return h
```

```python
import functools
import math

import jax
import jax.numpy as jnp
from jax import lax
from jax.experimental import pallas as pl
from jax.experimental.pallas import tpu as pltpu

F32 = jnp.float32
BF16 = jnp.bfloat16
I32 = jnp.int32
HIGHEST = lax.Precision.HIGHEST

LN_EPS = 1e-5
GRID_W = 64
HEAD_DIM = 64
HEAD_V = 128
N_HEADS = 4
QK_W = N_HEADS * 2 * HEAD_DIM
V_W = N_HEADS * HEAD_V
CONV_W = 512
CONV_TAPS = 31
CONV_HALO = 16
ROPE_BASE = 10000.0
CHUNK = 128
N_GROUPS = 4
GROUP_W = 128
N_EXPERTS = 32
TOP_K = 4
SWIGLU_LIMIT = 7.0
SWIGLU_ALPHA = 1.702
LANES = 128
SUBLANES = 8
EXPERT_ROWS = 512
ISSUE_UNROLL = 8
NEG_BIG = -1e30
VMEM_LIMIT = 56 * 1024 * 1024


def _params(n_axes):
    return pltpu.CompilerParams(dimension_semantics=("arbitrary",) * n_axes,
                                vmem_limit_bytes=VMEM_LIMIT)


def _pick(n, candidates):
    for c in candidates:
        if n % c == 0:
            return c
    return n


def _layer_norm(r, g, b):
    mu = jnp.mean(r, axis=-1, keepdims=True)
    d = r - mu
    var = jnp.mean(d * d, axis=-1, keepdims=True)
    return d * lax.rsqrt(var + LN_EPS) * g + b


def _sigmoid(x):
    return 1.0 / (1.0 + jnp.exp(-x))


def _mod_kernel(cs_ref, w_ref, b_ref, o_ref):
    cs = cs_ref[...]
    a = cs * _sigmoid(cs)
    o_ref[0] = jnp.dot(a, w_ref[0], precision=HIGHEST, preferred_element_type=F32) + b_ref[0]


def _modulation(c, c_ctx, w_mod, b_mod):
    depth, d, n = w_mod.shape
    b = c.shape[0]
    cs = jnp.concatenate([c, c_ctx[None], jnp.zeros((SUBLANES - b - 1, d), F32)], 0)
    tn = _pick(n, (1536, 1024, 512))
    return pl.pallas_call(
        _mod_kernel,
        grid=(depth, n // tn),
        in_specs=[pl.BlockSpec((SUBLANES, d), lambda l, j: (0, 0)),
                  pl.BlockSpec((1, d, tn), lambda l, j: (l, 0, j)),
                  pl.BlockSpec((1, 1, tn), lambda l, j: (l, 0, j))],
        out_specs=pl.BlockSpec((1, SUBLANES, tn), lambda l, j: (l, 0, j)),
        out_shape=jax.ShapeDtypeStruct((depth, SUBLANES, n), F32),
        compiler_params=_params(2),
        name="modulation",
    )(cs, w_mod, b_mod.reshape(depth, 1, n))


KV_TILE = 256
VT_ROWS = HEAD_V + 16


def _store_values_transposed(v, vt_ref):
    extra = (lax.broadcasted_iota(I32, (VT_ROWS - HEAD_V, v.shape[0]), 0) == 0).astype(BF16)
    for h in range(N_HEADS):
        vt_ref[0, h, 0, :HEAD_V, :] = v[:, h * HEAD_V:(h + 1) * HEAD_V].T.astype(BF16)
        vt_ref[0, h, 0, HEAD_V:, :] = extra


def _even_in_kernel(x_ref, sc_ref, sh_ref, w_ref, cos_ref, sin_ref, q_ref, k_ref, vt_ref, a_ref):
    u = (x_ref[0] * (1.0 + sc_ref[0]) + sh_ref[0]).astype(BF16)
    z = jnp.dot(u, w_ref[...], preferred_element_type=F32)
    cosv = cos_ref[...]
    sinv = sin_ref[...]
    lane = lax.broadcasted_iota(I32, cosv.shape, 1)
    first = (lane % 32) < 16

    def rope(t):
        partner = jnp.where(first, pltpu.roll(t, LANES - 16, 1), pltpu.roll(t, 16, 1))
        return t * cosv + partner * sinv

    for j in range(QK_W // LANES):
        sl = slice(j * LANES, (j + 1) * LANES)
        rq = rope(z[:, sl]) * (HEAD_DIM ** -0.5 * math.log2(math.e))
        for c in range(2):
            q_ref[c, 0, :, sl] = jnp.where(lane // HEAD_DIM == c, rq, 0.0).astype(BF16)
        k_ref[0, :, sl] = rope(z[:, QK_W + j * LANES:QK_W + (j + 1) * LANES]).astype(BF16)
    _store_values_transposed(z[:, 2 * QK_W:2 * QK_W + V_W], vt_ref)
    a0 = 2 * QK_W + V_W
    a_ref[0] = z[:, a0:a0 + CONV_W] * _sigmoid(z[:, a0 + CONV_W:a0 + 2 * CONV_W])


def _rope_tables(length):
    rows = length // GRID_W
    row = jnp.repeat(jnp.arange(rows, dtype=F32), GRID_W)
    col = jnp.tile(jnp.arange(GRID_W, dtype=F32), rows)
    n_freq = HEAD_DIM // 4
    inv_freq = ROPE_BASE ** (-jnp.arange(n_freq, dtype=F32) / n_freq)
    ar = row[:, None] * inv_freq
    ac = col[:, None] * inv_freq
    cos64 = jnp.concatenate([jnp.cos(ar), jnp.cos(ar), jnp.cos(ac), jnp.cos(ac)], 1)
    sin64 = jnp.concatenate([-jnp.sin(ar), jnp.sin(ar), -jnp.sin(ac), jnp.sin(ac)], 1)
    return jnp.tile(cos64, (1, LANES // HEAD_DIM)), jnp.tile(sin64, (1, LANES // HEAD_DIM))


def _key_chunk(n_keys):
    return next(c for c in (1280, 256) if n_keys % c == 0 and n_keys // c >= 3)


def _kv_specs(b, n_keys, kc, first_tile):
    per_chunk = kc // KV_TILE
    k_spec = pl.BlockSpec((1, KV_TILE, QK_W), lambda bi, i: (bi, first_tile + i, 0))
    vt_spec = pl.BlockSpec((1, N_HEADS, 1, VT_ROWS, KV_TILE),
                           lambda bi, i: (bi, 0, (first_tile + i) // per_chunk, 0, (first_tile + i) % per_chunk))
    shapes = [jax.ShapeDtypeStruct((b, n_keys, QK_W), BF16),
              jax.ShapeDtypeStruct((b, N_HEADS, n_keys // kc, VT_ROWS, kc), BF16)]
    return k_spec, vt_spec, shapes


def _even_in_proj(x, sc, sh, w_bf, cos_t, sin_t, n_keys):
    b, length, d = x.shape
    n = w_bf.shape[1]
    tm = KV_TILE
    row = lambda bi, i: (bi, i, 0)
    vec = lambda bi, i: (bi, 0, 0)
    k_spec, vt_spec, kv_shapes = _kv_specs(b, n_keys, _key_chunk(n_keys), 0)
    return pl.pallas_call(
        _even_in_kernel,
        grid=(b, length // tm),
        in_specs=[pl.BlockSpec((1, tm, d), row),
                  pl.BlockSpec((1, 1, d), vec),
                  pl.BlockSpec((1, 1, d), vec),
                  pl.BlockSpec((d, n), lambda bi, i: (0, 0)),
                  pl.BlockSpec((tm, LANES), lambda bi, i: (i, 0)),
                  pl.BlockSpec((tm, LANES), lambda bi, i: (i, 0))],
        out_specs=[pl.BlockSpec((2, 1, tm, QK_W), lambda bi, i: (0, bi, i, 0)), k_spec, vt_spec,
                   pl.BlockSpec((1, tm, CONV_W), row)],
        out_shape=[jax.ShapeDtypeStruct((2, b, length, QK_W), BF16)] + kv_shapes
                  + [jax.ShapeDtypeStruct((b, length, CONV_W), F32)],
        compiler_params=_params(2),
        name="even_in_proj",
    )(x, sc, sh, w_bf, cos_t, sin_t)


def _ctx_kv_kernel(x_ref, sc_ref, sh_ref, w_ref, k_in, vt_in, k_ref, vt_ref):
    del k_in, vt_in
    u = (x_ref[0] * (1.0 + sc_ref[0]) + sh_ref[0]).astype(BF16)
    z = jnp.dot(u, w_ref[...], preferred_element_type=F32)
    k_ref[0] = z[:, :QK_W].astype(BF16)
    _store_values_transposed(z[:, QK_W:], vt_ref)


def _context_kv(ctx, sc, sh, w_bf, k_all, vt, first_key):
    b, rows, d = ctx.shape
    n = w_bf.shape[1]
    n_keys = k_all.shape[1]
    k_spec, vt_spec, kv_shapes = _kv_specs(b, n_keys, vt.shape[4], first_key // KV_TILE)
    return pl.pallas_call(
        _ctx_kv_kernel,
        grid=(b, rows // KV_TILE),
        in_specs=[pl.BlockSpec((1, KV_TILE, d), lambda bi, i: (bi, i, 0)),
                  pl.BlockSpec((1, 1, d), lambda bi, i: (bi, 0, 0)),
                  pl.BlockSpec((1, 1, d), lambda bi, i: (bi, 0, 0)),
                  pl.BlockSpec((d, n), lambda bi, i: (0, 0)),
                  pl.BlockSpec(memory_space=pl.ANY), pl.BlockSpec(memory_space=pl.ANY)],
        out_specs=[k_spec, vt_spec],
        out_shape=kv_shapes,
        input_output_aliases={4: 0, 5: 1},
        compiler_params=_params(2),
        name="context_kv_proj",
    )(ctx, sc, sh, w_bf, k_all, vt)


def _attn_kernel(lq1_ref, lk1_ref, lq2_ref, lk2_ref, g_ref, q_ref, k_ref, vt_ref, o_ref,
                 s0_sc, s1_sc, p0_sc, p1_sc, acc_sc, *, lam_init, kc):
    mq = q_ref.shape[2]
    n = k_ref.shape[1] // kc
    s_sc = (s0_sc, s1_sc)
    p_sc = (p0_sc, p1_sc)

    def scores(j, slot):
        kj = k_ref[0, pl.ds(pl.multiple_of(j * kc, kc), kc), :]
        for c in range(2):
            s_sc[slot][c] = lax.dot_general(kj, q_ref[c, 0], (((1,), (1,)), ((), ())),
                                            preferred_element_type=F32)

    def softmax(slot, m):
        m_out, alpha = [], []
        for c in range(2):
            s = s_sc[slot][c]
            m_new = jnp.maximum(m[c], jnp.max(s, axis=0, keepdims=True))
            p_sc[slot][c] = jnp.exp2((s - m_new).astype(BF16))
            m_out.append(m_new)
            alpha.append(jnp.exp2(m[c] - m_new))
        return tuple(m_out), tuple(alpha)

    def accumulate(j, slot, alpha):
        vt = vt_ref[0, 0, j]
        for c in range(2):
            acc_sc[c] = alpha[c] * acc_sc[c] + jnp.dot(vt, p_sc[slot][c], preferred_element_type=F32)

    def step(j, slot, m, alpha):
        scores(j + 2, slot)
        m, alpha_next = softmax(1 - slot, m)
        accumulate(j, slot, alpha)
        return m, alpha_next

    acc_sc[...] = jnp.zeros(acc_sc.shape, F32)
    m = (jnp.full((1, mq), NEG_BIG, F32),) * 2
    scores(0, 0)
    m, alpha = softmax(0, m)
    scores(1, 1)

    def pair(t, carry):
        m, alpha = step(2 * t, 0, *carry)
        return step(2 * t + 1, 1, m, alpha)

    n_steps = n - 2
    m, alpha = lax.fori_loop(0, n_steps // 2, pair, (m, alpha))
    if n_steps % 2:
        m, alpha = step(n_steps - 1, 0, m, alpha)
    m, alpha_last = softmax((n - 1) % 2, m)
    accumulate(n - 2, (n - 2) % 2, alpha)
    accumulate(n - 1, (n - 1) % 2, alpha_last)
    acc0 = acc_sc[0]
    acc1 = acc_sc[1]
    lam = (jnp.exp(jnp.sum(lq1_ref[...] * lk1_ref[...], keepdims=True))
           - jnp.exp(jnp.sum(lq2_ref[...] * lk2_ref[...], keepdims=True)) + lam_init)
    o = (acc0[:HEAD_V] / acc0[HEAD_V:HEAD_V + 1]
         - lam * (acc1[:HEAD_V] / acc1[HEAD_V:HEAD_V + 1]))
    ms = jnp.mean(o * o, axis=0, keepdims=True)
    o = o * lax.rsqrt(ms + LN_EPS) * g_ref[...] * (1.0 - lam_init)
    o_ref[0] = o.T.astype(o_ref.dtype)


def _diff_attention(q, k_all, vt, lq1, lk1, lq2, lk2, norm_g, lam_init):
    _, b, length, _ = q.shape
    n_keys = k_all.shape[1]
    mq = _pick(length, (1024, 512, 256, 128))
    nc, kc = vt.shape[2], vt.shape[4]
    small = lambda bi, h, i: (0, 0)
    return pl.pallas_call(
        functools.partial(_attn_kernel, lam_init=lam_init, kc=kc),
        grid=(b, N_HEADS, length // mq),
        in_specs=[pl.BlockSpec((1, HEAD_DIM), small), pl.BlockSpec((1, HEAD_DIM), small),
                  pl.BlockSpec((1, HEAD_DIM), small), pl.BlockSpec((1, HEAD_DIM), small),
                  pl.BlockSpec((HEAD_V, 1), small),
                  pl.BlockSpec((2, 1, mq, HEAD_V), lambda bi, h, i: (0, bi, i, h)),
                  pl.BlockSpec((1, n_keys, HEAD_V), lambda bi, h, i: (bi, 0, h)),
                  pl.BlockSpec((1, 1, nc, VT_ROWS, kc), lambda bi, h, i: (bi, h, 0, 0, 0))],
        out_specs=pl.BlockSpec((1, mq, HEAD_V), lambda bi, h, i: (bi, i, h)),
        out_shape=jax.ShapeDtypeStruct((b, length, V_W), BF16),
        scratch_shapes=[pltpu.VMEM((2, kc, mq), F32), pltpu.VMEM((2, kc, mq), F32),
                        pltpu.VMEM((2, kc, mq), BF16), pltpu.VMEM((2, kc, mq), BF16),
                        pltpu.VMEM((2, VT_ROWS, mq), F32)],
        compiler_params=_params(3),
        name="diff_attention",
    )(lq1[None], lk1[None], lq2[None], lk2[None], norm_g[:, None], q, k_all, vt)


def _conv_kernel(prev_ref, cur_ref, next_ref, w_ref, b_ref, g_ref, beta_ref, o_ref, win, *, row_chunk):
    i = pl.program_id(1)
    tc = cur_ref.shape[1]
    span = tc + 2 * CONV_HALO - SUBLANES
    win[0, CONV_HALO:CONV_HALO + tc, :] = cur_ref[0]
    win[0, 0:CONV_HALO, :] = jnp.where(i > 0, prev_ref[0], 0.0)
    win[0, CONV_HALO + tc:, :] = jnp.where(i < pl.num_programs(1) - 1, next_ref[0], 0.0)
    for s in range(1, SUBLANES):
        win[s, 0:span, :] = win[0, s:s + span, :]
    first_tap = CONV_HALO - CONV_TAPS // 2
    for c0 in range(0, tc, row_chunk):
        acc = jnp.zeros((row_chunk, CONV_W), F32)
        for t in range(CONV_TAPS):
            s = (first_tap + t) % SUBLANES
            r0 = c0 + first_tap + t - s
            acc = acc + w_ref[t:t + 1, :] * win[s, r0:r0 + row_chunk, :]
        y = _layer_norm(acc + b_ref[...], g_ref[...], beta_ref[...])
        o_ref[0, c0:c0 + row_chunk, :] = (y * _sigmoid(y)).astype(o_ref.dtype)


def _conformer_conv(a, conv_w, conv_b, ln_g, ln_b):
    b, length, w = a.shape
    tc = _pick(length, (512, 256, 128))
    hb = tc // CONV_HALO
    n_halo = length // CONV_HALO
    w_pad = jnp.concatenate([conv_w, jnp.zeros((1, w), F32)], 0)
    vec = lambda bi, i: (0, 0)
    return pl.pallas_call(
        functools.partial(_conv_kernel, row_chunk=64),
        grid=(b, length // tc),
        in_specs=[pl.BlockSpec((1, CONV_HALO, w), lambda bi, i: (bi, jnp.maximum(i * hb - 1, 0), 0)),
                  pl.BlockSpec((1, tc, w), lambda bi, i: (bi, i, 0)),
                  pl.BlockSpec((1, CONV_HALO, w), lambda bi, i: (bi, jnp.minimum((i + 1) * hb, n_halo - 1), 0)),
                  pl.BlockSpec((CONV_TAPS + 1, w), vec),
                  pl.BlockSpec((1, w), vec), pl.BlockSpec((1, w), vec), pl.BlockSpec((1, w), vec)],
        out_specs=pl.BlockSpec((1, tc, w), lambda bi, i: (bi, i, 0)),
        out_shape=jax.ShapeDtypeStruct((b, length, w), BF16),
        scratch_shapes=[pltpu.VMEM((SUBLANES, tc + 2 * CONV_HALO, w), F32)],
        compiler_params=_params(2),
        name="conformer_conv",
    )(a, a, a, w_pad, conv_b[None], ln_g[None], ln_b[None])


def _post_mix_kernel(ya_ref, yb_ref, wa_ref, wb_ref, x_ref, g1_ref, lng_ref, lnb_ref, sc_ref, sh_ref,
                     rw_ref, rb_ref, h_ref, e_ref, gate_ref, pos_ref, cnt_ref, carry, *, alpha):
    first_step = jnp.logical_and(pl.program_id(0) == 0, pl.program_id(1) == 0)

    @pl.when(first_step)
    def _():
        carry[...] = jnp.zeros(carry.shape, F32)

    y = (jnp.dot(ya_ref[0].astype(BF16), wa_ref[...], preferred_element_type=F32)
         + jnp.dot(yb_ref[0].astype(BF16), wb_ref[...], preferred_element_type=F32))
    h = _layer_norm(alpha * x_ref[0] + g1_ref[0] * y, lng_ref[...], lnb_ref[...])
    h_ref[0] = h
    u2 = h * (1.0 + sc_ref[0]) + sh_ref[0]
    u_hi = u2.astype(BF16)
    u_lo = (u2 - u_hi.astype(F32)).astype(BF16)
    hh_hl = jnp.dot(u_hi, rw_ref[...], preferred_element_type=F32)
    lh = jnp.dot(u_lo, rw_ref[:, :LANES], preferred_element_type=F32)
    logits = hh_hl[:, :LANES] + hh_hl[:, LANES:] + lh + rb_ref[...]

    tm = logits.shape[0]
    lane = lax.broadcasted_iota(I32, logits.shape, 1).astype(F32)
    vals, idxs = [], []
    lg = logits
    for _ in range(TOP_K):
        mx = jnp.max(lg, axis=1, keepdims=True)
        idx = jnp.min(jnp.where(lg == mx, lane, float(LANES)), axis=1, keepdims=True)
        vals.append(mx)
        idxs.append(idx)
        lg = jnp.where(lane == idx, -3e38, lg)
    exps = [jnp.exp(v - vals[0]) for v in vals]
    den = exps[0] + exps[1] + exps[2] + exps[3]

    hot = jnp.zeros(logits.shape, F32)
    for idx in idxs:
        hot = hot + (lane == idx).astype(F32)
    r_i = lax.broadcasted_iota(I32, (tm, tm), 0)
    c_i = lax.broadcasted_iota(I32, (tm, tm), 1)
    tri = (c_i < r_i).astype(BF16)
    before = jnp.dot(tri, hot.astype(BF16), preferred_element_type=F32) + carry[...]
    e_out = jnp.zeros(logits.shape, F32)
    g_out = jnp.zeros(logits.shape, F32)
    p_out = jnp.zeros(logits.shape, F32)
    for j in range(TOP_K):
        pos_j = jnp.sum(jnp.where(lane == idxs[j], before, 0.0), axis=1, keepdims=True)
        sel = lane == float(j)
        e_out = jnp.where(sel, idxs[j], e_out)
        g_out = jnp.where(sel, exps[j] / den, g_out)
        p_out = jnp.where(sel, pos_j, p_out)
    e_ref[...] = e_out.T[:SUBLANES].astype(I32)
    gate_ref[...] = g_out
    pos_ref[...] = p_out.T[:SUBLANES].astype(I32)
    new_carry = carry[...] + jnp.sum(hot, axis=0, keepdims=True)
    carry[...] = new_carry
    cnt_ref[...] = jnp.broadcast_to(new_carry, cnt_ref.shape).astype(I32)


def _post_mix(ya, yb, w_out_bf, x, g1, ln_g, ln_b, sc2, sh2, router_w, router_b, alpha):
    b, length, d = x.shape
    wa = ya.shape[2]
    tm = _pick(length, (512, 256, 128))
    n_tok = b * length
    rw = jnp.concatenate([router_w, jnp.zeros((d, LANES - N_EXPERTS), F32)], 1)
    rw_hi = rw.astype(BF16)
    rw = jnp.concatenate([rw_hi, (rw - rw_hi.astype(F32)).astype(BF16)], 1)
    rb =jnp.concatenate([router_b, jnp.full((LANES - N_EXPERTS,), NEG_BIG, F32)])[None]
    row = lambda bi, i: (bi, i, 0)
    vec = lambda bi, i: (bi, 0, 0)
    const = lambda bi, i: (0, 0)
    nt = length // tm
    tok = lambda bi, i: (bi * nt + i, 0)
    return pl.pallas_call(
        functools.partial(_post_mix_kernel, alpha=alpha),
        grid=(b, nt),
        in_specs=[pl.BlockSpec((1, tm, wa), row), pl.BlockSpec((1, tm, d - wa), row),
                  pl.BlockSpec((wa, d), const), pl.BlockSpec((d - wa, d), const),
                  pl.BlockSpec((1, tm, d), row), pl.BlockSpec((1, 1, d), vec),
                  pl.BlockSpec((1, d), const), pl.BlockSpec((1, d), const),
                  pl.BlockSpec((1, 1, d), vec), pl.BlockSpec((1, 1, d), vec),
                  pl.BlockSpec((d, 2 * LANES), const), pl.BlockSpec((1, LANES), const)],
        out_specs=[pl.BlockSpec((1, tm, d), row), pl.BlockSpec((SUBLANES, tm), lambda bi, i: (0, bi * nt + i)),
                   pl.BlockSpec((tm, LANES), tok), pl.BlockSpec((SUBLANES, tm), lambda bi, i: (0, bi * nt + i)),
                   pl.BlockSpec((SUBLANES, LANES), const)],
        out_shape=[jax.ShapeDtypeStruct((b, length, d), F32), jax.ShapeDtypeStruct((SUBLANES, n_tok), I32),
                   jax.ShapeDtypeStruct((n_tok, LANES), F32), jax.ShapeDtypeStruct((SUBLANES, n_tok), I32),
                   jax.ShapeDtypeStruct((SUBLANES, LANES), I32)],
        scratch_shapes=[pltpu.VMEM((1, LANES), F32)],
        compiler_params=_params(2),
        name="out_proj_norm_router",
    )(ya, yb, w_out_bf[:wa], w_out_bf[wa:], x, g1, ln_g[None], ln_b[None], sc2, sh2, rw, rb)


def _tile_indices(dest, tm):
    n_tok = dest.shape[1]
    return dest.reshape(TOP_K, n_tok // tm, tm).transpose(1, 0, 2).reshape(n_tok // tm, 1, TOP_K * tm)


def _dispatch_kernel(zstart_ref, zvalid_ref, dest_ref, h_ref, sc_ref, sh_ref, xs_ref, u_sc, zero_sc, sem, zsem):
    tm = h_ref.shape[0]

    @pl.when(pl.program_id(0) == 0)
    def _():
        zero_sc[...] = jnp.zeros(zero_sc.shape, F32)

        def fill(e, c):
            @pl.when(zvalid_ref[e] > 0)
            def _():
                start = pl.multiple_of(zstart_ref[e], SUBLANES)
                pltpu.make_async_copy(zero_sc, xs_ref.at[pl.ds(start, EXPERT_ROWS)], zsem).start()
            return c

        lax.fori_loop(0, N_EXPERTS, fill, 0)

        def drain(e, c):
            @pl.when(zvalid_ref[e] > 0)
            def _():
                pltpu.make_async_copy(zero_sc, xs_ref.at[pl.ds(0, EXPERT_ROWS)], zsem).wait()
            return c

        lax.fori_loop(0, N_EXPERTS, drain, 0)

    i = pl.program_id(0)
    slot = i % 2
    src = u_sc.at[slot]
    src[...] = h_ref[...] * (1.0 + sc_ref[0]) + sh_ref[0]

    def issue(g, c):
        for rr in range(ISSUE_UNROLL):
            r = g * ISSUE_UNROLL + rr
            for j in range(TOP_K):
                d = dest_ref[0, 0, j * tm + r]
                pltpu.make_async_copy(src.at[pl.ds(r, 1)], xs_ref.at[pl.ds(d, 1)], sem.at[slot]).start()
        return c

    lax.fori_loop(0, tm // ISSUE_UNROLL, issue, 0)

    def wait_tile(which):
        rows = xs_ref.at[pl.ds(0, tm * TOP_K)]
        pltpu.make_async_copy(rows, rows, sem.at[which]).wait()

    pl.when(i > 0)(lambda: wait_tile(1 - slot))
    pl.when(i == pl.num_programs(0) - 1)(lambda: wait_tile(slot))


def _dispatch(h_flat, sc2, sh2, dest, zstart, zvalid, n_rows, tokens_per_batch):
    n_tok, d = h_flat.shape
    tm = _pick(tokens_per_batch, (512, 256, 128))
    per_b = tokens_per_batch // tm
    return pl.pallas_call(
        _dispatch_kernel,
        grid_spec=pltpu.PrefetchScalarGridSpec(
            num_scalar_prefetch=2,
            grid=(n_tok // tm,),
            in_specs=[pl.BlockSpec((1, 1, tm * TOP_K), lambda i, zs, zv: (i, 0, 0), memory_space=pltpu.SMEM),
                      pl.BlockSpec((tm, d), lambda i, zs, zv: (i, 0)),
                      pl.BlockSpec((1, 1, d), lambda i, zs, zv: (i // per_b, 0, 0)),
                      pl.BlockSpec((1, 1, d), lambda i, zs, zv: (i // per_b, 0, 0))],
            out_specs=pl.BlockSpec(memory_space=pl.ANY),
            scratch_shapes=[pltpu.VMEM((2, tm, d), F32), pltpu.VMEM((EXPERT_ROWS, d), F32),
                            pltpu.SemaphoreType.DMA((2,)), pltpu.SemaphoreType.DMA(())]),
        out_shape=jax.ShapeDtypeStruct((n_rows, d), F32),
        compiler_params=_params(1),
        name="moe_dispatch",
    )(zstart, zvalid, _tile_indices(dest, tm), h_flat, sc2, sh2)


def _expert_kernel(blk_e_ref, n_used_ref, x_ref, wgu_ref, bgu_ref, wd_ref, bd_ref, o_ref, wgu_bf, wd_bf):
    i = pl.program_id(0)
    new_expert = jnp.logical_or(i == 0, blk_e_ref[i] != blk_e_ref[jnp.maximum(i - 1, 0)])

    @pl.when(new_expert)
    def _():
        wgu_bf[...] = wgu_ref[0].astype(BF16)
        wd_bf[...] = wd_ref[0].astype(BF16)

    @pl.when(i < n_used_ref[0])
    def _():
        d_e = wd_ref.shape[1]
        gu = jnp.dot(x_ref[...].astype(BF16), wgu_bf[...], preferred_element_type=F32) + bgu_ref[0]
        x_glu = jnp.minimum(gu[:, :d_e], SWIGLU_LIMIT)
        x_lin = jnp.clip(gu[:, d_e:], -SWIGLU_LIMIT, SWIGLU_LIMIT)
        act = x_glu * _sigmoid(SWIGLU_ALPHA * x_glu) * (x_lin + 1.0)
        o_ref[...] = jnp.dot(act.astype(BF16), wd_bf[...], preferred_element_type=F32) + bd_ref[0]


def _experts(xs, blk_e, n_used, wgu, bgu, wd, bd, layer):
    n_rows, d = xs.shape
    depth, n_e, _, n_gu = wgu.shape
    n_blk = n_rows // EXPERT_ROWS
    rows = lambda i, be, nu: (jnp.minimum(i, nu[0] - 1), 0)
    first = layer * n_e
    by_e = lambda i, be, nu: (first + be[i], 0, 0)
    n_e = depth * n_e
    wgu = wgu.reshape(n_e, d, n_gu)
    wd = wd.reshape(n_e, n_gu // 2, d)
    return pl.pallas_call(
        _expert_kernel,
        grid_spec=pltpu.PrefetchScalarGridSpec(
            num_scalar_prefetch=2,
            grid=(n_blk,),
            in_specs=[pl.BlockSpec((EXPERT_ROWS, d), rows),
                      pl.BlockSpec((1, d, n_gu), by_e), pl.BlockSpec((1, 1, n_gu), by_e),
                      pl.BlockSpec((1, n_gu // 2, d), by_e), pl.BlockSpec((1, 1, d), by_e)],
            out_specs=pl.BlockSpec((EXPERT_ROWS, d), rows),
            scratch_shapes=[pltpu.VMEM((d, n_gu), BF16), pltpu.VMEM((n_gu // 2, d), BF16)]),
        out_shape=jax.ShapeDtypeStruct((n_rows, d), F32),
        compiler_params=_params(1),
        name="moe_experts",
    )(blk_e, n_used, xs, wgu, bgu.reshape(n_e, 1, n_gu), wd, bd.reshape(n_e, 1, d))


def _combine_kernel(dest_ref, dest_next_ref, y_ref, h_ref, gate_ref, g2_ref, lng_ref, lnb_ref, o_ref,
                    buf, sem, *, alpha):
    i = pl.program_id(0)
    tm = h_ref.shape[0]

    def fetch(idx_ref, slot):
        def issue(g, c):
            for rr in range(ISSUE_UNROLL):
                r = g * ISSUE_UNROLL + rr
                for j in range(TOP_K):
                    d = idx_ref[0, 0, j * tm + r]
                    pltpu.make_async_copy(y_ref.at[pl.ds(d, 1)], buf.at[slot, j, pl.ds(r, 1)], sem.at[slot]).start()
            return c

        lax.fori_loop(0, tm // ISSUE_UNROLL, issue, 0)

    def fetch_unrolled(idx_ref, slot):
        for r in range(tm):
            for j in range(TOP_K):
                d = idx_ref[0, 0, j * tm + r]
                pltpu.make_async_copy(y_ref.at[pl.ds(d, 1)], buf.at[slot, j, pl.ds(r, 1)], sem.at[slot]).start()

    slot = i % 2
    more = i + 1 < pl.num_programs(0)

    @pl.when(i == 0)
    def _():
        fetch(dest_ref, 0)

    pl.when(jnp.logical_and(more, slot == 0))(lambda: fetch_unrolled(dest_next_ref, 1))
    pl.when(jnp.logical_and(more, slot == 1))(lambda: fetch_unrolled(dest_next_ref, 0))

    pltpu.make_async_copy(buf.at[slot], buf.at[slot], sem.at[slot]).wait()
    gate = gate_ref[...]
    y2 = gate[:, 0:1] * buf[slot, 0]
    for j in range(1, TOP_K):
        y2 = y2 + gate[:, j:j + 1] * buf[slot, j]
    o_ref[...] = _layer_norm(alpha * h_ref[...] + g2_ref[0] * y2, lng_ref[...], lnb_ref[...])


def _combine(y_rows, dest, h_flat, gate, g2, ln_g, ln_b, alpha, tokens_per_batch):
    n_tok, d = h_flat.shape
    tm = _pick(tokens_per_batch, (256, 128))
    per_b = tokens_per_batch // tm
    n_tiles = n_tok // tm
    dest3 = _tile_indices(dest, tm)
    return pl.pallas_call(
        functools.partial(_combine_kernel, alpha=alpha),
        grid=(n_tiles,),
        in_specs=[pl.BlockSpec((1, 1, tm * TOP_K), lambda i: (i, 0, 0), memory_space=pltpu.SMEM),
                  pl.BlockSpec((1, 1, tm * TOP_K), lambda i: (jnp.minimum(i + 1, n_tiles - 1), 0, 0),
                               memory_space=pltpu.SMEM),
                  pl.BlockSpec(memory_space=pl.ANY),
                  pl.BlockSpec((tm, d), lambda i: (i, 0)),
                  pl.BlockSpec((tm, LANES), lambda i: (i, 0)),
                  pl.BlockSpec((1, 1, d), lambda i: (i // per_b, 0, 0)),
                  pl.BlockSpec((1, d), lambda i: (0, 0)), pl.BlockSpec((1, d), lambda i: (0, 0))],
        out_specs=pl.BlockSpec((tm, d), lambda i: (i, 0)),
        out_shape=jax.ShapeDtypeStruct((n_tok, d), F32),
        scratch_shapes=[pltpu.VMEM((2, TOP_K, tm, d), F32), pltpu.SemaphoreType.DMA((2,))],
        compiler_params=_params(1),
        name="moe_combine_norm",
    )(dest3, dest3, y_rows, h_flat, gate, g2, ln_g[None], ln_b[None])


def _moe_and_norm(h1, top_e, gate, pos, counts, sc2, sh2, g2, wgu, bgu, wd, bd, layer, ln_g, ln_b, alpha):
    b, length, d = h1.shape
    n_tok = b * length
    n_blk = n_tok * TOP_K // EXPERT_ROWS + N_EXPERTS
    padded = (counts + EXPERT_ROWS - 1) // EXPERT_ROWS * EXPERT_ROWS
    pad_end = jnp.cumsum(padded)
    offset = pad_end - padded
    n_used = (pad_end[-1:] // EXPERT_ROWS).astype(I32)
    blk_start = jnp.arange(n_blk, dtype=I32) * EXPERT_ROWS
    blk_e = jnp.minimum(jnp.sum(pad_end[None, :] <= blk_start[:, None], axis=1), N_EXPERTS - 1).astype(I32)
    dest = pos
    for e in range(N_EXPERTS):
        dest = dest + jnp.where(top_e == e, offset[e], 0)
    dest = dest.astype(I32)
    zstart = jnp.maximum(pad_end - EXPERT_ROWS, 0).astype(I32)
    zvalid = (counts > 0).astype(I32)

    h_flat = h1.reshape(n_tok, d)
    xs = _dispatch(h_flat, sc2, sh2, dest, zstart, zvalid, n_blk * EXPERT_ROWS, length)
    y_rows = _experts(xs, blk_e, n_used, wgu, bgu, wd, bd, layer)
    out = _combine(y_rows, dest, h_flat, gate, g2, ln_g, ln_b, alpha, length)
    return out.reshape(b, length, d)


def _odd_in_kernel(x_ref, sc_ref, sh_ref, w_ref, glg_ref, glb_ref, ws_ref, bs_ref, flg_ref, flb_ref, dft_ref,
                   sp_ref, zr_ref, zi_ref):
    u = (x_ref[0] * (1.0 + sc_ref[0]) + sh_ref[0]).astype(BF16)
    z = jnp.dot(u, w_ref[...], preferred_element_type=F32)
    tm = z.shape[0]
    gw = N_GROUPS * GROUP_W
    ug = jax.nn.gelu(z[:, :gw])
    vn = _layer_norm(jax.nn.gelu(z[:, gw:2 * gw]), glg_ref[...], glb_ref[...]).astype(BF16)
    f = z[:, 2 * gw:]
    for g in range(N_GROUPS):
        cols = slice(g * GROUP_W, (g + 1) * GROUP_W)
        for c0 in range(0, tm, CHUNK):
            rows = slice(c0, c0 + CHUNK)
            sv = jnp.dot(ws_ref[g], vn[rows, cols], preferred_element_type=F32) + bs_ref[:, cols]
            sp_ref[0, rows, cols] = (ug[rows, cols] * sv).astype(sp_ref.dtype)
        fn = _layer_norm(f[:, cols], flg_ref[:, cols], flb_ref[:, cols]).astype(BF16)
        zz = jnp.dot(fn, dft_ref[...], preferred_element_type=F32)
        zr_ref[0, :, cols] = zz[:, :GROUP_W]
        zi_ref[0, :, cols] = zz[:, GROUP_W:]


def _odd_in_proj(x, sc, sh, w_bf, gln_g, gln_b, ws, bs, fln_g, fln_b):
    b, length, d = x.shape
    n = w_bf.shape[1]
    gw = N_GROUPS * GROUP_W
    tm = _pick(length, (512, 256, 128))
    kk = jnp.arange(GROUP_W, dtype=I32)
    ang = (2.0 * math.pi / GROUP_W) * ((kk[:, None] * kk[None, :]) % GROUP_W).astype(F32)
    dft = jnp.concatenate([jnp.cos(ang), -jnp.sin(ang)], 1).astype(BF16)
    bs_exp = jnp.repeat(bs.T, GROUP_W, axis=1)
    row = lambda bi, i: (bi, i, 0)
    vec = lambda bi, i: (bi, 0, 0)
    const2 = lambda bi, i: (0, 0)
    return pl.pallas_call(
        _odd_in_kernel,
        grid=(b, length // tm),
        in_specs=[pl.BlockSpec((1, tm, d), row), pl.BlockSpec((1, 1, d), vec), pl.BlockSpec((1, 1, d), vec),
                  pl.BlockSpec((d, n), const2),
                  pl.BlockSpec((1, gw), const2), pl.BlockSpec((1, gw), const2),
                  pl.BlockSpec((N_GROUPS, CHUNK, CHUNK), lambda bi, i: (0, 0, 0)),
                  pl.BlockSpec((CHUNK, gw), const2),
                  pl.BlockSpec((1, gw), const2), pl.BlockSpec((1, gw), const2),
                  pl.BlockSpec((GROUP_W, 2 * GROUP_W), const2)],
        out_specs=[pl.BlockSpec((1, tm, gw), row), pl.BlockSpec((1, tm, gw), row), pl.BlockSpec((1, tm, gw), row)],
        out_shape=[jax.ShapeDtypeStruct((b, length, gw), BF16), jax.ShapeDtypeStruct((b, length, gw), F32),
                   jax.ShapeDtypeStruct((b, length, gw), F32)],
        compiler_params=_params(2),
        name="odd_in_proj",
    )(x, sc, sh, w_bf, gln_g[None], gln_b[None], ws.astype(BF16), bs_exp, fln_g[None], fln_b[None], dft)


def _fft_a_kernel(zr_ref, zi_ref, cms_ref, c_ref, s_ref, ar_ref, ai_ref):
    l1, nb, w = zr_ref.shape[1:]
    zr = zr_ref[0].reshape(l1 * nb, w)
    zi = zi_ref[0].reshape(l1 * nb, w)
    k1 = jnp.dot(cms_ref[...], zr.astype(BF16), preferred_element_type=F32)
    k2 = jnp.dot(c_ref[...], (zi - zr).astype(BF16), preferred_element_type=F32)
    k3 = jnp.dot(s_ref[...], (zr + zi).astype(BF16), preferred_element_type=F32)
    ar_ref[0] = (k1 + k3).reshape(l1, nb, w)
    ai_ref[0] = (k1 + k2).reshape(l1, nb, w)


def _fft_b_kernel(ar_ref, ai_ref, c_ref, s_ref, mask_ref, o_ref, *, norm):
    nb, l2, w = ar_ref.shape[1:]
    ar = ar_ref[0].reshape(nb * l2, w).astype(BF16)
    ai = ai_ref[0].reshape(nb * l2, w).astype(BF16)
    mask = mask_ref[...]
    c = jnp.concatenate([c_ref[0]] * nb, axis=1) * mask
    s = jnp.concatenate([s_ref[0]] * nb, axis=1) * mask
    y = jnp.dot(c, ar, preferred_element_type=F32) + jnp.dot(s, ai, preferred_element_type=F32)
    o_ref[0] = (y * norm).reshape(l2, nb, w)


def _length_dft_real(zr, zi):
    b, length, w = zr.shape
    l2 = CHUNK
    l1 = length // l2
    nb = SUBLANES
    k1 = jnp.arange(l1, dtype=I32)
    ang_a = (2.0 * math.pi / l1) * ((k1[:, None] * k1[None, :]) % l1).astype(F32)
    r = lax.broadcasted_iota(I32, (l1 * nb, l1 * nb), 0)
    c = lax.broadcasted_iota(I32, (l1 * nb, l1 * nb), 1)
    same = r % nb == c % nb
    rep = (lax.broadcasted_iota(I32, (l1 * nb, l1), 0) // nb == lax.broadcasted_iota(I32, (l1 * nb, l1), 1))
    rep = rep.astype(BF16)

    def expand_a(t):
        t = jnp.dot(jnp.dot(rep, t.astype(BF16), preferred_element_type=F32).astype(BF16), rep.T,
                    preferred_element_type=F32)
        return jnp.where(same, t, 0.0).astype(BF16)

    cos_a = expand_a(jnp.cos(ang_a))
    sin_a = expand_a(jnp.sin(ang_a))
    cms_a = expand_a(jnp.cos(ang_a) - jnp.sin(ang_a))
    shape_t = (l1 // nb, l2 * nb, l2)
    kb = lax.broadcasted_iota(I32, shape_t, 0)
    r = lax.broadcasted_iota(I32, shape_t, 1)
    n2 = lax.broadcasted_iota(I32, shape_t, 2)
    ang_b = (2.0 * math.pi / length) * ((n2 * (kb * nb + r % nb + l1 * (r // nb))) % length).astype(F32)
    cos_b = jnp.cos(ang_b).astype(BF16)
    sin_b = jnp.sin(ang_b).astype(BF16)
    shape_m = (l2 * nb, nb * l2)
    mask_b = (lax.broadcasted_iota(I32, shape_m, 0) % nb == lax.broadcasted_iota(I32, shape_m, 1) // l2)
    mask_b = mask_b.astype(BF16)

    zr4 = zr.reshape(b, l1, l2, w)
    zi4 = zi.reshape(b, l1, l2, w)
    blk_a = pl.BlockSpec((1, l1, nb, w), lambda bi, i: (bi, 0, i, 0))
    ar, ai = pl.pallas_call(
        _fft_a_kernel,
        grid=(b, l2 // nb),
        in_specs=[blk_a, blk_a] + [pl.BlockSpec((l1 * nb, l1 * nb), lambda bi, i: (0, 0))] * 3,
        out_specs=[blk_a, blk_a],
        out_shape=[jax.ShapeDtypeStruct((b, l1, l2, w), F32)] * 2,
        compiler_params=_params(2),
        name="fourier_stage_a",
    )(zr4, zi4, cms_a, cos_a, sin_a)
    blk_b = pl.BlockSpec((1, nb, l2, w), lambda bi, i: (bi, i, 0, 0))
    out = pl.pallas_call(
        functools.partial(_fft_b_kernel, norm=float((length * GROUP_W) ** -0.5)),
        grid=(b, l1 // nb),
        in_specs=[blk_b, blk_b, pl.BlockSpec((1, l2 * nb, l2), lambda bi, i: (i, 0, 0)),
                  pl.BlockSpec((1, l2 * nb, l2), lambda bi, i: (i, 0, 0)),
                  pl.BlockSpec((l2 * nb, nb * l2), lambda bi, i: (0, 0))],
        out_specs=pl.BlockSpec((1, l2, nb, w), lambda bi, i: (bi, 0, i, 0)),
        out_shape=jax.ShapeDtypeStruct((b, l2, l1, w), F32),
        compiler_params=_params(2),
        name="fourier_stage_b",
    )(ar, ai, cos_b, sin_b, mask_b)
    return out.reshape(b, length, w)


def kernel(x, c, ctx, c_ctx, w_mod, b_mod, ln1_g, ln1_b, ln2_g, ln2_b, ev_w_in, ev_w_out, conv_w, conv_b, conv_ln_g, conv_ln_b, lam_q1, lam_k1, lam_q2, lam_k2, diff_norm_g, od_w_in, od_w_out, gmlp_ln_g, gmlp_ln_b, gmlp_ws, gmlp_bs, four_ln_g, four_ln_b, router_w, router_b, w_gate_up, b_gate_up, w_down, b_down):
    b, length, d = x.shape
    depth = w_mod.shape[0]
    alpha = float((2 * depth) ** 0.25)
    assert b + 1 <= SUBLANES and length % (CHUNK * SUBLANES) == 0 and length % GRID_W == 0
    assert length % KV_TILE == 0 and ctx.shape[1] % KV_TILE == 0

    mod = _modulation(c, c_ctx, w_mod, b_mod)
    cos_t, sin_t = _rope_tables(length)
    h = x
    for layer in range(depth):
        j = layer // 2
        m = mod[layer]
        sh1, sc1, g1, sh2, sc2, g2 = [m[:b, i * d:(i + 1) * d][:, None, :] for i in range(6)]
        if layer % 2 == 0:
            lam_init = 0.8 - 0.6 * math.exp(-0.3 * layer)
            w_in = ev_w_in[j].astype(BF16)
            q, k_all, vt, a = _even_in_proj(h, sc1, sh1, w_in, cos_t, sin_t, length + ctx.shape[1])
            csh1 = jnp.broadcast_to(m[b:b + 1, 0:d][:, None, :], (b, 1, d))
            csc1 = jnp.broadcast_to(m[b:b + 1, d:2 * d][:, None, :], (b, 1, d))
            k_all, vt = _context_kv(ctx, csc1, csh1, w_in[:, QK_W:2 * QK_W + V_W], k_all, vt, length)
            att = _diff_attention(q, k_all, vt, lam_q1[j], lam_k1[j], lam_q2[j], lam_k2[j],
                                  diff_norm_g[j], lam_init)
            conv = _conformer_conv(a, conv_w[j], conv_b[j], conv_ln_g[j], conv_ln_b[j])
            ya, yb, w_out = conv, att, ev_w_out[j]
        else:
            spatial, zr, zi = _odd_in_proj(h, sc1, sh1, od_w_in[j].astype(BF16), gmlp_ln_g[j], gmlp_ln_b[j],
                                           gmlp_ws[j], gmlp_bs[j], four_ln_g[j], four_ln_b[j])
            ya, yb, w_out = spatial, _length_dft_real(zr, zi), od_w_out[j]
        h1, top_e, gate, pos, counts = _post_mix(ya, yb, w_out.astype(BF16), h, g1, ln1_g[layer], ln1_b[layer],
                                                 sc2, sh2, router_w[layer], router_b[layer], alpha)
        h = _moe_and_norm(h1, top_e[:TOP_K], gate, pos[:TOP_K], counts[0, :N_EXPERTS], sc2, sh2, g2,
                          w_gate_up, b_gate_up, w_down, b_down, layer, ln2_g[layer], ln2_b[layer], alpha)
    return h
```

```python
import functools
import math

import jax
import jax.numpy as jnp
from jax import lax
from jax.experimental import pallas as pl
from jax.experimental.pallas import tpu as pltpu

F32 = jnp.float32
BF16 = jnp.bfloat16
I32 = jnp.int32
HIGHEST = lax.Precision.HIGHEST

LN_EPS = 1e-5
GRID_W = 64
HEAD_DIM = 64
HEAD_V = 128
N_HEADS = 4
QK_W = N_HEADS * 2 * HEAD_DIM
V_W = N_HEADS * HEAD_V
CONV_W = 512
CONV_TAPS = 31
CONV_HALO = 16
ROPE_BASE = 10000.0
CHUNK = 128
N_GROUPS = 4
GROUP_W = 128
N_EXPERTS = 32
TOP_K = 4
SWIGLU_LIMIT = 7.0
SWIGLU_ALPHA = 1.702
LANES = 128
SUBLANES = 8
EXPERT_ROWS = 512
ISSUE_UNROLL = 8
ATT_KEY_CHUNKS = (1280, 256)
ATT_QUERY_BLOCKS = (1024, 512, 256, 128)
CONV_ROW_CHUNK = 64
NEG_BIG = -1e30
VMEM_LIMIT = 56 * 1024 * 1024


def _params(n_axes):
    return pltpu.CompilerParams(dimension_semantics=("arbitrary",) * n_axes,
                                vmem_limit_bytes=VMEM_LIMIT)


def _pick(n, candidates):
    for c in candidates:
        if n % c == 0:
            return c
    return n


def _layer_norm(r, g, b):
    mu = jnp.mean(r, axis=-1, keepdims=True)
    d = r - mu
    var = jnp.mean(d * d, axis=-1, keepdims=True)
    return d * lax.rsqrt(var + LN_EPS) * g + b


def _sigmoid(x):
    return 1.0 / (1.0 + jnp.exp(-x))


def _mod_kernel(cs_ref, w_ref, b_ref, o_ref):
    cs = cs_ref[...]
    a = cs * _sigmoid(cs)
    o_ref[0] = jnp.dot(a, w_ref[0], precision=HIGHEST, preferred_element_type=F32) + b_ref[0]


def _modulation(c, c_ctx, w_mod, b_mod):
    depth, d, n = w_mod.shape
    b = c.shape[0]
    cs = jnp.concatenate([c, c_ctx[None], jnp.zeros((SUBLANES - b - 1, d), F32)], 0)
    tn = _pick(n, (1536, 1024, 512))
    return pl.pallas_call(
        _mod_kernel,
        grid=(depth, n // tn),
        in_specs=[pl.BlockSpec((SUBLANES, d), lambda l, j: (0, 0)),
                  pl.BlockSpec((1, d, tn), lambda l, j: (l, 0, j)),
                  pl.BlockSpec((1, 1, tn), lambda l, j: (l, 0, j))],
        out_specs=pl.BlockSpec((1, SUBLANES, tn), lambda l, j: (l, 0, j)),
        out_shape=jax.ShapeDtypeStruct((depth, SUBLANES, n), F32),
        compiler_params=_params(2),
        name="modulation",
    )(cs, w_mod, b_mod.reshape(depth, 1, n))


KV_TILE = 256
VT_ROWS = HEAD_V + 16


def _store_values_transposed(v, vt_ref):
    extra = (lax.broadcasted_iota(I32, (VT_ROWS - HEAD_V, v.shape[0]), 0) == 0).astype(BF16)
    for h in range(N_HEADS):
        vt_ref[0, h, 0, :HEAD_V, :] = v[:, h * HEAD_V:(h + 1) * HEAD_V].T.astype(BF16)
        vt_ref[0, h, 0, HEAD_V:, :] = extra


def _even_in_kernel(x_ref, sc_ref, sh_ref, w_ref, cos_ref, sin_ref, q_ref, k_ref, vt_ref, a_ref):
    u = (x_ref[0] * (1.0 + sc_ref[0]) + sh_ref[0]).astype(BF16)
    z = jnp.dot(u, w_ref[...], preferred_element_type=F32)
    cosv = cos_ref[...]
    sinv = sin_ref[...]
    lane = lax.broadcasted_iota(I32, cosv.shape, 1)
    first = (lane % 32) < 16

    def rope(t):
        partner = jnp.where(first, pltpu.roll(t, LANES - 16, 1), pltpu.roll(t, 16, 1))
        return t * cosv + partner * sinv

    for j in range(QK_W // LANES):
        sl = slice(j * LANES, (j + 1) * LANES)
        rq = rope(z[:, sl]) * (HEAD_DIM ** -0.5 * math.log2(math.e))
        for c in range(2):
            q_ref[c, 0, :, sl] = jnp.where(lane // HEAD_DIM == c, rq, 0.0).astype(BF16)
        k_ref[0, :, sl] = rope(z[:, QK_W + j * LANES:QK_W + (j + 1) * LANES]).astype(BF16)
    _store_values_transposed(z[:, 2 * QK_W:2 * QK_W + V_W], vt_ref)
    a0 = 2 * QK_W + V_W
    a_ref[0] = z[:, a0:a0 + CONV_W] * _sigmoid(z[:, a0 + CONV_W:a0 + 2 * CONV_W])


def _rope_tables(length):
    rows = length // GRID_W
    row = jnp.repeat(jnp.arange(rows, dtype=F32), GRID_W)
    col = jnp.tile(jnp.arange(GRID_W, dtype=F32), rows)
    n_freq = HEAD_DIM // 4
    inv_freq = ROPE_BASE ** (-jnp.arange(n_freq, dtype=F32) / n_freq)
    ar = row[:, None] * inv_freq
    ac = col[:, None] * inv_freq
    cos64 = jnp.concatenate([jnp.cos(ar), jnp.cos(ar), jnp.cos(ac), jnp.cos(ac)], 1)
    sin64 = jnp.concatenate([-jnp.sin(ar), jnp.sin(ar), -jnp.sin(ac), jnp.sin(ac)], 1)
    return jnp.tile(cos64, (1, LANES // HEAD_DIM)), jnp.tile(sin64, (1, LANES // HEAD_DIM))


def _key_chunk(n_keys):
    return next(c for c in ATT_KEY_CHUNKS if n_keys % c == 0 and n_keys // c >= 3)


def _kv_specs(b, n_keys, kc, first_tile):
    per_chunk = kc // KV_TILE
    k_spec = pl.BlockSpec((1, KV_TILE, QK_W), lambda bi, i: (bi, first_tile + i, 0))
    vt_spec = pl.BlockSpec((1, N_HEADS, 1, VT_ROWS, KV_TILE),
                           lambda bi, i: (bi, 0, (first_tile + i) // per_chunk, 0, (first_tile + i) % per_chunk))
    shapes = [jax.ShapeDtypeStruct((b, n_keys, QK_W), BF16),
              jax.ShapeDtypeStruct((b, N_HEADS, n_keys // kc, VT_ROWS, kc), BF16)]
    return k_spec, vt_spec, shapes


def _even_in_proj(x, sc, sh, w_bf, cos_t, sin_t, n_keys):
    b, length, d = x.shape
    n = w_bf.shape[1]
    tm = KV_TILE
    row = lambda bi, i: (bi, i, 0)
    vec = lambda bi, i: (bi, 0, 0)
    k_spec, vt_spec, kv_shapes = _kv_specs(b, n_keys, _key_chunk(n_keys), 0)
    return pl.pallas_call(
        _even_in_kernel,
        grid=(b, length // tm),
        in_specs=[pl.BlockSpec((1, tm, d), row),
                  pl.BlockSpec((1, 1, d), vec),
                  pl.BlockSpec((1, 1, d), vec),
                  pl.BlockSpec((d, n), lambda bi, i: (0, 0)),
                  pl.BlockSpec((tm, LANES), lambda bi, i: (i, 0)),
                  pl.BlockSpec((tm, LANES), lambda bi, i: (i, 0))],
        out_specs=[pl.BlockSpec((2, 1, tm, QK_W), lambda bi, i: (0, bi, i, 0)), k_spec, vt_spec,
                   pl.BlockSpec((1, tm, CONV_W), row)],
        out_shape=[jax.ShapeDtypeStruct((2, b, length, QK_W), BF16)] + kv_shapes
                  + [jax.ShapeDtypeStruct((b, length, CONV_W), F32)],
        compiler_params=_params(2),
        name="even_in_proj",
    )(x, sc, sh, w_bf, cos_t, sin_t)


def _ctx_kv_kernel(x_ref, sc_ref, sh_ref, w_ref, k_in, vt_in, k_ref, vt_ref):
    del k_in, vt_in
    u = (x_ref[0] * (1.0 + sc_ref[0]) + sh_ref[0]).astype(BF16)
    z = jnp.dot(u, w_ref[...], preferred_element_type=F32)
    k_ref[0] = z[:, :QK_W].astype(BF16)
    _store_values_transposed(z[:, QK_W:], vt_ref)


def _context_kv(ctx, sc, sh, w_bf, k_all, vt, first_key):
    b, rows, d = ctx.shape
    n = w_bf.shape[1]
    n_keys = k_all.shape[1]
    k_spec, vt_spec, kv_shapes = _kv_specs(b, n_keys, vt.shape[4], first_key // KV_TILE)
    return pl.pallas_call(
        _ctx_kv_kernel,
        grid=(b, rows // KV_TILE),
        in_specs=[pl.BlockSpec((1, KV_TILE, d), lambda bi, i: (bi, i, 0)),
                  pl.BlockSpec((1, 1, d), lambda bi, i: (bi, 0, 0)),
                  pl.BlockSpec((1, 1, d), lambda bi, i: (bi, 0, 0)),
                  pl.BlockSpec((d, n), lambda bi, i: (0, 0)),
                  pl.BlockSpec(memory_space=pl.ANY), pl.BlockSpec(memory_space=pl.ANY)],
        out_specs=[k_spec, vt_spec],
        out_shape=kv_shapes,
        input_output_aliases={4: 0, 5: 1},
        compiler_params=_params(2),
        name="context_kv_proj",
    )(ctx, sc, sh, w_bf, k_all, vt)


def _attn_kernel(lq1_ref, lk1_ref, lq2_ref, lk2_ref, g_ref, q_ref, k_ref, vt_ref, o_ref,
                 s0_sc, s1_sc, p0_sc, p1_sc, acc_sc, *, lam_init, kc):
    mq = q_ref.shape[2]
    n = k_ref.shape[1] // kc
    s_sc = (s0_sc, s1_sc)
    p_sc = (p0_sc, p1_sc)

    def scores(j, slot):
        kj = k_ref[0, pl.ds(pl.multiple_of(j * kc, kc), kc), :]
        for c in range(2):
            s_sc[slot][c] = lax.dot_general(kj, q_ref[c, 0], (((1,), (1,)), ((), ())),
                                            preferred_element_type=F32)

    def softmax(slot, m):
        m_out, alpha = [], []
        for c in range(2):
            s = s_sc[slot][c]
            m_new = jnp.maximum(m[c], jnp.max(s, axis=0, keepdims=True))
            p_sc[slot][c] = jnp.exp2((s - m_new).astype(BF16))
            m_out.append(m_new)
            alpha.append(jnp.exp2(m[c] - m_new))
        return tuple(m_out), tuple(alpha)

    def accumulate(j, slot, alpha):
        vt = vt_ref[0, 0, j]
        for c in range(2):
            acc_sc[c] = alpha[c] * acc_sc[c] + jnp.dot(vt, p_sc[slot][c], preferred_element_type=F32)

    def step(j, slot, m, alpha):
        scores(j + 2, slot)
        m, alpha_next = softmax(1 - slot, m)
        accumulate(j, slot, alpha)
        return m, alpha_next

    acc_sc[...] = jnp.zeros(acc_sc.shape, F32)
    m = (jnp.full((1, mq), NEG_BIG, F32),) * 2
    scores(0, 0)
    m, alpha = softmax(0, m)
    scores(1, 1)

    def pair(t, carry):
        m, alpha = step(2 * t, 0, *carry)
        return step(2 * t + 1, 1, m, alpha)

    n_steps = n - 2
    m, alpha = lax.fori_loop(0, n_steps // 2, pair, (m, alpha))
    if n_steps % 2:
        m, alpha = step(n_steps - 1, 0, m, alpha)
    m, alpha_last = softmax((n - 1) % 2, m)
    accumulate(n - 2, (n - 2) % 2, alpha)
    accumulate(n - 1, (n - 1) % 2, alpha_last)
    acc0 = acc_sc[0]
    acc1 = acc_sc[1]
    lam = (jnp.exp(jnp.sum(lq1_ref[...] * lk1_ref[...], keepdims=True))
           - jnp.exp(jnp.sum(lq2_ref[...] * lk2_ref[...], keepdims=True)) + lam_init)
    o = (acc0[:HEAD_V] / acc0[HEAD_V:HEAD_V + 1]
         - lam * (acc1[:HEAD_V] / acc1[HEAD_V:HEAD_V + 1]))
    ms = jnp.mean(o * o, axis=0, keepdims=True)
    o = o * lax.rsqrt(ms + LN_EPS) * g_ref[...] * (1.0 - lam_init)
    o_ref[0] = o.T.astype(o_ref.dtype)


def _diff_attention(q, k_all, vt, lq1, lk1, lq2, lk2, norm_g, lam_init):
    _, b, length, _ = q.shape
    n_keys = k_all.shape[1]
    mq = _pick(length, ATT_QUERY_BLOCKS)
    nc, kc = vt.shape[2], vt.shape[4]
    small = lambda bi, h, i: (0, 0)
    return pl.pallas_call(
        functools.partial(_attn_kernel, lam_init=lam_init, kc=kc),
        grid=(b, N_HEADS, length // mq),
        in_specs=[pl.BlockSpec((1, HEAD_DIM), small), pl.BlockSpec((1, HEAD_DIM), small),
                  pl.BlockSpec((1, HEAD_DIM), small), pl.BlockSpec((1, HEAD_DIM), small),
                  pl.BlockSpec((HEAD_V, 1), small),
                  pl.BlockSpec((2, 1, mq, HEAD_V), lambda bi, h, i: (0, bi, i, h)),
                  pl.BlockSpec((1, n_keys, HEAD_V), lambda bi, h, i: (bi, 0, h)),
                  pl.BlockSpec((1, 1, nc, VT_ROWS, kc), lambda bi, h, i: (bi, h, 0, 0, 0))],
        out_specs=pl.BlockSpec((1, mq, HEAD_V), lambda bi, h, i: (bi, i, h)),
        out_shape=jax.ShapeDtypeStruct((b, length, V_W), BF16),
        scratch_shapes=[pltpu.VMEM((2, kc, mq), F32), pltpu.VMEM((2, kc, mq), F32),
                        pltpu.VMEM((2, kc, mq), BF16), pltpu.VMEM((2, kc, mq), BF16),
                        pltpu.VMEM((2, VT_ROWS, mq), F32)],
        compiler_params=_params(3),
        name="diff_attention",
    )(lq1[None], lk1[None], lq2[None], lk2[None], norm_g[:, None], q, k_all, vt)


def _conv_kernel(prev_ref, cur_ref, next_ref, w_ref, b_ref, g_ref, beta_ref, o_ref, win, *, row_chunk):
    i = pl.program_id(1)
    tc = cur_ref.shape[1]
    span = tc + 2 * CONV_HALO - SUBLANES
    win[0, CONV_HALO:CONV_HALO + tc, :] = cur_ref[0]
    win[0, 0:CONV_HALO, :] = jnp.where(i > 0, prev_ref[0], 0.0)
    win[0, CONV_HALO + tc:, :] = jnp.where(i < pl.num_programs(1) - 1, next_ref[0], 0.0)
    for s in range(1, SUBLANES):
        win[s, 0:span, :] = win[0, s:s + span, :]
    first_tap = CONV_HALO - CONV_TAPS // 2
    for c0 in range(0, tc, row_chunk):
        acc = jnp.zeros((row_chunk, CONV_W), F32)
        for t in range(CONV_TAPS):
            s = (first_tap + t) % SUBLANES
            r0 = c0 + first_tap + t - s
            acc = acc + w_ref[t:t + 1, :] * win[s, r0:r0 + row_chunk, :]
        y = _layer_norm(acc + b_ref[...], g_ref[...], beta_ref[...])
        o_ref[0, c0:c0 + row_chunk, :] = (y * _sigmoid(y)).astype(o_ref.dtype)


def _conformer_conv(a, conv_w, conv_b, ln_g, ln_b):
    b, length, w = a.shape
    tc = _pick(length, (512, 256, 128))
    hb = tc // CONV_HALO
    n_halo = length // CONV_HALO
    w_pad = jnp.concatenate([conv_w, jnp.zeros((1, w), F32)], 0)
    vec = lambda bi, i: (0, 0)
    return pl.pallas_call(
        functools.partial(_conv_kernel, row_chunk=CONV_ROW_CHUNK),
        grid=(b, length // tc),
        in_specs=[pl.BlockSpec((1, CONV_HALO, w), lambda bi, i: (bi, jnp.maximum(i * hb - 1, 0), 0)),
                  pl.BlockSpec((1, tc, w), lambda bi, i: (bi, i, 0)),
                  pl.BlockSpec((1, CONV_HALO, w), lambda bi, i: (bi, jnp.minimum((i + 1) * hb, n_halo - 1), 0)),
                  pl.BlockSpec((CONV_TAPS + 1, w), vec),
                  pl.BlockSpec((1, w), vec), pl.BlockSpec((1, w), vec), pl.BlockSpec((1, w), vec)],
        out_specs=pl.BlockSpec((1, tc, w), lambda bi, i: (bi, i, 0)),
        out_shape=jax.ShapeDtypeStruct((b, length, w), BF16),
        scratch_shapes=[pltpu.VMEM((SUBLANES, tc + 2 * CONV_HALO, w), F32)],
        compiler_params=_params(2),
        name="conformer_conv",
    )(a, a, a, w_pad, conv_b[None], ln_g[None], ln_b[None])


def _post_mix_kernel(ya_ref, yb_ref, wa_ref, wb_ref, x_ref, g1_ref, lng_ref, lnb_ref, sc_ref, sh_ref,
                     rw_ref, rb_ref, h_ref, e_ref, gate_ref, pos_ref, cnt_ref, carry, *, alpha):
    first_step = jnp.logical_and(pl.program_id(0) == 0, pl.program_id(1) == 0)

    @pl.when(first_step)
    def _():
        carry[...] = jnp.zeros(carry.shape, F32)

    y = (jnp.dot(ya_ref[0].astype(BF16), wa_ref[...], preferred_element_type=F32)
         + jnp.dot(yb_ref[0].astype(BF16), wb_ref[...], preferred_element_type=F32))
    h = _layer_norm(alpha * x_ref[0] + g1_ref[0] * y, lng_ref[...], lnb_ref[...])
    h_ref[0] = h
    u2 = h * (1.0 + sc_ref[0]) + sh_ref[0]
    u_hi = u2.astype(BF16)
    u_lo = (u2 - u_hi.astype(F32)).astype(BF16)
    hh_hl = jnp.dot(u_hi, rw_ref[...], preferred_element_type=F32)
    lh = jnp.dot(u_lo, rw_ref[:, :LANES], preferred_element_type=F32)
    logits = hh_hl[:, :LANES] + hh_hl[:, LANES:] + lh + rb_ref[...]

    tm = logits.shape[0]
    lane = lax.broadcasted_iota(I32, logits.shape, 1).astype(F32)
    vals, idxs = [], []
    lg = logits
    for _ in range(TOP_K):
        mx = jnp.max(lg, axis=1, keepdims=True)
        idx = jnp.min(jnp.where(lg == mx, lane, float(LANES)), axis=1, keepdims=True)
        vals.append(mx)
        idxs.append(idx)
        lg = jnp.where(lane == idx, -3e38, lg)
    exps = [jnp.exp(v - vals[0]) for v in vals]
    den = exps[0] + exps[1] + exps[2] + exps[3]

    hot = jnp.zeros(logits.shape, F32)
    for idx in idxs:
        hot = hot + (lane == idx).astype(F32)
    r_i = lax.broadcasted_iota(I32, (tm, tm), 0)
    c_i = lax.broadcasted_iota(I32, (tm, tm), 1)
    tri = (c_i < r_i).astype(BF16)
    before = jnp.dot(tri, hot.astype(BF16), preferred_element_type=F32) + carry[...]
    e_out = jnp.zeros(logits.shape, F32)
    g_out = jnp.zeros(logits.shape, F32)
    p_out = jnp.zeros(logits.shape, F32)
    for j in range(TOP_K):
        pos_j = jnp.sum(jnp.where(lane == idxs[j], before, 0.0), axis=1, keepdims=True)
        sel = lane == float(j)
        e_out = jnp.where(sel, idxs[j], e_out)
        g_out = jnp.where(sel, exps[j] / den, g_out)
        p_out = jnp.where(sel, pos_j, p_out)
    e_ref[...] = e_out.T[:SUBLANES].astype(I32)
    gate_ref[...] = g_out
    pos_ref[...] = p_out.T[:SUBLANES].astype(I32)
    new_carry = carry[...] + jnp.sum(hot, axis=0, keepdims=True)
    carry[...] = new_carry
    cnt_ref[...] = jnp.broadcast_to(new_carry, cnt_ref.shape).astype(I32)


def _post_mix(ya, yb, w_out_bf, x, g1, ln_g, ln_b, sc2, sh2, router_w, router_b, alpha):
    b, length, d = x.shape
    wa = ya.shape[2]
    tm = _pick(length, (512, 256, 128))
    n_tok = b * length
    rw = jnp.concatenate([router_w, jnp.zeros((d, LANES - N_EXPERTS), F32)], 1)
    rw_hi = rw.astype(BF16)
    rw = jnp.concatenate([rw_hi, (rw - rw_hi.astype(F32)).astype(BF16)], 1)
    rb = jnp.concatenate([router_b, jnp.full((LANES - N_EXPERTS,), NEG_BIG, F32)])[None]
    row = lambda bi, i: (bi, i, 0)
    vec = lambda bi, i: (bi, 0, 0)
    const = lambda bi, i: (0, 0)
    nt = length // tm
    tok = lambda bi, i: (bi * nt + i, 0)
    return pl.pallas_call(
        functools.partial(_post_mix_kernel, alpha=alpha),
        grid=(b, nt),
        in_specs=[pl.BlockSpec((1, tm, wa), row), pl.BlockSpec((1, tm, d - wa), row),
                  pl.BlockSpec((wa, d), const), pl.BlockSpec((d - wa, d), const),
                  pl.BlockSpec((1, tm, d), row), pl.BlockSpec((1, 1, d), vec),
                  pl.BlockSpec((1, d), const), pl.BlockSpec((1, d), const),
                  pl.BlockSpec((1, 1, d), vec), pl.BlockSpec((1, 1, d), vec),
                  pl.BlockSpec((d, 2 * LANES), const), pl.BlockSpec((1, LANES), const)],
        out_specs=[pl.BlockSpec((1, tm, d), row), pl.BlockSpec((SUBLANES, tm), lambda bi, i: (0, bi * nt + i)),
                   pl.BlockSpec((tm, LANES), tok), pl.BlockSpec((SUBLANES, tm), lambda bi, i: (0, bi * nt + i)),
                   pl.BlockSpec((SUBLANES, LANES), const)],
        out_shape=[jax.ShapeDtypeStruct((b, length, d), F32), jax.ShapeDtypeStruct((SUBLANES, n_tok), I32),
                   jax.ShapeDtypeStruct((n_tok, LANES), F32), jax.ShapeDtypeStruct((SUBLANES, n_tok), I32),
                   jax.ShapeDtypeStruct((SUBLANES, LANES), I32)],
        scratch_shapes=[pltpu.VMEM((1, LANES), F32)],
        compiler_params=_params(2),
        name="out_proj_norm_router",
    )(ya, yb, w_out_bf[:wa], w_out_bf[wa:], x, g1, ln_g[None], ln_b[None], sc2, sh2, rw, rb)


def _tile_indices(dest, tm):
    n_tok = dest.shape[1]
    return dest.reshape(TOP_K, n_tok // tm, tm).transpose(1, 0, 2).reshape(n_tok // tm, 1, TOP_K * tm)


def _dispatch_kernel(zstart_ref, zvalid_ref, dest_ref, h_ref, sc_ref, sh_ref, xs_ref, u_sc, zero_sc, sem, zsem):
    tm = h_ref.shape[0]

    @pl.when(pl.program_id(0) == 0)
    def _():
        zero_sc[...] = jnp.zeros(zero_sc.shape, F32)

        def fill(e, c):
            @pl.when(zvalid_ref[e] > 0)
            def _():
                start = pl.multiple_of(zstart_ref[e], SUBLANES)
                pltpu.make_async_copy(zero_sc, xs_ref.at[pl.ds(start, EXPERT_ROWS)], zsem).start()
            return c

        lax.fori_loop(0, N_EXPERTS, fill, 0)

        def drain(e, c):
            @pl.when(zvalid_ref[e] > 0)
            def _():
                pltpu.make_async_copy(zero_sc, xs_ref.at[pl.ds(0, EXPERT_ROWS)], zsem).wait()
            return c

        lax.fori_loop(0, N_EXPERTS, drain, 0)

    i = pl.program_id(0)
    slot = i % 2
    src = u_sc.at[slot]
    src[...] = h_ref[...] * (1.0 + sc_ref[0]) + sh_ref[0]

    def issue(g, c):
        for rr in range(ISSUE_UNROLL):
            r = g * ISSUE_UNROLL + rr
            for j in range(TOP_K):
                d = dest_ref[0, 0, j * tm + r]
                pltpu.make_async_copy(src.at[pl.ds(r, 1)], xs_ref.at[pl.ds(d, 1)], sem.at[slot]).start()
        return c

    lax.fori_loop(0, tm // ISSUE_UNROLL, issue, 0)

    def wait_tile(which):
        rows = xs_ref.at[pl.ds(0, tm * TOP_K)]
        pltpu.make_async_copy(rows, rows, sem.at[which]).wait()

    pl.when(i > 0)(lambda: wait_tile(1 - slot))
    pl.when(i == pl.num_programs(0) - 1)(lambda: wait_tile(slot))


def _dispatch(h_flat, sc2, sh2, dest, zstart, zvalid, n_rows, tokens_per_batch):
    n_tok, d = h_flat.shape
    tm = _pick(tokens_per_batch, (512, 256, 128))
    per_b = tokens_per_batch // tm
    return pl.pallas_call(
        _dispatch_kernel,
        grid_spec=pltpu.PrefetchScalarGridSpec(
            num_scalar_prefetch=2,
            grid=(n_tok // tm,),
            in_specs=[pl.BlockSpec((1, 1, tm * TOP_K), lambda i, zs, zv: (i, 0, 0), memory_space=pltpu.SMEM),
                      pl.BlockSpec((tm, d), lambda i, zs, zv: (i, 0)),
                      pl.BlockSpec((1, 1, d), lambda i, zs, zv: (i // per_b, 0, 0)),
                      pl.BlockSpec((1, 1, d), lambda i, zs, zv: (i // per_b, 0, 0))],
            out_specs=pl.BlockSpec(memory_space=pl.ANY),
            scratch_shapes=[pltpu.VMEM((2, tm, d), F32), pltpu.VMEM((EXPERT_ROWS, d), F32),
                            pltpu.SemaphoreType.DMA((2,)), pltpu.SemaphoreType.DMA(())]),
        out_shape=jax.ShapeDtypeStruct((n_rows, d), F32),
        compiler_params=_params(1),
        name="moe_dispatch",
    )(zstart, zvalid, _tile_indices(dest, tm), h_flat, sc2, sh2)


def _expert_kernel(blk_e_ref, n_used_ref, x_ref, wgu_ref, bgu_ref, wd_ref, bd_ref, o_ref, wgu_bf, wd_bf):
    i = pl.program_id(0)
    new_expert = jnp.logical_or(i == 0, blk_e_ref[i] != blk_e_ref[jnp.maximum(i - 1, 0)])

    @pl.when(new_expert)
    def _():
        wgu_bf[...] = wgu_ref[0].astype(BF16)
        wd_bf[...] = wd_ref[0].astype(BF16)

    @pl.when(i < n_used_ref[0])
    def _():
        d_e = wd_ref.shape[1]
        gu = jnp.dot(x_ref[...].astype(BF16), wgu_bf[...], preferred_element_type=F32) + bgu_ref[0]
        x_glu = jnp.minimum(gu[:, :d_e], SWIGLU_LIMIT)
        x_lin = jnp.clip(gu[:, d_e:], -SWIGLU_LIMIT, SWIGLU_LIMIT)
        act = x_glu * _sigmoid(SWIGLU_ALPHA * x_glu) * (x_lin + 1.0)
        o_ref[...] = jnp.dot(act.astype(BF16), wd_bf[...], preferred_element_type=F32) + bd_ref[0]


def _experts(xs, blk_e, n_used, wgu, bgu, wd, bd, layer):
    n_rows, d = xs.shape
    depth, n_e, _, n_gu = wgu.shape
    n_blk = n_rows // EXPERT_ROWS
    rows = lambda i, be, nu: (jnp.minimum(i, nu[0] - 1), 0)
    first = layer * n_e
    by_e = lambda i, be, nu: (first + be[i], 0, 0)
    n_e = depth * n_e
    wgu = wgu.reshape(n_e, d, n_gu)
    wd = wd.reshape(n_e, n_gu // 2, d)
    return pl.pallas_call(
        _expert_kernel,
        grid_spec=pltpu.PrefetchScalarGridSpec(
            num_scalar_prefetch=2,
            grid=(n_blk,),
            in_specs=[pl.BlockSpec((EXPERT_ROWS, d), rows),
                      pl.BlockSpec((1, d, n_gu), by_e), pl.BlockSpec((1, 1, n_gu), by_e),
                      pl.BlockSpec((1, n_gu // 2, d), by_e), pl.BlockSpec((1, 1, d), by_e)],
            out_specs=pl.BlockSpec((EXPERT_ROWS, d), rows),
            scratch_shapes=[pltpu.VMEM((d, n_gu), BF16), pltpu.VMEM((n_gu // 2, d), BF16)]),
        out_shape=jax.ShapeDtypeStruct((n_rows, d), F32),
        compiler_params=_params(1),
        name="moe_experts",
    )(blk_e, n_used, xs, wgu, bgu.reshape(n_e, 1, n_gu), wd, bd.reshape(n_e, 1, d))


def _combine_kernel(dest_ref, dest_next_ref, y_ref, h_ref, gate_ref, g2_ref, lng_ref, lnb_ref, o_ref,
                    buf, sem, *, alpha):
    i = pl.program_id(0)
    tm = h_ref.shape[0]

    def fetch(idx_ref, slot):
        def issue(g, c):
            for rr in range(ISSUE_UNROLL):
                r = g * ISSUE_UNROLL + rr
                for j in range(TOP_K):
                    d = idx_ref[0, 0, j * tm + r]
                    pltpu.make_async_copy(y_ref.at[pl.ds(d, 1)], buf.at[slot, j, pl.ds(r, 1)], sem.at[slot]).start()
            return c

        lax.fori_loop(0, tm // ISSUE_UNROLL, issue, 0)

    def fetch_unrolled(idx_ref, slot):
        for r in range(tm):
            for j in range(TOP_K):
                d = idx_ref[0, 0, j * tm + r]
                pltpu.make_async_copy(y_ref.at[pl.ds(d, 1)], buf.at[slot, j, pl.ds(r, 1)], sem.at[slot]).start()

    slot = i % 2
    more = i + 1 < pl.num_programs(0)

    @pl.when(i == 0)
    def _():
        fetch(dest_ref, 0)

    pl.when(jnp.logical_and(more, slot == 0))(lambda: fetch_unrolled(dest_next_ref, 1))
    pl.when(jnp.logical_and(more, slot == 1))(lambda: fetch_unrolled(dest_next_ref, 0))

    pltpu.make_async_copy(buf.at[slot], buf.at[slot], sem.at[slot]).wait()
    gate = gate_ref[...]
    y2 = gate[:, 0:1] * buf[slot, 0]
    for j in range(1, TOP_K):
        y2 = y2 + gate[:, j:j + 1] * buf[slot, j]
    o_ref[...] = _layer_norm(alpha * h_ref[...] + g2_ref[0] * y2, lng_ref[...], lnb_ref[...])


def _combine(y_rows, dest, h_flat, gate, g2, ln_g, ln_b, alpha, tokens_per_batch):
    n_tok, d = h_flat.shape
    tm = _pick(tokens_per_batch, (256, 128))
    per_b = tokens_per_batch // tm
    n_tiles = n_tok // tm
    dest3 = _tile_indices(dest, tm)
    return pl.pallas_call(
        functools.partial(_combine_kernel, alpha=alpha),
        grid=(n_tiles,),
        in_specs=[pl.BlockSpec((1, 1, tm * TOP_K), lambda i: (i, 0, 0), memory_space=pltpu.SMEM),
                  pl.BlockSpec((1, 1, tm * TOP_K), lambda i: (jnp.minimum(i + 1, n_tiles - 1), 0, 0),
                               memory_space=pltpu.SMEM),
                  pl.BlockSpec(memory_space=pl.ANY),
                  pl.BlockSpec((tm, d), lambda i: (i, 0)),
                  pl.BlockSpec((tm, LANES), lambda i: (i, 0)),
                  pl.BlockSpec((1, 1, d), lambda i: (i // per_b, 0, 0)),
                  pl.BlockSpec((1, d), lambda i: (0, 0)), pl.BlockSpec((1, d), lambda i: (0, 0))],
        out_specs=pl.BlockSpec((tm, d), lambda i: (i, 0)),
        out_shape=jax.ShapeDtypeStruct((n_tok, d), F32),
        scratch_shapes=[pltpu.VMEM((2, TOP_K, tm, d), F32), pltpu.SemaphoreType.DMA((2,))],
        compiler_params=_params(1),
        name="moe_combine_norm",
    )(dest3, dest3, y_rows, h_flat, gate, g2, ln_g[None], ln_b[None])


def _moe_and_norm(h1, top_e, gate, pos, counts, sc2, sh2, g2, wgu, bgu, wd, bd, layer, ln_g, ln_b, alpha):
    b, length, d = h1.shape
    n_tok = b * length
    n_blk = n_tok * TOP_K // EXPERT_ROWS + N_EXPERTS
    padded = (counts + EXPERT_ROWS - 1) // EXPERT_ROWS * EXPERT_ROWS
    pad_end = jnp.cumsum(padded)
    offset = pad_end - padded
    n_used = (pad_end[-1:] // EXPERT_ROWS).astype(I32)
    blk_start = jnp.arange(n_blk, dtype=I32) * EXPERT_ROWS
    blk_e = jnp.minimum(jnp.sum(pad_end[None, :] <= blk_start[:, None], axis=1), N_EXPERTS - 1).astype(I32)
    dest = pos
    for e in range(N_EXPERTS):
        dest = dest + jnp.where(top_e == e, offset[e], 0)
    dest = dest.astype(I32)
    zstart = jnp.maximum(pad_end - EXPERT_ROWS, 0).astype(I32)
    zvalid = (counts > 0).astype(I32)

    h_flat = h1.reshape(n_tok, d)
    xs = _dispatch(h_flat, sc2, sh2, dest, zstart, zvalid, n_blk * EXPERT_ROWS, length)
    y_rows = _experts(xs, blk_e, n_used, wgu, bgu, wd, bd, layer)
    out = _combine(y_rows, dest, h_flat, gate, g2, ln_g, ln_b, alpha, length)
    return out.reshape(b, length, d)


def _odd_in_kernel(x_ref, sc_ref, sh_ref, w_ref, glg_ref, glb_ref, ws_ref, bs_ref, flg_ref, flb_ref, dft_ref,
                   sp_ref, zr_ref, zi_ref):
    u = (x_ref[0] * (1.0 + sc_ref[0]) + sh_ref[0]).astype(BF16)
    z = jnp.dot(u, w_ref[...], preferred_element_type=F32)
    tm = z.shape[0]
    gw = N_GROUPS * GROUP_W
    ug = jax.nn.gelu(z[:, :gw])
    vn = _layer_norm(jax.nn.gelu(z[:, gw:2 * gw]), glg_ref[...], glb_ref[...]).astype(BF16)
    f = z[:, 2 * gw:]
    for g in range(N_GROUPS):
        cols = slice(g * GROUP_W, (g + 1) * GROUP_W)
        for c0 in range(0, tm, CHUNK):
            rows = slice(c0, c0 + CHUNK)
            sv = jnp.dot(ws_ref[g], vn[rows, cols], preferred_element_type=F32) + bs_ref[:, cols]
            sp_ref[0, rows, cols] = (ug[rows, cols] * sv).astype(sp_ref.dtype)
        fn = _layer_norm(f[:, cols], flg_ref[:, cols], flb_ref[:, cols]).astype(BF16)
        zz = jnp.dot(fn, dft_ref[...], preferred_element_type=F32)
        zr_ref[0, :, cols] = zz[:, :GROUP_W]
        zi_ref[0, :, cols] = zz[:, GROUP_W:]


def _odd_in_proj(x, sc, sh, w_bf, gln_g, gln_b, ws, bs, fln_g, fln_b):
    b, length, d = x.shape
    n = w_bf.shape[1]
    gw = N_GROUPS * GROUP_W
    tm = _pick(length, (512, 256, 128))
    kk = jnp.arange(GROUP_W, dtype=I32)
    ang = (2.0 * math.pi / GROUP_W) * ((kk[:, None] * kk[None, :]) % GROUP_W).astype(F32)
    dft = jnp.concatenate([jnp.cos(ang), -jnp.sin(ang)], 1).astype(BF16)
    bs_exp = jnp.repeat(bs.T, GROUP_W, axis=1)
    row = lambda bi, i: (bi, i, 0)
    vec = lambda bi, i: (bi, 0, 0)
    const2 = lambda bi, i: (0, 0)
    return pl.pallas_call(
        _odd_in_kernel,
        grid=(b, length // tm),
        in_specs=[pl.BlockSpec((1, tm, d), row), pl.BlockSpec((1, 1, d), vec), pl.BlockSpec((1, 1, d), vec),
                  pl.BlockSpec((d, n), const2),
                  pl.BlockSpec((1, gw), const2), pl.BlockSpec((1, gw), const2),
                  pl.BlockSpec((N_GROUPS, CHUNK, CHUNK), lambda bi, i: (0, 0, 0)),
                  pl.BlockSpec((CHUNK, gw), const2),
                  pl.BlockSpec((1, gw), const2), pl.BlockSpec((1, gw), const2),
                  pl.BlockSpec((GROUP_W, 2 * GROUP_W), const2)],
        out_specs=[pl.BlockSpec((1, tm, gw), row), pl.BlockSpec((1, tm, gw), row), pl.BlockSpec((1, tm, gw), row)],
        out_shape=[jax.ShapeDtypeStruct((b, length, gw), BF16), jax.ShapeDtypeStruct((b, length, gw), F32),
                   jax.ShapeDtypeStruct((b, length, gw), F32)],
        compiler_params=_params(2),
        name="odd_in_proj",
    )(x, sc, sh, w_bf, gln_g[None], gln_b[None], ws.astype(BF16), bs_exp, fln_g[None], fln_b[None], dft)


def _fft_a_kernel(zr_ref, zi_ref, cms_ref, c_ref, s_ref, ar_ref, ai_ref):
    l1, nb, w = zr_ref.shape[1:]
    zr = zr_ref[0].reshape(l1 * nb, w)
    zi = zi_ref[0].reshape(l1 * nb, w)
    k1 = jnp.dot(cms_ref[...], zr.astype(BF16), preferred_element_type=F32)
    k2 = jnp.dot(c_ref[...], (zi - zr).astype(BF16), preferred_element_type=F32)
    k3 = jnp.dot(s_ref[...], (zr + zi).astype(BF16), preferred_element_type=F32)
    ar_ref[0] = (k1 + k3).reshape(l1, nb, w)
    ai_ref[0] = (k1 + k2).reshape(l1, nb, w)


def _fft_b_kernel(ar_ref, ai_ref, c_ref, s_ref, mask_ref, o_ref, *, norm):
    nb, l2, w = ar_ref.shape[1:]
    ar = ar_ref[0].reshape(nb * l2, w).astype(BF16)
    ai = ai_ref[0].reshape(nb * l2, w).astype(BF16)
    mask = mask_ref[...]
    c = jnp.concatenate([c_ref[0]] * nb, axis=1) * mask
    s = jnp.concatenate([s_ref[0]] * nb, axis=1) * mask
    y = jnp.dot(c, ar, preferred_element_type=F32) + jnp.dot(s, ai, preferred_element_type=F32)
    o_ref[0] = (y * norm).reshape(l2, nb, w)


def _length_dft_real(zr, zi):
    b, length, w = zr.shape
    l2 = CHUNK
    l1 = length // l2
    nb = SUBLANES
    k1 = jnp.arange(l1, dtype=I32)
    ang_a = (2.0 * math.pi / l1) * ((k1[:, None] * k1[None, :]) % l1).astype(F32)
    r = lax.broadcasted_iota(I32, (l1 * nb, l1 * nb), 0)
    c = lax.broadcasted_iota(I32, (l1 * nb, l1 * nb), 1)
    same = r % nb == c % nb
    rep = (lax.broadcasted_iota(I32, (l1 * nb, l1), 0) // nb == lax.broadcasted_iota(I32, (l1 * nb, l1), 1))
    rep = rep.astype(BF16)

    def expand_a(t):
        t = jnp.dot(jnp.dot(rep, t.astype(BF16), preferred_element_type=F32).astype(BF16), rep.T,
                    preferred_element_type=F32)
        return jnp.where(same, t, 0.0).astype(BF16)

    cos_a = expand_a(jnp.cos(ang_a))
    sin_a = expand_a(jnp.sin(ang_a))
    cms_a = expand_a(jnp.cos(ang_a) - jnp.sin(ang_a))
    shape_t = (l1 // nb, l2 * nb, l2)
    kb = lax.broadcasted_iota(I32, shape_t, 0)
    r = lax.broadcasted_iota(I32, shape_t, 1)
    n2 = lax.broadcasted_iota(I32, shape_t, 2)
    ang_b = (2.0 * math.pi / length) * ((n2 * (kb * nb + r % nb + l1 * (r // nb))) % length).astype(F32)
    cos_b = jnp.cos(ang_b).astype(BF16)
    sin_b = jnp.sin(ang_b).astype(BF16)
    shape_m = (l2 * nb, nb * l2)
    mask_b = (lax.broadcasted_iota(I32, shape_m, 0) % nb == lax.broadcasted_iota(I32, shape_m, 1) // l2)
    mask_b = mask_b.astype(BF16)

    zr4 = zr.reshape(b, l1, l2, w)
    zi4 = zi.reshape(b, l1, l2, w)
    blk_a = pl.BlockSpec((1, l1, nb, w), lambda bi, i: (bi, 0, i, 0))
    ar, ai = pl.pallas_call(
        _fft_a_kernel,
        grid=(b, l2 // nb),
        in_specs=[blk_a, blk_a] + [pl.BlockSpec((l1 * nb, l1 * nb), lambda bi, i: (0, 0))] * 3,
        out_specs=[blk_a, blk_a],
        out_shape=[jax.ShapeDtypeStruct((b, l1, l2, w), F32)] * 2,
        compiler_params=_params(2),
        name="fourier_stage_a",
    )(zr4, zi4, cms_a, cos_a, sin_a)
    blk_b = pl.BlockSpec((1, nb, l2, w), lambda bi, i: (bi, i, 0, 0))
    out = pl.pallas_call(
        functools.partial(_fft_b_kernel, norm=float((length * GROUP_W) ** -0.5)),
        grid=(b, l1 // nb),
        in_specs=[blk_b, blk_b, pl.BlockSpec((1, l2 * nb, l2), lambda bi, i: (i, 0, 0)),
                  pl.BlockSpec((1, l2 * nb, l2), lambda bi, i: (i, 0, 0)),
                  pl.BlockSpec((l2 * nb, nb * l2), lambda bi, i: (0, 0))],
        out_specs=pl.BlockSpec((1, l2, nb, w), lambda bi, i: (bi, 0, i, 0)),
        out_shape=jax.ShapeDtypeStruct((b, l2, l1, w), F32),
        compiler_params=_params(2),
        name="fourier_stage_b",
    )(ar, ai, cos_b, sin_b, mask_b)
    return out.reshape(b, length, w)


def kernel(x, c, ctx, c_ctx, w_mod, b_mod, ln1_g, ln1_b, ln2_g, ln2_b, ev_w_in, ev_w_out, conv_w, conv_b, conv_ln_g, conv_ln_b, lam_q1, lam_k1, lam_q2, lam_k2, diff_norm_g, od_w_in, od_w_out, gmlp_ln_g, gmlp_ln_b, gmlp_ws, gmlp_bs, four_ln_g, four_ln_b, router_w, router_b, w_gate_up, b_gate_up, w_down, b_down):
    b, length, d = x.shape
    depth = w_mod.shape[0]
    alpha = float((2 * depth) ** 0.25)
    assert b + 1 <= SUBLANES and length % (CHUNK * SUBLANES) == 0 and length % GRID_W == 0
    assert length % KV_TILE == 0 and ctx.shape[1] % KV_TILE == 0

    mod = _modulation(c, c_ctx, w_mod, b_mod)
    cos_t, sin_t = _rope_tables(length)
    h = x
    for layer in range(depth):
        j = layer // 2
        m = mod[layer]
        sh1, sc1, g1, sh2, sc2, g2 = [m[:b, i * d:(i + 1) * d][:, None, :] for i in range(6)]
        if layer % 2 == 0:
            lam_init = 0.8 - 0.6 * math.exp(-0.3 * layer)
            w_in = ev_w_in[j].astype(BF16)
            q, k_all, vt, a = _even_in_proj(h, sc1, sh1, w_in, cos_t, sin_t, length + ctx.shape[1])
            csh1 = jnp.broadcast_to(m[b:b + 1, 0:d][:, None, :], (b, 1, d))
            csc1 = jnp.broadcast_to(m[b:b + 1, d:2 * d][:, None, :], (b, 1, d))
            k_all, vt = _context_kv(ctx, csc1, csh1, w_in[:, QK_W:2 * QK_W + V_W], k_all, vt, length)
            att = _diff_attention(q, k_all, vt, lam_q1[j], lam_k1[j], lam_q2[j], lam_k2[j],
                                  diff_norm_g[j], lam_init)
            conv = _conformer_conv(a, conv_w[j], conv_b[j], conv_ln_g[j], conv_ln_b[j])
            ya, yb, w_out = conv, att, ev_w_out[j]
        else:
            spatial, zr, zi = _odd_in_proj(h, sc1, sh1, od_w_in[j].astype(BF16), gmlp_ln_g[j], gmlp_ln_b[j],
                                           gmlp_ws[j], gmlp_bs[j], four_ln_g[j], four_ln_b[j])
            ya, yb, w_out = spatial, _length_dft_real(zr, zi), od_w_out[j]
        h1, top_e, gate, pos, counts = _post_mix(ya, yb, w_out.astype(BF16), h, g1, ln1_g[layer], ln1_b[layer],
                                                 sc2, sh2, router_w[layer], router_b[layer], alpha)
        h = _moe_and_norm(h1, top_e[:TOP_K], gate, pos[:TOP_K], counts[0, :N_EXPERTS], sc2, sh2, g2,
                          w_gate_up, b_gate_up, w_down, b_down, layer, ln2_g[layer], ln2_b[layer], alpha)
    return h
```

```python
import functools
import math

import jax
import jax.numpy as jnp
from jax import lax
from jax.experimental import pallas as pl
from jax.experimental.pallas import tpu as pltpu

F32 = jnp.float32
BF16 = jnp.bfloat16
I32 = jnp.int32
HIGHEST = lax.Precision.HIGHEST

LN_EPS = 1e-5
GRID_W = 64
HEAD_DIM = 64
HEAD_V = 128
N_HEADS = 4
QK_W = N_HEADS * 2 * HEAD_DIM
V_W = N_HEADS * HEAD_V
CONV_W = 512
CONV_TAPS = 31
CONV_HALO = 16
ROPE_BASE = 10000.0
CHUNK = 128
N_GROUPS = 4
GROUP_W = 128
N_EXPERTS = 32
TOP_K = 4
SWIGLU_LIMIT = 7.0
SWIGLU_ALPHA = 1.702
LANES = 128
SUBLANES = 8
EXPERT_ROWS = 512
ISSUE_UNROLL = 8
ATT_KEY_CHUNKS = (1280, 256)
ATT_QUERY_BLOCKS = (1024, 512, 256, 128)
CONV_ROW_CHUNK = 64
NEG_BIG = -1e30
VMEM_LIMIT = 56 * 1024 * 1024


def _params(n_axes):
    return pltpu.CompilerParams(dimension_semantics=("arbitrary",) * n_axes,
                                vmem_limit_bytes=VMEM_LIMIT)


def _pick(n, candidates):
    for c in candidates:
        if n % c == 0:
            return c
    return n


def _layer_norm(r, g, b):
    mu = jnp.mean(r, axis=-1, keepdims=True)
    d = r - mu
    var = jnp.mean(d * d, axis=-1, keepdims=True)
    return d * lax.rsqrt(var + LN_EPS) * g + b


def _sigmoid(x):
    return 1.0 / (1.0 + jnp.exp(-x))


def _mod_kernel(cs_ref, w_ref, b_ref, o_ref):
    cs = cs_ref[...]
    a = cs * _sigmoid(cs)
    o_ref[0] = jnp.dot(a, w_ref[0], precision=HIGHEST, preferred_element_type=F32) + b_ref[0]


def _modulation(c, c_ctx, w_mod, b_mod):
    depth, d, n = w_mod.shape
    b = c.shape[0]
    cs = jnp.concatenate([c, c_ctx[None], jnp.zeros((SUBLANES - b - 1, d), F32)], 0)
    tn = _pick(n, (1536, 1024, 512))
    return pl.pallas_call(
        _mod_kernel,
        grid=(depth, n // tn),
        in_specs=[pl.BlockSpec((SUBLANES, d), lambda l, j: (0, 0)),
                  pl.BlockSpec((1, d, tn), lambda l, j: (l, 0, j)),
                  pl.BlockSpec((1, 1, tn), lambda l, j: (l, 0, j))],
        out_specs=pl.BlockSpec((1, SUBLANES, tn), lambda l, j: (l, 0, j)),
        out_shape=jax.ShapeDtypeStruct((depth, SUBLANES, n), F32),
        compiler_params=_params(2),
        name="modulation",
    )(cs, w_mod, b_mod.reshape(depth, 1, n))


KV_TILE = 256
VT_ROWS = HEAD_V + 16


def _store_values_transposed(v, vt_ref):
    extra = (lax.broadcasted_iota(I32, (VT_ROWS - HEAD_V, v.shape[0]), 0) == 0).astype(BF16)
    for h in range(N_HEADS):
        vt_ref[0, h, 0, :HEAD_V, :] = v[:, h * HEAD_V:(h + 1) * HEAD_V].T.astype(BF16)
        vt_ref[0, h, 0, HEAD_V:, :] = extra


def _even_in_kernel(x_ref, sc_ref, sh_ref, w_ref, cos_ref, sin_ref, q_ref, k_ref, vt_ref, a_ref):
    u = (x_ref[0] * (1.0 + sc_ref[0]) + sh_ref[0]).astype(BF16)
    z = jnp.dot(u, w_ref[...], preferred_element_type=F32)
    cosv = cos_ref[...]
    sinv = sin_ref[...]
    lane = lax.broadcasted_iota(I32, cosv.shape, 1)
    first = (lane % 32) < 16

    def rope(t):
        partner = jnp.where(first, pltpu.roll(t, LANES - 16, 1), pltpu.roll(t, 16, 1))
        return t * cosv + partner * sinv

    for j in range(QK_W // LANES):
        sl = slice(j * LANES, (j + 1) * LANES)
        rq = rope(z[:, sl]) * (HEAD_DIM ** -0.5 * math.log2(math.e))
        for c in range(2):
            q_ref[c, 0, j] = jnp.where(lane // HEAD_DIM == c, rq, 0.0).T.astype(BF16)
        k_ref[0, :, sl] = rope(z[:, QK_W + j * LANES:QK_W + (j + 1) * LANES]).astype(BF16)
    _store_values_transposed(z[:, 2 * QK_W:2 * QK_W + V_W], vt_ref)
    a0 = 2 * QK_W + V_W
    a_ref[0] = z[:, a0:a0 + CONV_W] * _sigmoid(z[:, a0 + CONV_W:a0 + 2 * CONV_W])


def _rope_tables(length):
    rows = length // GRID_W
    row = jnp.repeat(jnp.arange(rows, dtype=F32), GRID_W)
    col = jnp.tile(jnp.arange(GRID_W, dtype=F32), rows)
    n_freq = HEAD_DIM // 4
    inv_freq = ROPE_BASE ** (-jnp.arange(n_freq, dtype=F32) / n_freq)
    ar = row[:, None] * inv_freq
    ac = col[:, None] * inv_freq
    cos64 = jnp.concatenate([jnp.cos(ar), jnp.cos(ar), jnp.cos(ac), jnp.cos(ac)], 1)
    sin64 = jnp.concatenate([-jnp.sin(ar), jnp.sin(ar), -jnp.sin(ac), jnp.sin(ac)], 1)
    return jnp.tile(cos64, (1, LANES // HEAD_DIM)), jnp.tile(sin64, (1, LANES // HEAD_DIM))


def _key_chunk(n_keys):
    return next(c for c in ATT_KEY_CHUNKS if n_keys % c == 0 and n_keys // c >= 3)


def _kv_specs(b, n_keys, kc, first_tile):
    per_chunk = kc // KV_TILE
    k_spec = pl.BlockSpec((1, KV_TILE, QK_W), lambda bi, i: (bi, first_tile + i, 0))
    vt_spec = pl.BlockSpec((1, N_HEADS, 1, VT_ROWS, KV_TILE),
                           lambda bi, i: (bi, 0, (first_tile + i) // per_chunk, 0, (first_tile + i) % per_chunk))
    shapes = [jax.ShapeDtypeStruct((b, n_keys, QK_W), BF16),
              jax.ShapeDtypeStruct((b, N_HEADS, n_keys // kc, VT_ROWS, kc), BF16)]
    return k_spec, vt_spec, shapes


def _even_in_proj(x, sc, sh, w_bf, cos_t, sin_t, n_keys):
    b, length, d = x.shape
    n = w_bf.shape[1]
    tm = KV_TILE
    row = lambda bi, i: (bi, i, 0)
    vec = lambda bi, i: (bi, 0, 0)
    k_spec, vt_spec, kv_shapes = _kv_specs(b, n_keys, _key_chunk(n_keys), 0)
    return pl.pallas_call(
        _even_in_kernel,
        grid=(b, length // tm),
        in_specs=[pl.BlockSpec((1, tm, d), row),
                  pl.BlockSpec((1, 1, d), vec),
                  pl.BlockSpec((1, 1, d), vec),
                  pl.BlockSpec((d, n), lambda bi, i: (0, 0)),
                  pl.BlockSpec((tm, LANES), lambda bi, i: (i, 0)),
                  pl.BlockSpec((tm, LANES), lambda bi, i: (i, 0))],
        out_specs=[pl.BlockSpec((2, 1, N_HEADS, HEAD_V, tm), lambda bi, i: (0, bi, 0, 0, i)), k_spec, vt_spec,
                   pl.BlockSpec((1, tm, CONV_W), row)],
        out_shape=[jax.ShapeDtypeStruct((2, b, N_HEADS, HEAD_V, length), BF16)] + kv_shapes
                  + [jax.ShapeDtypeStruct((b, length, CONV_W), F32)],
        compiler_params=_params(2),
        name="even_in_proj",
    )(x, sc, sh, w_bf, cos_t, sin_t)


def _ctx_kv_kernel(x_ref, sc_ref, sh_ref, w_ref, k_in, vt_in, k_ref, vt_ref):
    del k_in, vt_in
    u = (x_ref[0] * (1.0 + sc_ref[0]) + sh_ref[0]).astype(BF16)
    z = jnp.dot(u, w_ref[...], preferred_element_type=F32)
    k_ref[0] = z[:, :QK_W].astype(BF16)
    _store_values_transposed(z[:, QK_W:], vt_ref)


def _context_kv(ctx, sc, sh, w_bf, k_all, vt, first_key):
    b, rows, d = ctx.shape
    n = w_bf.shape[1]
    n_keys = k_all.shape[1]
    k_spec, vt_spec, kv_shapes = _kv_specs(b, n_keys, vt.shape[4], first_key // KV_TILE)
    return pl.pallas_call(
        _ctx_kv_kernel,
        grid=(b, rows // KV_TILE),
        in_specs=[pl.BlockSpec((1, KV_TILE, d), lambda bi, i: (bi, i, 0)),
                  pl.BlockSpec((1, 1, d), lambda bi, i: (bi, 0, 0)),
                  pl.BlockSpec((1, 1, d), lambda bi, i: (bi, 0, 0)),
                  pl.BlockSpec((d, n), lambda bi, i: (0, 0)),
                  pl.BlockSpec(memory_space=pl.ANY), pl.BlockSpec(memory_space=pl.ANY)],
        out_specs=[k_spec, vt_spec],
        out_shape=kv_shapes,
        input_output_aliases={4: 0, 5: 1},
        compiler_params=_params(2),
        name="context_kv_proj",
    )(ctx, sc, sh, w_bf, k_all, vt)


def _attn_kernel(lq1_ref, lk1_ref, lq2_ref, lk2_ref, g_ref, q_ref, k_ref, vt_ref, o_ref,
                 s0_sc, s1_sc, p0_sc, p1_sc, acc_sc, *, lam_init, kc):
    mq = q_ref.shape[4]
    n = k_ref.shape[1] // kc
    s_sc = (s0_sc, s1_sc)
    p_sc = (p0_sc, p1_sc)

    def scores(j, slot):
        kj = k_ref[0, pl.ds(pl.multiple_of(j * kc, kc), kc), :]
        for c in range(2):
            s_sc[slot][c] = jnp.dot(kj, q_ref[c, 0, 0], preferred_element_type=F32)

    def softmax(slot, m):
        m_out, alpha = [], []
        for c in range(2):
            s = s_sc[slot][c]
            m_new = jnp.maximum(m[c], jnp.max(s, axis=0, keepdims=True))
            p_sc[slot][c] = jnp.exp2((s - m_new).astype(BF16))
            m_out.append(m_new)
            alpha.append(jnp.exp2(m[c] - m_new))
        return tuple(m_out), tuple(alpha)

    def accumulate(j, slot, alpha):
        vt = vt_ref[0, 0, j]
        for c in range(2):
            acc_sc[c] = alpha[c] * acc_sc[c] + jnp.dot(vt, p_sc[slot][c], preferred_element_type=F32)

    def step(j, slot, m, alpha):
        scores(j + 2, slot)
        m, alpha_next = softmax(1 - slot, m)
        accumulate(j, slot, alpha)
        return m, alpha_next

    acc_sc[...] = jnp.zeros(acc_sc.shape, F32)
    m = (jnp.full((1, mq), NEG_BIG, F32),) * 2
    scores(0, 0)
    m, alpha = softmax(0, m)
    scores(1, 1)

    def pair(t, carry):
        m, alpha = step(2 * t, 0, *carry)
        return step(2 * t + 1, 1, m, alpha)

    n_steps = n - 2
    m, alpha = lax.fori_loop(0, n_steps // 2, pair, (m, alpha))
    if n_steps % 2:
        m, alpha = step(n_steps - 1, 0, m, alpha)
    m, alpha_last = softmax((n - 1) % 2, m)
    accumulate(n - 2, (n - 2) % 2, alpha)
    accumulate(n - 1, (n - 1) % 2, alpha_last)
    acc0 = acc_sc[0]
    acc1 = acc_sc[1]
    lam = (jnp.exp(jnp.sum(lq1_ref[...] * lk1_ref[...], keepdims=True))
           - jnp.exp(jnp.sum(lq2_ref[...] * lk2_ref[...], keepdims=True)) + lam_init)
    o = (acc0[:HEAD_V] / acc0[HEAD_V:HEAD_V + 1]
         - lam * (acc1[:HEAD_V] / acc1[HEAD_V:HEAD_V + 1]))
    ms = jnp.mean(o * o, axis=0, keepdims=True)
    o = o * lax.rsqrt(ms + LN_EPS) * g_ref[...] * (1.0 - lam_init)
    o_ref[0] = o.T.astype(o_ref.dtype)


def _diff_attention(q, k_all, vt, lq1, lk1, lq2, lk2, norm_g, lam_init):
    _, b, _, _, length = q.shape
    n_keys = k_all.shape[1]
    mq = _pick(length, ATT_QUERY_BLOCKS)
    nc, kc = vt.shape[2], vt.shape[4]
    small = lambda bi, h, i: (0, 0)
    return pl.pallas_call(
        functools.partial(_attn_kernel, lam_init=lam_init, kc=kc),
        grid=(b, N_HEADS, length // mq),
        in_specs=[pl.BlockSpec((1, HEAD_DIM), small), pl.BlockSpec((1, HEAD_DIM), small),
                  pl.BlockSpec((1, HEAD_DIM), small), pl.BlockSpec((1, HEAD_DIM), small),
                  pl.BlockSpec((HEAD_V, 1), small),
                  pl.BlockSpec((2, 1, 1, HEAD_V, mq), lambda bi, h, i: (0, bi, h, 0, i)),
                  pl.BlockSpec((1, n_keys, HEAD_V), lambda bi, h, i: (bi, 0, h)),
                  pl.BlockSpec((1, 1, nc, VT_ROWS, kc), lambda bi, h, i: (bi, h, 0, 0, 0))],
        out_specs=pl.BlockSpec((1, mq, HEAD_V), lambda bi, h, i: (bi, i, h)),
        out_shape=jax.ShapeDtypeStruct((b, length, V_W), BF16),
        scratch_shapes=[pltpu.VMEM((2, kc, mq), F32), pltpu.VMEM((2, kc, mq), F32),
                        pltpu.VMEM((2, kc, mq), BF16), pltpu.VMEM((2, kc, mq), BF16),
                        pltpu.VMEM((2, VT_ROWS, mq), F32)],
        compiler_params=_params(3),
        name="diff_attention",
    )(lq1[None], lk1[None], lq2[None], lk2[None], norm_g[:, None], q, k_all, vt)


def _conv_kernel(prev_ref, cur_ref, next_ref, w_ref, b_ref, g_ref, beta_ref, o_ref, win, *, row_chunk):
    i = pl.program_id(1)
    tc = cur_ref.shape[1]
    span = tc + 2 * CONV_HALO - SUBLANES
    win[0, CONV_HALO:CONV_HALO + tc, :] = cur_ref[0]
    win[0, 0:CONV_HALO, :] = jnp.where(i > 0, prev_ref[0], 0.0)
    win[0, CONV_HALO + tc:, :] = jnp.where(i < pl.num_programs(1) - 1, next_ref[0], 0.0)
    for s in range(1, SUBLANES):
        win[s, 0:span, :] = win[0, s:s + span, :]
    first_tap = CONV_HALO - CONV_TAPS // 2
    for c0 in range(0, tc, row_chunk):
        acc = jnp.zeros((row_chunk, CONV_W), F32)
        for t in range(CONV_TAPS):
            s = (first_tap + t) % SUBLANES
            r0 = c0 + first_tap + t - s
            acc = acc + w_ref[t:t + 1, :] * win[s, r0:r0 + row_chunk, :]
        y = _layer_norm(acc + b_ref[...], g_ref[...], beta_ref[...])
        o_ref[0, c0:c0 + row_chunk, :] = (y * _sigmoid(y)).astype(o_ref.dtype)


def _conformer_conv(a, conv_w, conv_b, ln_g, ln_b):
    b, length, w = a.shape
    tc = _pick(length, (512, 256, 128))
    hb = tc // CONV_HALO
    n_halo = length // CONV_HALO
    w_pad = jnp.concatenate([conv_w, jnp.zeros((1, w), F32)], 0)
    vec = lambda bi, i: (0, 0)
    return pl.pallas_call(
        functools.partial(_conv_kernel, row_chunk=CONV_ROW_CHUNK),
        grid=(b, length // tc),
        in_specs=[pl.BlockSpec((1, CONV_HALO, w), lambda bi, i: (bi, jnp.maximum(i * hb - 1, 0), 0)),
                  pl.BlockSpec((1, tc, w), lambda bi, i: (bi, i, 0)),
                  pl.BlockSpec((1, CONV_HALO, w), lambda bi, i: (bi, jnp.minimum((i + 1) * hb, n_halo - 1), 0)),
                  pl.BlockSpec((CONV_TAPS + 1, w), vec),
                  pl.BlockSpec((1, w), vec), pl.BlockSpec((1, w), vec), pl.BlockSpec((1, w), vec)],
        out_specs=pl.BlockSpec((1, tc, w), lambda bi, i: (bi, i, 0)),
        out_shape=jax.ShapeDtypeStruct((b, length, w), BF16),
        scratch_shapes=[pltpu.VMEM((SUBLANES, tc + 2 * CONV_HALO, w), F32)],
        compiler_params=_params(2),
        name="conformer_conv",
    )(a, a, a, w_pad, conv_b[None], ln_g[None], ln_b[None])


def _post_mix_kernel(ya_ref, yb_ref, wa_ref, wb_ref, x_ref, g1_ref, lng_ref, lnb_ref, sc_ref, sh_ref,
                     rw_ref, rb_ref, h_ref, e_ref, gate_ref, pos_ref, cnt_ref, carry, *, alpha):
    first_step = jnp.logical_and(pl.program_id(0) == 0, pl.program_id(1) == 0)

    @pl.when(first_step)
    def _():
        carry[...] = jnp.zeros(carry.shape, F32)

    y = (jnp.dot(ya_ref[0].astype(BF16), wa_ref[...], preferred_element_type=F32)
         + jnp.dot(yb_ref[0].astype(BF16), wb_ref[...], preferred_element_type=F32))
    h = _layer_norm(alpha * x_ref[0] + g1_ref[0] * y, lng_ref[...], lnb_ref[...])
    h_ref[0] = h
    u2 = h * (1.0 + sc_ref[0]) + sh_ref[0]
    u_hi = u2.astype(BF16)
    u_lo = (u2 - u_hi.astype(F32)).astype(BF16)
    hh_hl = jnp.dot(u_hi, rw_ref[...], preferred_element_type=F32)
    lh = jnp.dot(u_lo, rw_ref[:, :LANES], preferred_element_type=F32)
    logits = hh_hl[:, :LANES] + hh_hl[:, LANES:] + lh + rb_ref[...]

    tm = logits.shape[0]
    lane = lax.broadcasted_iota(I32, logits.shape, 1).astype(F32)
    vals, idxs = [], []
    lg = logits
    for _ in range(TOP_K):
        mx = jnp.max(lg, axis=1, keepdims=True)
        idx = jnp.min(jnp.where(lg == mx, lane, float(LANES)), axis=1, keepdims=True)
        vals.append(mx)
        idxs.append(idx)
        lg = jnp.where(lane == idx, -3e38, lg)
    exps = [jnp.exp(v - vals[0]) for v in vals]
    den = exps[0] + exps[1] + exps[2] + exps[3]

    hot = jnp.zeros(logits.shape, F32)
    for idx in idxs:
        hot = hot + (lane == idx).astype(F32)
    r_i = lax.broadcasted_iota(I32, (tm, tm), 0)
    c_i = lax.broadcasted_iota(I32, (tm, tm), 1)
    tri = (c_i < r_i).astype(BF16)
    before = jnp.dot(tri, hot.astype(BF16), preferred_element_type=F32) + carry[...]
    e_out = jnp.zeros(logits.shape, F32)
    g_out = jnp.zeros(logits.shape, F32)
    p_out = jnp.zeros(logits.shape, F32)
    for j in range(TOP_K):
        pos_j = jnp.sum(jnp.where(lane == idxs[j], before, 0.0), axis=1, keepdims=True)
        sel = lane == float(j)
        e_out = jnp.where(sel, idxs[j], e_out)
        g_out = jnp.where(sel, exps[j] / den, g_out)
        p_out = jnp.where(sel, pos_j, p_out)
    e_ref[...] = e_out.T[:SUBLANES].astype(I32)
    gate_ref[...] = g_out
    pos_ref[...] = p_out.T[:SUBLANES].astype(I32)
    new_carry = carry[...] + jnp.sum(hot, axis=0, keepdims=True)
    carry[...] = new_carry
    cnt_ref[...] = jnp.broadcast_to(new_carry, cnt_ref.shape).astype(I32)


def _post_mix(ya, yb, w_out_bf, x, g1, ln_g, ln_b, sc2, sh2, router_w, router_b, alpha):
    b, length, d = x.shape
    wa = ya.shape[2]
    tm = _pick(length, (512, 256, 128))
    n_tok = b * length
    rw = jnp.concatenate([router_w, jnp.zeros((d, LANES - N_EXPERTS), F32)], 1)
    rw_hi = rw.astype(BF16)
    rw = jnp.concatenate([rw_hi, (rw - rw_hi.astype(F32)).astype(BF16)], 1)
    rb = jnp.concatenate([router_b, jnp.full((LANES - N_EXPERTS,), NEG_BIG, F32)])[None]
    row = lambda bi, i: (bi, i, 0)
    vec = lambda bi, i: (bi, 0, 0)
    const = lambda bi, i: (0, 0)
    nt = length // tm
    tok = lambda bi, i: (bi * nt + i, 0)
    return pl.pallas_call(
        functools.partial(_post_mix_kernel, alpha=alpha),
        grid=(b, nt),
        in_specs=[pl.BlockSpec((1, tm, wa), row), pl.BlockSpec((1, tm, d - wa), row),
                  pl.BlockSpec((wa, d), const), pl.BlockSpec((d - wa, d), const),
                  pl.BlockSpec((1, tm, d), row), pl.BlockSpec((1, 1, d), vec),
                  pl.BlockSpec((1, d), const), pl.BlockSpec((1, d), const),
                  pl.BlockSpec((1, 1, d), vec), pl.BlockSpec((1, 1, d), vec),
                  pl.BlockSpec((d, 2 * LANES), const), pl.BlockSpec((1, LANES), const)],
        out_specs=[pl.BlockSpec((1, tm, d), row), pl.BlockSpec((SUBLANES, tm), lambda bi, i: (0, bi * nt + i)),
                   pl.BlockSpec((tm, LANES), tok), pl.BlockSpec((SUBLANES, tm), lambda bi, i: (0, bi * nt + i)),
                   pl.BlockSpec((SUBLANES, LANES), const)],
        out_shape=[jax.ShapeDtypeStruct((b, length, d), F32), jax.ShapeDtypeStruct((SUBLANES, n_tok), I32),
                   jax.ShapeDtypeStruct((n_tok, LANES), F32), jax.ShapeDtypeStruct((SUBLANES, n_tok), I32),
                   jax.ShapeDtypeStruct((SUBLANES, LANES), I32)],
        scratch_shapes=[pltpu.VMEM((1, LANES), F32)],
        compiler_params=_params(2),
        name="out_proj_norm_router",
    )(ya, yb, w_out_bf[:wa], w_out_bf[wa:], x, g1, ln_g[None], ln_b[None], sc2, sh2, rw, rb)


def _tile_indices(dest, tm):
    n_tok = dest.shape[1]
    return dest.reshape(TOP_K, n_tok // tm, tm).transpose(1, 0, 2).reshape(n_tok // tm, 1, TOP_K * tm)


def _dispatch_kernel(zstart_ref, zvalid_ref, dest_ref, h_ref, sc_ref, sh_ref, xs_ref, u_sc, zero_sc, sem, zsem):
    tm = h_ref.shape[0]

    @pl.when(pl.program_id(0) == 0)
    def _():
        zero_sc[...] = jnp.zeros(zero_sc.shape, F32)

        def fill(e, c):
            @pl.when(zvalid_ref[e] > 0)
            def _():
                start = pl.multiple_of(zstart_ref[e], SUBLANES)
                pltpu.make_async_copy(zero_sc, xs_ref.at[pl.ds(start, EXPERT_ROWS)], zsem).start()
            return c

        lax.fori_loop(0, N_EXPERTS, fill, 0)

        def drain(e, c):
            @pl.when(zvalid_ref[e] > 0)
            def _():
                pltpu.make_async_copy(zero_sc, xs_ref.at[pl.ds(0, EXPERT_ROWS)], zsem).wait()
            return c

        lax.fori_loop(0, N_EXPERTS, drain, 0)

    i = pl.program_id(0)
    slot = i % 2
    src = u_sc.at[slot]
    src[...] = h_ref[...] * (1.0 + sc_ref[0]) + sh_ref[0]

    def issue(g, c):
        for rr in range(ISSUE_UNROLL):
            r = g * ISSUE_UNROLL + rr
            for j in range(TOP_K):
                d = dest_ref[0, 0, j * tm + r]
                pltpu.make_async_copy(src.at[pl.ds(r, 1)], xs_ref.at[pl.ds(d, 1)], sem.at[slot]).start()
        return c

    lax.fori_loop(0, tm // ISSUE_UNROLL, issue, 0)

    def wait_tile(which):
        rows = xs_ref.at[pl.ds(0, tm * TOP_K)]
        pltpu.make_async_copy(rows, rows, sem.at[which]).wait()

    pl.when(i > 0)(lambda: wait_tile(1 - slot))
    pl.when(i == pl.num_programs(0) - 1)(lambda: wait_tile(slot))


def _dispatch(h_flat, sc2, sh2, dest, zstart, zvalid, n_rows, tokens_per_batch):
    n_tok, d = h_flat.shape
    tm = _pick(tokens_per_batch, (512, 256, 128))
    per_b = tokens_per_batch // tm
    return pl.pallas_call(
        _dispatch_kernel,
        grid_spec=pltpu.PrefetchScalarGridSpec(
            num_scalar_prefetch=2,
            grid=(n_tok // tm,),
            in_specs=[pl.BlockSpec((1, 1, tm * TOP_K), lambda i, zs, zv: (i, 0, 0), memory_space=pltpu.SMEM),
                      pl.BlockSpec((tm, d), lambda i, zs, zv: (i, 0)),
                      pl.BlockSpec((1, 1, d), lambda i, zs, zv: (i // per_b, 0, 0)),
                      pl.BlockSpec((1, 1, d), lambda i, zs, zv: (i // per_b, 0, 0))],
            out_specs=pl.BlockSpec(memory_space=pl.ANY),
            scratch_shapes=[pltpu.VMEM((2, tm, d), F32), pltpu.VMEM((EXPERT_ROWS, d), F32),
                            pltpu.SemaphoreType.DMA((2,)), pltpu.SemaphoreType.DMA(())]),
        out_shape=jax.ShapeDtypeStruct((n_rows, d), F32),
        compiler_params=_params(1),
        name="moe_dispatch",
    )(zstart, zvalid, _tile_indices(dest, tm), h_flat, sc2, sh2)


def _expert_kernel(blk_e_ref, n_used_ref, x_ref, wgu_ref, bgu_ref, wd_ref, bd_ref, o_ref, wgu_bf, wd_bf):
    i = pl.program_id(0)
    new_expert = jnp.logical_or(i == 0, blk_e_ref[i] != blk_e_ref[jnp.maximum(i - 1, 0)])

    @pl.when(new_expert)
    def _():
        wgu_bf[...] = wgu_ref[0].astype(BF16)
        wd_bf[...] = wd_ref[0].astype(BF16)

    @pl.when(i < n_used_ref[0])
    def _():
        d_e = wd_ref.shape[1]
        gu = jnp.dot(x_ref[...].astype(BF16), wgu_bf[...], preferred_element_type=F32) + bgu_ref[0]
        x_glu = jnp.minimum(gu[:, :d_e], SWIGLU_LIMIT)
        x_lin = jnp.clip(gu[:, d_e:], -SWIGLU_LIMIT, SWIGLU_LIMIT)
        act = x_glu * _sigmoid(SWIGLU_ALPHA * x_glu) * (x_lin + 1.0)
        o_ref[...] = jnp.dot(act.astype(BF16), wd_bf[...], preferred_element_type=F32) + bd_ref[0]


def _experts(xs, blk_e, n_used, wgu, bgu, wd, bd, layer):
    n_rows, d = xs.shape
    depth, n_e, _, n_gu = wgu.shape
    n_blk = n_rows // EXPERT_ROWS
    rows = lambda i, be, nu: (jnp.minimum(i, nu[0] - 1), 0)
    first = layer * n_e
    by_e = lambda i, be, nu: (first + be[i], 0, 0)
    n_e = depth * n_e
    wgu = wgu.reshape(n_e, d, n_gu)
    wd = wd.reshape(n_e, n_gu // 2, d)
    return pl.pallas_call(
        _expert_kernel,
        grid_spec=pltpu.PrefetchScalarGridSpec(
            num_scalar_prefetch=2,
            grid=(n_blk,),
            in_specs=[pl.BlockSpec((EXPERT_ROWS, d), rows),
                      pl.BlockSpec((1, d, n_gu), by_e), pl.BlockSpec((1, 1, n_gu), by_e),
                      pl.BlockSpec((1, n_gu // 2, d), by_e), pl.BlockSpec((1, 1, d), by_e)],
            out_specs=pl.BlockSpec((EXPERT_ROWS, d), rows),
            scratch_shapes=[pltpu.VMEM((d, n_gu), BF16), pltpu.VMEM((n_gu // 2, d), BF16)]),
        out_shape=jax.ShapeDtypeStruct((n_rows, d), F32),
        compiler_params=_params(1),
        name="moe_experts",
    )(blk_e, n_used, xs, wgu, bgu.reshape(n_e, 1, n_gu), wd, bd.reshape(n_e, 1, d))


def _combine_kernel(dest_ref, dest_next_ref, y_ref, h_ref, gate_ref, g2_ref, lng_ref, lnb_ref, o_ref,
                    buf, sem, *, alpha):
    i = pl.program_id(0)
    tm = h_ref.shape[0]

    def fetch(idx_ref, slot):
        def issue(g, c):
            for rr in range(ISSUE_UNROLL):
                r = g * ISSUE_UNROLL + rr
                for j in range(TOP_K):
                    d = idx_ref[0, 0, j * tm + r]
                    pltpu.make_async_copy(y_ref.at[pl.ds(d, 1)], buf.at[slot, j, pl.ds(r, 1)], sem.at[slot]).start()
            return c

        lax.fori_loop(0, tm // ISSUE_UNROLL, issue, 0)

    def fetch_unrolled(idx_ref, slot):
        for r in range(tm):
            for j in range(TOP_K):
                d = idx_ref[0, 0, j * tm + r]
                pltpu.make_async_copy(y_ref.at[pl.ds(d, 1)], buf.at[slot, j, pl.ds(r, 1)], sem.at[slot]).start()

    slot = i % 2
    more = i + 1 < pl.num_programs(0)

    @pl.when(i == 0)
    def _():
        fetch(dest_ref, 0)

    pl.when(jnp.logical_and(more, slot == 0))(lambda: fetch_unrolled(dest_next_ref, 1))
    pl.when(jnp.logical_and(more, slot == 1))(lambda: fetch_unrolled(dest_next_ref, 0))

    pltpu.make_async_copy(buf.at[slot], buf.at[slot], sem.at[slot]).wait()
    gate = gate_ref[...]
    y2 = gate[:, 0:1] * buf[slot, 0]
    for j in range(1, TOP_K):
        y2 = y2 + gate[:, j:j + 1] * buf[slot, j]
    o_ref[...] = _layer_norm(alpha * h_ref[...] + g2_ref[0] * y2, lng_ref[...], lnb_ref[...])


def _combine(y_rows, dest, h_flat, gate, g2, ln_g, ln_b, alpha, tokens_per_batch):
    n_tok, d = h_flat.shape
    tm = _pick(tokens_per_batch, (256, 128))
    per_b = tokens_per_batch // tm
    n_tiles = n_tok // tm
    dest3 = _tile_indices(dest, tm)
    return pl.pallas_call(
        functools.partial(_combine_kernel, alpha=alpha),
        grid=(n_tiles,),
        in_specs=[pl.BlockSpec((1, 1, tm * TOP_K), lambda i: (i, 0, 0), memory_space=pltpu.SMEM),
                  pl.BlockSpec((1, 1, tm * TOP_K), lambda i: (jnp.minimum(i + 1, n_tiles - 1), 0, 0),
                               memory_space=pltpu.SMEM),
                  pl.BlockSpec(memory_space=pl.ANY),
                  pl.BlockSpec((tm, d), lambda i: (i, 0)),
                  pl.BlockSpec((tm, LANES), lambda i: (i, 0)),
                  pl.BlockSpec((1, 1, d), lambda i: (i // per_b, 0, 0)),
                  pl.BlockSpec((1, d), lambda i: (0, 0)), pl.BlockSpec((1, d), lambda i: (0, 0))],
        out_specs=pl.BlockSpec((tm, d), lambda i: (i, 0)),
        out_shape=jax.ShapeDtypeStruct((n_tok, d), F32),
        scratch_shapes=[pltpu.VMEM((2, TOP_K, tm, d), F32), pltpu.SemaphoreType.DMA((2,))],
        compiler_params=_params(1),
        name="moe_combine_norm",
    )(dest3, dest3, y_rows, h_flat, gate, g2, ln_g[None], ln_b[None])


def _moe_and_norm(h1, top_e, gate, pos, counts, sc2, sh2, g2, wgu, bgu, wd, bd, layer, ln_g, ln_b, alpha):
    b, length, d = h1.shape
    n_tok = b * length
    n_blk = n_tok * TOP_K // EXPERT_ROWS + N_EXPERTS
    padded = (counts + EXPERT_ROWS - 1) // EXPERT_ROWS * EXPERT_ROWS
    pad_end = jnp.cumsum(padded)
    offset = pad_end - padded
    n_used = (pad_end[-1:] // EXPERT_ROWS).astype(I32)
    blk_start = jnp.arange(n_blk, dtype=I32) * EXPERT_ROWS
    blk_e = jnp.minimum(jnp.sum(pad_end[None, :] <= blk_start[:, None], axis=1), N_EXPERTS - 1).astype(I32)
    dest = pos
    for e in range(N_EXPERTS):
        dest = dest + jnp.where(top_e == e, offset[e], 0)
    dest = dest.astype(I32)
    zstart = jnp.maximum(pad_end - EXPERT_ROWS, 0).astype(I32)
    zvalid = (counts > 0).astype(I32)

    h_flat = h1.reshape(n_tok, d)
    xs = _dispatch(h_flat, sc2, sh2, dest, zstart, zvalid, n_blk * EXPERT_ROWS, length)
    y_rows = _experts(xs, blk_e, n_used, wgu, bgu, wd, bd, layer)
    out = _combine(y_rows, dest, h_flat, gate, g2, ln_g, ln_b, alpha, length)
    return out.reshape(b, length, d)


def _odd_in_kernel(x_ref, sc_ref, sh_ref, w_ref, glg_ref, glb_ref, ws_ref, bs_ref, flg_ref, flb_ref, dft_ref,
                   sp_ref, zr_ref, zi_ref):
    u = (x_ref[0] * (1.0 + sc_ref[0]) + sh_ref[0]).astype(BF16)
    z = jnp.dot(u, w_ref[...], preferred_element_type=F32)
    tm = z.shape[0]
    gw = N_GROUPS * GROUP_W
    ug = jax.nn.gelu(z[:, :gw])
    vn = _layer_norm(jax.nn.gelu(z[:, gw:2 * gw]), glg_ref[...], glb_ref[...]).astype(BF16)
    f = z[:, 2 * gw:]
    for g in range(N_GROUPS):
        cols = slice(g * GROUP_W, (g + 1) * GROUP_W)
        for c0 in range(0, tm, CHUNK):
            rows = slice(c0, c0 + CHUNK)
            sv = jnp.dot(ws_ref[g], vn[rows, cols], preferred_element_type=F32) + bs_ref[:, cols]
            sp_ref[0, rows, cols] = (ug[rows, cols] * sv).astype(sp_ref.dtype)
        fn = _layer_norm(f[:, cols], flg_ref[:, cols], flb_ref[:, cols]).astype(BF16)
        zz = jnp.dot(fn, dft_ref[...], preferred_element_type=F32)
        zr_ref[0, :, cols] = zz[:, :GROUP_W]
        zi_ref[0, :, cols] = zz[:, GROUP_W:]


def _odd_in_proj(x, sc, sh, w_bf, gln_g, gln_b, ws, bs, fln_g, fln_b):
    b, length, d = x.shape
    n = w_bf.shape[1]
    gw = N_GROUPS * GROUP_W
    tm = _pick(length, (512, 256, 128))
    kk = jnp.arange(GROUP_W, dtype=I32)
    ang = (2.0 * math.pi / GROUP_W) * ((kk[:, None] * kk[None, :]) % GROUP_W).astype(F32)
    dft = jnp.concatenate([jnp.cos(ang), -jnp.sin(ang)], 1).astype(BF16)
    bs_exp = jnp.repeat(bs.T, GROUP_W, axis=1)
    row = lambda bi, i: (bi, i, 0)
    vec = lambda bi, i: (bi, 0, 0)
    const2 = lambda bi, i: (0, 0)
    return pl.pallas_call(
        _odd_in_kernel,
        grid=(b, length // tm),
        in_specs=[pl.BlockSpec((1, tm, d), row), pl.BlockSpec((1, 1, d), vec), pl.BlockSpec((1, 1, d), vec),
                  pl.BlockSpec((d, n), const2),
                  pl.BlockSpec((1, gw), const2), pl.BlockSpec((1, gw), const2),
                  pl.BlockSpec((N_GROUPS, CHUNK, CHUNK), lambda bi, i: (0, 0, 0)),
                  pl.BlockSpec((CHUNK, gw), const2),
                  pl.BlockSpec((1, gw), const2), pl.BlockSpec((1, gw), const2),
                  pl.BlockSpec((GROUP_W, 2 * GROUP_W), const2)],
        out_specs=[pl.BlockSpec((1, tm, gw), row), pl.BlockSpec((1, tm, gw), row), pl.BlockSpec((1, tm, gw), row)],
        out_shape=[jax.ShapeDtypeStruct((b, length, gw), BF16), jax.ShapeDtypeStruct((b, length, gw), F32),
                   jax.ShapeDtypeStruct((b, length, gw), F32)],
        compiler_params=_params(2),
        name="odd_in_proj",
    )(x, sc, sh, w_bf, gln_g[None], gln_b[None], ws.astype(BF16), bs_exp, fln_g[None], fln_b[None], dft)


def _fft_a_kernel(zr_ref, zi_ref, cms_ref, c_ref, s_ref, ar_ref, ai_ref):
    l1, nb, w = zr_ref.shape[1:]
    zr = zr_ref[0].reshape(l1 * nb, w)
    zi = zi_ref[0].reshape(l1 * nb, w)
    k1 = jnp.dot(cms_ref[...], zr.astype(BF16), preferred_element_type=F32)
    k2 = jnp.dot(c_ref[...], (zi - zr).astype(BF16), preferred_element_type=F32)
    k3 = jnp.dot(s_ref[...], (zr + zi).astype(BF16), preferred_element_type=F32)
    ar_ref[0] = (k1 + k3).reshape(l1, nb, w)
    ai_ref[0] = (k1 + k2).reshape(l1, nb, w)


def _fft_b_kernel(ar_ref, ai_ref, c_ref, s_ref, mask_ref, o_ref, *, norm):
    nb, l2, w = ar_ref.shape[1:]
    ar = ar_ref[0].reshape(nb * l2, w).astype(BF16)
    ai = ai_ref[0].reshape(nb * l2, w).astype(BF16)
    mask = mask_ref[...]
    c = jnp.concatenate([c_ref[0]] * nb, axis=1) * mask
    s = jnp.concatenate([s_ref[0]] * nb, axis=1) * mask
    y = jnp.dot(c, ar, preferred_element_type=F32) + jnp.dot(s, ai, preferred_element_type=F32)
    o_ref[0] = (y * norm).reshape(l2, nb, w)


def _length_dft_real(zr, zi):
    b, length, w = zr.shape
    l2 = CHUNK
    l1 = length // l2
    nb = SUBLANES
    k1 = jnp.arange(l1, dtype=I32)
    ang_a = (2.0 * math.pi / l1) * ((k1[:, None] * k1[None, :]) % l1).astype(F32)
    r = lax.broadcasted_iota(I32, (l1 * nb, l1 * nb), 0)
    c = lax.broadcasted_iota(I32, (l1 * nb, l1 * nb), 1)
    same = r % nb == c % nb
    rep = (lax.broadcasted_iota(I32, (l1 * nb, l1), 0) // nb == lax.broadcasted_iota(I32, (l1 * nb, l1), 1))
    rep = rep.astype(BF16)

    def expand_a(t):
        t = jnp.dot(jnp.dot(rep, t.astype(BF16), preferred_element_type=F32).astype(BF16), rep.T,
                    preferred_element_type=F32)
        return jnp.where(same, t, 0.0).astype(BF16)

    cos_a = expand_a(jnp.cos(ang_a))
    sin_a = expand_a(jnp.sin(ang_a))
    cms_a = expand_a(jnp.cos(ang_a) - jnp.sin(ang_a))
    shape_t = (l1 // nb, l2 * nb, l2)
    kb = lax.broadcasted_iota(I32, shape_t, 0)
    r = lax.broadcasted_iota(I32, shape_t, 1)
    n2 = lax.broadcasted_iota(I32, shape_t, 2)
    ang_b = (2.0 * math.pi / length) * ((n2 * (kb * nb + r % nb + l1 * (r // nb))) % length).astype(F32)
    cos_b = jnp.cos(ang_b).astype(BF16)
    sin_b = jnp.sin(ang_b).astype(BF16)
    shape_m = (l2 * nb, nb * l2)
    mask_b = (lax.broadcasted_iota(I32, shape_m, 0) % nb == lax.broadcasted_iota(I32, shape_m, 1) // l2)
    mask_b = mask_b.astype(BF16)

    zr4 = zr.reshape(b, l1, l2, w)
    zi4 = zi.reshape(b, l1, l2, w)
    blk_a = pl.BlockSpec((1, l1, nb, w), lambda bi, i: (bi, 0, i, 0))
    ar, ai = pl.pallas_call(
        _fft_a_kernel,
        grid=(b, l2 // nb),
        in_specs=[blk_a, blk_a] + [pl.BlockSpec((l1 * nb, l1 * nb), lambda bi, i: (0, 0))] * 3,
        out_specs=[blk_a, blk_a],
        out_shape=[jax.ShapeDtypeStruct((b, l1, l2, w), F32)] * 2,
        compiler_params=_params(2),
        name="fourier_stage_a",
    )(zr4, zi4, cms_a, cos_a, sin_a)
    blk_b = pl.BlockSpec((1, nb, l2, w), lambda bi, i: (bi, i, 0, 0))
    out = pl.pallas_call(
        functools.partial(_fft_b_kernel, norm=float((length * GROUP_W) ** -0.5)),
        grid=(b, l1 // nb),
        in_specs=[blk_b, blk_b, pl.BlockSpec((1, l2 * nb, l2), lambda bi, i: (i, 0, 0)),
                  pl.BlockSpec((1, l2 * nb, l2), lambda bi, i: (i, 0, 0)),
                  pl.BlockSpec((l2 * nb, nb * l2), lambda bi, i: (0, 0))],
        out_specs=pl.BlockSpec((1, l2, nb, w), lambda bi, i: (bi, 0, i, 0)),
        out_shape=jax.ShapeDtypeStruct((b, l2, l1, w), F32),
        compiler_params=_params(2),
        name="fourier_stage_b",
    )(ar, ai, cos_b, sin_b, mask_b)
    return out.reshape(b, length, w)


def kernel(x, c, ctx, c_ctx, w_mod, b_mod, ln1_g, ln1_b, ln2_g, ln2_b, ev_w_in, ev_w_out, conv_w, conv_b, conv_ln_g, conv_ln_b, lam_q1, lam_k1, lam_q2, lam_k2, diff_norm_g, od_w_in, od_w_out, gmlp_ln_g, gmlp_ln_b, gmlp_ws, gmlp_bs, four_ln_g, four_ln_b, router_w, router_b, w_gate_up, b_gate_up, w_down, b_down):
    b, length, d = x.shape
    depth = w_mod.shape[0]
    alpha = float((2 * depth) ** 0.25)
    assert b + 1 <= SUBLANES and length % (CHUNK * SUBLANES) == 0 and length % GRID_W == 0
    assert length % KV_TILE == 0 and ctx.shape[1] % KV_TILE == 0

    mod = _modulation(c, c_ctx, w_mod, b_mod)
    cos_t, sin_t = _rope_tables(length)
    h = x
    for layer in range(depth):
        j = layer // 2
        m = mod[layer]
        sh1, sc1, g1, sh2, sc2, g2 = [m[:b, i * d:(i + 1) * d][:, None, :] for i in range(6)]
        if layer % 2 == 0:
            lam_init = 0.8 - 0.6 * math.exp(-0.3 * layer)
            w_in = ev_w_in[j].astype(BF16)
            q, k_all, vt, a = _even_in_proj(h, sc1, sh1, w_in, cos_t, sin_t, length + ctx.shape[1])
            csh1 = jnp.broadcast_to(m[b:b + 1, 0:d][:, None, :], (b, 1, d))
            csc1 = jnp.broadcast_to(m[b:b + 1, d:2 * d][:, None, :], (b, 1, d))
            k_all, vt = _context_kv(ctx, csc1, csh1, w_in[:, QK_W:2 * QK_W + V_W], k_all, vt, length)
            att = _diff_attention(q, k_all, vt, lam_q1[j], lam_k1[j], lam_q2[j], lam_k2[j],
                                  diff_norm_g[j], lam_init)
            conv = _conformer_conv(a, conv_w[j], conv_b[j], conv_ln_g[j], conv_ln_b[j])
            ya, yb, w_out = conv, att, ev_w_out[j]
        else:
            spatial, zr, zi = _odd_in_proj(h, sc1, sh1, od_w_in[j].astype(BF16), gmlp_ln_g[j], gmlp_ln_b[j],
                                           gmlp_ws[j], gmlp_bs[j], four_ln_g[j], four_ln_b[j])
            ya, yb, w_out = spatial, _length_dft_real(zr, zi), od_w_out[j]
        h1, top_e, gate, pos, counts = _post_mix(ya, yb, w_out.astype(BF16), h, g1, ln1_g[layer], ln1_b[layer],
                                                 sc2, sh2, router_w[layer], router_b[layer], alpha)
        h = _moe_and_norm(h1, top_e[:TOP_K], gate, pos[:TOP_K], counts[0, :N_EXPERTS], sc2, sh2, g2,
                          w_gate_up, b_gate_up, w_down, b_down, layer, ln2_g[layer], ln2_b[layer], alpha)
    return h
```

```python
import functools
import math

import jax
import jax.numpy as jnp
from jax import lax
from jax.experimental import pallas as pl
from jax.experimental.pallas import tpu as pltpu

F32 = jnp.float32
BF16 = jnp.bfloat16
I32 = jnp.int32
HIGHEST = lax.Precision.HIGHEST

LN_EPS = 1e-5
GRID_W = 64
HEAD_DIM = 64
HEAD_V = 128
N_HEADS = 4
QK_W = N_HEADS * 2 * HEAD_DIM
V_W = N_HEADS * HEAD_V
CONV_W = 512
CONV_TAPS = 31
CONV_HALO = 16
ROPE_BASE = 10000.0
CHUNK = 128
N_GROUPS = 4
GROUP_W = 128
N_EXPERTS = 32
TOP_K = 4
SWIGLU_LIMIT = 7.0
SWIGLU_ALPHA = 1.702
LANES = 128
SUBLANES = 8
EXPERT_ROWS = 512
ISSUE_UNROLL = 8
ATT_KEY_CHUNKS = (1280, 256)
ATT_QUERY_BLOCKS = (1024, 512, 256, 128)
CONV_ROW_CHUNK = 64
NEG_BIG = -1e30
VMEM_LIMIT = 56 * 1024 * 1024


def _params(n_axes):
    return pltpu.CompilerParams(dimension_semantics=("arbitrary",) * n_axes,
                                vmem_limit_bytes=VMEM_LIMIT)


def _pick(n, candidates):
    for c in candidates:
        if n % c == 0:
            return c
    return n


def _layer_norm(r, g, b):
    mu = jnp.mean(r, axis=-1, keepdims=True)
    d = r - mu
    var = jnp.mean(d * d, axis=-1, keepdims=True)
    return d * lax.rsqrt(var + LN_EPS) * g + b


def _sigmoid(x):
    return 1.0 / (1.0 + jnp.exp(-x))


def _mod_kernel(cs_ref, w_ref, b_ref, o_ref):
    cs = cs_ref[...]
    a = cs * _sigmoid(cs)
    o_ref[0] = jnp.dot(a, w_ref[0], precision=HIGHEST, preferred_element_type=F32) + b_ref[0]


def _modulation(c, c_ctx, w_mod, b_mod):
    depth, d, n = w_mod.shape
    b = c.shape[0]
    cs = jnp.concatenate([c, c_ctx[None], jnp.zeros((SUBLANES - b - 1, d), F32)], 0)
    tn = _pick(n, (1536, 1024, 512))
    return pl.pallas_call(
        _mod_kernel,
        grid=(depth, n // tn),
        in_specs=[pl.BlockSpec((SUBLANES, d), lambda l, j: (0, 0)),
                  pl.BlockSpec((1, d, tn), lambda l, j: (l, 0, j)),
                  pl.BlockSpec((1, 1, tn), lambda l, j: (l, 0, j))],
        out_specs=pl.BlockSpec((1, SUBLANES, tn), lambda l, j: (l, 0, j)),
        out_shape=jax.ShapeDtypeStruct((depth, SUBLANES, n), F32),
        compiler_params=_params(2),
        name="modulation",
    )(cs, w_mod, b_mod.reshape(depth, 1, n))


KV_TILE = 256
VT_ROWS = HEAD_V + 16


def _store_values_transposed(v, vt_ref):
    extra = (lax.broadcasted_iota(I32, (VT_ROWS - HEAD_V, v.shape[0]), 0) == 0).astype(BF16)
    for h in range(N_HEADS):
        vt_ref[0, h, 0, :HEAD_V, :] = v[:, h * HEAD_V:(h + 1) * HEAD_V].T.astype(BF16)
        vt_ref[0, h, 0, HEAD_V:, :] = extra


def _even_in_kernel(x_ref, sc_ref, sh_ref, w_ref, cos_ref, sin_ref, q_ref, k_ref, vt_ref, a_ref):
    u = (x_ref[0] * (1.0 + sc_ref[0]) + sh_ref[0]).astype(BF16)
    z = jnp.dot(u, w_ref[...], preferred_element_type=F32)
    cosv = cos_ref[...]
    sinv = sin_ref[...]
    lane = lax.broadcasted_iota(I32, cosv.shape, 1)
    first = (lane % 32) < 16

    def rope(t):
        partner = jnp.where(first, pltpu.roll(t, LANES - 16, 1), pltpu.roll(t, 16, 1))
        return t * cosv + partner * sinv

    for j in range(QK_W // LANES):
        sl = slice(j * LANES, (j + 1) * LANES)
        rq = rope(z[:, sl]) * (HEAD_DIM ** -0.5 * math.log2(math.e))
        for c in range(2):
            q_ref[c, 0, j] = jnp.where(lane // HEAD_DIM == c, rq, 0.0).T.astype(BF16)
        k_ref[0, :, sl] = rope(z[:, QK_W + j * LANES:QK_W + (j + 1) * LANES]).astype(BF16)
    _store_values_transposed(z[:, 2 * QK_W:2 * QK_W + V_W], vt_ref)
    a0 = 2 * QK_W + V_W
    a_ref[0] = z[:, a0:a0 + CONV_W] * _sigmoid(z[:, a0 + CONV_W:a0 + 2 * CONV_W])


def _rope_tables(length):
    rows = length // GRID_W
    row = jnp.repeat(jnp.arange(rows, dtype=F32), GRID_W)
    col = jnp.tile(jnp.arange(GRID_W, dtype=F32), rows)
    n_freq = HEAD_DIM // 4
    inv_freq = ROPE_BASE ** (-jnp.arange(n_freq, dtype=F32) / n_freq)
    ar = row[:, None] * inv_freq
    ac = col[:, None] * inv_freq
    cos64 = jnp.concatenate([jnp.cos(ar), jnp.cos(ar), jnp.cos(ac), jnp.cos(ac)], 1)
    sin64 = jnp.concatenate([-jnp.sin(ar), jnp.sin(ar), -jnp.sin(ac), jnp.sin(ac)], 1)
    return jnp.tile(cos64, (1, LANES // HEAD_DIM)), jnp.tile(sin64, (1, LANES // HEAD_DIM))


def _key_chunk(n_keys):
    return next(c for c in ATT_KEY_CHUNKS if n_keys % c == 0 and n_keys // c >= 3)


def _kv_specs(b, n_keys, kc, first_tile):
    per_chunk = kc // KV_TILE
    k_spec = pl.BlockSpec((1, KV_TILE, QK_W), lambda bi, i: (bi, first_tile + i, 0))
    vt_spec = pl.BlockSpec((1, N_HEADS, 1, VT_ROWS, KV_TILE),
                           lambda bi, i: (bi, 0, (first_tile + i) // per_chunk, 0, (first_tile + i) % per_chunk))
    shapes = [jax.ShapeDtypeStruct((b, n_keys, QK_W), BF16),
              jax.ShapeDtypeStruct((b, N_HEADS, n_keys // kc, VT_ROWS, kc), BF16)]
    return k_spec, vt_spec, shapes


def _even_in_proj(x, sc, sh, w_bf, cos_t, sin_t, n_keys):
    b, length, d = x.shape
    n = w_bf.shape[1]
    tm = KV_TILE
    row = lambda bi, i: (bi, i, 0)
    vec = lambda bi, i: (bi, 0, 0)
    k_spec, vt_spec, kv_shapes = _kv_specs(b, n_keys, _key_chunk(n_keys), 0)
    return pl.pallas_call(
        _even_in_kernel,
        grid=(b, length // tm),
        in_specs=[pl.BlockSpec((1, tm, d), row),
                  pl.BlockSpec((1, 1, d), vec),
                  pl.BlockSpec((1, 1, d), vec),
                  pl.BlockSpec((d, n), lambda bi, i: (0, 0)),
                  pl.BlockSpec((tm, LANES), lambda bi, i: (i, 0)),
                  pl.BlockSpec((tm, LANES), lambda bi, i: (i, 0))],
        out_specs=[pl.BlockSpec((2, 1, N_HEADS, HEAD_V, tm), lambda bi, i: (0, bi, 0, 0, i)), k_spec, vt_spec,
                   pl.BlockSpec((1, tm, CONV_W), row)],
        out_shape=[jax.ShapeDtypeStruct((2, b, N_HEADS, HEAD_V, length), BF16)] + kv_shapes
                  + [jax.ShapeDtypeStruct((b, length, CONV_W), F32)],
        compiler_params=_params(2),
        name="even_in_proj",
    )(x, sc, sh, w_bf, cos_t, sin_t)


def _ctx_kv_kernel(x_ref, sc_ref, sh_ref, w_ref, k_in, vt_in, k_ref, vt_ref):
    del k_in, vt_in
    u = (x_ref[0] * (1.0 + sc_ref[0]) + sh_ref[0]).astype(BF16)
    z = jnp.dot(u, w_ref[...], preferred_element_type=F32)
    k_ref[0] = z[:, :QK_W].astype(BF16)
    _store_values_transposed(z[:, QK_W:], vt_ref)


def _context_kv(ctx, sc, sh, w_bf, k_all, vt, first_key):
    b, rows, d = ctx.shape
    n = w_bf.shape[1]
    n_keys = k_all.shape[1]
    k_spec, vt_spec, kv_shapes = _kv_specs(b, n_keys, vt.shape[4], first_key // KV_TILE)
    return pl.pallas_call(
        _ctx_kv_kernel,
        grid=(b, rows // KV_TILE),
        in_specs=[pl.BlockSpec((1, KV_TILE, d), lambda bi, i: (bi, i, 0)),
                  pl.BlockSpec((1, 1, d), lambda bi, i: (bi, 0, 0)),
                  pl.BlockSpec((1, 1, d), lambda bi, i: (bi, 0, 0)),
                  pl.BlockSpec((d, n), lambda bi, i: (0, 0)),
                  pl.BlockSpec(memory_space=pl.ANY), pl.BlockSpec(memory_space=pl.ANY)],
        out_specs=[k_spec, vt_spec],
        out_shape=kv_shapes,
        input_output_aliases={4: 0, 5: 1},
        compiler_params=_params(2),
        name="context_kv_proj",
    )(ctx, sc, sh, w_bf, k_all, vt)


def _attn_kernel(lq1_ref, lk1_ref, lq2_ref, lk2_ref, g_ref, q_ref, k_ref, vt_ref, o_ref,
                 s0_sc, s1_sc, p0_sc, p1_sc, acc_sc, *, lam_init, kc):
    mq = q_ref.shape[4]
    n = k_ref.shape[1] // kc
    s_sc = (s0_sc, s1_sc)
    p_sc = (p0_sc, p1_sc)

    def scores(j, slot):
        kj = k_ref[0, pl.ds(pl.multiple_of(j * kc, kc), kc), :]
        for c in range(2):
            s_sc[slot][c] = jnp.dot(kj, q_ref[c, 0, 0], preferred_element_type=F32)

    def softmax(slot, m):
        m_out, alpha = [], []
        for c in range(2):
            s = s_sc[slot][c]
            m_new = jnp.maximum(m[c], jnp.max(s, axis=0, keepdims=True))
            p_sc[slot][c] = jnp.exp2((s - m_new).astype(BF16))
            m_out.append(m_new)
            alpha.append(jnp.exp2(m[c] - m_new))
        return tuple(m_out), tuple(alpha)

    def accumulate(j, slot, alpha):
        vt = vt_ref[0, 0, j]
        for c in range(2):
            acc_sc[c] = alpha[c] * acc_sc[c] + jnp.dot(vt, p_sc[slot][c], preferred_element_type=F32)

    def step(j, slot, m, alpha):
        scores(j + 2, slot)
        m, alpha_next = softmax(1 - slot, m)
        accumulate(j, slot, alpha)
        return m, alpha_next

    acc_sc[...] = jnp.zeros(acc_sc.shape, F32)
    m = (jnp.full((1, mq), NEG_BIG, F32),) * 2
    scores(0, 0)
    m, alpha = softmax(0, m)
    scores(1, 1)

    def pair(t, carry):
        m, alpha = step(2 * t, 0, *carry)
        return step(2 * t + 1, 1, m, alpha)

    n_steps = n - 2
    m, alpha = lax.fori_loop(0, n_steps // 2, pair, (m, alpha))
    if n_steps % 2:
        m, alpha = step(n_steps - 1, 0, m, alpha)
    m, alpha_last = softmax((n - 1) % 2, m)
    accumulate(n - 2, (n - 2) % 2, alpha)
    accumulate(n - 1, (n - 1) % 2, alpha_last)
    acc0 = acc_sc[0]
    acc1 = acc_sc[1]
    lam = (jnp.exp(jnp.sum(lq1_ref[...] * lk1_ref[...], keepdims=True))
           - jnp.exp(jnp.sum(lq2_ref[...] * lk2_ref[...], keepdims=True)) + lam_init)
    o = (acc0[:HEAD_V] / acc0[HEAD_V:HEAD_V + 1]
         - lam * (acc1[:HEAD_V] / acc1[HEAD_V:HEAD_V + 1]))
    ms = jnp.mean(o * o, axis=0, keepdims=True)
    o = o * lax.rsqrt(ms + LN_EPS) * g_ref[...] * (1.0 - lam_init)
    o_ref[0] = o.T.astype(o_ref.dtype)


def _diff_attention(q, k_all, vt, lq1, lk1, lq2, lk2, norm_g, lam_init):
    _, b, _, _, length = q.shape
    n_keys = k_all.shape[1]
    mq = _pick(length, ATT_QUERY_BLOCKS)
    nc, kc = vt.shape[2], vt.shape[4]
    small = lambda bi, h, i: (0, 0)
    return pl.pallas_call(
        functools.partial(_attn_kernel, lam_init=lam_init, kc=kc),
        grid=(b, N_HEADS, length // mq),
        in_specs=[pl.BlockSpec((1, HEAD_DIM), small), pl.BlockSpec((1, HEAD_DIM), small),
                  pl.BlockSpec((1, HEAD_DIM), small), pl.BlockSpec((1, HEAD_DIM), small),
                  pl.BlockSpec((HEAD_V, 1), small),
                  pl.BlockSpec((2, 1, 1, HEAD_V, mq), lambda bi, h, i: (0, bi, h, 0, i)),
                  pl.BlockSpec((1, n_keys, HEAD_V), lambda bi, h, i: (bi, 0, h)),
                  pl.BlockSpec((1, 1, nc, VT_ROWS, kc), lambda bi, h, i: (bi, h, 0, 0, 0))],
        out_specs=pl.BlockSpec((1, mq, HEAD_V), lambda bi, h, i: (bi, i, h)),
        out_shape=jax.ShapeDtypeStruct((b, length, V_W), BF16),
        scratch_shapes=[pltpu.VMEM((2, kc, mq), F32), pltpu.VMEM((2, kc, mq), F32),
                        pltpu.VMEM((2, kc, mq), BF16), pltpu.VMEM((2, kc, mq), BF16),
                        pltpu.VMEM((2, VT_ROWS, mq), F32)],
        compiler_params=_params(3),
        name="diff_attention",
    )(lq1[None], lk1[None], lq2[None], lk2[None], norm_g[:, None], q, k_all, vt)


def _conv_kernel(prev_ref, cur_ref, next_ref, w_ref, b_ref, g_ref, beta_ref, o_ref, win, *, row_chunk):
    i = pl.program_id(1)
    tc = cur_ref.shape[1]
    span = tc + 2 * CONV_HALO - SUBLANES
    win[0, CONV_HALO:CONV_HALO + tc, :] = cur_ref[0]
    win[0, 0:CONV_HALO, :] = jnp.where(i > 0, prev_ref[0], 0.0)
    win[0, CONV_HALO + tc:, :] = jnp.where(i < pl.num_programs(1) - 1, next_ref[0], 0.0)
    for s in range(1, SUBLANES):
        win[s, 0:span, :] = win[0, s:s + span, :]
    first_tap = CONV_HALO - CONV_TAPS // 2
    for c0 in range(0, tc, row_chunk):
        acc = jnp.zeros((row_chunk, CONV_W), F32)
        for t in range(CONV_TAPS):
            s = (first_tap + t) % SUBLANES
            r0 = c0 + first_tap + t - s
            acc = acc + w_ref[t:t + 1, :] * win[s, r0:r0 + row_chunk, :]
        y = _layer_norm(acc + b_ref[...], g_ref[...], beta_ref[...])
        o_ref[0, c0:c0 + row_chunk, :] = (y * _sigmoid(y)).astype(o_ref.dtype)


def _conformer_conv(a, conv_w, conv_b, ln_g, ln_b):
    b, length, w = a.shape
    tc = _pick(length, (512, 256, 128))
    hb = tc // CONV_HALO
    n_halo = length // CONV_HALO
    w_pad = jnp.concatenate([conv_w, jnp.zeros((1, w), F32)], 0)
    vec = lambda bi, i: (0, 0)
    return pl.pallas_call(
        functools.partial(_conv_kernel, row_chunk=CONV_ROW_CHUNK),
        grid=(b, length // tc),
        in_specs=[pl.BlockSpec((1, CONV_HALO, w), lambda bi, i: (bi, jnp.maximum(i * hb - 1, 0), 0)),
                  pl.BlockSpec((1, tc, w), lambda bi, i: (bi, i, 0)),
                  pl.BlockSpec((1, CONV_HALO, w), lambda bi, i: (bi, jnp.minimum((i + 1) * hb, n_halo - 1), 0)),
                  pl.BlockSpec((CONV_TAPS + 1, w), vec),
                  pl.BlockSpec((1, w), vec), pl.BlockSpec((1, w), vec), pl.BlockSpec((1, w), vec)],
        out_specs=pl.BlockSpec((1, tc, w), lambda bi, i: (bi, i, 0)),
        out_shape=jax.ShapeDtypeStruct((b, length, w), BF16),
        scratch_shapes=[pltpu.VMEM((SUBLANES, tc + 2 * CONV_HALO, w), F32)],
        compiler_params=_params(2),
        name="conformer_conv",
    )(a, a, a, w_pad, conv_b[None], ln_g[None], ln_b[None])


def _post_mix_kernel(ya_ref, yb_ref, wa_ref, wb_ref, x_ref, g1_ref, lng_ref, lnb_ref, sc_ref, sh_ref,
                     rw_ref, rb_ref, h_ref, e_ref, gate_ref, pos_ref, cnt_ref, carry, *, alpha):
    first_step = jnp.logical_and(pl.program_id(0) == 0, pl.program_id(1) == 0)

    @pl.when(first_step)
    def _():
        carry[...] = jnp.zeros(carry.shape, F32)

    y = (jnp.dot(ya_ref[0].astype(BF16), wa_ref[...], preferred_element_type=F32)
         + jnp.dot(yb_ref[0].astype(BF16), wb_ref[...], preferred_element_type=F32))
    h = _layer_norm(alpha * x_ref[0] + g1_ref[0] * y, lng_ref[...], lnb_ref[...])
    h_ref[0] = h
    u2 = h * (1.0 + sc_ref[0]) + sh_ref[0]
    u_hi = u2.astype(BF16)
    u_lo = (u2 - u_hi.astype(F32)).astype(BF16)
    hh_hl = jnp.dot(u_hi, rw_ref[...], preferred_element_type=F32)
    lh = jnp.dot(u_lo, rw_ref[:, :LANES], preferred_element_type=F32)
    logits = hh_hl[:, :LANES] + hh_hl[:, LANES:] + lh + rb_ref[...]

    tm = logits.shape[0]
    lane = lax.broadcasted_iota(I32, logits.shape, 1).astype(F32)
    vals, idxs = [], []
    lg = logits
    for _ in range(TOP_K):
        mx = jnp.max(lg, axis=1, keepdims=True)
        idx = jnp.min(jnp.where(lg == mx, lane, float(LANES)), axis=1, keepdims=True)
        vals.append(mx)
        idxs.append(idx)
        lg = jnp.where(lane == idx, -3e38, lg)
    exps = [jnp.exp(v - vals[0]) for v in vals]
    den = exps[0] + exps[1] + exps[2] + exps[3]

    hot = jnp.zeros(logits.shape, F32)
    for idx in idxs:
        hot = hot + (lane == idx).astype(F32)
    r_i = lax.broadcasted_iota(I32, (tm, tm), 0)
    c_i = lax.broadcasted_iota(I32, (tm, tm), 1)
    tri = (c_i < r_i).astype(BF16)
    before = jnp.dot(tri, hot.astype(BF16), preferred_element_type=F32) + carry[...]
    e_out = jnp.zeros(logits.shape, F32)
    g_out = jnp.zeros(logits.shape, F32)
    p_out = jnp.zeros(logits.shape, F32)
    for j in range(TOP_K):
        pos_j = jnp.sum(jnp.where(lane == idxs[j], before, 0.0), axis=1, keepdims=True)
        sel = lane == float(j)
        e_out = jnp.where(sel, idxs[j], e_out)
        g_out = jnp.where(sel, exps[j] / den, g_out)
        p_out = jnp.where(sel, pos_j, p_out)
    e_ref[...] = e_out.T[:SUBLANES].astype(I32)
    gate_ref[...] = g_out
    pos_ref[...] = p_out.T[:SUBLANES].astype(I32)
    new_carry = carry[...] + jnp.sum(hot, axis=0, keepdims=True)
    carry[...] = new_carry
    cnt_ref[...] = jnp.broadcast_to(new_carry, cnt_ref.shape).astype(I32)


def _post_mix(ya, yb, w_out_bf, x, g1, ln_g, ln_b, sc2, sh2, router_w, router_b, alpha):
    b, length, d = x.shape
    wa = ya.shape[2]
    tm = _pick(length, (512, 256, 128))
    n_tok = b * length
    rw = jnp.concatenate([router_w, jnp.zeros((d, LANES - N_EXPERTS), F32)], 1)
    rw_hi = rw.astype(BF16)
    rw = jnp.concatenate([rw_hi, (rw - rw_hi.astype(F32)).astype(BF16)], 1)
    rb = jnp.concatenate([router_b, jnp.full((LANES - N_EXPERTS,), NEG_BIG, F32)])[None]
    row = lambda bi, i: (bi, i, 0)
    vec = lambda bi, i: (bi, 0, 0)
    const = lambda bi, i: (0, 0)
    nt = length // tm
    tok = lambda bi, i: (bi * nt + i, 0)
    return pl.pallas_call(
        functools.partial(_post_mix_kernel, alpha=alpha),
        grid=(b, nt),
        in_specs=[pl.BlockSpec((1, tm, wa), row), pl.BlockSpec((1, tm, d - wa), row),
                  pl.BlockSpec((wa, d), const), pl.BlockSpec((d - wa, d), const),
                  pl.BlockSpec((1, tm, d), row), pl.BlockSpec((1, 1, d), vec),
                  pl.BlockSpec((1, d), const), pl.BlockSpec((1, d), const),
                  pl.BlockSpec((1, 1, d), vec), pl.BlockSpec((1, 1, d), vec),
                  pl.BlockSpec((d, 2 * LANES), const), pl.BlockSpec((1, LANES), const)],
        out_specs=[pl.BlockSpec((1, tm, d), row), pl.BlockSpec((SUBLANES, tm), lambda bi, i: (0, bi * nt + i)),
                   pl.BlockSpec((tm, LANES), tok), pl.BlockSpec((SUBLANES, tm), lambda bi, i: (0, bi * nt + i)),
                   pl.BlockSpec((SUBLANES, LANES), const)],
        out_shape=[jax.ShapeDtypeStruct((b, length, d), F32), jax.ShapeDtypeStruct((SUBLANES, n_tok), I32),
                   jax.ShapeDtypeStruct((n_tok, LANES), F32), jax.ShapeDtypeStruct((SUBLANES, n_tok), I32),
                   jax.ShapeDtypeStruct((SUBLANES, LANES), I32)],
        scratch_shapes=[pltpu.VMEM((1, LANES), F32)],
        compiler_params=_params(2),
        name="out_proj_norm_router",
    )(ya, yb, w_out_bf[:wa], w_out_bf[wa:], x, g1, ln_g[None], ln_b[None], sc2, sh2, rw, rb)


def _tile_indices(dest, tm):
    n_tok = dest.shape[1]
    return dest.reshape(TOP_K, n_tok // tm, tm).transpose(1, 0, 2).reshape(n_tok // tm, 1, TOP_K * tm)


def _dispatch_kernel(zstart_ref, zvalid_ref, dest_ref, h_ref, sc_ref, sh_ref, xs_ref, u_sc, zero_sc, sem, zsem):
    tm = h_ref.shape[0]

    @pl.when(pl.program_id(0) == 0)
    def _():
        zero_sc[...] = jnp.zeros(zero_sc.shape, F32)

        def fill(e, c):
            @pl.when(zvalid_ref[e] > 0)
            def _():
                start = pl.multiple_of(zstart_ref[e], SUBLANES)
                pltpu.make_async_copy(zero_sc, xs_ref.at[pl.ds(start, EXPERT_ROWS)], zsem).start()
            return c

        lax.fori_loop(0, N_EXPERTS, fill, 0)

        def drain(e, c):
            @pl.when(zvalid_ref[e] > 0)
            def _():
                pltpu.make_async_copy(zero_sc, xs_ref.at[pl.ds(0, EXPERT_ROWS)], zsem).wait()
            return c

        lax.fori_loop(0, N_EXPERTS, drain, 0)

    i = pl.program_id(0)
    slot = i % 2
    src = u_sc.at[slot]
    src[...] = h_ref[...] * (1.0 + sc_ref[0]) + sh_ref[0]

    def issue(g, c):
        for rr in range(ISSUE_UNROLL):
            r = g * ISSUE_UNROLL + rr
            for j in range(TOP_K):
                d = dest_ref[0, 0, j * tm + r]
                pltpu.make_async_copy(src.at[pl.ds(r, 1)], xs_ref.at[pl.ds(d, 1)],
                                      sem.at[slot]).start(priority=(rr * TOP_K + j) % 2)
        return c

    lax.fori_loop(0, tm // ISSUE_UNROLL, issue, 0)

    def wait_tile(which):
        rows = xs_ref.at[pl.ds(0, tm * TOP_K)]
        pltpu.make_async_copy(rows, rows, sem.at[which]).wait()

    pl.when(i > 0)(lambda: wait_tile(1 - slot))
    pl.when(i == pl.num_programs(0) - 1)(lambda: wait_tile(slot))


def _dispatch(h_flat, sc2, sh2, dest, zstart, zvalid, n_rows, tokens_per_batch):
    n_tok, d = h_flat.shape
    tm = _pick(tokens_per_batch, (512, 256, 128))
    per_b = tokens_per_batch // tm
    return pl.pallas_call(
        _dispatch_kernel,
        grid_spec=pltpu.PrefetchScalarGridSpec(
            num_scalar_prefetch=2,
            grid=(n_tok // tm,),
            in_specs=[pl.BlockSpec((1, 1, tm * TOP_K), lambda i, zs, zv: (i, 0, 0), memory_space=pltpu.SMEM),
                      pl.BlockSpec((tm, d), lambda i, zs, zv: (i, 0)),
                      pl.BlockSpec((1, 1, d), lambda i, zs, zv: (i // per_b, 0, 0)),
                      pl.BlockSpec((1, 1, d), lambda i, zs, zv: (i // per_b, 0, 0))],
            out_specs=pl.BlockSpec(memory_space=pl.ANY),
            scratch_shapes=[pltpu.VMEM((2, tm, d), F32), pltpu.VMEM((EXPERT_ROWS, d), F32),
                            pltpu.SemaphoreType.DMA((2,)), pltpu.SemaphoreType.DMA(())]),
        out_shape=jax.ShapeDtypeStruct((n_rows, d), F32),
        compiler_params=_params(1),
        name="moe_dispatch",
    )(zstart, zvalid, _tile_indices(dest, tm), h_flat, sc2, sh2)


def _expert_kernel(blk_e_ref, n_used_ref, x_ref, wgu_ref, bgu_ref, wd_ref, bd_ref, o_ref, wgu_bf, wd_bf):
    i = pl.program_id(0)
    new_expert = jnp.logical_or(i == 0, blk_e_ref[i] != blk_e_ref[jnp.maximum(i - 1, 0)])

    @pl.when(new_expert)
    def _():
        wgu_bf[...] = wgu_ref[0].astype(BF16)
        wd_bf[...] = wd_ref[0].astype(BF16)

    @pl.when(i < n_used_ref[0])
    def _():
        d_e = wd_ref.shape[1]
        gu = jnp.dot(x_ref[...].astype(BF16), wgu_bf[...], preferred_element_type=F32) + bgu_ref[0]
        x_glu = jnp.minimum(gu[:, :d_e], SWIGLU_LIMIT)
        x_lin = jnp.clip(gu[:, d_e:], -SWIGLU_LIMIT, SWIGLU_LIMIT)
        act = x_glu * _sigmoid(SWIGLU_ALPHA * x_glu) * (x_lin + 1.0)
        o_ref[...] = jnp.dot(act.astype(BF16), wd_bf[...], preferred_element_type=F32) + bd_ref[0]


def _experts(xs, blk_e, n_used, wgu, bgu, wd, bd, layer):
    n_rows, d = xs.shape
    depth, n_e, _, n_gu = wgu.shape
    n_blk = n_rows // EXPERT_ROWS
    rows = lambda i, be, nu: (jnp.minimum(i, nu[0] - 1), 0)
    first = layer * n_e
    by_e = lambda i, be, nu: (first + be[i], 0, 0)
    n_e = depth * n_e
    wgu = wgu.reshape(n_e, d, n_gu)
    wd = wd.reshape(n_e, n_gu // 2, d)
    return pl.pallas_call(
        _expert_kernel,
        grid_spec=pltpu.PrefetchScalarGridSpec(
            num_scalar_prefetch=2,
            grid=(n_blk,),
            in_specs=[pl.BlockSpec((EXPERT_ROWS, d), rows),
                      pl.BlockSpec((1, d, n_gu), by_e), pl.BlockSpec((1, 1, n_gu), by_e),
                      pl.BlockSpec((1, n_gu // 2, d), by_e), pl.BlockSpec((1, 1, d), by_e)],
            out_specs=pl.BlockSpec((EXPERT_ROWS, d), rows),
            scratch_shapes=[pltpu.VMEM((d, n_gu), BF16), pltpu.VMEM((n_gu // 2, d), BF16)]),
        out_shape=jax.ShapeDtypeStruct((n_rows, d), F32),
        compiler_params=_params(1),
        name="moe_experts",
    )(blk_e, n_used, xs, wgu, bgu.reshape(n_e, 1, n_gu), wd, bd.reshape(n_e, 1, d))


def _combine_kernel(dest_ref, dest_next_ref, y_ref, h_ref, gate_ref, g2_ref, lng_ref, lnb_ref, o_ref,
                    buf, sem, *, alpha):
    i = pl.program_id(0)
    tm = h_ref.shape[0]

    def fetch(idx_ref, slot):
        def issue(g, c):
            for rr in range(ISSUE_UNROLL):
                r = g * ISSUE_UNROLL + rr
                for j in range(TOP_K):
                    d = idx_ref[0, 0, j * tm + r]
                    pltpu.make_async_copy(y_ref.at[pl.ds(d, 1)], buf.at[slot, j, pl.ds(r, 1)],
                                          sem.at[slot]).start(priority=(rr * TOP_K + j) % 2)
            return c

        lax.fori_loop(0, tm // ISSUE_UNROLL, issue, 0)

    def fetch_unrolled(idx_ref, slot):
        for r in range(tm):
            for j in range(TOP_K):
                d = idx_ref[0, 0, j * tm + r]
                pltpu.make_async_copy(y_ref.at[pl.ds(d, 1)], buf.at[slot, j, pl.ds(r, 1)],
                                      sem.at[slot]).start(priority=(r * TOP_K + j) % 2)

    slot = i % 2
    more = i + 1 < pl.num_programs(0)

    @pl.when(i == 0)
    def _():
        fetch(dest_ref, 0)

    pl.when(jnp.logical_and(more, slot == 0))(lambda: fetch_unrolled(dest_next_ref, 1))
    pl.when(jnp.logical_and(more, slot == 1))(lambda: fetch_unrolled(dest_next_ref, 0))

    pltpu.make_async_copy(buf.at[slot], buf.at[slot], sem.at[slot]).wait()
    gate = gate_ref[...]
    y2 = gate[:, 0:1] * buf[slot, 0]
    for j in range(1, TOP_K):
        y2 = y2 + gate[:, j:j + 1] * buf[slot, j]
    o_ref[...] = _layer_norm(alpha * h_ref[...] + g2_ref[0] * y2, lng_ref[...], lnb_ref[...])


def _combine(y_rows, dest, h_flat, gate, g2, ln_g, ln_b, alpha, tokens_per_batch):
    n_tok, d = h_flat.shape
    tm = _pick(tokens_per_batch, (256, 128))
    per_b = tokens_per_batch // tm
    n_tiles = n_tok // tm
    dest3 = _tile_indices(dest, tm)
    return pl.pallas_call(
        functools.partial(_combine_kernel, alpha=alpha),
        grid=(n_tiles,),
        in_specs=[pl.BlockSpec((1, 1, tm * TOP_K), lambda i: (i, 0, 0), memory_space=pltpu.SMEM),
                  pl.BlockSpec((1, 1, tm * TOP_K), lambda i: (jnp.minimum(i + 1, n_tiles - 1), 0, 0),
                               memory_space=pltpu.SMEM),
                  pl.BlockSpec(memory_space=pl.ANY),
                  pl.BlockSpec((tm, d), lambda i: (i, 0)),
                  pl.BlockSpec((tm, LANES), lambda i: (i, 0)),
                  pl.BlockSpec((1, 1, d), lambda i: (i // per_b, 0, 0)),
                  pl.BlockSpec((1, d), lambda i: (0, 0)), pl.BlockSpec((1, d), lambda i: (0, 0))],
        out_specs=pl.BlockSpec((tm, d), lambda i: (i, 0)),
        out_shape=jax.ShapeDtypeStruct((n_tok, d), F32),
        scratch_shapes=[pltpu.VMEM((2, TOP_K, tm, d), F32), pltpu.SemaphoreType.DMA((2,))],
        compiler_params=_params(1),
        name="moe_combine_norm",
    )(dest3, dest3, y_rows, h_flat, gate, g2, ln_g[None], ln_b[None])


def _moe_and_norm(h1, top_e, gate, pos, counts, sc2, sh2, g2, wgu, bgu, wd, bd, layer, ln_g, ln_b, alpha):
    b, length, d = h1.shape
    n_tok = b * length
    n_blk = n_tok * TOP_K // EXPERT_ROWS + N_EXPERTS
    padded = (counts + EXPERT_ROWS - 1) // EXPERT_ROWS * EXPERT_ROWS
    pad_end = jnp.cumsum(padded)
    offset = pad_end - padded
    n_used = (pad_end[-1:] // EXPERT_ROWS).astype(I32)
    blk_start = jnp.arange(n_blk, dtype=I32) * EXPERT_ROWS
    blk_e = jnp.minimum(jnp.sum(pad_end[None, :] <= blk_start[:, None], axis=1), N_EXPERTS - 1).astype(I32)
    dest = pos
    for e in range(N_EXPERTS):
        dest = dest + jnp.where(top_e == e, offset[e], 0)
    dest = dest.astype(I32)
    zstart = jnp.maximum(pad_end - EXPERT_ROWS, 0).astype(I32)
    zvalid = (counts > 0).astype(I32)

    h_flat = h1.reshape(n_tok, d)
    xs = _dispatch(h_flat, sc2, sh2, dest, zstart, zvalid, n_blk * EXPERT_ROWS, length)
    y_rows = _experts(xs, blk_e, n_used, wgu, bgu, wd, bd, layer)
    out = _combine(y_rows, dest, h_flat, gate, g2, ln_g, ln_b, alpha, length)
    return out.reshape(b, length, d)


def _odd_in_kernel(x_ref, sc_ref, sh_ref, w_ref, glg_ref, glb_ref, ws_ref, bs_ref, flg_ref, flb_ref, dft_ref,
                   sp_ref, zr_ref, zi_ref):
    u = (x_ref[0] * (1.0 + sc_ref[0]) + sh_ref[0]).astype(BF16)
    z = jnp.dot(u, w_ref[...], preferred_element_type=F32)
    tm = z.shape[0]
    gw = N_GROUPS * GROUP_W
    ug = jax.nn.gelu(z[:, :gw])
    vn = _layer_norm(jax.nn.gelu(z[:, gw:2 * gw]), glg_ref[...], glb_ref[...]).astype(BF16)
    f = z[:, 2 * gw:]
    for g in range(N_GROUPS):
        cols = slice(g * GROUP_W, (g + 1) * GROUP_W)
        for c0 in range(0, tm, CHUNK):
            rows = slice(c0, c0 + CHUNK)
            sv = jnp.dot(ws_ref[g], vn[rows, cols], preferred_element_type=F32) + bs_ref[:, cols]
            sp_ref[0, rows, cols] = (ug[rows, cols] * sv).astype(sp_ref.dtype)
        fn = _layer_norm(f[:, cols], flg_ref[:, cols], flb_ref[:, cols]).astype(BF16)
        zz = jnp.dot(fn, dft_ref[...], preferred_element_type=F32)
        zr_ref[0, :, cols] = zz[:, :GROUP_W]
        zi_ref[0, :, cols] = zz[:, GROUP_W:]


def _odd_in_proj(x, sc, sh, w_bf, gln_g, gln_b, ws, bs, fln_g, fln_b):
    b, length, d = x.shape
    n = w_bf.shape[1]
    gw = N_GROUPS * GROUP_W
    tm = _pick(length, (512, 256, 128))
    kk = jnp.arange(GROUP_W, dtype=I32)
    ang = (2.0 * math.pi / GROUP_W) * ((kk[:, None] * kk[None, :]) % GROUP_W).astype(F32)
    dft = jnp.concatenate([jnp.cos(ang), -jnp.sin(ang)], 1).astype(BF16)
    bs_exp = jnp.repeat(bs.T, GROUP_W, axis=1)
    row = lambda bi, i: (bi, i, 0)
    vec = lambda bi, i: (bi, 0, 0)
    const2 = lambda bi, i: (0, 0)
    return pl.pallas_call(
        _odd_in_kernel,
        grid=(b, length // tm),
        in_specs=[pl.BlockSpec((1, tm, d), row), pl.BlockSpec((1, 1, d), vec), pl.BlockSpec((1, 1, d), vec),
                  pl.BlockSpec((d, n), const2),
                  pl.BlockSpec((1, gw), const2), pl.BlockSpec((1, gw), const2),
                  pl.BlockSpec((N_GROUPS, CHUNK, CHUNK), lambda bi, i: (0, 0, 0)),
                  pl.BlockSpec((CHUNK, gw), const2),
                  pl.BlockSpec((1, gw), const2), pl.BlockSpec((1, gw), const2),
                  pl.BlockSpec((GROUP_W, 2 * GROUP_W), const2)],
        out_specs=[pl.BlockSpec((1, tm, gw), row), pl.BlockSpec((1, tm, gw), row), pl.BlockSpec((1, tm, gw), row)],
        out_shape=[jax.ShapeDtypeStruct((b, length, gw), BF16), jax.ShapeDtypeStruct((b, length, gw), F32),
                   jax.ShapeDtypeStruct((b, length, gw), F32)],
        compiler_params=_params(2),
        name="odd_in_proj",
    )(x, sc, sh, w_bf, gln_g[None], gln_b[None], ws.astype(BF16), bs_exp, fln_g[None], fln_b[None], dft)


def _fft_a_kernel(zr_ref, zi_ref, cms_ref, c_ref, s_ref, ar_ref, ai_ref):
    l1, nb, w = zr_ref.shape[1:]
    zr = zr_ref[0].reshape(l1 * nb, w)
    zi = zi_ref[0].reshape(l1 * nb, w)
    k1 = jnp.dot(cms_ref[...], zr.astype(BF16), preferred_element_type=F32)
    k2 = jnp.dot(c_ref[...], (zi - zr).astype(BF16), preferred_element_type=F32)
    k3 = jnp.dot(s_ref[...], (zr + zi).astype(BF16), preferred_element_type=F32)
    ar_ref[0] = (k1 + k3).reshape(l1, nb, w)
    ai_ref[0] = (k1 + k2).reshape(l1, nb, w)


def _fft_b_kernel(ar_ref, ai_ref, c_ref, s_ref, mask_ref, o_ref, *, norm):
    nb, l2, w = ar_ref.shape[1:]
    ar = ar_ref[0].reshape(nb * l2, w).astype(BF16)
    ai = ai_ref[0].reshape(nb * l2, w).astype(BF16)
    mask = mask_ref[...]
    c = jnp.concatenate([c_ref[0]] * nb, axis=1) * mask
    s = jnp.concatenate([s_ref[0]] * nb, axis=1) * mask
    y = jnp.dot(c, ar, preferred_element_type=F32) + jnp.dot(s, ai, preferred_element_type=F32)
    o_ref[0] = (y * norm).reshape(l2, nb, w)


def _length_dft_real(zr, zi):
    b, length, w = zr.shape
    l2 = CHUNK
    l1 = length // l2
    nb = SUBLANES
    k1 = jnp.arange(l1, dtype=I32)
    ang_a = (2.0 * math.pi / l1) * ((k1[:, None] * k1[None, :]) % l1).astype(F32)
    r = lax.broadcasted_iota(I32, (l1 * nb, l1 * nb), 0)
    c = lax.broadcasted_iota(I32, (l1 * nb, l1 * nb), 1)
    same = r % nb == c % nb
    rep = (lax.broadcasted_iota(I32, (l1 * nb, l1), 0) // nb == lax.broadcasted_iota(I32, (l1 * nb, l1), 1))
    rep = rep.astype(BF16)

    def expand_a(t):
        t = jnp.dot(jnp.dot(rep, t.astype(BF16), preferred_element_type=F32).astype(BF16), rep.T,
                    preferred_element_type=F32)
        return jnp.where(same, t, 0.0).astype(BF16)

    cos_a = expand_a(jnp.cos(ang_a))
    sin_a = expand_a(jnp.sin(ang_a))
    cms_a = expand_a(jnp.cos(ang_a) - jnp.sin(ang_a))
    shape_t = (l1 // nb, l2 * nb, l2)
    kb = lax.broadcasted_iota(I32, shape_t, 0)
    r = lax.broadcasted_iota(I32, shape_t, 1)
    n2 = lax.broadcasted_iota(I32, shape_t, 2)
    ang_b = (2.0 * math.pi / length) * ((n2 * (kb * nb + r % nb + l1 * (r // nb))) % length).astype(F32)
    cos_b = jnp.cos(ang_b).astype(BF16)
    sin_b = jnp.sin(ang_b).astype(BF16)
    shape_m = (l2 * nb, nb * l2)
    mask_b = (lax.broadcasted_iota(I32, shape_m, 0) % nb == lax.broadcasted_iota(I32, shape_m, 1) // l2)
    mask_b = mask_b.astype(BF16)

    zr4 = zr.reshape(b, l1, l2, w)
    zi4 = zi.reshape(b, l1, l2, w)
    blk_a = pl.BlockSpec((1, l1, nb, w), lambda bi, i: (bi, 0, i, 0))
    ar, ai = pl.pallas_call(
        _fft_a_kernel,
        grid=(b, l2 // nb),
        in_specs=[blk_a, blk_a] + [pl.BlockSpec((l1 * nb, l1 * nb), lambda bi, i: (0, 0))] * 3,
        out_specs=[blk_a, blk_a],
        out_shape=[jax.ShapeDtypeStruct((b, l1, l2, w), F32)] * 2,
        compiler_params=_params(2),
        name="fourier_stage_a",
    )(zr4, zi4, cms_a, cos_a, sin_a)
    blk_b = pl.BlockSpec((1, nb, l2, w), lambda bi, i: (bi, i, 0, 0))
    out = pl.pallas_call(
        functools.partial(_fft_b_kernel, norm=float((length * GROUP_W) ** -0.5)),
        grid=(b, l1 // nb),
        in_specs=[blk_b, blk_b, pl.BlockSpec((1, l2 * nb, l2), lambda bi, i: (i, 0, 0)),
                  pl.BlockSpec((1, l2 * nb, l2), lambda bi, i: (i, 0, 0)),
                  pl.BlockSpec((l2 * nb, nb * l2), lambda bi, i: (0, 0))],
        out_specs=pl.BlockSpec((1, l2, nb, w), lambda bi, i: (bi, 0, i, 0)),
        out_shape=jax.ShapeDtypeStruct((b, l2, l1, w), F32),
        compiler_params=_params(2),
        name="fourier_stage_b",
    )(ar, ai, cos_b, sin_b, mask_b)
    return out.reshape(b, length, w)


def kernel(x, c, ctx, c_ctx, w_mod, b_mod, ln1_g, ln1_b, ln2_g, ln2_b, ev_w_in, ev_w_out, conv_w, conv_b, conv_ln_g, conv_ln_b, lam_q1, lam_k1, lam_q2, lam_k2, diff_norm_g, od_w_in, od_w_out, gmlp_ln_g, gmlp_ln_b, gmlp_ws, gmlp_bs, four_ln_g, four_ln_b, router_w, router_b, w_gate_up, b_gate_up, w_down, b_down):
    b, length, d = x.shape
    depth = w_mod.shape[0]
    alpha = float((2 * depth) ** 0.25)
    assert b + 1 <= SUBLANES and length % (CHUNK * SUBLANES) == 0 and length % GRID_W == 0
    assert length % KV_TILE == 0 and ctx.shape[1] % KV_TILE == 0

    mod = _modulation(c, c_ctx, w_mod, b_mod)
    cos_t, sin_t = _rope_tables(length)
    h = x
    for layer in range(depth):
        j = layer // 2
        m = mod[layer]
        sh1, sc1, g1, sh2, sc2, g2 = [m[:b, i * d:(i + 1) * d][:, None, :] for i in range(6)]
        if layer % 2 == 0:
            lam_init = 0.8 - 0.6 * math.exp(-0.3 * layer)
            w_in = ev_w_in[j].astype(BF16)
            q, k_all, vt, a = _even_in_proj(h, sc1, sh1, w_in, cos_t, sin_t, length + ctx.shape[1])
            csh1 = jnp.broadcast_to(m[b:b + 1, 0:d][:, None, :], (b, 1, d))
            csc1 = jnp.broadcast_to(m[b:b + 1, d:2 * d][:, None, :], (b, 1, d))
            k_all, vt = _context_kv(ctx, csc1, csh1, w_in[:, QK_W:2 * QK_W + V_W], k_all, vt, length)
            att = _diff_attention(q, k_all, vt, lam_q1[j], lam_k1[j], lam_q2[j], lam_k2[j],
                                  diff_norm_g[j], lam_init)
            conv = _conformer_conv(a, conv_w[j], conv_b[j], conv_ln_g[j], conv_ln_b[j])
            ya, yb, w_out = conv, att, ev_w_out[j]
        else:
            spatial, zr, zi = _odd_in_proj(h, sc1, sh1, od_w_in[j].astype(BF16), gmlp_ln_g[j], gmlp_ln_b[j],
                                           gmlp_ws[j], gmlp_bs[j], four_ln_g[j], four_ln_b[j])
            ya, yb, w_out = spatial, _length_dft_real(zr, zi), od_w_out[j]
        h1, top_e, gate, pos, counts = _post_mix(ya, yb, w_out.astype(BF16), h, g1, ln1_g[layer], ln1_b[layer],
                                                 sc2, sh2, router_w[layer], router_b[layer], alpha)
        h = _moe_and_norm(h1, top_e[:TOP_K], gate, pos[:TOP_K], counts[0, :N_EXPERTS], sc2, sh2, g2,
                          w_gate_up, b_gate_up, w_down, b_down, layer, ln2_g[layer], ln2_b[layer], alpha)
    return h
```

```python
import functools
import math

import jax
import jax.numpy as jnp
from jax import lax
from jax.experimental import pallas as pl
from jax.experimental.pallas import tpu as pltpu

F32 = jnp.float32
BF16 = jnp.bfloat16
I32 = jnp.int32
HIGHEST = lax.Precision.HIGHEST

LN_EPS = 1e-5
GRID_W = 64
HEAD_DIM = 64
HEAD_V = 128
N_HEADS = 4
QK_W = N_HEADS * 2 * HEAD_DIM
V_W = N_HEADS * HEAD_V
CONV_W = 512
CONV_TAPS = 31
CONV_HALO = 16
ROPE_BASE = 10000.0
CHUNK = 128
N_GROUPS = 4
GROUP_W = 128
N_EXPERTS = 32
TOP_K = 4
SWIGLU_LIMIT = 7.0
SWIGLU_ALPHA = 1.702
LANES = 128
SUBLANES = 8
EXPERT_ROWS = 512
ISSUE_UNROLL = 8
ATT_KEY_CHUNKS = (1280, 256)
ATT_QUERY_BLOCKS = (1024, 512, 256, 128)
CONV_ROW_CHUNK = 64
NEG_BIG = -1e30
VMEM_LIMIT = 56 * 1024 * 1024


def _params(n_axes):
    return pltpu.CompilerParams(dimension_semantics=("arbitrary",) * n_axes,
                                vmem_limit_bytes=VMEM_LIMIT)


def _pick(n, candidates):
    for c in candidates:
        if n % c == 0:
            return c
    return n


def _layer_norm(r, g, b):
    mu = jnp.mean(r, axis=-1, keepdims=True)
    d = r - mu
    var = jnp.mean(d * d, axis=-1, keepdims=True)
    return d * lax.rsqrt(var + LN_EPS) * g + b


def _sigmoid(x):
    return 1.0 / (1.0 + jnp.exp(-x))


def _mod_kernel(cs_ref, w_ref, b_ref, o_ref):
    cs = cs_ref[...]
    a = cs * _sigmoid(cs)
    o_ref[0] = jnp.dot(a, w_ref[0], precision=HIGHEST, preferred_element_type=F32) + b_ref[0]


def _modulation(c, c_ctx, w_mod, b_mod):
    depth, d, n = w_mod.shape
    b = c.shape[0]
    cs = jnp.concatenate([c, c_ctx[None], jnp.zeros((SUBLANES - b - 1, d), F32)], 0)
    tn = _pick(n, (1536, 1024, 512))
    return pl.pallas_call(
        _mod_kernel,
        grid=(depth, n // tn),
        in_specs=[pl.BlockSpec((SUBLANES, d), lambda l, j: (0, 0)),
                  pl.BlockSpec((1, d, tn), lambda l, j: (l, 0, j)),
                  pl.BlockSpec((1, 1, tn), lambda l, j: (l, 0, j))],
        out_specs=pl.BlockSpec((1, SUBLANES, tn), lambda l, j: (l, 0, j)),
        out_shape=jax.ShapeDtypeStruct((depth, SUBLANES, n), F32),
        compiler_params=_params(2),
        name="modulation",
    )(cs, w_mod, b_mod.reshape(depth, 1, n))


KV_TILE = 256
VT_ROWS = HEAD_V + 16


def _store_values_transposed(v, vt_ref):
    extra = (lax.broadcasted_iota(I32, (VT_ROWS - HEAD_V, v.shape[0]), 0) == 0).astype(BF16)
    for h in range(N_HEADS):
        vt_ref[0, h, 0, :HEAD_V, :] = v[:, h * HEAD_V:(h + 1) * HEAD_V].T.astype(BF16)
        vt_ref[0, h, 0, HEAD_V:, :] = extra


def _even_in_kernel(*refs, n_tiles):
    k_ref, vt_ref = refs[7], refs[8]
    pl.when(pl.program_id(1) < n_tiles)(lambda: _even_in_tile(*refs))

    @pl.when(pl.program_id(1) >= n_tiles)
    def _():
        k_ref[...] = jnp.zeros(k_ref.shape, k_ref.dtype)
        vt_ref[...] = jnp.zeros(vt_ref.shape, vt_ref.dtype)


def _even_in_tile(x_ref, sc_ref, sh_ref, w_ref, cos_ref, sin_ref, q_ref, k_ref, vt_ref, a_ref):
    u = (x_ref[0] * (1.0 + sc_ref[0]) + sh_ref[0]).astype(BF16)
    z = jnp.dot(u, w_ref[...], preferred_element_type=F32)
    cosv = cos_ref[...]
    sinv = sin_ref[...]
    lane = lax.broadcasted_iota(I32, cosv.shape, 1)
    first = (lane % 32) < 16

    def rope(t):
        partner = jnp.where(first, pltpu.roll(t, LANES - 16, 1), pltpu.roll(t, 16, 1))
        return t * cosv + partner * sinv

    for j in range(QK_W // LANES):
        sl = slice(j * LANES, (j + 1) * LANES)
        rq = rope(z[:, sl]) * (HEAD_DIM ** -0.5 * math.log2(math.e))
        for c in range(2):
            q_ref[c, 0, j] = jnp.where(lane // HEAD_DIM == c, rq, 0.0).T.astype(BF16)
        k_ref[0, :, sl] = rope(z[:, QK_W + j * LANES:QK_W + (j + 1) * LANES]).astype(BF16)
    _store_values_transposed(z[:, 2 * QK_W:2 * QK_W + V_W], vt_ref)
    a0 = 2 * QK_W + V_W
    a_ref[0] = z[:, a0:a0 + CONV_W] * _sigmoid(z[:, a0 + CONV_W:a0 + 2 * CONV_W])


def _rope_tables(length):
    rows = length // GRID_W
    row = jnp.repeat(jnp.arange(rows, dtype=F32), GRID_W)
    col = jnp.tile(jnp.arange(GRID_W, dtype=F32), rows)
    n_freq = HEAD_DIM // 4
    inv_freq = ROPE_BASE ** (-jnp.arange(n_freq, dtype=F32) / n_freq)
    ar = row[:, None] * inv_freq
    ac = col[:, None] * inv_freq
    cos64 = jnp.concatenate([jnp.cos(ar), jnp.cos(ar), jnp.cos(ac), jnp.cos(ac)], 1)
    sin64 = jnp.concatenate([-jnp.sin(ar), jnp.sin(ar), -jnp.sin(ac), jnp.sin(ac)], 1)
    return jnp.tile(cos64, (1, LANES // HEAD_DIM)), jnp.tile(sin64, (1, LANES // HEAD_DIM))


def _key_chunk(n_keys):
    return next(c for c in ATT_KEY_CHUNKS if n_keys % c == 0 and n_keys // c >= 3)


def _kv_specs(b, n_keys, kc, first_tile):
    per_chunk = kc // KV_TILE
    k_spec = pl.BlockSpec((1, KV_TILE, QK_W), lambda bi, i: (bi, first_tile + i, 0))
    vt_spec = pl.BlockSpec((1, N_HEADS, 1, VT_ROWS, KV_TILE),
                           lambda bi, i: (bi, 0, (first_tile + i) // per_chunk, 0, (first_tile + i) % per_chunk))
    shapes = [jax.ShapeDtypeStruct((b, n_keys, QK_W), BF16),
              jax.ShapeDtypeStruct((b, N_HEADS, n_keys // kc, VT_ROWS, kc), BF16)]
    return k_spec, vt_spec, shapes


def _even_in_proj(x, sc, sh, w_bf, cos_t, sin_t, n_keys):
    b, length, d = x.shape
    n = w_bf.shape[1]
    tm = KV_TILE
    n_tiles = length // tm
    last = lambda i: jnp.minimum(i, n_tiles - 1)
    row = lambda bi, i: (bi, last(i), 0)
    vec = lambda bi, i: (bi, 0, 0)
    k_spec, vt_spec, kv_shapes = _kv_specs(b, n_keys, _key_chunk(n_keys), 0)
    return pl.pallas_call(
        functools.partial(_even_in_kernel, n_tiles=n_tiles),
        grid=(b, n_keys // tm),
        in_specs=[pl.BlockSpec((1, tm, d), row),
                  pl.BlockSpec((1, 1, d), vec),
                  pl.BlockSpec((1, 1, d), vec),
                  pl.BlockSpec((d, n), lambda bi, i: (0, 0)),
                  pl.BlockSpec((tm, LANES), lambda bi, i: (last(i), 0)),
                  pl.BlockSpec((tm, LANES), lambda bi, i: (last(i), 0))],
        out_specs=[pl.BlockSpec((2, 1, N_HEADS, HEAD_V, tm), lambda bi, i: (0, bi, 0, 0, last(i))), k_spec,
                   vt_spec, pl.BlockSpec((1, tm, CONV_W), row)],
        out_shape=[jax.ShapeDtypeStruct((2, b, N_HEADS, HEAD_V, length), BF16)] + kv_shapes
                  + [jax.ShapeDtypeStruct((b, length, CONV_W), F32)],
        compiler_params=_params(2),
        name="even_in_proj",
    )(x, sc, sh, w_bf, cos_t, sin_t)


def _ctx_kv_kernel(x_ref, sc_ref, sh_ref, w_ref, k_in, vt_in, k_ref, vt_ref):
    del k_in, vt_in
    u = (x_ref[0] * (1.0 + sc_ref[0]) + sh_ref[0]).astype(BF16)
    z = jnp.dot(u, w_ref[...], preferred_element_type=F32)
    k_ref[0] = z[:, :QK_W].astype(BF16)
    _store_values_transposed(z[:, QK_W:], vt_ref)


def _context_kv(ctx, sc, sh, w_bf, k_all, vt, first_key):
    b, rows, d = ctx.shape
    n = w_bf.shape[1]
    n_keys = k_all.shape[1]
    k_spec, vt_spec, kv_shapes = _kv_specs(b, n_keys, vt.shape[4], first_key // KV_TILE)
    return pl.pallas_call(
        _ctx_kv_kernel,
        grid=(b, rows // KV_TILE),
        in_specs=[pl.BlockSpec((1, KV_TILE, d), lambda bi, i: (bi, i, 0)),
                  pl.BlockSpec((1, 1, d), lambda bi, i: (bi, 0, 0)),
                  pl.BlockSpec((1, 1, d), lambda bi, i: (bi, 0, 0)),
                  pl.BlockSpec((d, n), lambda bi, i: (0, 0)),
                  pl.BlockSpec(memory_space=pl.ANY), pl.BlockSpec(memory_space=pl.ANY)],
        out_specs=[k_spec, vt_spec],
        out_shape=kv_shapes,
        input_output_aliases={4: 0, 5: 1},
        compiler_params=_params(2),
        name="context_kv_proj",
    )(ctx, sc, sh, w_bf, k_all, vt)


def _attn_kernel(lq1_ref, lk1_ref, lq2_ref, lk2_ref, g_ref, q_ref, k_ref, vt_ref, o_ref,
                 s0_sc, s1_sc, p0_sc, p1_sc, acc_sc, *, lam_init, kc):
    mq = q_ref.shape[4]
    n = k_ref.shape[1] // kc
    s_sc = (s0_sc, s1_sc)
    p_sc = (p0_sc, p1_sc)

    def scores(j, slot):
        kj = k_ref[0, pl.ds(pl.multiple_of(j * kc, kc), kc), :]
        for c in range(2):
            s_sc[slot][c] = jnp.dot(kj, q_ref[c, 0, 0], preferred_element_type=F32)

    def softmax(slot, m):
        m_out, alpha = [], []
        for c in range(2):
            s = s_sc[slot][c]
            m_new = jnp.maximum(m[c], jnp.max(s, axis=0, keepdims=True))
            p_sc[slot][c] = jnp.exp2((s - m_new).astype(BF16))
            m_out.append(m_new)
            alpha.append(jnp.exp2(m[c] - m_new))
        return tuple(m_out), tuple(alpha)

    def accumulate(j, slot, alpha):
        vt = vt_ref[0, 0, j]
        for c in range(2):
            acc_sc[c] = alpha[c] * acc_sc[c] + jnp.dot(vt, p_sc[slot][c], preferred_element_type=F32)

    def step(j, slot, m, alpha):
        scores(j + 2, slot)
        m, alpha_next = softmax(1 - slot, m)
        accumulate(j, slot, alpha)
        return m, alpha_next

    acc_sc[...] = jnp.zeros(acc_sc.shape, F32)
    m = (jnp.full((1, mq), NEG_BIG, F32),) * 2
    scores(0, 0)
    m, alpha = softmax(0, m)
    scores(1, 1)

    def pair(t, carry):
        m, alpha = step(2 * t, 0, *carry)
        return step(2 * t + 1, 1, m, alpha)

    n_steps = n - 2
    m, alpha = lax.fori_loop(0, n_steps // 2, pair, (m, alpha))
    if n_steps % 2:
        m, alpha = step(n_steps - 1, 0, m, alpha)
    m, alpha_last = softmax((n - 1) % 2, m)
    accumulate(n - 2, (n - 2) % 2, alpha)
    accumulate(n - 1, (n - 1) % 2, alpha_last)
    acc0 = acc_sc[0]
    acc1 = acc_sc[1]
    lam = (jnp.exp(jnp.sum(lq1_ref[...] * lk1_ref[...], keepdims=True))
           - jnp.exp(jnp.sum(lq2_ref[...] * lk2_ref[...], keepdims=True)) + lam_init)
    o = (acc0[:HEAD_V] / acc0[HEAD_V:HEAD_V + 1]
         - lam * (acc1[:HEAD_V] / acc1[HEAD_V:HEAD_V + 1]))
    ms = jnp.mean(o * o, axis=0, keepdims=True)
    o = o * lax.rsqrt(ms + LN_EPS) * g_ref[...] * (1.0 - lam_init)
    o_ref[0] = o.T.astype(o_ref.dtype)


def _diff_attention(q, k_all, vt, lq1, lk1, lq2, lk2, norm_g, lam_init):
    _, b, _, _, length = q.shape
    n_keys = k_all.shape[1]
    mq = _pick(length, ATT_QUERY_BLOCKS)
    nc, kc = vt.shape[2], vt.shape[4]
    small = lambda bi, h, i: (0, 0)
    return pl.pallas_call(
        functools.partial(_attn_kernel, lam_init=lam_init, kc=kc),
        grid=(b, N_HEADS, length // mq),
        in_specs=[pl.BlockSpec((1, HEAD_DIM), small), pl.BlockSpec((1, HEAD_DIM), small),
                  pl.BlockSpec((1, HEAD_DIM), small), pl.BlockSpec((1, HEAD_DIM), small),
                  pl.BlockSpec((HEAD_V, 1), small),
                  pl.BlockSpec((2, 1, 1, HEAD_V, mq), lambda bi, h, i: (0, bi, h, 0, i)),
                  pl.BlockSpec((1, n_keys, HEAD_V), lambda bi, h, i: (bi, 0, h)),
                  pl.BlockSpec((1, 1, nc, VT_ROWS, kc), lambda bi, h, i: (bi, h, 0, 0, 0))],
        out_specs=pl.BlockSpec((1, mq, HEAD_V), lambda bi, h, i: (bi, i, h)),
        out_shape=jax.ShapeDtypeStruct((b, length, V_W), BF16),
        scratch_shapes=[pltpu.VMEM((2, kc, mq), F32), pltpu.VMEM((2, kc, mq), F32),
                        pltpu.VMEM((2, kc, mq), BF16), pltpu.VMEM((2, kc, mq), BF16),
                        pltpu.VMEM((2, VT_ROWS, mq), F32)],
        compiler_params=_params(3),
        name="diff_attention",
    )(lq1[None], lk1[None], lq2[None], lk2[None], norm_g[:, None], q, k_all, vt)


def _conv_kernel(prev_ref, cur_ref, next_ref, w_ref, b_ref, g_ref, beta_ref, o_ref, win, *, row_chunk):
    i = pl.program_id(1)
    tc = cur_ref.shape[1]
    span = tc + 2 * CONV_HALO - SUBLANES
    win[0, CONV_HALO:CONV_HALO + tc, :] = cur_ref[0]
    win[0, 0:CONV_HALO, :] = jnp.where(i > 0, prev_ref[0], 0.0)
    win[0, CONV_HALO + tc:, :] = jnp.where(i < pl.num_programs(1) - 1, next_ref[0], 0.0)
    for s in range(1, SUBLANES):
        win[s, 0:span, :] = win[0, s:s + span, :]
    first_tap = CONV_HALO - CONV_TAPS // 2
    for c0 in range(0, tc, row_chunk):
        acc = jnp.zeros((row_chunk, CONV_W), F32)
        for t in range(CONV_TAPS):
            s = (first_tap + t) % SUBLANES
            r0 = c0 + first_tap + t - s
            acc = acc + w_ref[t:t + 1, :] * win[s, r0:r0 + row_chunk, :]
        y = _layer_norm(acc + b_ref[...], g_ref[...], beta_ref[...])
        o_ref[0, c0:c0 + row_chunk, :] = (y * _sigmoid(y)).astype(o_ref.dtype)


def _conformer_conv(a, conv_w, conv_b, ln_g, ln_b):
    b, length, w = a.shape
    tc = _pick(length, (512, 256, 128))
    hb = tc // CONV_HALO
    n_halo = length // CONV_HALO
    w_pad = jnp.concatenate([conv_w, jnp.zeros((1, w), F32)], 0)
    vec = lambda bi, i: (0, 0)
    return pl.pallas_call(
        functools.partial(_conv_kernel, row_chunk=CONV_ROW_CHUNK),
        grid=(b, length // tc),
        in_specs=[pl.BlockSpec((1, CONV_HALO, w), lambda bi, i: (bi, jnp.maximum(i * hb - 1, 0), 0)),
                  pl.BlockSpec((1, tc, w), lambda bi, i: (bi, i, 0)),
                  pl.BlockSpec((1, CONV_HALO, w), lambda bi, i: (bi, jnp.minimum((i + 1) * hb, n_halo - 1), 0)),
                  pl.BlockSpec((CONV_TAPS + 1, w), vec),
                  pl.BlockSpec((1, w), vec), pl.BlockSpec((1, w), vec), pl.BlockSpec((1, w), vec)],
        out_specs=pl.BlockSpec((1, tc, w), lambda bi, i: (bi, i, 0)),
        out_shape=jax.ShapeDtypeStruct((b, length, w), BF16),
        scratch_shapes=[pltpu.VMEM((SUBLANES, tc + 2 * CONV_HALO, w), F32)],
        compiler_params=_params(2),
        name="conformer_conv",
    )(a, a, a, w_pad, conv_b[None], ln_g[None], ln_b[None])


def _post_mix_kernel(ya_ref, yb_ref, wa_ref, wb_ref, x_ref, g1_ref, lng_ref, lnb_ref, sc_ref, sh_ref,
                     rw_ref, rb_ref, h_ref, e_ref, gate_ref, pos_ref, cnt_ref, carry, *, alpha):
    first_step = jnp.logical_and(pl.program_id(0) == 0, pl.program_id(1) == 0)

    @pl.when(first_step)
    def _():
        carry[...] = jnp.zeros(carry.shape, F32)

    y = (jnp.dot(ya_ref[0].astype(BF16), wa_ref[...], preferred_element_type=F32)
         + jnp.dot(yb_ref[0].astype(BF16), wb_ref[...], preferred_element_type=F32))
    h = _layer_norm(alpha * x_ref[0] + g1_ref[0] * y, lng_ref[...], lnb_ref[...])
    h_ref[0] = h
    u2 = h * (1.0 + sc_ref[0]) + sh_ref[0]
    u_hi = u2.astype(BF16)
    u_lo = (u2 - u_hi.astype(F32)).astype(BF16)
    hh_hl = jnp.dot(u_hi, rw_ref[...], preferred_element_type=F32)
    lh = jnp.dot(u_lo, rw_ref[:, :LANES], preferred_element_type=F32)
    logits = hh_hl[:, :LANES] + hh_hl[:, LANES:] + lh + rb_ref[...]

    tm = logits.shape[0]
    lane = lax.broadcasted_iota(I32, logits.shape, 1).astype(F32)
    vals, idxs = [], []
    lg = logits
    for _ in range(TOP_K):
        mx = jnp.max(lg, axis=1, keepdims=True)
        idx = jnp.min(jnp.where(lg == mx, lane, float(LANES)), axis=1, keepdims=True)
        vals.append(mx)
        idxs.append(idx)
        lg = jnp.where(lane == idx, -3e38, lg)
    exps = [jnp.exp(v - vals[0]) for v in vals]
    den = exps[0] + exps[1] + exps[2] + exps[3]

    hot = jnp.zeros(logits.shape, F32)
    for idx in idxs:
        hot = hot + (lane == idx).astype(F32)
    r_i = lax.broadcasted_iota(I32, (tm, tm), 0)
    c_i = lax.broadcasted_iota(I32, (tm, tm), 1)
    tri = (c_i < r_i).astype(BF16)
    before = jnp.dot(tri, hot.astype(BF16), preferred_element_type=F32) + carry[...]
    e_out = jnp.zeros(logits.shape, F32)
    g_out = jnp.zeros(logits.shape, F32)
    p_out = jnp.zeros(logits.shape, F32)
    for j in range(TOP_K):
        pos_j = jnp.sum(jnp.where(lane == idxs[j], before, 0.0), axis=1, keepdims=True)
        sel = lane == float(j)
        e_out = jnp.where(sel, idxs[j], e_out)
        g_out = jnp.where(sel, exps[j] / den, g_out)
        p_out = jnp.where(sel, pos_j, p_out)
    e_ref[...] = e_out.T[:SUBLANES].astype(I32)
    gate_ref[...] = g_out
    pos_ref[...] = p_out.T[:SUBLANES].astype(I32)
    new_carry = carry[...] + jnp.sum(hot, axis=0, keepdims=True)
    carry[...] = new_carry
    cnt_ref[...] = jnp.broadcast_to(new_carry, cnt_ref.shape).astype(I32)


def _post_mix(ya, yb, w_out_bf, x, g1, ln_g, ln_b, sc2, sh2, router_w, router_b, alpha):
    b, length, d = x.shape
    wa = ya.shape[2]
    tm = _pick(length, (512, 256, 128))
    n_tok = b * length
    rw = jnp.concatenate([router_w, jnp.zeros((d, LANES - N_EXPERTS), F32)], 1)
    rw_hi = rw.astype(BF16)
    rw = jnp.concatenate([rw_hi, (rw - rw_hi.astype(F32)).astype(BF16)], 1)
    rb = jnp.concatenate([router_b, jnp.full((LANES - N_EXPERTS,), NEG_BIG, F32)])[None]
    row = lambda bi, i: (bi, i, 0)
    vec = lambda bi, i: (bi, 0, 0)
    const = lambda bi, i: (0, 0)
    nt = length // tm
    tok = lambda bi, i: (bi * nt + i, 0)
    return pl.pallas_call(
        functools.partial(_post_mix_kernel, alpha=alpha),
        grid=(b, nt),
        in_specs=[pl.BlockSpec((1, tm, wa), row), pl.BlockSpec((1, tm, d - wa), row),
                  pl.BlockSpec((wa, d), const), pl.BlockSpec((d - wa, d), const),
                  pl.BlockSpec((1, tm, d), row), pl.BlockSpec((1, 1, d), vec),
                  pl.BlockSpec((1, d), const), pl.BlockSpec((1, d), const),
                  pl.BlockSpec((1, 1, d), vec), pl.BlockSpec((1, 1, d), vec),
                  pl.BlockSpec((d, 2 * LANES), const), pl.BlockSpec((1, LANES), const)],
        out_specs=[pl.BlockSpec((1, tm, d), row), pl.BlockSpec((SUBLANES, tm), lambda bi, i: (0, bi * nt + i)),
                   pl.BlockSpec((tm, LANES), tok), pl.BlockSpec((SUBLANES, tm), lambda bi, i: (0, bi * nt + i)),
                   pl.BlockSpec((SUBLANES, LANES), const)],
        out_shape=[jax.ShapeDtypeStruct((b, length, d), F32), jax.ShapeDtypeStruct((SUBLANES, n_tok), I32),
                   jax.ShapeDtypeStruct((n_tok, LANES), F32), jax.ShapeDtypeStruct((SUBLANES, n_tok), I32),
                   jax.ShapeDtypeStruct((SUBLANES, LANES), I32)],
        scratch_shapes=[pltpu.VMEM((1, LANES), F32)],
        compiler_params=_params(2),
        name="out_proj_norm_router",
    )(ya, yb, w_out_bf[:wa], w_out_bf[wa:], x, g1, ln_g[None], ln_b[None], sc2, sh2, rw, rb)


def _tile_indices(dest, tm):
    n_tok = dest.shape[1]
    return dest.reshape(TOP_K, n_tok // tm, tm).transpose(1, 0, 2).reshape(n_tok // tm, 1, TOP_K * tm)


def _dispatch_kernel(zstart_ref, zvalid_ref, dest_ref, h_ref, sc_ref, sh_ref, xs_ref, u_sc, zero_sc, sem, zsem):
    tm = h_ref.shape[0]

    @pl.when(pl.program_id(0) == 0)
    def _():
        zero_sc[...] = jnp.zeros(zero_sc.shape, F32)

        def fill(e, c):
            @pl.when(zvalid_ref[e] > 0)
            def _():
                start = pl.multiple_of(zstart_ref[e], SUBLANES)
                pltpu.make_async_copy(zero_sc, xs_ref.at[pl.ds(start, EXPERT_ROWS)], zsem).start()
            return c

        lax.fori_loop(0, N_EXPERTS, fill, 0)

        def drain(e, c):
            @pl.when(zvalid_ref[e] > 0)
            def _():
                pltpu.make_async_copy(zero_sc, xs_ref.at[pl.ds(0, EXPERT_ROWS)], zsem).wait()
            return c

        lax.fori_loop(0, N_EXPERTS, drain, 0)

    i = pl.program_id(0)
    slot = i % 2
    src = u_sc.at[slot]
    src[...] = h_ref[...] * (1.0 + sc_ref[0]) + sh_ref[0]

    def issue_from(which):
        for r in range(tm):
            for j in range(TOP_K):
                d = dest_ref[0, 0, j * tm + r]
                pltpu.make_async_copy(u_sc.at[which, pl.ds(r, 1)], xs_ref.at[pl.ds(d, 1)],
                                      sem.at[which]).start(priority=(r * TOP_K + j) % 2)

    pl.when(slot == 0)(lambda: issue_from(0))
    pl.when(slot == 1)(lambda: issue_from(1))

    def wait_tile(which):
        rows = xs_ref.at[pl.ds(0, tm * TOP_K)]
        pltpu.make_async_copy(rows, rows, sem.at[which]).wait()

    pl.when(i > 0)(lambda: wait_tile(1 - slot))
    pl.when(i == pl.num_programs(0) - 1)(lambda: wait_tile(slot))


def _dispatch(h_flat, sc2, sh2, dest, zstart, zvalid, n_rows, tokens_per_batch):
    n_tok, d = h_flat.shape
    tm = _pick(tokens_per_batch, (256, 128))
    per_b = tokens_per_batch // tm
    return pl.pallas_call(
        _dispatch_kernel,
        grid_spec=pltpu.PrefetchScalarGridSpec(
            num_scalar_prefetch=2,
            grid=(n_tok // tm,),
            in_specs=[pl.BlockSpec((1, 1, tm * TOP_K), lambda i, zs, zv: (i, 0, 0), memory_space=pltpu.SMEM),
                      pl.BlockSpec((tm, d), lambda i, zs, zv: (i, 0)),
                      pl.BlockSpec((1, 1, d), lambda i, zs, zv: (i // per_b, 0, 0)),
                      pl.BlockSpec((1, 1, d), lambda i, zs, zv: (i // per_b, 0, 0))],
            out_specs=pl.BlockSpec(memory_space=pl.ANY),
            scratch_shapes=[pltpu.VMEM((2, tm, d), F32), pltpu.VMEM((EXPERT_ROWS, d), F32),
                            pltpu.SemaphoreType.DMA((2,)), pltpu.SemaphoreType.DMA(())]),
        out_shape=jax.ShapeDtypeStruct((n_rows, d), F32),
        compiler_params=_params(1),
        name="moe_dispatch",
    )(zstart, zvalid, _tile_indices(dest, tm), h_flat, sc2, sh2)


def _expert_kernel(blk_e_ref, n_used_ref, x_ref, wgu_ref, bgu_ref, wd_ref, bd_ref, o_ref, wgu_bf, wd_bf):
    i = pl.program_id(0)
    new_expert = jnp.logical_or(i == 0, blk_e_ref[i] != blk_e_ref[jnp.maximum(i - 1, 0)])

    @pl.when(new_expert)
    def _():
        wgu_bf[...] = wgu_ref[0].astype(BF16)
        wd_bf[...] = wd_ref[0].astype(BF16)

    @pl.when(i < n_used_ref[0])
    def _():
        d_e = wd_ref.shape[1]
        gu = jnp.dot(x_ref[...].astype(BF16), wgu_bf[...], preferred_element_type=F32) + bgu_ref[0]
        x_glu = jnp.minimum(gu[:, :d_e], SWIGLU_LIMIT)
        x_lin = jnp.clip(gu[:, d_e:], -SWIGLU_LIMIT, SWIGLU_LIMIT)
        act = x_glu * _sigmoid(SWIGLU_ALPHA * x_glu) * (x_lin + 1.0)
        o_ref[...] = jnp.dot(act.astype(BF16), wd_bf[...], preferred_element_type=F32) + bd_ref[0]


def _experts(xs, blk_e, n_used, wgu, bgu, wd, bd, layer):
    n_rows, d = xs.shape
    depth, n_e, _, n_gu = wgu.shape
    n_blk = n_rows // EXPERT_ROWS
    rows = lambda i, be, nu: (jnp.minimum(i, nu[0] - 1), 0)
    first = layer * n_e
    by_e = lambda i, be, nu: (first + be[i], 0, 0)
    n_e = depth * n_e
    wgu = wgu.reshape(n_e, d, n_gu)
    wd = wd.reshape(n_e, n_gu // 2, d)
    return pl.pallas_call(
        _expert_kernel,
        grid_spec=pltpu.PrefetchScalarGridSpec(
            num_scalar_prefetch=2,
            grid=(n_blk,),
            in_specs=[pl.BlockSpec((EXPERT_ROWS, d), rows),
                      pl.BlockSpec((1, d, n_gu), by_e), pl.BlockSpec((1, 1, n_gu), by_e),
                      pl.BlockSpec((1, n_gu // 2, d), by_e), pl.BlockSpec((1, 1, d), by_e)],
            out_specs=pl.BlockSpec((EXPERT_ROWS, d), rows),
            scratch_shapes=[pltpu.VMEM((d, n_gu), BF16), pltpu.VMEM((n_gu // 2, d), BF16)]),
        out_shape=jax.ShapeDtypeStruct((n_rows, d), F32),
        compiler_params=_params(1),
        name="moe_experts",
    )(blk_e, n_used, xs, wgu, bgu.reshape(n_e, 1, n_gu), wd, bd.reshape(n_e, 1, d))


def _combine_kernel(dest_ref, dest_next_ref, y_ref, h_ref, gate_ref, g2_ref, lng_ref, lnb_ref, o_ref,
                    buf, sem, *, alpha):
    i = pl.program_id(0)
    tm = h_ref.shape[0]

    def fetch(idx_ref, slot):
        def issue(g, c):
            for rr in range(ISSUE_UNROLL):
                r = g * ISSUE_UNROLL + rr
                for j in range(TOP_K):
                    d = idx_ref[0, 0, j * tm + r]
                    pltpu.make_async_copy(y_ref.at[pl.ds(d, 1)], buf.at[slot, j, pl.ds(r, 1)],
                                          sem.at[slot]).start(priority=(rr * TOP_K + j) % 2)
            return c

        lax.fori_loop(0, tm // ISSUE_UNROLL, issue, 0)

    def fetch_unrolled(idx_ref, slot):
        for r in range(tm):
            for j in range(TOP_K):
                d = idx_ref[0, 0, j * tm + r]
                pltpu.make_async_copy(y_ref.at[pl.ds(d, 1)], buf.at[slot, j, pl.ds(r, 1)],
                                      sem.at[slot]).start(priority=(r * TOP_K + j) % 2)

    slot = i % 2
    more = i + 1 < pl.num_programs(0)

    @pl.when(i == 0)
    def _():
        fetch(dest_ref, 0)

    pl.when(jnp.logical_and(more, slot == 0))(lambda: fetch_unrolled(dest_next_ref, 1))
    pl.when(jnp.logical_and(more, slot == 1))(lambda: fetch_unrolled(dest_next_ref, 0))

    pltpu.make_async_copy(buf.at[slot], buf.at[slot], sem.at[slot]).wait()
    gate = gate_ref[...]
    y2 = gate[:, 0:1] * buf[slot, 0]
    for j in range(1, TOP_K):
        y2 = y2 + gate[:, j:j + 1] * buf[slot, j]
    o_ref[...] = _layer_norm(alpha * h_ref[...] + g2_ref[0] * y2, lng_ref[...], lnb_ref[...])


def _combine(y_rows, dest, h_flat, gate, g2, ln_g, ln_b, alpha, tokens_per_batch):
    n_tok, d = h_flat.shape
    tm = _pick(tokens_per_batch, (256, 128))
    per_b = tokens_per_batch // tm
    n_tiles = n_tok // tm
    dest3 = _tile_indices(dest, tm)
    return pl.pallas_call(
        functools.partial(_combine_kernel, alpha=alpha),
        grid=(n_tiles,),
        in_specs=[pl.BlockSpec((1, 1, tm * TOP_K), lambda i: (i, 0, 0), memory_space=pltpu.SMEM),
                  pl.BlockSpec((1, 1, tm * TOP_K), lambda i: (jnp.minimum(i + 1, n_tiles - 1), 0, 0),
                               memory_space=pltpu.SMEM),
                  pl.BlockSpec(memory_space=pl.ANY),
                  pl.BlockSpec((tm, d), lambda i: (i, 0)),
                  pl.BlockSpec((tm, LANES), lambda i: (i, 0)),
                  pl.BlockSpec((1, 1, d), lambda i: (i // per_b, 0, 0)),
                  pl.BlockSpec((1, d), lambda i: (0, 0)), pl.BlockSpec((1, d), lambda i: (0, 0))],
        out_specs=pl.BlockSpec((tm, d), lambda i: (i, 0)),
        out_shape=jax.ShapeDtypeStruct((n_tok, d), F32),
        scratch_shapes=[pltpu.VMEM((2, TOP_K, tm, d), F32), pltpu.SemaphoreType.DMA((2,))],
        compiler_params=_params(1),
        name="moe_combine_norm",
    )(dest3, dest3, y_rows, h_flat, gate, g2, ln_g[None], ln_b[None])


def _moe_and_norm(h1, top_e, gate, pos, counts, sc2, sh2, g2, wgu, bgu, wd, bd, layer, ln_g, ln_b, alpha):
    b, length, d = h1.shape
    n_tok = b * length
    n_blk = n_tok * TOP_K // EXPERT_ROWS + N_EXPERTS
    padded = (counts + EXPERT_ROWS - 1) // EXPERT_ROWS * EXPERT_ROWS
    pad_end = jnp.cumsum(padded)
    offset = pad_end - padded
    n_used = (pad_end[-1:] // EXPERT_ROWS).astype(I32)
    blk_start = jnp.arange(n_blk, dtype=I32) * EXPERT_ROWS
    blk_e = jnp.minimum(jnp.sum(pad_end[None, :] <= blk_start[:, None], axis=1), N_EXPERTS - 1).astype(I32)
    dest = pos
    for e in range(N_EXPERTS):
        dest = dest + jnp.where(top_e == e, offset[e], 0)
    dest = dest.astype(I32)
    zstart = jnp.maximum(pad_end - EXPERT_ROWS, 0).astype(I32)
    zvalid = (counts > 0).astype(I32)

    h_flat = h1.reshape(n_tok, d)
    xs = _dispatch(h_flat, sc2, sh2, dest, zstart, zvalid, n_blk * EXPERT_ROWS, length)
    y_rows = _experts(xs, blk_e, n_used, wgu, bgu, wd, bd, layer)
    out = _combine(y_rows, dest, h_flat, gate, g2, ln_g, ln_b, alpha, length)
    return out.reshape(b, length, d)


def _odd_in_kernel(x_ref, sc_ref, sh_ref, w_ref, glg_ref, glb_ref, ws_ref, bs_ref, flg_ref, flb_ref, dft_ref,
                   sp_ref, zr_ref, zi_ref):
    u = (x_ref[0] * (1.0 + sc_ref[0]) + sh_ref[0]).astype(BF16)
    z = jnp.dot(u, w_ref[...], preferred_element_type=F32)
    tm = z.shape[0]
    gw = N_GROUPS * GROUP_W
    ug = jax.nn.gelu(z[:, :gw])
    vn = _layer_norm(jax.nn.gelu(z[:, gw:2 * gw]), glg_ref[...], glb_ref[...]).astype(BF16)
    f = z[:, 2 * gw:]
    for g in range(N_GROUPS):
        cols = slice(g * GROUP_W, (g + 1) * GROUP_W)
        for c0 in range(0, tm, CHUNK):
            rows = slice(c0, c0 + CHUNK)
            sv = jnp.dot(ws_ref[g], vn[rows, cols], preferred_element_type=F32) + bs_ref[:, cols]
            sp_ref[0, rows, cols] = (ug[rows, cols] * sv).astype(sp_ref.dtype)
        fn = _layer_norm(f[:, cols], flg_ref[:, cols], flb_ref[:, cols]).astype(BF16)
        zz = jnp.dot(fn, dft_ref[...], preferred_element_type=F32)
        zr_ref[0, :, cols] = zz[:, :GROUP_W]
        zi_ref[0, :, cols] = zz[:, GROUP_W:]


def _odd_in_proj(x, sc, sh, w_bf, gln_g, gln_b, ws, bs, fln_g, fln_b):
    b, length, d = x.shape
    n = w_bf.shape[1]
    gw = N_GROUPS * GROUP_W
    tm = _pick(length, (512, 256, 128))
    kk = jnp.arange(GROUP_W, dtype=I32)
    ang = (2.0 * math.pi / GROUP_W) * ((kk[:, None] * kk[None, :]) % GROUP_W).astype(F32)
    dft = jnp.concatenate([jnp.cos(ang), -jnp.sin(ang)], 1).astype(BF16)
    bs_exp = jnp.repeat(bs.T, GROUP_W, axis=1)
    row = lambda bi, i: (bi, i, 0)
    vec = lambda bi, i: (bi, 0, 0)
    const2 = lambda bi, i: (0, 0)
    return pl.pallas_call(
        _odd_in_kernel,
        grid=(b, length // tm),
        in_specs=[pl.BlockSpec((1, tm, d), row), pl.BlockSpec((1, 1, d), vec), pl.BlockSpec((1, 1, d), vec),
                  pl.BlockSpec((d, n), const2),
                  pl.BlockSpec((1, gw), const2), pl.BlockSpec((1, gw), const2),
                  pl.BlockSpec((N_GROUPS, CHUNK, CHUNK), lambda bi, i: (0, 0, 0)),
                  pl.BlockSpec((CHUNK, gw), const2),
                  pl.BlockSpec((1, gw), const2), pl.BlockSpec((1, gw), const2),
                  pl.BlockSpec((GROUP_W, 2 * GROUP_W), const2)],
        out_specs=[pl.BlockSpec((1, tm, gw), row), pl.BlockSpec((1, tm, gw), row), pl.BlockSpec((1, tm, gw), row)],
        out_shape=[jax.ShapeDtypeStruct((b, length, gw), BF16), jax.ShapeDtypeStruct((b, length, gw), F32),
                   jax.ShapeDtypeStruct((b, length, gw), F32)],
        compiler_params=_params(2),
        name="odd_in_proj",
    )(x, sc, sh, w_bf, gln_g[None], gln_b[None], ws.astype(BF16), bs_exp, fln_g[None], fln_b[None], dft)


def _fft_a_kernel(zr_ref, zi_ref, cms_ref, c_ref, s_ref, ar_ref, ai_ref):
    l1, nb, w = zr_ref.shape[1:]
    zr = zr_ref[0].reshape(l1 * nb, w)
    zi = zi_ref[0].reshape(l1 * nb, w)
    k1 = jnp.dot(cms_ref[...], zr.astype(BF16), preferred_element_type=F32)
    k2 = jnp.dot(c_ref[...], (zi - zr).astype(BF16), preferred_element_type=F32)
    k3 = jnp.dot(s_ref[...], (zr + zi).astype(BF16), preferred_element_type=F32)
    ar_ref[0] = (k1 + k3).reshape(l1, nb, w)
    ai_ref[0] = (k1 + k2).reshape(l1, nb, w)


def _fft_b_kernel(ar_ref, ai_ref, c_ref, s_ref, mask_ref, o_ref, *, norm):
    nb, l2, w = ar_ref.shape[1:]
    ar = ar_ref[0].reshape(nb * l2, w).astype(BF16)
    ai = ai_ref[0].reshape(nb * l2, w).astype(BF16)
    mask = mask_ref[...]
    c = jnp.concatenate([c_ref[0]] * nb, axis=1) * mask
    s = jnp.concatenate([s_ref[0]] * nb, axis=1) * mask
    y = jnp.dot(c, ar, preferred_element_type=F32) + jnp.dot(s, ai, preferred_element_type=F32)
    o_ref[0] = (y * norm).reshape(l2, nb, w)


def _length_dft_real(zr, zi):
    b, length, w = zr.shape
    l2 = CHUNK
    l1 = length // l2
    nb = SUBLANES
    k1 = jnp.arange(l1, dtype=I32)
    ang_a = (2.0 * math.pi / l1) * ((k1[:, None] * k1[None, :]) % l1).astype(F32)
    r = lax.broadcasted_iota(I32, (l1 * nb, l1 * nb), 0)
    c = lax.broadcasted_iota(I32, (l1 * nb, l1 * nb), 1)
    same = r % nb == c % nb
    rep = (lax.broadcasted_iota(I32, (l1 * nb, l1), 0) // nb == lax.broadcasted_iota(I32, (l1 * nb, l1), 1))
    rep = rep.astype(BF16)

    def expand_a(t):
        t = jnp.dot(jnp.dot(rep, t.astype(BF16), preferred_element_type=F32).astype(BF16), rep.T,
                    preferred_element_type=F32)
        return jnp.where(same, t, 0.0).astype(BF16)

    cos_a = expand_a(jnp.cos(ang_a))
    sin_a = expand_a(jnp.sin(ang_a))
    cms_a = expand_a(jnp.cos(ang_a) - jnp.sin(ang_a))
    shape_t = (l1 // nb, l2 * nb, l2)
    kb = lax.broadcasted_iota(I32, shape_t, 0)
    r = lax.broadcasted_iota(I32, shape_t, 1)
    n2 = lax.broadcasted_iota(I32, shape_t, 2)
    ang_b = (2.0 * math.pi / length) * ((n2 * (kb * nb + r % nb + l1 * (r // nb))) % length).astype(F32)
    cos_b = jnp.cos(ang_b).astype(BF16)
    sin_b = jnp.sin(ang_b).astype(BF16)
    shape_m = (l2 * nb, nb * l2)
    mask_b = (lax.broadcasted_iota(I32, shape_m, 0) % nb == lax.broadcasted_iota(I32, shape_m, 1) // l2)
    mask_b = mask_b.astype(BF16)

    zr4 = zr.reshape(b, l1, l2, w)
    zi4 = zi.reshape(b, l1, l2, w)
    blk_a = pl.BlockSpec((1, l1, nb, w), lambda bi, i: (bi, 0, i, 0))
    ar, ai = pl.pallas_call(
        _fft_a_kernel,
        grid=(b, l2 // nb),
        in_specs=[blk_a, blk_a] + [pl.BlockSpec((l1 * nb, l1 * nb), lambda bi, i: (0, 0))] * 3,
        out_specs=[blk_a, blk_a],
        out_shape=[jax.ShapeDtypeStruct((b, l1, l2, w), F32)] * 2,
        compiler_params=_params(2),
        name="fourier_stage_a",
    )(zr4, zi4, cms_a, cos_a, sin_a)
    blk_b = pl.BlockSpec((1, nb, l2, w), lambda bi, i: (bi, i, 0, 0))
    out = pl.pallas_call(
        functools.partial(_fft_b_kernel, norm=float((length * GROUP_W) ** -0.5)),
        grid=(b, l1 // nb),
        in_specs=[blk_b, blk_b, pl.BlockSpec((1, l2 * nb, l2), lambda bi, i: (i, 0, 0)),
                  pl.BlockSpec((1, l2 * nb, l2), lambda bi, i: (i, 0, 0)),
                  pl.BlockSpec((l2 * nb, nb * l2), lambda bi, i: (0, 0))],
        out_specs=pl.BlockSpec((1, l2, nb, w), lambda bi, i: (bi, 0, i, 0)),
        out_shape=jax.ShapeDtypeStruct((b, l2, l1, w), F32),
        compiler_params=_params(2),
        name="fourier_stage_b",
    )(ar, ai, cos_b, sin_b, mask_b)
    return out.reshape(b, length, w)


def kernel(x, c, ctx, c_ctx, w_mod, b_mod, ln1_g, ln1_b, ln2_g, ln2_b, ev_w_in, ev_w_out, conv_w, conv_b, conv_ln_g, conv_ln_b, lam_q1, lam_k1, lam_q2, lam_k2, diff_norm_g, od_w_in, od_w_out, gmlp_ln_g, gmlp_ln_b, gmlp_ws, gmlp_bs, four_ln_g, four_ln_b, router_w, router_b, w_gate_up, b_gate_up, w_down, b_down):
    b, length, d = x.shape
    depth = w_mod.shape[0]
    alpha = float((2 * depth) ** 0.25)
    assert b + 1 <= SUBLANES and length % (CHUNK * SUBLANES) == 0 and length % GRID_W == 0
    assert length % KV_TILE == 0 and ctx.shape[1] % KV_TILE == 0

    mod = _modulation(c, c_ctx, w_mod, b_mod)
    cos_t, sin_t = _rope_tables(length)
    h = x
    for layer in range(depth):
        j = layer // 2
        m = mod[layer]
        sh1, sc1, g1, sh2, sc2, g2 = [m[:b, i * d:(i + 1) * d][:, None, :] for i in range(6)]
        if layer % 2 == 0:
            lam_init = 0.8 - 0.6 * math.exp(-0.3 * layer)
            w_in = ev_w_in[j].astype(BF16)
            q, k_all, vt, a = _even_in_proj(h, sc1, sh1, w_in, cos_t, sin_t, length + ctx.shape[1])
            csh1 = jnp.broadcast_to(m[b:b + 1, 0:d][:, None, :], (b, 1, d))
            csc1 = jnp.broadcast_to(m[b:b + 1, d:2 * d][:, None, :], (b, 1, d))
            k_all, vt = _context_kv(ctx, csc1, csh1, w_in[:, QK_W:2 * QK_W + V_W], k_all, vt, length)
            att = _diff_attention(q, k_all, vt, lam_q1[j], lam_k1[j], lam_q2[j], lam_k2[j],
                                  diff_norm_g[j], lam_init)
            conv = _conformer_conv(a, conv_w[j], conv_b[j], conv_ln_g[j], conv_ln_b[j])
            ya, yb, w_out = conv, att, ev_w_out[j]
        else:
            spatial, zr, zi = _odd_in_proj(h, sc1, sh1, od_w_in[j].astype(BF16), gmlp_ln_g[j], gmlp_ln_b[j],
                                           gmlp_ws[j], gmlp_bs[j], four_ln_g[j], four_ln_b[j])
            ya, yb, w_out = spatial, _length_dft_real(zr, zi), od_w_out[j]
        h1, top_e, gate, pos, counts = _post_mix(ya, yb, w_out.astype(BF16), h, g1, ln1_g[layer], ln1_b[layer],
                                                 sc2, sh2, router_w[layer], router_b[layer], alpha)
        h = _moe_and_norm(h1, top_e[:TOP_K], gate, pos[:TOP_K], counts[0, :N_EXPERTS], sc2, sh2, g2,
                          w_gate_up, b_gate_up, w_down, b_down, layer, ln2_g[layer], ln2_b[layer], alpha)
    return h
```

```python
import functools
import math

import jax
import jax.numpy as jnp
from jax import lax
from jax.experimental import pallas as pl
from jax.experimental.pallas import tpu as pltpu

F32 = jnp.float32
BF16 = jnp.bfloat16
I32 = jnp.int32
HIGHEST = lax.Precision.HIGHEST

LN_EPS = 1e-5
GRID_W = 64
HEAD_DIM = 64
HEAD_V = 128
N_HEADS = 4
QK_W = N_HEADS * 2 * HEAD_DIM
V_W = N_HEADS * HEAD_V
CONV_W = 512
CONV_TAPS = 31
CONV_HALO = 16
ROPE_BASE = 10000.0
CHUNK = 128
N_GROUPS = 4
GROUP_W = 128
N_EXPERTS = 32
TOP_K = 4
SWIGLU_LIMIT = 7.0
SWIGLU_ALPHA = 1.702
LANES = 128
SUBLANES = 8
EXPERT_ROWS = 512
ISSUE_UNROLL = 8
ATT_KEY_CHUNKS = (1280, 256)
ATT_QUERY_BLOCKS = (1024, 512, 256, 128)
CONV_ROW_CHUNK = 64
NEG_BIG = -1e30
VMEM_LIMIT = 56 * 1024 * 1024


def _params(n_axes):
    return pltpu.CompilerParams(dimension_semantics=("arbitrary",) * n_axes,
                                vmem_limit_bytes=VMEM_LIMIT)


def _pick(n, candidates):
    for c in candidates:
        if n % c == 0:
            return c
    return n


def _layer_norm(r, g, b):
    mu = jnp.mean(r, axis=-1, keepdims=True)
    d = r - mu
    var = jnp.mean(d * d, axis=-1, keepdims=True)
    return d * lax.rsqrt(var + LN_EPS) * g + b


def _sigmoid(x):
    return 1.0 / (1.0 + jnp.exp(-x))


def _mod_kernel(cs_ref, w_ref, b_ref, o_ref):
    cs = cs_ref[...]
    a = cs * _sigmoid(cs)
    o_ref[0] = jnp.dot(a, w_ref[0], precision=HIGHEST, preferred_element_type=F32) + b_ref[0]


def _modulation(c, c_ctx, w_mod, b_mod):
    depth, d, n = w_mod.shape
    b = c.shape[0]
    cs = jnp.concatenate([c, c_ctx[None], jnp.zeros((SUBLANES - b - 1, d), F32)], 0)
    tn = _pick(n, (1536, 1024, 512))
    return pl.pallas_call(
        _mod_kernel,
        grid=(depth, n // tn),
        in_specs=[pl.BlockSpec((SUBLANES, d), lambda l, j: (0, 0)),
                  pl.BlockSpec((1, d, tn), lambda l, j: (l, 0, j)),
                  pl.BlockSpec((1, 1, tn), lambda l, j: (l, 0, j))],
        out_specs=pl.BlockSpec((1, SUBLANES, tn), lambda l, j: (l, 0, j)),
        out_shape=jax.ShapeDtypeStruct((depth, SUBLANES, n), F32),
        compiler_params=_params(2),
        name="modulation",
    )(cs, w_mod, b_mod.reshape(depth, 1, n))


KV_TILE = 256
VT_ROWS = HEAD_V + 16


def _store_values_transposed(v, vt_ref):
    extra = (lax.broadcasted_iota(I32, (VT_ROWS - HEAD_V, v.shape[0]), 0) == 0).astype(BF16)
    for h in range(N_HEADS):
        vt_ref[0, h, 0, :HEAD_V, :] = v[:, h * HEAD_V:(h + 1) * HEAD_V].T.astype(BF16)
        vt_ref[0, h, 0, HEAD_V:, :] = extra


def _even_in_kernel(*refs, n_tiles):
    k_ref, vt_ref = refs[7], refs[8]
    pl.when(pl.program_id(1) < n_tiles)(lambda: _even_in_tile(*refs))

    @pl.when(pl.program_id(1) >= n_tiles)
    def _():
        k_ref[...] = jnp.zeros(k_ref.shape, k_ref.dtype)
        vt_ref[...] = jnp.zeros(vt_ref.shape, vt_ref.dtype)


def _even_in_tile(x_ref, sc_ref, sh_ref, w_ref, cos_ref, sin_ref, q_ref, k_ref, vt_ref, a_ref):
    u = (x_ref[0] * (1.0 + sc_ref[0]) + sh_ref[0]).astype(BF16)
    z = jnp.dot(u, w_ref[...], preferred_element_type=F32)
    cosv = cos_ref[...]
    sinv = sin_ref[...]
    lane = lax.broadcasted_iota(I32, cosv.shape, 1)
    first = (lane % 32) < 16

    def rope(t):
        partner = jnp.where(first, pltpu.roll(t, LANES - 16, 1), pltpu.roll(t, 16, 1))
        return t * cosv + partner * sinv

    for j in range(QK_W // LANES):
        sl = slice(j * LANES, (j + 1) * LANES)
        rq = rope(z[:, sl]) * (HEAD_DIM ** -0.5 * math.log2(math.e))
        for c in range(2):
            q_ref[c, 0, j] = jnp.where(lane // HEAD_DIM == c, rq, 0.0).T.astype(BF16)
        k_ref[0, :, sl] = rope(z[:, QK_W + j * LANES:QK_W + (j + 1) * LANES]).astype(BF16)
    _store_values_transposed(z[:, 2 * QK_W:2 * QK_W + V_W], vt_ref)
    a0 = 2 * QK_W + V_W
    a_ref[0] = z[:, a0:a0 + CONV_W] * _sigmoid(z[:, a0 + CONV_W:a0 + 2 * CONV_W])


def _rope_tables(length):
    rows = length // GRID_W
    row = jnp.repeat(jnp.arange(rows, dtype=F32), GRID_W)
    col = jnp.tile(jnp.arange(GRID_W, dtype=F32), rows)
    n_freq = HEAD_DIM // 4
    inv_freq = ROPE_BASE ** (-jnp.arange(n_freq, dtype=F32) / n_freq)
    ar = row[:, None] * inv_freq
    ac = col[:, None] * inv_freq
    cos64 = jnp.concatenate([jnp.cos(ar), jnp.cos(ar), jnp.cos(ac), jnp.cos(ac)], 1)
    sin64 = jnp.concatenate([-jnp.sin(ar), jnp.sin(ar), -jnp.sin(ac), jnp.sin(ac)], 1)
    return jnp.tile(cos64, (1, LANES // HEAD_DIM)), jnp.tile(sin64, (1, LANES // HEAD_DIM))


def _key_chunk(n_keys):
    return next(c for c in ATT_KEY_CHUNKS if n_keys % c == 0 and n_keys // c >= 3)


def _kv_specs(b, n_keys, kc, first_tile):
    per_chunk = kc // KV_TILE
    k_spec = pl.BlockSpec((1, KV_TILE, QK_W), lambda bi, i: (bi, first_tile + i, 0))
    vt_spec = pl.BlockSpec((1, N_HEADS, 1, VT_ROWS, KV_TILE),
                           lambda bi, i: (bi, 0, (first_tile + i) // per_chunk, 0, (first_tile + i) % per_chunk))
    shapes = [jax.ShapeDtypeStruct((b, n_keys, QK_W), BF16),
              jax.ShapeDtypeStruct((b, N_HEADS, n_keys // kc, VT_ROWS, kc), BF16)]
    return k_spec, vt_spec, shapes


def _even_in_proj(x, sc, sh, w_bf, cos_t, sin_t, n_keys):
    b, length, d = x.shape
    n = w_bf.shape[1]
    tm = KV_TILE
    n_tiles = length // tm
    last = lambda i: jnp.minimum(i, n_tiles - 1)
    row = lambda bi, i: (bi, last(i), 0)
    vec = lambda bi, i: (bi, 0, 0)
    k_spec, vt_spec, kv_shapes = _kv_specs(b, n_keys, _key_chunk(n_keys), 0)
    return pl.pallas_call(
        functools.partial(_even_in_kernel, n_tiles=n_tiles),
        grid=(b, n_keys // tm),
        in_specs=[pl.BlockSpec((1, tm, d), row),
                  pl.BlockSpec((1, 1, d), vec),
                  pl.BlockSpec((1, 1, d), vec),
                  pl.BlockSpec((d, n), lambda bi, i: (0, 0)),
                  pl.BlockSpec((tm, LANES), lambda bi, i: (last(i), 0)),
                  pl.BlockSpec((tm, LANES), lambda bi, i: (last(i), 0))],
        out_specs=[pl.BlockSpec((2, 1, N_HEADS, HEAD_V, tm), lambda bi, i: (0, bi, 0, 0, last(i))), k_spec,
                   vt_spec, pl.BlockSpec((1, tm, CONV_W), row)],
        out_shape=[jax.ShapeDtypeStruct((2, b, N_HEADS, HEAD_V, length), BF16)] + kv_shapes
                  + [jax.ShapeDtypeStruct((b, length, CONV_W), F32)],
        compiler_params=_params(2),
        name="even_in_proj",
    )(x, sc, sh, w_bf, cos_t, sin_t)


def _ctx_kv_kernel(x_ref, sc_ref, sh_ref, w_ref, k_in, vt_in, k_ref, vt_ref):
    del k_in, vt_in
    u = (x_ref[0] * (1.0 + sc_ref[0]) + sh_ref[0]).astype(BF16)
    z = jnp.dot(u, w_ref[...], preferred_element_type=F32)
    k_ref[0] = z[:, :QK_W].astype(BF16)
    _store_values_transposed(z[:, QK_W:], vt_ref)


def _context_kv(ctx, sc, sh, w_bf, k_all, vt, first_key):
    b, rows, d = ctx.shape
    n = w_bf.shape[1]
    n_keys = k_all.shape[1]
    k_spec, vt_spec, kv_shapes = _kv_specs(b, n_keys, vt.shape[4], first_key // KV_TILE)
    return pl.pallas_call(
        _ctx_kv_kernel,
        grid=(b, rows // KV_TILE),
        in_specs=[pl.BlockSpec((1, KV_TILE, d), lambda bi, i: (bi, i, 0)),
                  pl.BlockSpec((1, 1, d), lambda bi, i: (bi, 0, 0)),
                  pl.BlockSpec((1, 1, d), lambda bi, i: (bi, 0, 0)),
                  pl.BlockSpec((d, n), lambda bi, i: (0, 0)),
                  pl.BlockSpec(memory_space=pl.ANY), pl.BlockSpec(memory_space=pl.ANY)],
        out_specs=[k_spec, vt_spec],
        out_shape=kv_shapes,
        input_output_aliases={4: 0, 5: 1},
        compiler_params=_params(2),
        name="context_kv_proj",
    )(ctx, sc, sh, w_bf, k_all, vt)


def _attn_kernel(lq1_ref, lk1_ref, lq2_ref, lk2_ref, g_ref, q_ref, k_ref, vt_ref, o_ref,
                 s0_sc, s1_sc, p0_sc, p1_sc, acc_sc, *, lam_init, kc):
    mq = q_ref.shape[4]
    n = k_ref.shape[1] // kc
    s_sc = (s0_sc, s1_sc)
    p_sc = (p0_sc, p1_sc)

    def scores(j, slot):
        kj = k_ref[0, pl.ds(pl.multiple_of(j * kc, kc), kc), :]
        for c in range(2):
            s_sc[slot][c] = jnp.dot(kj, q_ref[c, 0, 0], preferred_element_type=F32)

    def softmax(slot, m):
        m_out, alpha = [], []
        for c in range(2):
            s = s_sc[slot][c]
            m_new = jnp.maximum(m[c], jnp.max(s, axis=0, keepdims=True))
            p_sc[slot][c] = jnp.exp2((s - m_new).astype(BF16))
            m_out.append(m_new)
            alpha.append(jnp.exp2(m[c] - m_new))
        return tuple(m_out), tuple(alpha)

    def accumulate(j, slot, alpha):
        vt = vt_ref[0, 0, j]
        for c in range(2):
            acc_sc[c] = alpha[c] * acc_sc[c] + jnp.dot(vt, p_sc[slot][c], preferred_element_type=F32)

    def step(j, slot, m, alpha):
        scores(j + 2, slot)
        m, alpha_next = softmax(1 - slot, m)
        accumulate(j, slot, alpha)
        return m, alpha_next

    acc_sc[...] = jnp.zeros(acc_sc.shape, F32)
    m = (jnp.full((1, mq), NEG_BIG, F32),) * 2
    scores(0, 0)
    m, alpha = softmax(0, m)
    scores(1, 1)

    def pair(t, carry):
        m, alpha = step(2 * t, 0, *carry)
        return step(2 * t + 1, 1, m, alpha)

    n_steps = n - 2
    m, alpha = lax.fori_loop(0, n_steps // 2, pair, (m, alpha))
    if n_steps % 2:
        m, alpha = step(n_steps - 1, 0, m, alpha)
    m, alpha_last = softmax((n - 1) % 2, m)
    accumulate(n - 2, (n - 2) % 2, alpha)
    accumulate(n - 1, (n - 1) % 2, alpha_last)
    acc0 = acc_sc[0]
    acc1 = acc_sc[1]
    lam = (jnp.exp(jnp.sum(lq1_ref[...] * lk1_ref[...], keepdims=True))
           - jnp.exp(jnp.sum(lq2_ref[...] * lk2_ref[...], keepdims=True)) + lam_init)
    o = (acc0[:HEAD_V] / acc0[HEAD_V:HEAD_V + 1]
         - lam * (acc1[:HEAD_V] / acc1[HEAD_V:HEAD_V + 1]))
    ms = jnp.mean(o * o, axis=0, keepdims=True)
    o = o * lax.rsqrt(ms + LN_EPS) * g_ref[...] * (1.0 - lam_init)
    o_ref[0] = o.T.astype(o_ref.dtype)


def _diff_attention(q, k_all, vt, lq1, lk1, lq2, lk2, norm_g, lam_init):
    _, b, _, _, length = q.shape
    n_keys = k_all.shape[1]
    mq = _pick(length, ATT_QUERY_BLOCKS)
    nc, kc = vt.shape[2], vt.shape[4]
    small = lambda bi, h, i: (0, 0)
    return pl.pallas_call(
        functools.partial(_attn_kernel, lam_init=lam_init, kc=kc),
        grid=(b, N_HEADS, length // mq),
        in_specs=[pl.BlockSpec((1, HEAD_DIM), small), pl.BlockSpec((1, HEAD_DIM), small),
                  pl.BlockSpec((1, HEAD_DIM), small), pl.BlockSpec((1, HEAD_DIM), small),
                  pl.BlockSpec((HEAD_V, 1), small),
                  pl.BlockSpec((2, 1, 1, HEAD_V, mq), lambda bi, h, i: (0, bi, h, 0, i)),
                  pl.BlockSpec((1, n_keys, HEAD_V), lambda bi, h, i: (bi, 0, h)),
                  pl.BlockSpec((1, 1, nc, VT_ROWS, kc), lambda bi, h, i: (bi, h, 0, 0, 0))],
        out_specs=pl.BlockSpec((1, mq, HEAD_V), lambda bi, h, i: (bi, i, h)),
        out_shape=jax.ShapeDtypeStruct((b, length, V_W), BF16),
        scratch_shapes=[pltpu.VMEM((2, kc, mq), F32), pltpu.VMEM((2, kc, mq), F32),
                        pltpu.VMEM((2, kc, mq), BF16), pltpu.VMEM((2, kc, mq), BF16),
                        pltpu.VMEM((2, VT_ROWS, mq), F32)],
        compiler_params=_params(3),
        name="diff_attention",
    )(lq1[None], lk1[None], lq2[None], lk2[None], norm_g[:, None], q, k_all, vt)


def _conv_kernel(prev_ref, cur_ref, next_ref, w_ref, b_ref, g_ref, beta_ref, o_ref, win, *, row_chunk):
    i = pl.program_id(1)
    tc = cur_ref.shape[1]
    span = tc + 2 * CONV_HALO - SUBLANES
    win[0, CONV_HALO:CONV_HALO + tc, :] = cur_ref[0]
    win[0, 0:CONV_HALO, :] = jnp.where(i > 0, prev_ref[0], 0.0)
    win[0, CONV_HALO + tc:, :] = jnp.where(i < pl.num_programs(1) - 1, next_ref[0], 0.0)
    for s in range(1, SUBLANES):
        win[s, 0:span, :] = win[0, s:s + span, :]
    first_tap = CONV_HALO - CONV_TAPS // 2
    for c0 in range(0, tc, row_chunk):
        acc = jnp.zeros((row_chunk, CONV_W), F32)
        for t in range(CONV_TAPS):
            s = (first_tap + t) % SUBLANES
            r0 = c0 + first_tap + t - s
            acc = acc + w_ref[t:t + 1, :] * win[s, r0:r0 + row_chunk, :]
        y = _layer_norm(acc + b_ref[...], g_ref[...], beta_ref[...])
        o_ref[0, c0:c0 + row_chunk, :] = (y * _sigmoid(y)).astype(o_ref.dtype)


def _conformer_conv(a, conv_w, conv_b, ln_g, ln_b):
    b, length, w = a.shape
    tc = _pick(length, (512, 256, 128))
    hb = tc // CONV_HALO
    n_halo = length // CONV_HALO
    w_pad = jnp.concatenate([conv_w, jnp.zeros((1, w), F32)], 0)
    vec = lambda bi, i: (0, 0)
    return pl.pallas_call(
        functools.partial(_conv_kernel, row_chunk=CONV_ROW_CHUNK),
        grid=(b, length // tc),
        in_specs=[pl.BlockSpec((1, CONV_HALO, w), lambda bi, i: (bi, jnp.maximum(i * hb - 1, 0), 0)),
                  pl.BlockSpec((1, tc, w), lambda bi, i: (bi, i, 0)),
                  pl.BlockSpec((1, CONV_HALO, w), lambda bi, i: (bi, jnp.minimum((i + 1) * hb, n_halo - 1), 0)),
                  pl.BlockSpec((CONV_TAPS + 1, w), vec),
                  pl.BlockSpec((1, w), vec), pl.BlockSpec((1, w), vec), pl.BlockSpec((1, w), vec)],
        out_specs=pl.BlockSpec((1, tc, w), lambda bi, i: (bi, i, 0)),
        out_shape=jax.ShapeDtypeStruct((b, length, w), BF16),
        scratch_shapes=[pltpu.VMEM((SUBLANES, tc + 2 * CONV_HALO, w), F32)],
        compiler_params=_params(2),
        name="conformer_conv",
    )(a, a, a, w_pad, conv_b[None], ln_g[None], ln_b[None])


def _post_mix_kernel(ya_ref, yb_ref, wa_ref, wb_ref, x_ref, g1_ref, lng_ref, lnb_ref, sc_ref, sh_ref,
                     rw_ref, rb_ref, h_ref, e_ref, gate_ref, pos_ref, cnt_ref, carry, *, alpha):
    first_step = jnp.logical_and(pl.program_id(0) == 0, pl.program_id(1) == 0)

    @pl.when(first_step)
    def _():
        carry[...] = jnp.zeros(carry.shape, F32)

    y = (jnp.dot(ya_ref[0].astype(BF16), wa_ref[...], preferred_element_type=F32)
         + jnp.dot(yb_ref[0].astype(BF16), wb_ref[...], preferred_element_type=F32))
    h = _layer_norm(alpha * x_ref[0] + g1_ref[0] * y, lng_ref[...], lnb_ref[...])
    h_ref[0] = h
    u2 = h * (1.0 + sc_ref[0]) + sh_ref[0]
    u_hi = u2.astype(BF16)
    u_lo = (u2 - u_hi.astype(F32)).astype(BF16)
    hh_hl = jnp.dot(u_hi, rw_ref[...], preferred_element_type=F32)
    lh = jnp.dot(u_lo, rw_ref[:, :LANES], preferred_element_type=F32)
    logits = hh_hl[:, :LANES] + hh_hl[:, LANES:] + lh + rb_ref[...]

    tm = logits.shape[0]
    lane = lax.broadcasted_iota(I32, logits.shape, 1).astype(F32)
    vals, idxs = [], []
    lg = logits
    for _ in range(TOP_K):
        mx = jnp.max(lg, axis=1, keepdims=True)
        idx = jnp.min(jnp.where(lg == mx, lane, float(LANES)), axis=1, keepdims=True)
        vals.append(mx)
        idxs.append(idx)
        lg = jnp.where(lane == idx, -3e38, lg)
    exps = [jnp.exp(v - vals[0]) for v in vals]
    den = exps[0] + exps[1] + exps[2] + exps[3]

    hot = jnp.zeros(logits.shape, F32)
    for idx in idxs:
        hot = hot + (lane == idx).astype(F32)
    r_i = lax.broadcasted_iota(I32, (tm, tm), 0)
    c_i = lax.broadcasted_iota(I32, (tm, tm), 1)
    tri = (c_i < r_i).astype(BF16)
    before = jnp.dot(tri, hot.astype(BF16), preferred_element_type=F32) + carry[...]
    e_out = jnp.zeros(logits.shape, F32)
    g_out = jnp.zeros(logits.shape, F32)
    p_out = jnp.zeros(logits.shape, F32)
    for j in range(TOP_K):
        pos_j = jnp.sum(jnp.where(lane == idxs[j], before, 0.0), axis=1, keepdims=True)
        sel = lane == float(j)
        e_out = jnp.where(sel, idxs[j], e_out)
        g_out = jnp.where(sel, exps[j] / den, g_out)
        p_out = jnp.where(sel, pos_j, p_out)
    e_ref[...] = e_out.T[:SUBLANES].astype(I32)
    gate_ref[...] = g_out
    pos_ref[...] = p_out.T[:SUBLANES].astype(I32)
    new_carry = carry[...] + jnp.sum(hot, axis=0, keepdims=True)
    carry[...] = new_carry
    cnt_ref[...] = jnp.broadcast_to(new_carry, cnt_ref.shape).astype(I32)


def _post_mix(ya, yb, w_out_bf, x, g1, ln_g, ln_b, sc2, sh2, router_w, router_b, alpha):
    b, length, d = x.shape
    wa = ya.shape[2]
    tm = _pick(length, (512, 256, 128))
    n_tok = b * length
    rw = jnp.concatenate([router_w, jnp.zeros((d, LANES - N_EXPERTS), F32)], 1)
    rw_hi = rw.astype(BF16)
    rw = jnp.concatenate([rw_hi, (rw - rw_hi.astype(F32)).astype(BF16)], 1)
    rb = jnp.concatenate([router_b, jnp.full((LANES - N_EXPERTS,), NEG_BIG, F32)])[None]
    row = lambda bi, i: (bi, i, 0)
    vec = lambda bi, i: (bi, 0, 0)
    const = lambda bi, i: (0, 0)
    nt = length // tm
    tok = lambda bi, i: (bi * nt + i, 0)
    return pl.pallas_call(
        functools.partial(_post_mix_kernel, alpha=alpha),
        grid=(b, nt),
        in_specs=[pl.BlockSpec((1, tm, wa), row), pl.BlockSpec((1, tm, d - wa), row),
                  pl.BlockSpec((wa, d), const), pl.BlockSpec((d - wa, d), const),
                  pl.BlockSpec((1, tm, d), row), pl.BlockSpec((1, 1, d), vec),
                  pl.BlockSpec((1, d), const), pl.BlockSpec((1, d), const),
                  pl.BlockSpec((1, 1, d), vec), pl.BlockSpec((1, 1, d), vec),
                  pl.BlockSpec((d, 2 * LANES), const), pl.BlockSpec((1, LANES), const)],
        out_specs=[pl.BlockSpec((1, tm, d), row), pl.BlockSpec((SUBLANES, tm), lambda bi, i: (0, bi * nt + i)),
                   pl.BlockSpec((tm, LANES), tok), pl.BlockSpec((SUBLANES, tm), lambda bi, i: (0, bi * nt + i)),
                   pl.BlockSpec((SUBLANES, LANES), const)],
        out_shape=[jax.ShapeDtypeStruct((b, length, d), F32), jax.ShapeDtypeStruct((SUBLANES, n_tok), I32),
                   jax.ShapeDtypeStruct((n_tok, LANES), F32), jax.ShapeDtypeStruct((SUBLANES, n_tok), I32),
                   jax.ShapeDtypeStruct((SUBLANES, LANES), I32)],
        scratch_shapes=[pltpu.VMEM((1, LANES), F32)],
        compiler_params=_params(2),
        name="out_proj_norm_router",
    )(ya, yb, w_out_bf[:wa], w_out_bf[wa:], x, g1, ln_g[None], ln_b[None], sc2, sh2, rw, rb)


def _tile_indices(dest, tm):
    n_tok = dest.shape[1]
    return dest.reshape(TOP_K, n_tok // tm, tm).transpose(1, 0, 2).reshape(n_tok // tm, 1, TOP_K * tm)


def _dispatch_kernel(zstart_ref, zvalid_ref, dest_ref, h_ref, sc_ref, sh_ref, xs_ref, u_sc, zero_sc, sem, zsem):
    tm = h_ref.shape[0]

    @pl.when(pl.program_id(0) == 0)
    def _():
        zero_sc[...] = jnp.zeros(zero_sc.shape, F32)

        def fill(e, c):
            @pl.when(zvalid_ref[e] > 0)
            def _():
                start = pl.multiple_of(zstart_ref[e], SUBLANES)
                pltpu.make_async_copy(zero_sc, xs_ref.at[pl.ds(start, EXPERT_ROWS)], zsem).start()
            return c

        lax.fori_loop(0, N_EXPERTS, fill, 0)

        def drain(e, c):
            @pl.when(zvalid_ref[e] > 0)
            def _():
                pltpu.make_async_copy(zero_sc, xs_ref.at[pl.ds(0, EXPERT_ROWS)], zsem).wait()
            return c

        lax.fori_loop(0, N_EXPERTS, drain, 0)

    i = pl.program_id(0)
    slot = i % 2
    src = u_sc.at[slot]
    src[...] = h_ref[...] * (1.0 + sc_ref[0]) + sh_ref[0]

    def issue_from(which):
        for r in range(tm):
            for j in range(TOP_K):
                d = dest_ref[0, 0, j * tm + r]
                pltpu.make_async_copy(u_sc.at[which, pl.ds(r, 1)], xs_ref.at[pl.ds(d, 1)],
                                      sem.at[which]).start(priority=(r * TOP_K + j) % 2)

    pl.when(slot == 0)(lambda: issue_from(0))
    pl.when(slot == 1)(lambda: issue_from(1))

    def wait_tile(which):
        rows = xs_ref.at[pl.ds(0, tm * TOP_K)]
        pltpu.make_async_copy(rows, rows, sem.at[which]).wait()

    pl.when(i > 0)(lambda: wait_tile(1 - slot))
    pl.when(i == pl.num_programs(0) - 1)(lambda: wait_tile(slot))


def _dispatch(h_flat, sc2, sh2, dest, zstart, zvalid, n_rows, tokens_per_batch):
    n_tok, d = h_flat.shape
    tm = _pick(tokens_per_batch, (256, 128))
    per_b = tokens_per_batch // tm
    return pl.pallas_call(
        _dispatch_kernel,
        grid_spec=pltpu.PrefetchScalarGridSpec(
            num_scalar_prefetch=2,
            grid=(n_tok // tm,),
            in_specs=[pl.BlockSpec((1, 1, tm * TOP_K), lambda i, zs, zv: (i, 0, 0), memory_space=pltpu.SMEM),
                      pl.BlockSpec((tm, d), lambda i, zs, zv: (i, 0)),
                      pl.BlockSpec((1, 1, d), lambda i, zs, zv: (i // per_b, 0, 0)),
                      pl.BlockSpec((1, 1, d), lambda i, zs, zv: (i // per_b, 0, 0))],
            out_specs=pl.BlockSpec(memory_space=pl.ANY),
            scratch_shapes=[pltpu.VMEM((2, tm, d), F32), pltpu.VMEM((EXPERT_ROWS, d), F32),
                            pltpu.SemaphoreType.DMA((2,)), pltpu.SemaphoreType.DMA(())]),
        out_shape=jax.ShapeDtypeStruct((n_rows, d), F32),
        compiler_params=_params(1),
        name="moe_dispatch",
    )(zstart, zvalid, _tile_indices(dest, tm), h_flat, sc2, sh2)


def _expert_kernel(blk_e_ref, n_used_ref, x_ref, wgu_ref, bgu_ref, wd_ref, bd_ref, o_ref, wgu_bf, wd_bf):
    i = pl.program_id(0)
    new_expert = jnp.logical_or(i == 0, blk_e_ref[i] != blk_e_ref[jnp.maximum(i - 1, 0)])

    @pl.when(new_expert)
    def _():
        wgu_bf[...] = wgu_ref[0].astype(BF16)
        wd_bf[...] = wd_ref[0].astype(BF16)

    @pl.when(i < n_used_ref[0])
    def _():
        d_e = wd_ref.shape[1]
        gu = jnp.dot(x_ref[...].astype(BF16), wgu_bf[...], preferred_element_type=F32) + bgu_ref[0]
        x_glu = jnp.minimum(gu[:, :d_e], SWIGLU_LIMIT)
        x_lin = jnp.clip(gu[:, d_e:], -SWIGLU_LIMIT, SWIGLU_LIMIT)
        act = x_glu * _sigmoid(SWIGLU_ALPHA * x_glu) * (x_lin + 1.0)
        o_ref[...] = jnp.dot(act.astype(BF16), wd_bf[...], preferred_element_type=F32) + bd_ref[0]


def _experts(xs, blk_e, n_used, wgu, bgu, wd, bd, layer):
    n_rows, d = xs.shape
    depth, n_e, _, n_gu = wgu.shape
    n_blk = n_rows // EXPERT_ROWS
    rows = lambda i, be, nu: (jnp.minimum(i, nu[0] - 1), 0)
    first = layer * n_e
    by_e = lambda i, be, nu: (first + be[i], 0, 0)
    n_e = depth * n_e
    wgu = wgu.reshape(n_e, d, n_gu)
    wd = wd.reshape(n_e, n_gu // 2, d)
    return pl.pallas_call(
        _expert_kernel,
        grid_spec=pltpu.PrefetchScalarGridSpec(
            num_scalar_prefetch=2,
            grid=(n_blk,),
            in_specs=[pl.BlockSpec((EXPERT_ROWS, d), rows),
                      pl.BlockSpec((1, d, n_gu), by_e), pl.BlockSpec((1, 1, n_gu), by_e),
                      pl.BlockSpec((1, n_gu // 2, d), by_e), pl.BlockSpec((1, 1, d), by_e)],
            out_specs=pl.BlockSpec((EXPERT_ROWS, d), rows),
            scratch_shapes=[pltpu.VMEM((d, n_gu), BF16), pltpu.VMEM((n_gu // 2, d), BF16)]),
        out_shape=jax.ShapeDtypeStruct((n_rows, d), F32),
        compiler_params=_params(1),
        name="moe_experts",
    )(blk_e, n_used, xs, wgu, bgu.reshape(n_e, 1, n_gu), wd, bd.reshape(n_e, 1, d))


def _combine_kernel(dest_ref, dest_next_ref, y_ref, h_ref, gate_ref, g2_ref, lng_ref, lnb_ref, o_ref,
                    buf, sem, *, alpha):
    i = pl.program_id(0)
    tm = h_ref.shape[0]

    def fetch(idx_ref, slot):
        def issue(g, c):
            for rr in range(ISSUE_UNROLL):
                r = g * ISSUE_UNROLL + rr
                for j in range(TOP_K):
                    d = idx_ref[0, 0, j * tm + r]
                    pltpu.make_async_copy(y_ref.at[pl.ds(d, 1)], buf.at[slot, j, pl.ds(r, 1)],
                                          sem.at[slot]).start(priority=(rr * TOP_K + j) % 2)
            return c

        lax.fori_loop(0, tm // ISSUE_UNROLL, issue, 0)

    def fetch_unrolled(idx_ref, slot):
        for r in range(tm):
            for j in range(TOP_K):
                d = idx_ref[0, 0, j * tm + r]
                pltpu.make_async_copy(y_ref.at[pl.ds(d, 1)], buf.at[slot, j, pl.ds(r, 1)],
                                      sem.at[slot]).start(priority=(r * TOP_K + j) % 2)

    @pl.when(i == 0)
    def _():
        fetch(dest_ref, 0)

    def wait_tile(which):
        pltpu.make_async_copy(buf.at[which], buf.at[which], sem.at[which]).wait()

    def run(slot):
        wait_tile(slot)
        gate = gate_ref[...]
        y2 = gate[:, 0:1] * buf[slot, 0]
        for j in range(1, TOP_K):
            y2 = y2 + gate[:, j:j + 1] * buf[slot, j]
        o_ref[...] = _layer_norm(alpha * h_ref[...] + g2_ref[0] * y2, lng_ref[...], lnb_ref[...])
        fetch_unrolled(dest_next_ref, 1 - slot)
        pl.when(i == pl.num_programs(0) - 1)(lambda: wait_tile(1 - slot))

    pl.when(i % 2 == 0)(lambda: run(0))
    pl.when(i % 2 == 1)(lambda: run(1))


def _combine(y_rows, dest, h_flat, gate, g2, ln_g, ln_b, alpha, tokens_per_batch):
    n_tok, d = h_flat.shape
    tm = _pick(tokens_per_batch, (256, 128))
    per_b = tokens_per_batch // tm
    n_tiles = n_tok // tm
    dest3 = _tile_indices(dest, tm)
    return pl.pallas_call(
        functools.partial(_combine_kernel, alpha=alpha),
        grid=(n_tiles,),
        in_specs=[pl.BlockSpec((1, 1, tm * TOP_K), lambda i: (i, 0, 0), memory_space=pltpu.SMEM),
                  pl.BlockSpec((1, 1, tm * TOP_K), lambda i: (jnp.minimum(i + 1, n_tiles - 1), 0, 0),
                               memory_space=pltpu.SMEM),
                  pl.BlockSpec(memory_space=pl.ANY),
                  pl.BlockSpec((tm, d), lambda i: (i, 0)),
                  pl.BlockSpec((tm, LANES), lambda i: (i, 0)),
                  pl.BlockSpec((1, 1, d), lambda i: (i // per_b, 0, 0)),
                  pl.BlockSpec((1, d), lambda i: (0, 0)), pl.BlockSpec((1, d), lambda i: (0, 0))],
        out_specs=pl.BlockSpec((tm, d), lambda i: (i, 0)),
        out_shape=jax.ShapeDtypeStruct((n_tok, d), F32),
        scratch_shapes=[pltpu.VMEM((2, TOP_K, tm, d), F32), pltpu.SemaphoreType.DMA((2,))],
        compiler_params=_params(1),
        name="moe_combine_norm",
    )(dest3, dest3, y_rows, h_flat, gate, g2, ln_g[None], ln_b[None])


def _moe_and_norm(h1, top_e, gate, pos, counts, sc2, sh2, g2, wgu, bgu, wd, bd, layer, ln_g, ln_b, alpha):
    b, length, d = h1.shape
    n_tok = b * length
    n_blk = n_tok * TOP_K // EXPERT_ROWS + N_EXPERTS
    padded = (counts + EXPERT_ROWS - 1) // EXPERT_ROWS * EXPERT_ROWS
    pad_end = jnp.cumsum(padded)
    offset = pad_end - padded
    n_used = (pad_end[-1:] // EXPERT_ROWS).astype(I32)
    blk_start = jnp.arange(n_blk, dtype=I32) * EXPERT_ROWS
    blk_e = jnp.minimum(jnp.sum(pad_end[None, :] <= blk_start[:, None], axis=1), N_EXPERTS - 1).astype(I32)
    dest = pos
    for e in range(N_EXPERTS):
        dest = dest + jnp.where(top_e == e, offset[e], 0)
    dest = dest.astype(I32)
    zstart = jnp.maximum(pad_end - EXPERT_ROWS, 0).astype(I32)
    zvalid = (counts > 0).astype(I32)

    h_flat = h1.reshape(n_tok, d)
    xs = _dispatch(h_flat, sc2, sh2, dest, zstart, zvalid, n_blk * EXPERT_ROWS, length)
    y_rows = _experts(xs, blk_e, n_used, wgu, bgu, wd, bd, layer)
    out = _combine(y_rows, dest, h_flat, gate, g2, ln_g, ln_b, alpha, length)
    return out.reshape(b, length, d)


def _odd_in_kernel(x_ref, sc_ref, sh_ref, w_ref, glg_ref, glb_ref, ws_ref, bs_ref, flg_ref, flb_ref, dft_ref,
                   sp_ref, zr_ref, zi_ref):
    u = (x_ref[0] * (1.0 + sc_ref[0]) + sh_ref[0]).astype(BF16)
    z = jnp.dot(u, w_ref[...], preferred_element_type=F32)
    tm = z.shape[0]
    gw = N_GROUPS * GROUP_W
    ug = jax.nn.gelu(z[:, :gw])
    vn = _layer_norm(jax.nn.gelu(z[:, gw:2 * gw]), glg_ref[...], glb_ref[...]).astype(BF16)
    f = z[:, 2 * gw:]
    for g in range(N_GROUPS):
        cols = slice(g * GROUP_W, (g + 1) * GROUP_W)
        for c0 in range(0, tm, CHUNK):
            rows = slice(c0, c0 + CHUNK)
            sv = jnp.dot(ws_ref[g], vn[rows, cols], preferred_element_type=F32) + bs_ref[:, cols]
            sp_ref[0, rows, cols] = (ug[rows, cols] * sv).astype(sp_ref.dtype)
        fn = _layer_norm(f[:, cols], flg_ref[:, cols], flb_ref[:, cols]).astype(BF16)
        zz = jnp.dot(fn, dft_ref[...], preferred_element_type=F32)
        zr_ref[0, :, cols] = zz[:, :GROUP_W]
        zi_ref[0, :, cols] = zz[:, GROUP_W:]


def _odd_in_proj(x, sc, sh, w_bf, gln_g, gln_b, ws, bs, fln_g, fln_b):
    b, length, d = x.shape
    n = w_bf.shape[1]
    gw = N_GROUPS * GROUP_W
    tm = _pick(length, (512, 256, 128))
    kk = jnp.arange(GROUP_W, dtype=I32)
    ang = (2.0 * math.pi / GROUP_W) * ((kk[:, None] * kk[None, :]) % GROUP_W).astype(F32)
    dft = jnp.concatenate([jnp.cos(ang), -jnp.sin(ang)], 1).astype(BF16)
    bs_exp = jnp.repeat(bs.T, GROUP_W, axis=1)
    row = lambda bi, i: (bi, i, 0)
    vec = lambda bi, i: (bi, 0, 0)
    const2 = lambda bi, i: (0, 0)
    return pl.pallas_call(
        _odd_in_kernel,
        grid=(b, length // tm),
        in_specs=[pl.BlockSpec((1, tm, d), row), pl.BlockSpec((1, 1, d), vec), pl.BlockSpec((1, 1, d), vec),
                  pl.BlockSpec((d, n), const2),
                  pl.BlockSpec((1, gw), const2), pl.BlockSpec((1, gw), const2),
                  pl.BlockSpec((N_GROUPS, CHUNK, CHUNK), lambda bi, i: (0, 0, 0)),
                  pl.BlockSpec((CHUNK, gw), const2),
                  pl.BlockSpec((1, gw), const2), pl.BlockSpec((1, gw), const2),
                  pl.BlockSpec((GROUP_W, 2 * GROUP_W), const2)],
        out_specs=[pl.BlockSpec((1, tm, gw), row), pl.BlockSpec((1, tm, gw), row), pl.BlockSpec((1, tm, gw), row)],
        out_shape=[jax.ShapeDtypeStruct((b, length, gw), BF16), jax.ShapeDtypeStruct((b, length, gw), F32),
                   jax.ShapeDtypeStruct((b, length, gw), F32)],
        compiler_params=_params(2),
        name="odd_in_proj",
    )(x, sc, sh, w_bf, gln_g[None], gln_b[None], ws.astype(BF16), bs_exp, fln_g[None], fln_b[None], dft)


def _fft_a_kernel(zr_ref, zi_ref, cms_ref, c_ref, s_ref, ar_ref, ai_ref):
    l1, nb, w = zr_ref.shape[1:]
    zr = zr_ref[0].reshape(l1 * nb, w)
    zi = zi_ref[0].reshape(l1 * nb, w)
    k1 = jnp.dot(cms_ref[...], zr.astype(BF16), preferred_element_type=F32)
    k2 = jnp.dot(c_ref[...], (zi - zr).astype(BF16), preferred_element_type=F32)
    k3 = jnp.dot(s_ref[...], (zr + zi).astype(BF16), preferred_element_type=F32)
    ar_ref[0] = (k1 + k3).reshape(l1, nb, w)
    ai_ref[0] = (k1 + k2).reshape(l1, nb, w)


def _fft_b_kernel(ar_ref, ai_ref, c_ref, s_ref, mask_ref, o_ref, *, norm):
    nb, l2, w = ar_ref.shape[1:]
    ar = ar_ref[0].reshape(nb * l2, w).astype(BF16)
    ai = ai_ref[0].reshape(nb * l2, w).astype(BF16)
    mask = mask_ref[...]
    c = jnp.concatenate([c_ref[0]] * nb, axis=1) * mask
    s = jnp.concatenate([s_ref[0]] * nb, axis=1) * mask
    y = jnp.dot(c, ar, preferred_element_type=F32) + jnp.dot(s, ai, preferred_element_type=F32)
    o_ref[0] = (y * norm).reshape(l2, nb, w)


def _length_dft_real(zr, zi):
    b, length, w = zr.shape
    l2 = CHUNK
    l1 = length // l2
    nb = SUBLANES
    k1 = jnp.arange(l1, dtype=I32)
    ang_a = (2.0 * math.pi / l1) * ((k1[:, None] * k1[None, :]) % l1).astype(F32)
    r = lax.broadcasted_iota(I32, (l1 * nb, l1 * nb), 0)
    c = lax.broadcasted_iota(I32, (l1 * nb, l1 * nb), 1)
    same = r % nb == c % nb
    rep = (lax.broadcasted_iota(I32, (l1 * nb, l1), 0) // nb == lax.broadcasted_iota(I32, (l1 * nb, l1), 1))
    rep = rep.astype(BF16)

    def expand_a(t):
        t = jnp.dot(jnp.dot(rep, t.astype(BF16), preferred_element_type=F32).astype(BF16), rep.T,
                    preferred_element_type=F32)
        return jnp.where(same, t, 0.0).astype(BF16)

    cos_a = expand_a(jnp.cos(ang_a))
    sin_a = expand_a(jnp.sin(ang_a))
    cms_a = expand_a(jnp.cos(ang_a) - jnp.sin(ang_a))
    shape_t = (l1 // nb, l2 * nb, l2)
    kb = lax.broadcasted_iota(I32, shape_t, 0)
    r = lax.broadcasted_iota(I32, shape_t, 1)
    n2 = lax.broadcasted_iota(I32, shape_t, 2)
    ang_b = (2.0 * math.pi / length) * ((n2 * (kb * nb + r % nb + l1 * (r // nb))) % length).astype(F32)
    cos_b = jnp.cos(ang_b).astype(BF16)
    sin_b = jnp.sin(ang_b).astype(BF16)
    shape_m = (l2 * nb, nb * l2)
    mask_b = (lax.broadcasted_iota(I32, shape_m, 0) % nb == lax.broadcasted_iota(I32, shape_m, 1) // l2)
    mask_b = mask_b.astype(BF16)

    zr4 = zr.reshape(b, l1, l2, w)
    zi4 = zi.reshape(b, l1, l2, w)
    blk_a = pl.BlockSpec((1, l1, nb, w), lambda bi, i: (bi, 0, i, 0))
    ar, ai = pl.pallas_call(
        _fft_a_kernel,
        grid=(b, l2 // nb),
        in_specs=[blk_a, blk_a] + [pl.BlockSpec((l1 * nb, l1 * nb), lambda bi, i: (0, 0))] * 3,
        out_specs=[blk_a, blk_a],
        out_shape=[jax.ShapeDtypeStruct((b, l1, l2, w), F32)] * 2,
        compiler_params=_params(2),
        name="fourier_stage_a",
    )(zr4, zi4, cms_a, cos_a, sin_a)
    blk_b = pl.BlockSpec((1, nb, l2, w), lambda bi, i: (bi, i, 0, 0))
    out = pl.pallas_call(
        functools.partial(_fft_b_kernel, norm=float((length * GROUP_W) ** -0.5)),
        grid=(b, l1 // nb),
        in_specs=[blk_b, blk_b, pl.BlockSpec((1, l2 * nb, l2), lambda bi, i: (i, 0, 0)),
                  pl.BlockSpec((1, l2 * nb, l2), lambda bi, i: (i, 0, 0)),
                  pl.BlockSpec((l2 * nb, nb * l2), lambda bi, i: (0, 0))],
        out_specs=pl.BlockSpec((1, l2, nb, w), lambda bi, i: (bi, 0, i, 0)),
        out_shape=jax.ShapeDtypeStruct((b, l2, l1, w), F32),
        compiler_params=_params(2),
        name="fourier_stage_b",
    )(ar, ai, cos_b, sin_b, mask_b)
    return out.reshape(b, length, w)


def kernel(x, c, ctx, c_ctx, w_mod, b_mod, ln1_g, ln1_b, ln2_g, ln2_b, ev_w_in, ev_w_out, conv_w, conv_b, conv_ln_g, conv_ln_b, lam_q1, lam_k1, lam_q2, lam_k2, diff_norm_g, od_w_in, od_w_out, gmlp_ln_g, gmlp_ln_b, gmlp_ws, gmlp_bs, four_ln_g, four_ln_b, router_w, router_b, w_gate_up, b_gate_up, w_down, b_down):
    b, length, d = x.shape
    depth = w_mod.shape[0]
    alpha = float((2 * depth) ** 0.25)
    assert b + 1 <= SUBLANES and length % (CHUNK * SUBLANES) == 0 and length % GRID_W == 0
    assert length % KV_TILE == 0 and ctx.shape[1] % KV_TILE == 0

    mod = _modulation(c, c_ctx, w_mod, b_mod)
    cos_t, sin_t = _rope_tables(length)
    h = x
    for layer in range(depth):
        j = layer // 2
        m = mod[layer]
        sh1, sc1, g1, sh2, sc2, g2 = [m[:b, i * d:(i + 1) * d][:, None, :] for i in range(6)]
        if layer % 2 == 0:
            lam_init = 0.8 - 0.6 * math.exp(-0.3 * layer)
            w_in = ev_w_in[j].astype(BF16)
            q, k_all, vt, a = _even_in_proj(h, sc1, sh1, w_in, cos_t, sin_t, length + ctx.shape[1])
            csh1 = jnp.broadcast_to(m[b:b + 1, 0:d][:, None, :], (b, 1, d))
            csc1 = jnp.broadcast_to(m[b:b + 1, d:2 * d][:, None, :], (b, 1, d))
            k_all, vt = _context_kv(ctx, csc1, csh1, w_in[:, QK_W:2 * QK_W + V_W], k_all, vt, length)
            att = _diff_attention(q, k_all, vt, lam_q1[j], lam_k1[j], lam_q2[j], lam_k2[j],
                                  diff_norm_g[j], lam_init)
            conv = _conformer_conv(a, conv_w[j], conv_b[j], conv_ln_g[j], conv_ln_b[j])
            ya, yb, w_out = conv, att, ev_w_out[j]
        else:
            spatial, zr, zi = _odd_in_proj(h, sc1, sh1, od_w_in[j].astype(BF16), gmlp_ln_g[j], gmlp_ln_b[j],
                                           gmlp_ws[j], gmlp_bs[j], four_ln_g[j], four_ln_b[j])
            ya, yb, w_out = spatial, _length_dft_real(zr, zi), od_w_out[j]
        h1, top_e, gate, pos, counts = _post_mix(ya, yb, w_out.astype(BF16), h, g1, ln1_g[layer], ln1_b[layer],
                                                 sc2, sh2, router_w[layer], router_b[layer], alpha)
        h = _moe_and_norm(h1, top_e[:TOP_K], gate, pos[:TOP_K], counts[0, :N_EXPERTS], sc2, sh2, g2,
                          w_gate_up, b_gate_up, w_down, b_down, layer, ln2_g[layer], ln2_b[layer], alpha)
    return h
```
